```python
import math
import jax, jax.numpy as jnp
from jax import lax
import numpy as np

D_MODEL = 4096
BATCH = 2
SEQ = 4096
DEPTH = 1

PLE_DIM = 256
BLOCK = 128
ROPE_THETA = 10000.0
LN_EPS = 1e-5
RMS_EPS = 1e-6
NEG_INF = -1e30

SWA_HEAD_DIM = 64
SWA_WIDTH = D_MODEL // 2
SWA_HEADS = SWA_WIDTH // SWA_HEAD_DIM
SWA_KV_HEADS = SWA_HEADS // 8
SWA_KV_WIDTH = SWA_KV_HEADS * SWA_HEAD_DIM
SWA_WINDOW = 128

SB_HEAD_DIM = 128
SB_WIDTH = D_MODEL - SWA_WIDTH
SB_HEADS = SB_WIDTH // SB_HEAD_DIM

MIX_WIDTH = SWA_WIDTH + SB_WIDTH
IN_WIDTH = SWA_WIDTH + 2 * SWA_KV_WIDTH + 3 * SB_WIDTH

N_GROUPS = 4
EXPERTS_PER_GROUP = 8
N_EXPERTS = N_GROUPS * EXPERTS_PER_GROUP
EXPERT_TOP_K = 2
D_EXPERT = D_MODEL // 8

DEEPNORM_ALPHA = (2.0 * DEPTH) ** 0.25
DEEPNORM_BETA = (8.0 * DEPTH) ** -0.25

kernel_name = "hymba_swa_sink_stickbreak_hmoe_deepnorm"


def layer_norm(x, g, b):
    xf = x.astype(jnp.float32)
    mu = jnp.mean(xf, axis=-1, keepdims=True)
    var = jnp.mean(jnp.square(xf - mu), axis=-1, keepdims=True)
    return ((xf - mu) * lax.rsqrt(var + LN_EPS) * g.astype(jnp.float32) + b.astype(jnp.float32)).astype(x.dtype)


def rms_norm(x, g):
    xf = x.astype(jnp.float32)
    return (xf * lax.rsqrt(jnp.mean(xf * xf, axis=-1, keepdims=True) + RMS_EPS) * g.astype(jnp.float32)).astype(x.dtype)


def apply_rope(x, positions):
    d = x.shape[-1]
    inv_freq = ROPE_THETA ** (-jnp.arange(0, d, 2, dtype=jnp.float32) / d)
    ang = positions.astype(jnp.float32)[..., None] * inv_freq
    cos = jnp.cos(ang)[:, :, None, :]
    sin = jnp.sin(ang)[:, :, None, :]
    x1, x2 = jnp.split(x.astype(jnp.float32), 2, axis=-1)
    return jnp.concatenate([x1 * cos - x2 * sin, x2 * cos + x1 * sin], axis=-1).astype(x.dtype)


def sliding_window_sink_attention(q, k, v, sinks):
    B, S, Hq, d = q.shape
    Hkv = k.shape[2]
    G = Hq // Hkv
    nb = S // BLOCK
    qb = q.reshape(B, nb, BLOCK, Hkv, G, d)

    def band(t):
        tb = t.reshape(B, nb, BLOCK, Hkv, d)
        prev = jnp.pad(tb, ((0, 0), (1, 0), (0, 0), (0, 0), (0, 0)))[:, :-1]
        return jnp.concatenate([prev, tb], axis=2)

    kb, vb = band(k), band(v)
    scores = jnp.einsum('bnqhgd,bnkhd->bnhgqk', qb, kb).astype(jnp.float32) / math.sqrt(d)
    blk = jnp.arange(nb)[:, None, None]
    qpos = blk * BLOCK + jnp.arange(BLOCK)[None, :, None]
    kpos = (blk - 1) * BLOCK + jnp.arange(2 * BLOCK)[None, None, :]
    rel = qpos - kpos
    valid = (rel >= 0) & (rel < SWA_WINDOW) & (kpos >= 0)
    scores = jnp.where(valid[None, :, None, None], scores, NEG_INF)
    sink_col = jnp.broadcast_to(
        sinks.astype(jnp.float32).reshape(Hkv, G)[None, None, :, :, None, None],
        scores.shape[:-1] + (1,))
    probs = jax.nn.softmax(jnp.concatenate([scores, sink_col], axis=-1), axis=-1)[..., :-1]
    out = jnp.einsum('bnhgqk,bnkhd->bnqhgd', probs.astype(v.dtype), vb)
    return out.reshape(B, S, Hq * d)


def stick_breaking_attention(q, k, v):
    B, S, H, d = q.shape
    nb = S // BLOCK
    q_blocks = q.reshape(B, nb, BLOCK, H, d).transpose(1, 0, 2, 3, 4)
    kpos = jnp.arange(S)

    def one_block(args):
        q_blk, n = args
        z = jnp.einsum('bqhd,bkhd->bhqk', q_blk, k).astype(jnp.float32) / math.sqrt(d)
        qpos = n * BLOCK + jnp.arange(BLOCK)
        causal = kpos[None, :] < qpos[:, None]
        log_not = jnp.where(causal, jax.nn.log_sigmoid(-z), 0.0)
        suffix = lax.cumsum(log_not, axis=log_not.ndim - 1, reverse=True)
        log_a = jax.nn.log_sigmoid(z) + suffix - log_not
        a = jnp.where(causal, jnp.exp(log_a), 0.0)
        return jnp.einsum('bhqk,bkhd->bqhd', a.astype(v.dtype), v)

    out = lax.map(one_block, (q_blocks, jnp.arange(nb)))
    return out.transpose(1, 0, 2, 3, 4).reshape(B, S, H * d)


def hierarchical_moe(h, w_group, b_group, w_er, b_er, w_gate, w_up, w_down):
    B, S, D = h.shape
    t = h.reshape(-1, D)
    g_logits = (t @ w_group + b_group).astype(jnp.float32)
    g_prob = jax.nn.softmax(g_logits, axis=-1)
    g_idx = jnp.argmax(g_logits, axis=-1)
    g_w = jnp.take_along_axis(g_prob, g_idx[:, None], axis=-1)
    e_all = jnp.einsum('nd,gde->nge', t, w_er) + b_er
    e_logits = jnp.take_along_axis(e_all, g_idx[:, None, None], axis=1)[:, 0].astype(jnp.float32)
    top_val, top_idx = lax.top_k(e_logits, EXPERT_TOP_K)
    top_w = jax.nn.softmax(top_val, axis=-1) * g_w
    flat_idx = g_idx[:, None] * EXPERTS_PER_GROUP + top_idx
    combine = jnp.sum(jax.nn.one_hot(flat_idx, N_EXPERTS, dtype=jnp.float32) * top_w[..., None], axis=1)
    gate = jnp.einsum('nd,xdf->nxf', t, w_gate)
    up = jnp.einsum('nd,xdf->nxf', t, w_up)
    act = jax.nn.silu(gate) * up * combine[..., None].astype(t.dtype)
    out = jnp.einsum('nxf,xfd->nd', act, w_down)
    return out.reshape(B, S, D)


def setup_inputs(seed: int = 0) -> dict:
    key = jax.random.key(seed)
    ks = jax.random.split(key, 24)
    f32 = jnp.float32
    nrm = lambda k, shape, s: jax.random.normal(k, shape, f32) * s
    x = nrm(ks[0], (BATCH, SEQ, D_MODEL), 1.0)
    p = nrm(ks[1], (DEPTH, BATCH, SEQ, PLE_DIM), 1.0)
    offset = jax.random.randint(ks[2], (BATCH, 1), 0, 1024, dtype=jnp.int32)
    positions = (jnp.arange(SEQ, dtype=jnp.int32)[None, :] + offset).astype(jnp.int32)
    return {
        "x": x,
        "p": p,
        "positions": positions,
        "w_in": nrm(ks[3], (DEPTH, D_MODEL, IN_WIDTH), D_MODEL ** -0.5),
        "b_in": nrm(ks[4], (DEPTH, IN_WIDTH), 0.01),
        "sinks": nrm(ks[5], (DEPTH, SWA_HEADS), 0.5),
        "g_norm_a": 1.0 + nrm(ks[6], (DEPTH, SWA_WIDTH), 0.05),
        "g_norm_b": 1.0 + nrm(ks[7], (DEPTH, SB_WIDTH), 0.05),
        "w_out": nrm(ks[8], (DEPTH, MIX_WIDTH, D_MODEL), DEEPNORM_BETA * MIX_WIDTH ** -0.5),
        "b_out": nrm(ks[9], (DEPTH, D_MODEL), 0.01),
        "ln1_g": 1.0 + nrm(ks[10], (DEPTH, D_MODEL), 0.05),
        "ln1_b": nrm(ks[11], (DEPTH, D_MODEL), 0.01),
        "w_group": nrm(ks[12], (DEPTH, D_MODEL, N_GROUPS), D_MODEL ** -0.5),
        "b_group": nrm(ks[13], (DEPTH, N_GROUPS), 0.01),
        "w_er": nrm(ks[14], (DEPTH, N_GROUPS, D_MODEL, EXPERTS_PER_GROUP), D_MODEL ** -0.5),
        "b_er": nrm(ks[15], (DEPTH, N_GROUPS, EXPERTS_PER_GROUP), 0.01),
        "w_gate": nrm(ks[16], (DEPTH, N_EXPERTS, D_MODEL, D_EXPERT), D_MODEL ** -0.5),
        "w_up": nrm(ks[17], (DEPTH, N_EXPERTS, D_MODEL, D_EXPERT), D_MODEL ** -0.5),
        "w_down": nrm(ks[18], (DEPTH, N_EXPERTS, D_EXPERT, D_MODEL), DEEPNORM_BETA * D_EXPERT ** -0.5),
        "w_ple_gate": nrm(ks[19], (DEPTH, D_MODEL, D_MODEL), D_MODEL ** -0.5),
        "b_ple_gate": nrm(ks[20], (DEPTH, D_MODEL), 0.01),
        "w_ple_proj": nrm(ks[21], (DEPTH, PLE_DIM, D_MODEL), DEEPNORM_BETA * PLE_DIM ** -0.5),
        "ln2_g": 1.0 + nrm(ks[22], (DEPTH, D_MODEL), 0.05),
        "ln2_b": nrm(ks[23], (DEPTH, D_MODEL), 0.01),
    }


def reference(x, p, positions, w_in, b_in, sinks, g_norm_a, g_norm_b, w_out, b_out,
              ln1_g, ln1_b, w_group, b_group, w_er, b_er, w_gate, w_up, w_down,
              w_ple_gate, b_ple_gate, w_ple_proj, ln2_g, ln2_b):
    B, S, _ = x.shape
    splits = [SWA_WIDTH,
              SWA_WIDTH + SWA_KV_WIDTH,
              SWA_WIDTH + 2 * SWA_KV_WIDTH,
              SWA_WIDTH + 2 * SWA_KV_WIDTH + SB_WIDTH,
              SWA_WIDTH + 2 * SWA_KV_WIDTH + 2 * SB_WIDTH]
    for i in range(DEPTH):
        proj = x @ w_in[i] + b_in[i]
        qa, ka, va, qb, kb, vb = jnp.split(proj, splits, axis=-1)
        qa = apply_rope(qa.reshape(B, S, SWA_HEADS, SWA_HEAD_DIM), positions)
        ka = apply_rope(ka.reshape(B, S, SWA_KV_HEADS, SWA_HEAD_DIM), positions)
        va = va.reshape(B, S, SWA_KV_HEADS, SWA_HEAD_DIM)
        o_a = sliding_window_sink_attention(qa, ka, va, sinks[i])
        o_b = stick_breaking_attention(qb.reshape(B, S, SB_HEADS, SB_HEAD_DIM),
                                       kb.reshape(B, S, SB_HEADS, SB_HEAD_DIM),
                                       vb.reshape(B, S, SB_HEADS, SB_HEAD_DIM))
        mixed = jnp.concatenate([rms_norm(o_a, g_norm_a[i]), rms_norm(o_b, g_norm_b[i])], axis=-1)
        mix_out = mixed @ w_out[i] + b_out[i]
        x = layer_norm(DEEPNORM_ALPHA * x + mix_out, ln1_g[i], ln1_b[i])
        moe_out = hierarchical_moe(x, w_group[i], b_group[i], w_er[i], b_er[i],
                                   w_gate[i], w_up[i], w_down[i])
        ple = jax.nn.sigmoid(x @ w_ple_gate[i] + b_ple_gate[i]) * (p[i] @ w_ple_proj[i])
        x = layer_norm(DEEPNORM_ALPHA * x + moe_out + ple, ln2_g[i], ln2_b[i])
    return x
```

```python
import math

import jax
import jax.numpy as jnp
from jax import lax
from jax.experimental import pallas as pl
from jax.experimental.pallas import tpu as pltpu

F32 = jnp.float32
BF16 = jnp.bfloat16
I32 = jnp.int32

D_MODEL = 4096
PLE_DIM = 256
BLOCK = 128
ROPE_THETA = 10000.0
LN_EPS = 1e-5
RMS_EPS = 1e-6
NEG_INF = -1e30

SWA_HEAD_DIM = 64
SWA_WIDTH = D_MODEL // 2
SWA_HEADS = SWA_WIDTH // SWA_HEAD_DIM
SWA_KV_HEADS = SWA_HEADS // 8
SWA_GROUP = SWA_HEADS // SWA_KV_HEADS
SWA_KV_WIDTH = SWA_KV_HEADS * SWA_HEAD_DIM

SB_HEAD_DIM = 128
SB_WIDTH = D_MODEL - SWA_WIDTH
SB_HEADS = SB_WIDTH // SB_HEAD_DIM

IN_WIDTH = SWA_WIDTH + 2 * SWA_KV_WIDTH + 3 * SB_WIDTH
K_A_COL = SWA_WIDTH
V_A_COL = K_A_COL + SWA_KV_WIDTH
Q_B_COL = V_A_COL + SWA_KV_WIDTH
K_B_COL = Q_B_COL + SB_WIDTH
V_B_COL = K_B_COL + SB_WIDTH

N_GROUPS = 4
EXPERTS_PER_GROUP = 8
N_EXPERTS = N_GROUPS * EXPERTS_PER_GROUP
DEPTH = 1
DEEPNORM_ALPHA = (2.0 * DEPTH) ** 0.25

LANES = 128
VMEM_LIMIT_BYTES = 56 * 1024 * 1024

EXP_ZERO_LOG = -104.0 - 2.0

TM_PROJ = 512
TN_PROJ = 512
TM_MOE = 256
ROW_CHUNK = 16
DMA_WINDOW = 256


def _cparams(sem):
    return pltpu.CompilerParams(dimension_semantics=sem,
                                vmem_limit_bytes=VMEM_LIMIT_BYTES)


def _pack_halves(lo, hi):
    return pltpu.pack_elementwise([lo, hi], packed_dtype=BF16)


def _unpack_halves(words):
    return tuple(
        pltpu.unpack_elementwise(words, index=k, packed_dtype=BF16,
                                 unpacked_dtype=F32).astype(BF16) for k in (0, 1))


def _dot_halves(lo, hi, w_ref):
    half = lo.shape[1]
    return (jnp.dot(lo, w_ref[:half], preferred_element_type=F32)
            + jnp.dot(hi, w_ref[half:], preferred_element_type=F32))


def _dot_nt(a, b):
    return lax.dot_general(a, b, (((1,), (1,)), ((), ())),
                           preferred_element_type=F32)


def _inproj_kernel(x_ref, w_ref, b_ref, o_ref, xb_ref):
    @pl.when(pl.program_id(1) == 0)
    def _():
        xb_ref[...] = x_ref[...].astype(BF16)

    acc = jnp.dot(xb_ref[...], w_ref[...], preferred_element_type=F32)
    o_ref[...] = (acc + b_ref[...]).astype(o_ref.dtype)


def _in_projection(x2, w_b, b):
    m, k = x2.shape
    n = w_b.shape[1]
    return pl.pallas_call(
        _inproj_kernel,
        out_shape=jax.ShapeDtypeStruct((m, n), BF16),
        grid=(m // TM_PROJ, n // TN_PROJ),
        in_specs=[
            pl.BlockSpec((TM_PROJ, k), lambda i, j: (i, 0)),
            pl.BlockSpec((k, TN_PROJ), lambda i, j: (0, j)),
            pl.BlockSpec((1, TN_PROJ), lambda i, j: (0, j)),
        ],
        out_specs=pl.BlockSpec((TM_PROJ, TN_PROJ), lambda i, j: (i, j)),
        scratch_shapes=[pltpu.VMEM((TM_PROJ, k), BF16)],
        compiler_params=_cparams(("arbitrary", "arbitrary")),
        name="in_projection",
    )(x2, w_b, b)


def _swa_kernel(sinks_ref, q_ref, kc_ref, kp_ref, vc_ref, vp_ref,
                posc_ref, posp_ref, invf_ref, o_ref):
    n = pl.program_id(1)
    lane = lax.broadcasted_iota(I32, (1, LANES), 1)
    first_half = (lane % SWA_HEAD_DIM) < (SWA_HEAD_DIM // 2)

    def tables(pos_ref):
        ang = pos_ref[...].astype(F32) * invf_ref[...]
        sin = jnp.sin(ang)
        return jnp.cos(ang), jnp.where(first_half, -sin, sin)

    def rope(x, cos, sin_signed):
        partner = jnp.where(first_half,
                            pltpu.roll(x, LANES - SWA_HEAD_DIM // 2, 1),
                            pltpu.roll(x, SWA_HEAD_DIM // 2, 1))
        return x * cos + partner * sin_signed

    cos_c, sin_c = tables(posc_ref)
    cos_p, sin_p = tables(posp_ref)

    def rope_block(ref, cos, sin_signed):
        width = ref.shape[1]
        return [rope(ref[:, c * LANES:(c + 1) * LANES].astype(F32), cos,
                     sin_signed).astype(BF16) for c in range(width // LANES)]

    q_chunks = rope_block(q_ref, cos_c, sin_c)
    k_chunks = [jnp.concatenate([p_, c_], axis=0)
                for p_, c_ in zip(rope_block(kp_ref, cos_p, sin_p),
                                  rope_block(kc_ref, cos_c, sin_c))]
    v_all = jnp.concatenate([vp_ref[...], vc_ref[...]], axis=0)

    qi = lax.broadcasted_iota(I32, (BLOCK, 2 * BLOCK), 0)
    kj = lax.broadcasted_iota(I32, (BLOCK, 2 * BLOCK), 1)
    rel = qi - (kj - BLOCK)
    valid = (rel >= 0) & (rel < BLOCK) & ((kj >= BLOCK) | (n > 0))

    def head_slice(chunks, head):
        half = head % 2
        return chunks[head // 2][:, half * SWA_HEAD_DIM:(half + 1) * SWA_HEAD_DIM]

    scale = 1.0 / math.sqrt(SWA_HEAD_DIM)
    for h in range(SWA_KV_HEADS):
        k_h = head_slice(k_chunks, h)
        v_h = v_all[:, h * SWA_HEAD_DIM:(h + 1) * SWA_HEAD_DIM]
        q_h = jnp.concatenate(
            [head_slice(q_chunks, h * SWA_GROUP + g) for g in range(SWA_GROUP)],
            axis=0)
        s = _dot_nt(q_h, k_h) * scale
        probs = []
        for g in range(SWA_GROUP):
            sink = sinks_ref[h * SWA_GROUP + g]
            s_g = jnp.where(valid, s[g * BLOCK:(g + 1) * BLOCK], NEG_INF)
            m = jnp.maximum(jnp.max(s_g, axis=-1, keepdims=True), sink)
            e = jnp.exp(s_g - m)
            den = jnp.sum(e, axis=-1, keepdims=True) + jnp.exp(sink - m)
            probs.append((e / den).astype(BF16))
        o_h = jnp.dot(jnp.concatenate(probs, axis=0), v_h,
                      preferred_element_type=F32)
        for g in range(SWA_GROUP):
            col = (h * SWA_GROUP + g) * SWA_HEAD_DIM
            o_ref[:, col:col + SWA_HEAD_DIM] = o_h[g * BLOCK:(g + 1) * BLOCK]


def _swa_attention(proj, pos2, inv_freq, sinks, batch, seq):
    nb = seq // BLOCK
    kcol = K_A_COL // SWA_KV_WIDTH
    vcol = V_A_COL // SWA_KV_WIDTH

    def cur(b, n):
        return b * nb + n

    def prev(b, n):
        return b * nb + jnp.maximum(n - 1, 0)

    return pl.pallas_call(
        _swa_kernel,
        out_shape=jax.ShapeDtypeStruct((batch * seq, SWA_WIDTH), F32),
        grid=(batch, nb),
        in_specs=[
            pl.BlockSpec(memory_space=pltpu.SMEM),
            pl.BlockSpec((BLOCK, SWA_WIDTH), lambda b, n: (cur(b, n), 0)),
            pl.BlockSpec((BLOCK, SWA_KV_WIDTH), lambda b, n: (cur(b, n), kcol)),
            pl.BlockSpec((BLOCK, SWA_KV_WIDTH), lambda b, n: (prev(b, n), kcol)),
            pl.BlockSpec((BLOCK, SWA_KV_WIDTH), lambda b, n: (cur(b, n), vcol)),
            pl.BlockSpec((BLOCK, SWA_KV_WIDTH), lambda b, n: (prev(b, n), vcol)),
            pl.BlockSpec((BLOCK, 1), lambda b, n: (cur(b, n), 0)),
            pl.BlockSpec((BLOCK, 1), lambda b, n: (prev(b, n), 0)),
            pl.BlockSpec((1, LANES), lambda b, n: (0, 0)),
        ],
        out_specs=pl.BlockSpec((BLOCK, SWA_WIDTH), lambda b, n: (cur(b, n), 0)),
        compiler_params=_cparams(("arbitrary", "arbitrary")),
        name="swa_attention",
    )(sinks, proj, proj, proj, proj, proj, pos2, pos2, inv_freq)


def _sb_kernel(q_ref, k_ref, v_ref, o_ref):
    n = pl.program_id(2)
    q = q_ref[...]
    scale = 1.0 / math.sqrt(SB_HEAD_DIM)
    row = lax.broadcasted_iota(I32, (BLOCK, BLOCK), 0)
    col = lax.broadcasted_iota(I32, (BLOCK, BLOCK), 1)
    later = (row > col).astype(BF16)
    causal = col < row

    def block(kb, carry, acc, mask):
        start = pl.multiple_of(kb * BLOCK, BLOCK)
        k = k_ref[pl.ds(start, BLOCK), :]
        v = v_ref[pl.ds(start, BLOCK), :]
        z = _dot_nt(q, k) * scale
        t = jnp.log1p(jnp.exp(-jnp.abs(z)))
        log_not = jnp.minimum(-z, 0.0) - t
        log_beta = jnp.minimum(z, 0.0) - t
        if mask is not None:
            log_not = jnp.where(mask, log_not, 0.0)
        hi = log_not.astype(BF16)
        r1 = log_not - hi.astype(F32)
        mid = r1.astype(BF16)
        lo = (r1 - mid.astype(F32)).astype(BF16)
        parts = jnp.dot(jnp.concatenate([hi, mid, lo], axis=0), later,
                        preferred_element_type=F32)
        suffix = (parts[:BLOCK] + parts[BLOCK:2 * BLOCK]) + parts[2 * BLOCK:]
        a = jnp.exp(log_beta + suffix + carry)
        if mask is not None:
            a = jnp.where(mask, a, 0.0)
        acc = acc + jnp.dot(a.astype(BF16), v, preferred_element_type=F32)
        carry = carry + jnp.sum(log_not, axis=-1, keepdims=True)
        return carry, acc

    carry0 = jnp.zeros((BLOCK, 1), F32)
    acc0 = jnp.zeros((BLOCK, SB_HEAD_DIM), F32)
    carry, acc = block(n, carry0, acc0, causal)

    def cond(state):
        kb, carry, _ = state
        return jnp.logical_and(kb >= 0, jnp.max(carry) > EXP_ZERO_LOG)

    def body(state):
        kb, carry, acc = state
        carry, acc = block(kb, carry, acc, None)
        return kb - 1, carry, acc

    _, _, acc = lax.while_loop(cond, body, (n - 1, carry, acc))
    o_ref[...] = acc


def _sb_attention(proj, batch, seq):
    nb = seq // BLOCK
    qcol = Q_B_COL // SB_HEAD_DIM
    kcol = K_B_COL // SB_HEAD_DIM
    vcol = V_B_COL // SB_HEAD_DIM
    return pl.pallas_call(
        _sb_kernel,
        out_shape=jax.ShapeDtypeStruct((batch * seq, SB_WIDTH), F32),
        grid=(batch, SB_HEADS, nb),
        in_specs=[
            pl.BlockSpec((BLOCK, SB_HEAD_DIM), lambda b, h, n: (b * nb + n, qcol + h)),
            pl.BlockSpec((seq, SB_HEAD_DIM), lambda b, h, n: (b, kcol + h)),
            pl.BlockSpec((seq, SB_HEAD_DIM), lambda b, h, n: (b, vcol + h)),
        ],
        out_specs=pl.BlockSpec((BLOCK, SB_HEAD_DIM), lambda b, h, n: (b * nb + n, h)),
        compiler_params=_cparams(("arbitrary", "arbitrary", "arbitrary")),
        name="sb_attention",
    )(proj, proj, proj)


def _row_stats(chunks_ref, n_chunks, width):
    total = chunks_ref[0].sum(axis=-1, keepdims=True)
    for c in range(1, n_chunks):
        total = total + chunks_ref[c].sum(axis=-1, keepdims=True)
    mu = total / width
    sq = jnp.square(chunks_ref[0] - mu).sum(axis=-1, keepdims=True)
    for c in range(1, n_chunks):
        sq = sq + jnp.square(chunks_ref[c] - mu).sum(axis=-1, keepdims=True)
    return mu, lax.rsqrt(sq / width + LN_EPS)


def _outproj_kernel(oa_ref, ob_ref, ga_ref, gb_ref, wa_ref, wb_ref, x_ref, bo_ref,
                    lg_ref, lb_ref, y_ref, mu_ref, rs_ref, x1p_ref,
                    ma_ref, mb_ref, acc_ref):
    j = pl.program_id(1)
    n_chunks = acc_ref.shape[0]
    tn = acc_ref.shape[2]

    @pl.when(j == 0)
    def _():
        for o_ref, g_ref, m_ref in ((oa_ref, ga_ref, ma_ref), (ob_ref, gb_ref, mb_ref)):
            width = o_ref.shape[1]
            cols = [slice(c * tn, (c + 1) * tn) for c in range(width // tn)]
            sq = sum(jnp.square(o_ref[:, sl]).sum(axis=-1, keepdims=True) for sl in cols)
            r = lax.rsqrt(sq / width + RMS_EPS)
            for sl in cols:
                m_ref[:, sl] = (o_ref[:, sl] * r * g_ref[:, sl]).astype(BF16)

    mix = (jnp.dot(ma_ref[...], wa_ref[...], preferred_element_type=F32)
           + jnp.dot(mb_ref[...], wb_ref[...], preferred_element_type=F32))
    y = DEEPNORM_ALPHA * x_ref[...] + (mix + bo_ref[...])
    y_ref[...] = y
    acc_ref[j] = y

    @pl.when(j == n_chunks - 1)
    def _():
        mu, rs = _row_stats(acc_ref, n_chunks, n_chunks * tn)
        mu_ref[...] = mu
        rs_ref[...] = rs

        def normed(c):
            sl = slice(c * tn, (c + 1) * tn)
            return (acc_ref[c] - mu) * rs * lg_ref[:, sl] + lb_ref[:, sl]

        for c in range(n_chunks // 2):
            x1p_ref[:, c * tn:(c + 1) * tn] = _pack_halves(
                normed(c), normed(c + n_chunks // 2))


def _out_projection(o_a, o_b, g_a, g_b, w_b, x2, b_out, ln_g, ln_b):
    m, d = x2.shape
    tm, tn = TM_PROJ, TN_PROJ
    half = o_a.shape[1]
    once = dict(pipeline_mode=pl.Buffered(1))
    return pl.pallas_call(
        _outproj_kernel,
        out_shape=(jax.ShapeDtypeStruct((m, d), F32),
                   jax.ShapeDtypeStruct((m, 1), F32),
                   jax.ShapeDtypeStruct((m, 1), F32),
                   jax.ShapeDtypeStruct((m, d // 2), I32)),
        grid=(m // tm, d // tn),
        in_specs=[
            pl.BlockSpec((tm, half), lambda i, j: (i, 0), **once),
            pl.BlockSpec((tm, half), lambda i, j: (i, 0), **once),
            pl.BlockSpec((1, half), lambda i, j: (0, 0)),
            pl.BlockSpec((1, half), lambda i, j: (0, 0)),
            pl.BlockSpec((half, tn), lambda i, j: (0, j)),
            pl.BlockSpec((half, tn), lambda i, j: (1, j)),
            pl.BlockSpec((tm, tn), lambda i, j: (i, j)),
            pl.BlockSpec((1, tn), lambda i, j: (0, j)),
            pl.BlockSpec((1, d), lambda i, j: (0, 0)),
            pl.BlockSpec((1, d), lambda i, j: (0, 0)),
        ],
        out_specs=(pl.BlockSpec((tm, tn), lambda i, j: (i, j)),
                   pl.BlockSpec((tm, 1), lambda i, j: (i, 0)),
                   pl.BlockSpec((tm, 1), lambda i, j: (i, 0)),
                   pl.BlockSpec((tm, d // 2), lambda i, j: (i, 0))),
        scratch_shapes=[pltpu.VMEM((tm, half), BF16), pltpu.VMEM((tm, half), BF16),
                        pltpu.VMEM((d // tn, tm, tn), F32)],
        compiler_params=_cparams(("arbitrary", "arbitrary")),
        name="out_projection_ln1",
    )(o_a, o_b, g_a, g_b, w_b, w_b, x2, b_out, ln_g, ln_b)


ROUTER_COLS = LANES
EXPERT_LANE0 = N_GROUPS


def _router_kernel(x_ref, w_ref, b_ref, ids_ref, wts_ref, cnt_ref, carry_ref):
    i = pl.program_id(0)
    tm = x_ref.shape[0]

    @pl.when(i == 0)
    def _():
        carry_ref[...] = jnp.zeros_like(carry_ref)

    dh = x_ref.shape[1]
    logits = b_ref[...]
    for c in range(dh // TN_PROJ):
        sl = slice(c * TN_PROJ, (c + 1) * TN_PROJ)
        lo_x, hi_x = _unpack_halves(x_ref[:, sl])
        logits = logits + (
            jnp.dot(lo_x, w_ref[sl], preferred_element_type=F32)
            + jnp.dot(hi_x, w_ref[dh + c * TN_PROJ:dh + (c + 1) * TN_PROJ],
                      preferred_element_type=F32))
    lane = lax.broadcasted_iota(I32, (tm, ROUTER_COLS), 1)
    big = jnp.int32(ROUTER_COLS)

    def first_argmax(vals):
        top = jnp.max(vals, axis=-1, keepdims=True)
        idx = jnp.min(jnp.where(vals == top, lane, big), axis=-1, keepdims=True)
        return top, idx

    is_group = lane < N_GROUPS
    g_logits = jnp.where(is_group, logits, -jnp.inf)
    g_top, g_idx = first_argmax(g_logits)
    g_w = 1.0 / jnp.sum(jnp.exp(g_logits - g_top), axis=-1, keepdims=True)

    first = EXPERT_LANE0 + g_idx * EXPERTS_PER_GROUP
    in_group = (lane >= first) & (lane < first + EXPERTS_PER_GROUP)
    e_logits = jnp.where(in_group, logits, -jnp.inf)
    top1, idx1 = first_argmax(e_logits)
    top2, idx2 = first_argmax(jnp.where(lane == idx1, -jnp.inf, e_logits))
    e2 = jnp.exp(top2 - top1)
    w1 = g_w / (1.0 + e2)
    w2 = g_w * e2 / (1.0 + e2)

    hit1 = lane == idx1
    hit2 = lane == idx2
    onehot = (hit1 | hit2).astype(BF16)
    r = lax.broadcasted_iota(I32, (tm, tm), 0)
    c = lax.broadcasted_iota(I32, (tm, tm), 1)
    before = (c < r).astype(BF16)
    prior = jnp.dot(before, onehot, preferred_element_type=F32) + carry_ref[0:1, :]
    rank1 = jnp.sum(jnp.where(hit1, prior, 0.0), axis=-1, keepdims=True)
    rank2 = jnp.sum(jnp.where(hit2, prior, 0.0), axis=-1, keepdims=True)
    counts = carry_ref[0:1, :] + jnp.sum(onehot.astype(F32), axis=0, keepdims=True)
    carry_ref[...] = jnp.broadcast_to(counts, carry_ref.shape)
    cnt_ref[...] = jnp.broadcast_to(counts, cnt_ref.shape).astype(I32)

    ids = jnp.where(lane == 0, idx1 - EXPERT_LANE0,
          jnp.where(lane == 1, idx2 - EXPERT_LANE0,
          jnp.where(lane == 2, rank1.astype(I32),
          jnp.where(lane == 3, rank2.astype(I32), 0))))
    ids_ref[...] = ids
    wts_ref[...] = jnp.where(lane == 0, w1, jnp.where(lane == 1, w2, 0.0))


def _router(x1p, w_r, b_r):
    m, dh = x1p.shape
    tm = TM_PROJ
    return pl.pallas_call(
        _router_kernel,
        out_shape=(jax.ShapeDtypeStruct((m, ROUTER_COLS), I32),
                   jax.ShapeDtypeStruct((m, ROUTER_COLS), F32),
                   jax.ShapeDtypeStruct((8, ROUTER_COLS), I32)),
        grid=(m // tm,),
        in_specs=[
            pl.BlockSpec((tm, dh), lambda i: (i, 0)),
            pl.BlockSpec((2 * dh, ROUTER_COLS), lambda i: (0, 0)),
            pl.BlockSpec((1, ROUTER_COLS), lambda i: (0, 0)),
        ],
        out_specs=(pl.BlockSpec((tm, ROUTER_COLS), lambda i: (i, 0)),
                   pl.BlockSpec((tm, ROUTER_COLS), lambda i: (i, 0)),
                   pl.BlockSpec((8, ROUTER_COLS), lambda i: (0, 0))),
        scratch_shapes=[pltpu.VMEM((8, ROUTER_COLS), F32)],
        compiler_params=_cparams(("arbitrary",)),
        name="router",
    )(x1p, w_r, b_r)


def _gather_rows(idx_ref, n_rows, src_ref, dst_ref, sem):
    def copy(r):
        return pltpu.make_async_copy(src_ref.at[pl.ds(idx_ref[r], 1)],
                                     dst_ref.at[pl.ds(r, 1)], sem)

    def start_chunk(c):
        for u in range(ROW_CHUNK):
            copy(c * ROW_CHUNK + u).start()

    def wait_chunk():
        for _ in range(ROW_CHUNK):
            pltpu.make_async_copy(src_ref.at[pl.ds(0, 1)], dst_ref.at[pl.ds(0, 1)],
                                  sem).wait()

    n_chunks = n_rows // ROW_CHUNK
    head = jnp.minimum(n_chunks, DMA_WINDOW // ROW_CHUNK)

    def prime(c, _):
        start_chunk(c)
        return 0

    lax.fori_loop(0, head, prime, 0)

    def steady(c, _):
        wait_chunk()
        start_chunk(c)
        return 0

    lax.fori_loop(head, n_chunks, steady, 0)

    def drain(c, _):
        wait_chunk()
        return 0

    lax.fori_loop(0, head, drain, 0)


def _sort_gather_kernel(tos_ref, nrows_ref, src_ref, dst_ref, sem, tail_sem):
    n_rows = nrows_ref[0]
    p_rows = dst_ref.shape[0]

    def tail_copy(t):
        return pltpu.make_async_copy(
            src_ref.at[pl.ds(0, TM_MOE)],
            dst_ref.at[pl.ds(pl.multiple_of(t * TM_MOE, TM_MOE), TM_MOE)], tail_sem)

    def tail_start(t, _):
        tail_copy(t).start()
        return 0

    lax.fori_loop(n_rows // TM_MOE, p_rows // TM_MOE, tail_start, 0)
    _gather_rows(tos_ref, n_rows, src_ref, dst_ref, sem)

    def tail_wait(t, _):
        tail_copy(t).wait()
        return 0

    lax.fori_loop(n_rows // TM_MOE, p_rows // TM_MOE, tail_wait, 0)


def _gather_to_sorted(token_of_slot, n_rows, x1p3):
    p_rows = token_of_slot.shape[0]
    return pl.pallas_call(
        _sort_gather_kernel,
        out_shape=jax.ShapeDtypeStruct((p_rows,) + x1p3.shape[1:], x1p3.dtype),
        in_specs=[pl.BlockSpec(memory_space=pltpu.SMEM),
                  pl.BlockSpec(memory_space=pltpu.SMEM),
                  pl.BlockSpec(memory_space=pl.ANY)],
        out_specs=pl.BlockSpec(memory_space=pl.ANY),
        scratch_shapes=[pltpu.SemaphoreType.DMA(()), pltpu.SemaphoreType.DMA(())],
        name="gather_to_sorted",
    )(token_of_slot, n_rows, x1p3)


def _unsort_gather_kernel(slot_ref, src_ref, dst_ref, sem):
    _gather_rows(slot_ref, slot_ref.shape[0], src_ref, dst_ref, sem)


def _gather_to_tokens(slot_kt, y_sorted):
    return pl.pallas_call(
        _unsort_gather_kernel,
        out_shape=jax.ShapeDtypeStruct((slot_kt.shape[0],) + y_sorted.shape[1:],
                                       y_sorted.dtype),
        in_specs=[pl.BlockSpec(memory_space=pltpu.SMEM),
                  pl.BlockSpec(memory_space=pl.ANY)],
        out_specs=pl.BlockSpec(memory_space=pl.ANY),
        scratch_shapes=[pltpu.SemaphoreType.DMA(())],
        name="gather_to_tokens",
    )(slot_kt, y_sorted)


def _expert_changed(te_ref, i):
    return jnp.logical_or(i == 0, te_ref[i] != te_ref[jnp.maximum(i - 1, 0)])


def _tile_row(i, nu):
    return jnp.minimum(i, nu[0] - 1)


def _gateup_kernel(te_ref, nu_ref, x_ref, wg_ref, wu_ref, a_ref, wgb_ref, wub_ref):
    i = pl.program_id(0)

    @pl.when(_expert_changed(te_ref, i))
    def _():
        wgb_ref[...] = wg_ref[0].astype(BF16)
        wub_ref[...] = wu_ref[0].astype(BF16)

    @pl.when(i < nu_ref[0])
    def _():
        lo, hi = _unpack_halves(x_ref[:, 0, :])
        gate = _dot_halves(lo, hi, wgb_ref)
        up = _dot_halves(lo, hi, wub_ref)
        a_ref[...] = (gate * jax.nn.sigmoid(gate) * up).astype(a_ref.dtype)

    @pl.when(i >= nu_ref[0])
    def _():
        a_ref[...] = jnp.zeros_like(a_ref)


def _grouped_gate_up(tile_expert, n_used, x_sorted, w_gate, w_up):
    p_rows, _, dh = x_sorted.shape
    _, d, f = w_gate.shape
    return pl.pallas_call(
        _gateup_kernel,
        out_shape=jax.ShapeDtypeStruct((p_rows, f), BF16),
        grid_spec=pltpu.PrefetchScalarGridSpec(
            num_scalar_prefetch=2,
            grid=(p_rows // TM_MOE,),
            in_specs=[
                pl.BlockSpec((TM_MOE, 1, dh), lambda i, te, nu: (_tile_row(i, nu), 0, 0)),
                pl.BlockSpec((1, d, f), lambda i, te, nu: (te[i], 0, 0)),
                pl.BlockSpec((1, d, f), lambda i, te, nu: (te[i], 0, 0)),
            ],
            out_specs=pl.BlockSpec((TM_MOE, f), lambda i, te, nu: (i, 0)),
            scratch_shapes=[pltpu.VMEM((d, f), BF16), pltpu.VMEM((d, f), BF16)],
        ),
        compiler_params=_cparams(("arbitrary",)),
        name="moe_gate_up",
    )(tile_expert, n_used, x_sorted, w_gate, w_up)


def _down_kernel(te_ref, nu_ref, a_ref, wd_ref, y_ref, wdb_ref):
    i = pl.program_id(0)

    @pl.when(_expert_changed(te_ref, i))
    def _():
        wdb_ref[...] = wd_ref[0].astype(BF16)

    @pl.when(i < nu_ref[0])
    def _():
        y_ref[:, 0, :] = jnp.dot(a_ref[...], wdb_ref[...], preferred_element_type=F32)

    @pl.when(i >= nu_ref[0])
    def _():
        y_ref[...] = jnp.zeros_like(y_ref)


def _grouped_down(tile_expert, n_used, act, w_down):
    p_rows, f = act.shape
    d = w_down.shape[2]
    return pl.pallas_call(
        _down_kernel,
        out_shape=jax.ShapeDtypeStruct((p_rows, 1, d), F32),
        grid_spec=pltpu.PrefetchScalarGridSpec(
            num_scalar_prefetch=2,
            grid=(p_rows // TM_MOE,),
            in_specs=[
                pl.BlockSpec((TM_MOE, f), lambda i, te, nu: (_tile_row(i, nu), 0)),
                pl.BlockSpec((1, f, d), lambda i, te, nu: (te[i], 0, 0)),
            ],
            out_specs=pl.BlockSpec((TM_MOE, 1, d), lambda i, te, nu: (i, 0, 0)),
            scratch_shapes=[pltpu.VMEM((f, d), BF16)],
        ),
        compiler_params=_cparams(("arbitrary",)),
        name="moe_down",
    )(tile_expert, n_used, act, w_down)


def _final_kernel(x1p_ref, wg_ref, bg_ref, p_ref, wp_ref, y1_ref, mu_ref, rs_ref,
                  l1g_ref, l1b_ref, m0_ref, m1_ref, wts_ref, l2g_ref, l2b_ref,
                  o_ref, lo_ref, hi_ref, acc_ref, mu2_ref, rs2_ref):
    j = pl.program_id(1)
    n_chunks = acc_ref.shape[0]
    tn = acc_ref.shape[2]

    @pl.when(j == 0)
    def _():
        for c in range(lo_ref.shape[1] // tn):
            sl = slice(c * tn, (c + 1) * tn)
            lo_ref[:, sl], hi_ref[:, sl] = _unpack_halves(x1p_ref[:, sl])

    @pl.when(j < n_chunks)
    def _():
        gate = _dot_halves(lo_ref[...], hi_ref[...], wg_ref) + bg_ref[...]
        emb = jnp.dot(p_ref[...].astype(BF16), wp_ref[...].astype(BF16),
                      preferred_element_type=F32)
        x1 = (y1_ref[...] - mu_ref[...]) * rs_ref[...] * l1g_ref[...] + l1b_ref[...]
        moe = wts_ref[:, 0:1] * m0_ref[:, 0, :] + wts_ref[:, 1:2] * m1_ref[:, 0, :]
        acc_ref[j] = DEEPNORM_ALPHA * x1 + moe + jax.nn.sigmoid(gate) * emb

    @pl.when(j == n_chunks)
    def _():
        mu2_ref[...], rs2_ref[...] = _row_stats(acc_ref, n_chunks, n_chunks * tn)

    @pl.when(j >= n_chunks)
    def _():
        o_ref[...] = ((acc_ref[j - n_chunks] - mu2_ref[...]) * rs2_ref[...]
                      * l2g_ref[...] + l2b_ref[...])


def _final_stage(x1p, w_pg_b, b_pg, p2, w_pp, y1, mu1, rs1, ln1_g, ln1_b,
                 moe_kt, wts, ln2_g, ln2_b):
    m, d = y1.shape
    tm, tn = TM_PROJ, TN_PROJ
    ple = p2.shape[1]
    n_chunks = d // tn
    second = m // tm

    def build(j):
        return jnp.minimum(j, n_chunks - 1)

    def emit(j):
        return jnp.maximum(j - n_chunks, 0)

    return pl.pallas_call(
        _final_kernel,
        out_shape=jax.ShapeDtypeStruct((m, d), F32),
        grid=(m // tm, 2 * n_chunks),
        in_specs=[
            pl.BlockSpec((tm, d // 2), lambda i, j: (i, 0),
                         pipeline_mode=pl.Buffered(1)),
            pl.BlockSpec((d, tn), lambda i, j: (0, build(j))),
            pl.BlockSpec((1, tn), lambda i, j: (0, build(j))),
            pl.BlockSpec((tm, ple), lambda i, j: (i, 0)),
            pl.BlockSpec((ple, tn), lambda i, j: (0, build(j))),
            pl.BlockSpec((tm, tn), lambda i, j: (i, build(j))),
            pl.BlockSpec((tm, 1), lambda i, j: (i, 0)),
            pl.BlockSpec((tm, 1), lambda i, j: (i, 0)),
            pl.BlockSpec((1, tn), lambda i, j: (0, build(j))),
            pl.BlockSpec((1, tn), lambda i, j: (0, build(j))),
            pl.BlockSpec((tm, 1, tn), lambda i, j: (i, 0, build(j))),
            pl.BlockSpec((tm, 1, tn), lambda i, j: (second + i, 0, build(j))),
            pl.BlockSpec((tm, ROUTER_COLS), lambda i, j: (i, 0)),
            pl.BlockSpec((1, tn), lambda i, j: (0, emit(j))),
            pl.BlockSpec((1, tn), lambda i, j: (0, emit(j))),
        ],
        out_specs=pl.BlockSpec((tm, tn), lambda i, j: (i, emit(j))),
        scratch_shapes=[pltpu.VMEM((tm, d // 2), BF16), pltpu.VMEM((tm, d // 2), BF16),
                        pltpu.VMEM((n_chunks, tm, tn), F32),
                        pltpu.VMEM((tm, 1), F32), pltpu.VMEM((tm, 1), F32)],
        compiler_params=_cparams(("arbitrary", "arbitrary")),
        name="ple_moe_ln2",
    )(x1p, w_pg_b, b_pg, p2, w_pp, y1, mu1, rs1, ln1_g, ln1_b,
      moe_kt, moe_kt, wts, ln2_g, ln2_b)


def _routing_tables(ids, counts_row):
    counts = counts_row[EXPERT_LANE0:EXPERT_LANE0 + N_EXPERTS]
    tiles = (counts + TM_MOE - 1) // TM_MOE
    tile_end = jnp.cumsum(tiles)
    offsets = (tile_end - tiles) * TM_MOE
    n_used = tile_end[-1:]
    experts = ids[:, 0:2]
    slots = offsets[experts] + ids[:, 2:4]
    n_tiles = (2 * ids.shape[0]) // TM_MOE + N_EXPERTS
    tile_ids = jnp.arange(n_tiles, dtype=I32)
    tile_expert = jnp.searchsorted(tile_end, jnp.minimum(tile_ids, n_used - 1),
                                   side="right").astype(I32)
    token_of_slot = jnp.zeros((n_tiles * TM_MOE,), I32).at[slots.reshape(-1)].set(
        jnp.arange(slots.size, dtype=I32) // slots.shape[1], unique_indices=True)
    return slots.astype(I32), token_of_slot, tile_expert, n_used.astype(I32)


def kernel(x, p, positions, w_in, b_in, sinks, g_norm_a, g_norm_b, w_out, b_out,
           ln1_g, ln1_b, w_group, b_group, w_er, b_er, w_gate, w_up, w_down,
           w_ple_gate, b_ple_gate, w_ple_proj, ln2_g, ln2_b):
    batch, seq, d = x.shape
    m = batch * seq
    row = lambda v: v.reshape(1, -1)
    x2 = x.reshape(m, d)
    for i in range(DEPTH):
        proj = _in_projection(x2, w_in[i].astype(BF16), row(b_in[i]))
        inv_freq = ROPE_THETA ** (-jnp.arange(0, SWA_HEAD_DIM, 2, dtype=F32) / SWA_HEAD_DIM)
        inv_freq = jnp.tile(inv_freq, LANES // inv_freq.shape[0]).reshape(1, LANES)
        o_a = _swa_attention(proj, positions.reshape(m, 1), inv_freq, sinks[i], batch, seq)
        o_b = _sb_attention(proj, batch, seq)
        y1, mu1, rs1, x1p = _out_projection(
            o_a, o_b, row(g_norm_a[i]), row(g_norm_b[i]), w_out[i].astype(BF16), x2,
            row(b_out[i]), row(ln1_g[i]), row(ln1_b[i]))
        pad = ROUTER_COLS - N_GROUPS - N_EXPERTS
        w_r = jnp.concatenate(
            [w_group[i], w_er[i].transpose(1, 0, 2).reshape(d, N_EXPERTS),
             jnp.zeros((d, pad), F32)], axis=1).astype(BF16)
        b_r = jnp.concatenate([b_group[i], b_er[i].reshape(-1), jnp.zeros((pad,), F32)])
        ids, wts, counts = _router(x1p, w_r, row(b_r))
        slots, token_of_slot, tile_expert, n_used = _routing_tables(ids, counts[0])
        x_sorted = _gather_to_sorted(token_of_slot, n_used * TM_MOE,
                                     x1p.reshape(m, 1, d // 2))
        act = _grouped_gate_up(tile_expert, n_used, x_sorted, w_gate[i], w_up[i])
        y_sorted = _grouped_down(tile_expert, n_used, act, w_down[i])
        moe_kt = _gather_to_tokens(slots.T.reshape(-1), y_sorted)
        x2 = _final_stage(x1p, w_ple_gate[i].astype(BF16), row(b_ple_gate[i]),
                          p[i].reshape(m, PLE_DIM), w_ple_proj[i], y1, mu1, rs1,
                          row(ln1_g[i]), row(ln1_b[i]), moe_kt, wts,
                          row(ln2_g[i]), row(ln2_b[i]))
    return x2.reshape(batch, seq, d)
```

```python
import math

import jax
import jax.numpy as jnp
from jax import lax
from jax.experimental import pallas as pl
from jax.experimental.pallas import tpu as pltpu

F32 = jnp.float32
BF16 = jnp.bfloat16
I32 = jnp.int32

D_MODEL = 4096
PLE_DIM = 256
BLOCK = 128
ROPE_THETA = 10000.0
LN_EPS = 1e-5
RMS_EPS = 1e-6
NEG_INF = -1e30

SWA_HEAD_DIM = 64
SWA_WIDTH = D_MODEL // 2
SWA_HEADS = SWA_WIDTH // SWA_HEAD_DIM
SWA_KV_HEADS = SWA_HEADS // 8
SWA_GROUP = SWA_HEADS // SWA_KV_HEADS
SWA_KV_WIDTH = SWA_KV_HEADS * SWA_HEAD_DIM

SB_HEAD_DIM = 128
SB_WIDTH = D_MODEL - SWA_WIDTH
SB_HEADS = SB_WIDTH // SB_HEAD_DIM

IN_WIDTH = SWA_WIDTH + 2 * SWA_KV_WIDTH + 3 * SB_WIDTH
K_A_COL = SWA_WIDTH
V_A_COL = K_A_COL + SWA_KV_WIDTH
Q_B_COL = V_A_COL + SWA_KV_WIDTH
K_B_COL = Q_B_COL + SB_WIDTH
V_B_COL = K_B_COL + SB_WIDTH

N_GROUPS = 4
EXPERTS_PER_GROUP = 8
N_EXPERTS = N_GROUPS * EXPERTS_PER_GROUP
EXPERT_TOP_K = 2
DEPTH = 1
DEEPNORM_ALPHA = (2.0 * DEPTH) ** 0.25

LANES = 128
VMEM_LIMIT_BYTES = 56 * 1024 * 1024

EXP_ZERO_LOG = -104.0 - 2.0

TM_PROJ = 512
TN_PROJ = 512
TM_MOE = 256
TM_COMBINE = 128
ROW_CHUNK = 16


def _cparams(sem):
    return pltpu.CompilerParams(dimension_semantics=sem,
                                vmem_limit_bytes=VMEM_LIMIT_BYTES)


def _pack_halves(lo, hi):
    return pltpu.pack_elementwise([lo, hi], packed_dtype=BF16)


def _unpack_halves(words):
    return tuple(
        pltpu.unpack_elementwise(words, index=k, packed_dtype=BF16,
                                 unpacked_dtype=F32).astype(BF16) for k in (0, 1))


def _dot_halves(lo, hi, w_ref):
    half = lo.shape[1]
    return (jnp.dot(lo, w_ref[:half], preferred_element_type=F32)
            + jnp.dot(hi, w_ref[half:], preferred_element_type=F32))


def _dot_nt(a, b):
    return lax.dot_general(a, b, (((1,), (1,)), ((), ())),
                           preferred_element_type=F32)


def _inproj_kernel(x_ref, w_ref, b_ref, o_ref, xb_ref):
    @pl.when(pl.program_id(1) == 0)
    def _():
        xb_ref[...] = x_ref[...].astype(BF16)

    acc = jnp.dot(xb_ref[...], w_ref[...], preferred_element_type=F32)
    o_ref[...] = (acc + b_ref[...]).astype(o_ref.dtype)


def _in_projection(x2, w_b, b):
    m, k = x2.shape
    n = w_b.shape[1]
    return pl.pallas_call(
        _inproj_kernel,
        out_shape=jax.ShapeDtypeStruct((m, n), BF16),
        grid=(m // TM_PROJ, n // TN_PROJ),
        in_specs=[
            pl.BlockSpec((TM_PROJ, k), lambda i, j: (i, 0)),
            pl.BlockSpec((k, TN_PROJ), lambda i, j: (0, j)),
            pl.BlockSpec((1, TN_PROJ), lambda i, j: (0, j)),
        ],
        out_specs=pl.BlockSpec((TM_PROJ, TN_PROJ), lambda i, j: (i, j)),
        scratch_shapes=[pltpu.VMEM((TM_PROJ, k), BF16)],
        compiler_params=_cparams(("arbitrary", "arbitrary")),
        name="in_projection",
    )(x2, w_b, b)


def _swa_kernel(sinks_ref, q_ref, kc_ref, kp_ref, vc_ref, vp_ref,
                posc_ref, posp_ref, invf_ref, o_ref):
    n = pl.program_id(1)
    lane = lax.broadcasted_iota(I32, (1, LANES), 1)
    first_half = (lane % SWA_HEAD_DIM) < (SWA_HEAD_DIM // 2)

    def tables(pos_ref):
        ang = pos_ref[...].astype(F32) * invf_ref[...]
        sin = jnp.sin(ang)
        return jnp.cos(ang), jnp.where(first_half, -sin, sin)

    def rope(x, cos, sin_signed):
        partner = jnp.where(first_half,
                            pltpu.roll(x, LANES - SWA_HEAD_DIM // 2, 1),
                            pltpu.roll(x, SWA_HEAD_DIM // 2, 1))
        return x * cos + partner * sin_signed

    cos_c, sin_c = tables(posc_ref)
    cos_p, sin_p = tables(posp_ref)

    def rope_block(ref, cos, sin_signed):
        width = ref.shape[1]
        return [rope(ref[:, c * LANES:(c + 1) * LANES].astype(F32), cos,
                     sin_signed).astype(BF16) for c in range(width // LANES)]

    q_chunks = rope_block(q_ref, cos_c, sin_c)
    k_chunks = [jnp.concatenate([p_, c_], axis=0)
                for p_, c_ in zip(rope_block(kp_ref, cos_p, sin_p),
                                  rope_block(kc_ref, cos_c, sin_c))]
    v_all = jnp.concatenate([vp_ref[...], vc_ref[...]], axis=0)

    qi = lax.broadcasted_iota(I32, (BLOCK, 2 * BLOCK), 0)
    kj = lax.broadcasted_iota(I32, (BLOCK, 2 * BLOCK), 1)
    rel = qi - (kj - BLOCK)
    valid = (rel >= 0) & (rel < BLOCK) & ((kj >= BLOCK) | (n > 0))

    def head_slice(chunks, head):
        half = head % 2
        return chunks[head // 2][:, half * SWA_HEAD_DIM:(half + 1) * SWA_HEAD_DIM]

    scale = 1.0 / math.sqrt(SWA_HEAD_DIM)
    for h in range(SWA_KV_HEADS):
        k_h = head_slice(k_chunks, h)
        v_h = v_all[:, h * SWA_HEAD_DIM:(h + 1) * SWA_HEAD_DIM]
        q_h = jnp.concatenate(
            [head_slice(q_chunks, h * SWA_GROUP + g) for g in range(SWA_GROUP)],
            axis=0)
        s = _dot_nt(q_h, k_h) * scale
        probs = []
        for g in range(SWA_GROUP):
            sink = sinks_ref[h * SWA_GROUP + g]
            s_g = jnp.where(valid, s[g * BLOCK:(g + 1) * BLOCK], NEG_INF)
            m = jnp.maximum(jnp.max(s_g, axis=-1, keepdims=True), sink)
            e = jnp.exp(s_g - m)
            den = jnp.sum(e, axis=-1, keepdims=True) + jnp.exp(sink - m)
            probs.append((e / den).astype(BF16))
        o_h = jnp.dot(jnp.concatenate(probs, axis=0), v_h,
                      preferred_element_type=F32)
        for g in range(SWA_GROUP):
            col = (h * SWA_GROUP + g) * SWA_HEAD_DIM
            o_ref[:, col:col + SWA_HEAD_DIM] = o_h[g * BLOCK:(g + 1) * BLOCK]


def _swa_attention(proj, pos2, inv_freq, sinks, batch, seq):
    nb = seq // BLOCK
    kcol = K_A_COL // SWA_KV_WIDTH
    vcol = V_A_COL // SWA_KV_WIDTH

    def cur(b, n):
        return b * nb + n

    def prev(b, n):
        return b * nb + jnp.maximum(n - 1, 0)

    return pl.pallas_call(
        _swa_kernel,
        out_shape=jax.ShapeDtypeStruct((batch * seq, SWA_WIDTH), F32),
        grid=(batch, nb),
        in_specs=[
            pl.BlockSpec(memory_space=pltpu.SMEM),
            pl.BlockSpec((BLOCK, SWA_WIDTH), lambda b, n: (cur(b, n), 0)),
            pl.BlockSpec((BLOCK, SWA_KV_WIDTH), lambda b, n: (cur(b, n), kcol)),
            pl.BlockSpec((BLOCK, SWA_KV_WIDTH), lambda b, n: (prev(b, n), kcol)),
            pl.BlockSpec((BLOCK, SWA_KV_WIDTH), lambda b, n: (cur(b, n), vcol)),
            pl.BlockSpec((BLOCK, SWA_KV_WIDTH), lambda b, n: (prev(b, n), vcol)),
            pl.BlockSpec((BLOCK, 1), lambda b, n: (cur(b, n), 0)),
            pl.BlockSpec((BLOCK, 1), lambda b, n: (prev(b, n), 0)),
            pl.BlockSpec((1, LANES), lambda b, n: (0, 0)),
        ],
        out_specs=pl.BlockSpec((BLOCK, SWA_WIDTH), lambda b, n: (cur(b, n), 0)),
        compiler_params=_cparams(("arbitrary", "arbitrary")),
        name="swa_attention",
    )(sinks, proj, proj, proj, proj, proj, pos2, pos2, inv_freq)


def _sb_kernel(q_ref, k_ref, v_ref, o_ref):
    n = pl.program_id(2)
    q = q_ref[...]
    scale = 1.0 / math.sqrt(SB_HEAD_DIM)
    row = lax.broadcasted_iota(I32, (BLOCK, BLOCK), 0)
    col = lax.broadcasted_iota(I32, (BLOCK, BLOCK), 1)
    later = (row > col).astype(BF16)
    causal = col < row

    def block(kb, carry, acc, mask):
        start = pl.multiple_of(kb * BLOCK, BLOCK)
        k = k_ref[pl.ds(start, BLOCK), :]
        v = v_ref[pl.ds(start, BLOCK), :]
        z = _dot_nt(q, k) * scale
        t = jnp.log1p(jnp.exp(-jnp.abs(z)))
        log_not = jnp.minimum(-z, 0.0) - t
        log_beta = jnp.minimum(z, 0.0) - t
        if mask is not None:
            log_not = jnp.where(mask, log_not, 0.0)
        hi = log_not.astype(BF16)
        r1 = log_not - hi.astype(F32)
        mid = r1.astype(BF16)
        lo = (r1 - mid.astype(F32)).astype(BF16)
        parts = jnp.dot(jnp.concatenate([hi, mid, lo], axis=0), later,
                        preferred_element_type=F32)
        suffix = (parts[:BLOCK] + parts[BLOCK:2 * BLOCK]) + parts[2 * BLOCK:]
        a = jnp.exp(log_beta + suffix + carry)
        if mask is not None:
            a = jnp.where(mask, a, 0.0)
        acc = acc + jnp.dot(a.astype(BF16), v, preferred_element_type=F32)
        carry = carry + jnp.sum(log_not, axis=-1, keepdims=True)
        return carry, acc

    carry0 = jnp.zeros((BLOCK, 1), F32)
    acc0 = jnp.zeros((BLOCK, SB_HEAD_DIM), F32)
    carry, acc = block(n, carry0, acc0, causal)

    def cond(state):
        kb, carry, _ = state
        return jnp.logical_and(kb >= 0, jnp.max(carry) > EXP_ZERO_LOG)

    def body(state):
        kb, carry, acc = state
        carry, acc = block(kb, carry, acc, None)
        return kb - 1, carry, acc

    _, _, acc = lax.while_loop(cond, body, (n - 1, carry, acc))
    o_ref[...] = acc


def _sb_attention(proj, batch, seq):
    nb = seq // BLOCK
    qcol = Q_B_COL // SB_HEAD_DIM
    kcol = K_B_COL // SB_HEAD_DIM
    vcol = V_B_COL // SB_HEAD_DIM
    return pl.pallas_call(
        _sb_kernel,
        out_shape=jax.ShapeDtypeStruct((batch * seq, SB_WIDTH), F32),
        grid=(batch, SB_HEADS, nb),
        in_specs=[
            pl.BlockSpec((BLOCK, SB_HEAD_DIM), lambda b, h, n: (b * nb + n, qcol + h)),
            pl.BlockSpec((seq, SB_HEAD_DIM), lambda b, h, n: (b, kcol + h)),
            pl.BlockSpec((seq, SB_HEAD_DIM), lambda b, h, n: (b, vcol + h)),
        ],
        out_specs=pl.BlockSpec((BLOCK, SB_HEAD_DIM), lambda b, h, n: (b * nb + n, h)),
        compiler_params=_cparams(("arbitrary", "arbitrary", "arbitrary")),
        name="sb_attention",
    )(proj, proj, proj)


def _row_stats(chunks_ref, n_chunks, width):
    total = chunks_ref[0].sum(axis=-1, keepdims=True)
    for c in range(1, n_chunks):
        total = total + chunks_ref[c].sum(axis=-1, keepdims=True)
    mu = total / width
    sq = jnp.square(chunks_ref[0] - mu).sum(axis=-1, keepdims=True)
    for c in range(1, n_chunks):
        sq = sq + jnp.square(chunks_ref[c] - mu).sum(axis=-1, keepdims=True)
    return mu, lax.rsqrt(sq / width + LN_EPS)


def _outproj_kernel(oa_ref, ob_ref, ga_ref, gb_ref, wa_ref, wb_ref, x_ref, bo_ref,
                    lg_ref, lb_ref, y_ref, mu_ref, rs_ref, x1p_ref,
                    ma_ref, mb_ref, acc_ref):
    j = pl.program_id(1)
    n_chunks = acc_ref.shape[0]
    tn = acc_ref.shape[2]

    @pl.when(j == 0)
    def _():
        for o_ref, g_ref, m_ref in ((oa_ref, ga_ref, ma_ref), (ob_ref, gb_ref, mb_ref)):
            width = o_ref.shape[1]
            cols = [slice(c * tn, (c + 1) * tn) for c in range(width // tn)]
            sq = sum(jnp.square(o_ref[:, sl]).sum(axis=-1, keepdims=True) for sl in cols)
            r = lax.rsqrt(sq / width + RMS_EPS)
            for sl in cols:
                m_ref[:, sl] = (o_ref[:, sl] * r * g_ref[:, sl]).astype(BF16)

    mix = (jnp.dot(ma_ref[...], wa_ref[...], preferred_element_type=F32)
           + jnp.dot(mb_ref[...], wb_ref[...], preferred_element_type=F32))
    y = DEEPNORM_ALPHA * x_ref[...] + (mix + bo_ref[...])
    y_ref[...] = y
    acc_ref[j] = y

    @pl.when(j == n_chunks - 1)
    def _():
        mu, rs = _row_stats(acc_ref, n_chunks, n_chunks * tn)
        mu_ref[...] = mu
        rs_ref[...] = rs

        def normed(c):
            sl = slice(c * tn, (c + 1) * tn)
            return (acc_ref[c] - mu) * rs * lg_ref[:, sl] + lb_ref[:, sl]

        for c in range(n_chunks // 2):
            x1p_ref[:, c * tn:(c + 1) * tn] = _pack_halves(
                normed(c), normed(c + n_chunks // 2))


def _out_projection(o_a, o_b, g_a, g_b, w_b, x2, b_out, ln_g, ln_b):
    m, d = x2.shape
    tm, tn = TM_PROJ, TN_PROJ
    half = o_a.shape[1]
    once = dict(pipeline_mode=pl.Buffered(1))
    return pl.pallas_call(
        _outproj_kernel,
        out_shape=(jax.ShapeDtypeStruct((m, d), F32),
                   jax.ShapeDtypeStruct((m, 1), F32),
                   jax.ShapeDtypeStruct((m, 1), F32),
                   jax.ShapeDtypeStruct((m, d // 2), I32)),
        grid=(m // tm, d // tn),
        in_specs=[
            pl.BlockSpec((tm, half), lambda i, j: (i, 0), **once),
            pl.BlockSpec((tm, half), lambda i, j: (i, 0), **once),
            pl.BlockSpec((1, half), lambda i, j: (0, 0)),
            pl.BlockSpec((1, half), lambda i, j: (0, 0)),
            pl.BlockSpec((half, tn), lambda i, j: (0, j)),
            pl.BlockSpec((half, tn), lambda i, j: (1, j)),
            pl.BlockSpec((tm, tn), lambda i, j: (i, j)),
            pl.BlockSpec((1, tn), lambda i, j: (0, j)),
            pl.BlockSpec((1, d), lambda i, j: (0, 0)),
            pl.BlockSpec((1, d), lambda i, j: (0, 0)),
        ],
        out_specs=(pl.BlockSpec((tm, tn), lambda i, j: (i, j)),
                   pl.BlockSpec((tm, 1), lambda i, j: (i, 0)),
                   pl.BlockSpec((tm, 1), lambda i, j: (i, 0)),
                   pl.BlockSpec((tm, d // 2), lambda i, j: (i, 0))),
        scratch_shapes=[pltpu.VMEM((tm, half), BF16), pltpu.VMEM((tm, half), BF16),
                        pltpu.VMEM((d // tn, tm, tn), F32)],
        compiler_params=_cparams(("arbitrary", "arbitrary")),
        name="out_projection_ln1",
    )(o_a, o_b, g_a, g_b, w_b, w_b, x2, b_out, ln_g, ln_b)


ROUTER_COLS = LANES
EXPERT_LANE0 = N_GROUPS


def _router_kernel(x_ref, w_ref, b_ref, ids_ref, wts_ref, cnt_ref, carry_ref):
    i = pl.program_id(0)
    tm = x_ref.shape[0]

    @pl.when(i == 0)
    def _():
        carry_ref[...] = jnp.zeros_like(carry_ref)

    dh = x_ref.shape[1]
    logits = b_ref[...]
    for c in range(dh // TN_PROJ):
        sl = slice(c * TN_PROJ, (c + 1) * TN_PROJ)
        lo_x, hi_x = _unpack_halves(x_ref[:, sl])
        logits = logits + (
            jnp.dot(lo_x, w_ref[sl], preferred_element_type=F32)
            + jnp.dot(hi_x, w_ref[dh + c * TN_PROJ:dh + (c + 1) * TN_PROJ],
                      preferred_element_type=F32))
    lane = lax.broadcasted_iota(I32, (tm, ROUTER_COLS), 1)
    big = jnp.int32(ROUTER_COLS)

    def first_argmax(vals):
        top = jnp.max(vals, axis=-1, keepdims=True)
        idx = jnp.min(jnp.where(vals == top, lane, big), axis=-1, keepdims=True)
        return top, idx

    is_group = lane < N_GROUPS
    g_logits = jnp.where(is_group, logits, -jnp.inf)
    g_top, g_idx = first_argmax(g_logits)
    g_w = 1.0 / jnp.sum(jnp.exp(g_logits - g_top), axis=-1, keepdims=True)

    first = EXPERT_LANE0 + g_idx * EXPERTS_PER_GROUP
    in_group = (lane >= first) & (lane < first + EXPERTS_PER_GROUP)
    e_logits = jnp.where(in_group, logits, -jnp.inf)
    top1, idx1 = first_argmax(e_logits)
    top2, idx2 = first_argmax(jnp.where(lane == idx1, -jnp.inf, e_logits))
    e2 = jnp.exp(top2 - top1)
    w1 = g_w / (1.0 + e2)
    w2 = g_w * e2 / (1.0 + e2)

    hit1 = lane == idx1
    hit2 = lane == idx2
    onehot = (hit1 | hit2).astype(BF16)
    r = lax.broadcasted_iota(I32, (tm, tm), 0)
    c = lax.broadcasted_iota(I32, (tm, tm), 1)
    before = (c < r).astype(BF16)
    prior = jnp.dot(before, onehot, preferred_element_type=F32) + carry_ref[0:1, :]
    rank1 = jnp.sum(jnp.where(hit1, prior, 0.0), axis=-1, keepdims=True)
    rank2 = jnp.sum(jnp.where(hit2, prior, 0.0), axis=-1, keepdims=True)
    counts = carry_ref[0:1, :] + jnp.sum(onehot.astype(F32), axis=0, keepdims=True)
    carry_ref[...] = jnp.broadcast_to(counts, carry_ref.shape)
    cnt_ref[...] = jnp.broadcast_to(counts, cnt_ref.shape).astype(I32)

    ids = jnp.where(lane == 0, idx1 - EXPERT_LANE0,
          jnp.where(lane == 1, idx2 - EXPERT_LANE0,
          jnp.where(lane == 2, rank1.astype(I32),
          jnp.where(lane == 3, rank2.astype(I32), 0))))
    ids_ref[...] = ids
    wts_ref[...] = jnp.where(lane == 0, w1, jnp.where(lane == 1, w2, 0.0))


def _router(x1p, w_r, b_r):
    m, dh = x1p.shape
    tm = TM_PROJ
    return pl.pallas_call(
        _router_kernel,
        out_shape=(jax.ShapeDtypeStruct((m, ROUTER_COLS), I32),
                   jax.ShapeDtypeStruct((m, ROUTER_COLS), F32),
                   jax.ShapeDtypeStruct((8, ROUTER_COLS), I32)),
        grid=(m // tm,),
        in_specs=[
            pl.BlockSpec((tm, dh), lambda i: (i, 0)),
            pl.BlockSpec((2 * dh, ROUTER_COLS), lambda i: (0, 0)),
            pl.BlockSpec((1, ROUTER_COLS), lambda i: (0, 0)),
        ],
        out_specs=(pl.BlockSpec((tm, ROUTER_COLS), lambda i: (i, 0)),
                   pl.BlockSpec((tm, ROUTER_COLS), lambda i: (i, 0)),
                   pl.BlockSpec((8, ROUTER_COLS), lambda i: (0, 0))),
        scratch_shapes=[pltpu.VMEM((8, ROUTER_COLS), F32)],
        compiler_params=_cparams(("arbitrary",)),
        name="router",
    )(x1p, w_r, b_r)


def _start_row_gather(idx_ref, first, n_rows, src_ref, dst_ref, sem):
    def body(c, _):
        for u in range(ROW_CHUNK):
            r = c * ROW_CHUNK + u
            pltpu.make_async_copy(src_ref.at[pl.ds(idx_ref[first + r], 1)],
                                  dst_ref.at[pl.ds(r, 1)], sem).start()
        return 0

    lax.fori_loop(0, n_rows // ROW_CHUNK, body, 0)


def _wait_row_gather(n_rows, src_ref, dst_ref, sem):
    def body(c, _):
        for _u in range(ROW_CHUNK):
            pltpu.make_async_copy(src_ref.at[pl.ds(0, 1)], dst_ref.at[pl.ds(0, 1)],
                                  sem).wait()
        return 0

    lax.fori_loop(0, n_rows // ROW_CHUNK, body, 0)


def _expert_changed(te_ref, i):
    return jnp.logical_or(i == 0, te_ref[i] != te_ref[jnp.maximum(i - 1, 0)])


def _tile_row(i, nu):
    return jnp.minimum(i, nu[0] - 1)


def _gateup_kernel(te_ref, nu_ref, tos_ref, x_hbm, wg_ref, wu_ref, a_ref,
                   xbuf_ref, wgb_ref, wub_ref, sems):
    i = pl.program_id(0)
    n_used = nu_ref[0]
    slot = lax.rem(i, 2)

    def start(tile, buf):
        _start_row_gather(tos_ref, tile * TM_MOE, TM_MOE, x_hbm,
                          xbuf_ref.at[buf], sems.at[buf])

    @pl.when(i == 0)
    def _():
        start(0, 0)

    @pl.when(i + 1 < n_used)
    def _():
        start(i + 1, 1 - slot)

    @pl.when(_expert_changed(te_ref, i))
    def _():
        wgb_ref[...] = wg_ref[0].astype(BF16)
        wub_ref[...] = wu_ref[0].astype(BF16)

    @pl.when(i < n_used)
    def _():
        _wait_row_gather(TM_MOE, x_hbm, xbuf_ref.at[slot], sems.at[slot])
        lo, hi = _unpack_halves(xbuf_ref[slot, :, 0, :])
        gate = _dot_halves(lo, hi, wgb_ref)
        up = _dot_halves(lo, hi, wub_ref)
        a_ref[...] = (gate * jax.nn.sigmoid(gate) * up).astype(a_ref.dtype)

    @pl.when(i >= n_used)
    def _():
        a_ref[...] = jnp.zeros_like(a_ref)


def _grouped_gate_up(tile_expert, n_used, token_of_slot, x1p3, w_gate, w_up):
    p_rows = token_of_slot.shape[0]
    dh = x1p3.shape[2]
    _, d, f = w_gate.shape
    return pl.pallas_call(
        _gateup_kernel,
        out_shape=jax.ShapeDtypeStruct((p_rows, f), BF16),
        grid_spec=pltpu.PrefetchScalarGridSpec(
            num_scalar_prefetch=3,
            grid=(p_rows // TM_MOE,),
            in_specs=[
                pl.BlockSpec(memory_space=pl.ANY),
                pl.BlockSpec((1, d, f), lambda i, te, nu, tos: (te[i], 0, 0)),
                pl.BlockSpec((1, d, f), lambda i, te, nu, tos: (te[i], 0, 0)),
            ],
            out_specs=pl.BlockSpec((TM_MOE, f), lambda i, te, nu, tos: (i, 0)),
            scratch_shapes=[pltpu.VMEM((2, TM_MOE, 1, dh), I32),
                            pltpu.VMEM((d, f), BF16), pltpu.VMEM((d, f), BF16),
                            pltpu.SemaphoreType.DMA((2,))],
        ),
        compiler_params=_cparams(("arbitrary",)),
        name="moe_gate_up",
    )(tile_expert, n_used, token_of_slot, x1p3, w_gate, w_up)


def _down_kernel(te_ref, nu_ref, a_ref, wd_ref, y_ref, wdb_ref):
    i = pl.program_id(0)

    @pl.when(_expert_changed(te_ref, i))
    def _():
        wdb_ref[...] = wd_ref[0].astype(BF16)

    @pl.when(i < nu_ref[0])
    def _():
        y_ref[:, 0, :] = jnp.dot(a_ref[...], wdb_ref[...], preferred_element_type=F32)

    @pl.when(i >= nu_ref[0])
    def _():
        y_ref[...] = jnp.zeros_like(y_ref)


def _grouped_down(tile_expert, n_used, act, w_down):
    p_rows, f = act.shape
    d = w_down.shape[2]
    return pl.pallas_call(
        _down_kernel,
        out_shape=jax.ShapeDtypeStruct((p_rows, 1, d), F32),
        grid_spec=pltpu.PrefetchScalarGridSpec(
            num_scalar_prefetch=2,
            grid=(p_rows // TM_MOE,),
            in_specs=[
                pl.BlockSpec((TM_MOE, f), lambda i, te, nu: (_tile_row(i, nu), 0)),
                pl.BlockSpec((1, f, d), lambda i, te, nu: (te[i], 0, 0)),
            ],
            out_specs=pl.BlockSpec((TM_MOE, 1, d), lambda i, te, nu: (i, 0, 0)),
            scratch_shapes=[pltpu.VMEM((f, d), BF16)],
        ),
        compiler_params=_cparams(("arbitrary",)),
        name="moe_down",
    )(tile_expert, n_used, act, w_down)


def _combine_kernel(slot_ref, y_hbm, wts_ref, o_ref, buf_ref, sems):
    i = pl.program_id(0)
    tm = o_ref.shape[0]
    n_tokens = slot_ref.shape[0] // EXPERT_TOP_K
    slot = lax.rem(i, 2)

    def start(tile, buf):
        for k in range(EXPERT_TOP_K):
            _start_row_gather(slot_ref, k * n_tokens + tile * tm, tm, y_hbm,
                              buf_ref.at[buf, k], sems.at[buf])

    @pl.when(i == 0)
    def _():
        start(0, 0)

    @pl.when(i + 1 < pl.num_programs(0))
    def _():
        start(i + 1, 1 - slot)

    _wait_row_gather(EXPERT_TOP_K * tm, y_hbm, buf_ref.at[slot, 0], sems.at[slot])
    for c in range(o_ref.shape[1] // TN_PROJ):
        sl = slice(c * TN_PROJ, (c + 1) * TN_PROJ)
        o_ref[:, sl] = sum(wts_ref[:, k:k + 1] * buf_ref[slot, k, :, 0, sl]
                           for k in range(EXPERT_TOP_K))


def _combine_experts(slot_kt, y_sorted, wts):
    d = y_sorted.shape[2]
    m = slot_kt.shape[0] // EXPERT_TOP_K
    tm = TM_COMBINE
    return pl.pallas_call(
        _combine_kernel,
        out_shape=jax.ShapeDtypeStruct((m, d), F32),
        grid_spec=pltpu.PrefetchScalarGridSpec(
            num_scalar_prefetch=1,
            grid=(m // tm,),
            in_specs=[
                pl.BlockSpec(memory_space=pl.ANY),
                pl.BlockSpec((tm, ROUTER_COLS), lambda i, sl: (i, 0)),
            ],
            out_specs=pl.BlockSpec((tm, d), lambda i, sl: (i, 0)),
            scratch_shapes=[pltpu.VMEM((2, EXPERT_TOP_K, tm, 1, d), F32),
                            pltpu.SemaphoreType.DMA((2,))],
        ),
        compiler_params=_cparams(("arbitrary",)),
        name="moe_combine",
    )(slot_kt, y_sorted, wts)


def _final_kernel(x1p_ref, wg_ref, bg_ref, p_ref, wp_ref, y1_ref, mu_ref, rs_ref,
                  l1g_ref, l1b_ref, moe_ref, l2g_ref, l2b_ref,
                  o_ref, lo_ref, hi_ref, acc_ref, mu2_ref, rs2_ref):
    j = pl.program_id(1)
    n_chunks = acc_ref.shape[0]
    tn = acc_ref.shape[2]

    @pl.when(j == 0)
    def _():
        for c in range(lo_ref.shape[1] // tn):
            sl = slice(c * tn, (c + 1) * tn)
            lo_ref[:, sl], hi_ref[:, sl] = _unpack_halves(x1p_ref[:, sl])

    @pl.when(j < n_chunks)
    def _():
        gate = _dot_halves(lo_ref[...], hi_ref[...], wg_ref) + bg_ref[...]
        emb = jnp.dot(p_ref[...].astype(BF16), wp_ref[...].astype(BF16),
                      preferred_element_type=F32)
        x1 = (y1_ref[...] - mu_ref[...]) * rs_ref[...] * l1g_ref[...] + l1b_ref[...]
        acc_ref[j] = DEEPNORM_ALPHA * x1 + moe_ref[...] + jax.nn.sigmoid(gate) * emb

    @pl.when(j == n_chunks)
    def _():
        mu2_ref[...], rs2_ref[...] = _row_stats(acc_ref, n_chunks, n_chunks * tn)

    @pl.when(j >= n_chunks)
    def _():
        o_ref[...] = ((acc_ref[j - n_chunks] - mu2_ref[...]) * rs2_ref[...]
                      * l2g_ref[...] + l2b_ref[...])


def _final_stage(x1p, w_pg_b, b_pg, p2, w_pp, y1, mu1, rs1, ln1_g, ln1_b,
                 moe, ln2_g, ln2_b):
    m, d = y1.shape
    tm, tn = TM_PROJ, TN_PROJ
    ple = p2.shape[1]
    n_chunks = d // tn

    def build(j):
        return jnp.minimum(j, n_chunks - 1)

    def emit(j):
        return jnp.maximum(j - n_chunks, 0)

    return pl.pallas_call(
        _final_kernel,
        out_shape=jax.ShapeDtypeStruct((m, d), F32),
        grid=(m // tm, 2 * n_chunks),
        in_specs=[
            pl.BlockSpec((tm, d // 2), lambda i, j: (i, 0),
                         pipeline_mode=pl.Buffered(1)),
            pl.BlockSpec((d, tn), lambda i, j: (0, build(j))),
            pl.BlockSpec((1, tn), lambda i, j: (0, build(j))),
            pl.BlockSpec((tm, ple), lambda i, j: (i, 0)),
            pl.BlockSpec((ple, tn), lambda i, j: (0, build(j))),
            pl.BlockSpec((tm, tn), lambda i, j: (i, build(j))),
            pl.BlockSpec((tm, 1), lambda i, j: (i, 0)),
            pl.BlockSpec((tm, 1), lambda i, j: (i, 0)),
            pl.BlockSpec((1, tn), lambda i, j: (0, build(j))),
            pl.BlockSpec((1, tn), lambda i, j: (0, build(j))),
            pl.BlockSpec((tm, tn), lambda i, j: (i, build(j))),
            pl.BlockSpec((1, tn), lambda i, j: (0, emit(j))),
            pl.BlockSpec((1, tn), lambda i, j: (0, emit(j))),
        ],
        out_specs=pl.BlockSpec((tm, tn), lambda i, j: (i, emit(j))),
        scratch_shapes=[pltpu.VMEM((tm, d // 2), BF16), pltpu.VMEM((tm, d // 2), BF16),
                        pltpu.VMEM((n_chunks, tm, tn), F32),
                        pltpu.VMEM((tm, 1), F32), pltpu.VMEM((tm, 1), F32)],
        compiler_params=_cparams(("arbitrary", "arbitrary")),
        name="ple_moe_ln2",
    )(x1p, w_pg_b, b_pg, p2, w_pp, y1, mu1, rs1, ln1_g, ln1_b, moe, ln2_g, ln2_b)


def _routing_tables(ids, counts_row):
    counts = counts_row[EXPERT_LANE0:EXPERT_LANE0 + N_EXPERTS]
    tiles = (counts + TM_MOE - 1) // TM_MOE
    tile_end = jnp.cumsum(tiles)
    offsets = (tile_end - tiles) * TM_MOE
    n_used = tile_end[-1:]
    experts = ids[:, 0:EXPERT_TOP_K]
    slots = offsets[experts] + ids[:, EXPERT_TOP_K:2 * EXPERT_TOP_K]
    n_tiles = (EXPERT_TOP_K * ids.shape[0]) // TM_MOE + N_EXPERTS
    tile_ids = jnp.minimum(jnp.arange(n_tiles, dtype=I32), n_used - 1)
    tile_expert = jnp.sum(tile_end[None, :] <= tile_ids[:, None], axis=1).astype(I32)
    token_of_slot = jnp.zeros((n_tiles * TM_MOE,), I32).at[slots.reshape(-1)].set(
        jnp.arange(slots.size, dtype=I32) // EXPERT_TOP_K, unique_indices=True)
    return slots.astype(I32), token_of_slot, tile_expert, n_used.astype(I32)


def kernel(x, p, positions, w_in, b_in, sinks, g_norm_a, g_norm_b, w_out, b_out,
           ln1_g, ln1_b, w_group, b_group, w_er, b_er, w_gate, w_up, w_down,
           w_ple_gate, b_ple_gate, w_ple_proj, ln2_g, ln2_b):
    batch, seq, d = x.shape
    m = batch * seq
    row = lambda v: v.reshape(1, -1)
    x2 = x.reshape(m, d)
    for i in range(DEPTH):
        proj = _in_projection(x2, w_in[i].astype(BF16), row(b_in[i]))
        inv_freq = ROPE_THETA ** (-jnp.arange(0, SWA_HEAD_DIM, 2, dtype=F32) / SWA_HEAD_DIM)
        inv_freq = jnp.tile(inv_freq, LANES // inv_freq.shape[0]).reshape(1, LANES)
        o_a = _swa_attention(proj, positions.reshape(m, 1), inv_freq, sinks[i], batch, seq)
        o_b = _sb_attention(proj, batch, seq)
        y1, mu1, rs1, x1p = _out_projection(
            o_a, o_b, row(g_norm_a[i]), row(g_norm_b[i]), w_out[i].astype(BF16), x2,
            row(b_out[i]), row(ln1_g[i]), row(ln1_b[i]))
        pad = ROUTER_COLS - N_GROUPS - N_EXPERTS
        w_r = jnp.concatenate(
            [w_group[i], w_er[i].transpose(1, 0, 2).reshape(d, N_EXPERTS),
             jnp.zeros((d, pad), F32)], axis=1).astype(BF16)
        b_r = jnp.concatenate([b_group[i], b_er[i].reshape(-1), jnp.zeros((pad,), F32)])
        ids, wts, counts = _router(x1p, w_r, row(b_r))
        slots, token_of_slot, tile_expert, n_used = _routing_tables(ids, counts[0])
        act = _grouped_gate_up(tile_expert, n_used, token_of_slot,
                               x1p.reshape(m, 1, d // 2), w_gate[i], w_up[i])
        y_sorted = _grouped_down(tile_expert, n_used, act, w_down[i])
        moe = _combine_experts(slots.T.reshape(-1), y_sorted, wts)
        x2 = _final_stage(x1p, w_ple_gate[i].astype(BF16), row(b_ple_gate[i]),
                          p[i].reshape(m, PLE_DIM), w_ple_proj[i], y1, mu1, rs1,
                          row(ln1_g[i]), row(ln1_b[i]), moe, row(ln2_g[i]), row(ln2_b[i]))
    return x2.reshape(batch, seq, d)
```

```python
import math

import jax
import jax.numpy as jnp
from jax import lax
from jax.experimental import pallas as pl
from jax.experimental.pallas import tpu as pltpu

F32 = jnp.float32
BF16 = jnp.bfloat16
I32 = jnp.int32

D_MODEL = 4096
PLE_DIM = 256
BLOCK = 128
ROPE_THETA = 10000.0
LN_EPS = 1e-5
RMS_EPS = 1e-6
NEG_INF = -1e30

SWA_HEAD_DIM = 64
SWA_WIDTH = D_MODEL // 2
SWA_HEADS = SWA_WIDTH // SWA_HEAD_DIM
SWA_KV_HEADS = SWA_HEADS // 8
SWA_GROUP = SWA_HEADS // SWA_KV_HEADS
SWA_KV_WIDTH = SWA_KV_HEADS * SWA_HEAD_DIM

SB_HEAD_DIM = 128
SB_WIDTH = D_MODEL - SWA_WIDTH
SB_HEADS = SB_WIDTH // SB_HEAD_DIM

IN_WIDTH = SWA_WIDTH + 2 * SWA_KV_WIDTH + 3 * SB_WIDTH
A_WIDTH = SWA_WIDTH + 2 * SWA_KV_WIDTH
Q_B_COL = 0
K_B_COL = Q_B_COL + SB_WIDTH
V_B_COL = K_B_COL + SB_WIDTH
Q_A_COL = V_B_COL + SB_WIDTH
K_A_COL = Q_A_COL + SWA_WIDTH
V_A_COL = K_A_COL + SWA_KV_WIDTH


def _mixer_b_first(a):
    return jnp.concatenate([a[..., A_WIDTH:], a[..., :A_WIDTH]], axis=-1)

N_GROUPS = 4
EXPERTS_PER_GROUP = 8
N_EXPERTS = N_GROUPS * EXPERTS_PER_GROUP
EXPERT_TOP_K = 2
DEPTH = 1
DEEPNORM_ALPHA = (2.0 * DEPTH) ** 0.25

LANES = 128
VMEM_LIMIT_BYTES = 56 * 1024 * 1024

EXP_ZERO_LOG = -104.0 - 2.0

TM_IN = 1024
TM_PROJ = 512
TN_PROJ = 512
TM_MOE = 256
TM_COMBINE = 128
SB_HEADS_PER_STEP = 8
ROW_CHUNK = 16


def _cparams(sem):
    return pltpu.CompilerParams(dimension_semantics=sem,
                                vmem_limit_bytes=VMEM_LIMIT_BYTES)


def _pack_halves(lo, hi):
    return pltpu.pack_elementwise([lo, hi], packed_dtype=BF16)


def _unpack_halves(words):
    return tuple(
        pltpu.unpack_elementwise(words, index=k, packed_dtype=BF16,
                                 unpacked_dtype=F32).astype(BF16) for k in (0, 1))


def _dot_halves(lo, hi, w_ref):
    half = lo.shape[1]
    return (jnp.dot(lo, w_ref[:half], preferred_element_type=F32)
            + jnp.dot(hi, w_ref[half:], preferred_element_type=F32))


WORD_ROWS = (D_MODEL // 2) // LANES
SLAB_PAIR = 2 * LANES


def _slab_words(slab_ref, pair, tokens):
    return jnp.concatenate(
        [slab_ref[pl.ds(2 * pair + k, tokens, stride=WORD_ROWS), :] for k in (0, 1)],
        axis=1)


def _dot_slab(slab_ref, tokens, w_refs):
    half = D_MODEL // 2
    outs = [None] * len(w_refs)
    for pair in range(half // SLAB_PAIR):
        lo, hi = _unpack_halves(_slab_words(slab_ref, pair, tokens))
        rows_lo = slice(pair * SLAB_PAIR, (pair + 1) * SLAB_PAIR)
        rows_hi = slice(half + pair * SLAB_PAIR, half + (pair + 1) * SLAB_PAIR)
        for n, w_ref in enumerate(w_refs):
            part = (jnp.dot(lo, w_ref[rows_lo], preferred_element_type=F32)
                    + jnp.dot(hi, w_ref[rows_hi], preferred_element_type=F32))
            outs[n] = part if outs[n] is None else outs[n] + part
    return outs


def _dot_nt(a, b):
    return lax.dot_general(a, b, (((1,), (1,)), ((), ())),
                           preferred_element_type=F32)


def _inproj_kernel(x_ref, w_ref, b_ref, o_ref, xb_ref):
    @pl.when(pl.program_id(1) == 0)
    def _():
        xb_ref[...] = x_ref[...].astype(BF16)

    acc = jnp.dot(xb_ref[...], w_ref[...], preferred_element_type=F32)
    o_ref[...] = (acc + b_ref[...]).astype(o_ref.dtype)


def _in_projection(x2, w_b, b):
    m, k = x2.shape
    n = w_b.shape[1]
    return pl.pallas_call(
        _inproj_kernel,
        out_shape=jax.ShapeDtypeStruct((m, n), BF16),
        grid=(m // TM_IN, n // TN_PROJ),
        in_specs=[
            pl.BlockSpec((TM_IN, k), lambda i, j: (i, 0)),
            pl.BlockSpec((k, TN_PROJ), lambda i, j: (0, j)),
            pl.BlockSpec((1, TN_PROJ), lambda i, j: (0, j)),
        ],
        out_specs=pl.BlockSpec((TM_IN, TN_PROJ), lambda i, j: (i, j)),
        scratch_shapes=[pltpu.VMEM((TM_IN, k), BF16)],
        compiler_params=_cparams(("arbitrary", "arbitrary")),
        name="in_projection",
    )(x2, w_b, b)


def _swa_kernel(sinks_ref, q_ref, kc_ref, kp_ref, vc_ref, vp_ref,
                posc_ref, posp_ref, invf_ref, o_ref):
    n = pl.program_id(1)
    lane = lax.broadcasted_iota(I32, (1, LANES), 1)
    first_half = (lane % SWA_HEAD_DIM) < (SWA_HEAD_DIM // 2)

    def tables(pos_ref):
        ang = pos_ref[...].astype(F32) * invf_ref[...]
        sin = jnp.sin(ang)
        return jnp.cos(ang), jnp.where(first_half, -sin, sin)

    def rope(x, cos, sin_signed):
        partner = jnp.where(first_half,
                            pltpu.roll(x, LANES - SWA_HEAD_DIM // 2, 1),
                            pltpu.roll(x, SWA_HEAD_DIM // 2, 1))
        return x * cos + partner * sin_signed

    cos_c, sin_c = tables(posc_ref)
    cos_p, sin_p = tables(posp_ref)

    def rope_block(ref, cos, sin_signed):
        width = ref.shape[1]
        return [rope(ref[:, c * LANES:(c + 1) * LANES].astype(F32), cos,
                     sin_signed).astype(BF16) for c in range(width // LANES)]

    q_chunks = rope_block(q_ref, cos_c, sin_c)
    k_chunks = [jnp.concatenate([p_, c_], axis=0)
                for p_, c_ in zip(rope_block(kp_ref, cos_p, sin_p),
                                  rope_block(kc_ref, cos_c, sin_c))]
    v_all = jnp.concatenate([vp_ref[...], vc_ref[...]], axis=0)

    qi = lax.broadcasted_iota(I32, (BLOCK, 2 * BLOCK), 0)
    kj = lax.broadcasted_iota(I32, (BLOCK, 2 * BLOCK), 1)
    rel = qi - (kj - BLOCK)
    valid = (rel >= 0) & (rel < BLOCK) & ((kj >= BLOCK) | (n > 0))

    def head_slice(chunks, head):
        half = head % 2
        return chunks[head // 2][:, half * SWA_HEAD_DIM:(half + 1) * SWA_HEAD_DIM]

    scale = 1.0 / math.sqrt(SWA_HEAD_DIM)
    for h in range(SWA_KV_HEADS):
        k_h = head_slice(k_chunks, h)
        v_h = v_all[:, h * SWA_HEAD_DIM:(h + 1) * SWA_HEAD_DIM]
        q_h = jnp.concatenate(
            [head_slice(q_chunks, h * SWA_GROUP + g) for g in range(SWA_GROUP)],
            axis=0)
        s = _dot_nt(q_h, k_h) * scale
        probs = []
        for g in range(SWA_GROUP):
            sink = sinks_ref[h * SWA_GROUP + g]
            s_g = jnp.where(valid, s[g * BLOCK:(g + 1) * BLOCK], NEG_INF)
            m = jnp.maximum(jnp.max(s_g, axis=-1, keepdims=True), sink)
            e = jnp.exp(s_g - m)
            den = jnp.sum(e, axis=-1, keepdims=True) + jnp.exp(sink - m)
            probs.append((e / den).astype(BF16))
        o_h = jnp.dot(jnp.concatenate(probs, axis=0), v_h,
                      preferred_element_type=F32)
        for g in range(SWA_GROUP):
            col = (h * SWA_GROUP + g) * SWA_HEAD_DIM
            o_ref[:, col:col + SWA_HEAD_DIM] = o_h[g * BLOCK:(g + 1) * BLOCK]


def _swa_attention(proj, pos2, inv_freq, sinks, batch, seq):
    nb = seq // BLOCK
    assert Q_A_COL % SWA_WIDTH == 0 and K_A_COL % SWA_KV_WIDTH == 0
    assert V_A_COL % SWA_KV_WIDTH == 0
    qcol = Q_A_COL // SWA_WIDTH
    kcol = K_A_COL // SWA_KV_WIDTH
    vcol = V_A_COL // SWA_KV_WIDTH

    def cur(b, n):
        return b * nb + n

    def prev(b, n):
        return b * nb + jnp.maximum(n - 1, 0)

    return pl.pallas_call(
        _swa_kernel,
        out_shape=jax.ShapeDtypeStruct((batch * seq, SWA_WIDTH), F32),
        grid=(batch, nb),
        in_specs=[
            pl.BlockSpec(memory_space=pltpu.SMEM),
            pl.BlockSpec((BLOCK, SWA_WIDTH), lambda b, n: (cur(b, n), qcol)),
            pl.BlockSpec((BLOCK, SWA_KV_WIDTH), lambda b, n: (cur(b, n), kcol)),
            pl.BlockSpec((BLOCK, SWA_KV_WIDTH), lambda b, n: (prev(b, n), kcol)),
            pl.BlockSpec((BLOCK, SWA_KV_WIDTH), lambda b, n: (cur(b, n), vcol)),
            pl.BlockSpec((BLOCK, SWA_KV_WIDTH), lambda b, n: (prev(b, n), vcol)),
            pl.BlockSpec((BLOCK, 1), lambda b, n: (cur(b, n), 0)),
            pl.BlockSpec((BLOCK, 1), lambda b, n: (prev(b, n), 0)),
            pl.BlockSpec((1, LANES), lambda b, n: (0, 0)),
        ],
        out_specs=pl.BlockSpec((BLOCK, SWA_WIDTH), lambda b, n: (cur(b, n), 0)),
        compiler_params=_cparams(("arbitrary", "arbitrary")),
        name="swa_attention",
    )(sinks, proj, proj, proj, proj, proj, pos2, pos2, inv_freq)


def _sb_kernel(q_ref, k_ref, v_ref, o_ref):
    n = pl.program_id(2)
    heads = [slice(h * SB_HEAD_DIM, (h + 1) * SB_HEAD_DIM)
             for h in range(SB_HEADS_PER_STEP)]
    rows = SB_HEADS_PER_STEP * BLOCK
    scale = 1.0 / math.sqrt(SB_HEAD_DIM)
    key_j = lax.broadcasted_iota(I32, (BLOCK, 2 * BLOCK), 0)
    out_c = lax.broadcasted_iota(I32, (BLOCK, 2 * BLOCK), 1)
    later_total = ((key_j > out_c) | (out_c >= BLOCK)).astype(BF16)
    q_row = lax.broadcasted_iota(I32, (rows, BLOCK), 0) % BLOCK
    k_col = lax.broadcasted_iota(I32, (rows, BLOCK), 1)
    causal = k_col < q_row

    def block(kb, carry, acc, mask):
        start = pl.multiple_of(kb * BLOCK, BLOCK)
        z = jnp.concatenate(
            [_dot_nt(q_ref[:, hd], k_ref[pl.ds(start, BLOCK), hd]) for hd in heads],
            axis=0) * scale
        t = jnp.log(1.0 + jnp.exp(-jnp.abs(z)))
        log_not = jnp.minimum(-z, 0.0) - t
        log_beta = jnp.minimum(z, 0.0) - t
        if mask is not None:
            log_not = jnp.where(mask, log_not, 0.0)
        hi = log_not.astype(BF16)
        r1 = log_not - hi.astype(F32)
        mid = r1.astype(BF16)
        lo = (r1 - mid.astype(F32)).astype(BF16)
        parts = jnp.dot(jnp.concatenate([hi, mid, lo], axis=0), later_total,
                        preferred_element_type=F32)
        sums = (parts[:rows] + parts[rows:2 * rows]) + parts[2 * rows:]
        a = jnp.exp(log_beta + sums[:, :BLOCK] + carry)
        if mask is not None:
            a = jnp.where(mask, a, 0.0)
        a = a.astype(BF16)
        pv = jnp.concatenate(
            [jnp.dot(a[h * BLOCK:(h + 1) * BLOCK], v_ref[pl.ds(start, BLOCK), hd],
                     preferred_element_type=F32) for h, hd in enumerate(heads)], axis=0)
        return carry + sums[:, BLOCK:], acc + pv

    zeros = jnp.zeros((rows, BLOCK), F32)
    carry, acc = block(n, zeros, zeros, causal)

    def cond(state):
        kb, carry, _ = state
        return jnp.logical_and(kb >= 0, jnp.max(carry) > EXP_ZERO_LOG)

    def body(state):
        kb, carry, acc = state
        carry, acc = block(kb, carry, acc, None)
        return kb - 1, carry, acc

    _, _, acc = lax.while_loop(cond, body, (n - 1, carry, acc))
    for h, hd in enumerate(heads):
        o_ref[:, hd] = acc[h * BLOCK:(h + 1) * BLOCK]


def _sb_attention(proj, batch, seq):
    nb = seq // BLOCK
    width = SB_HEADS_PER_STEP * SB_HEAD_DIM
    assert Q_B_COL % width == 0 and K_B_COL % width == 0 and V_B_COL % width == 0
    qcol = Q_B_COL // width
    kcol = K_B_COL // width
    vcol = V_B_COL // width
    return pl.pallas_call(
        _sb_kernel,
        out_shape=jax.ShapeDtypeStruct((batch * seq, SB_WIDTH), F32),
        grid=(batch, SB_HEADS // SB_HEADS_PER_STEP, nb),
        in_specs=[
            pl.BlockSpec((BLOCK, width), lambda b, h, n: (b * nb + n, qcol + h)),
            pl.BlockSpec((seq, width), lambda b, h, n: (b, kcol + h)),
            pl.BlockSpec((seq, width), lambda b, h, n: (b, vcol + h)),
        ],
        out_specs=pl.BlockSpec((BLOCK, width), lambda b, h, n: (b * nb + n, h)),
        compiler_params=_cparams(("arbitrary", "arbitrary", "arbitrary")),
        name="sb_attention",
    )(proj, proj, proj)


def _row_stats(chunks_ref, n_chunks, width):
    total = chunks_ref[0].sum(axis=-1, keepdims=True)
    for c in range(1, n_chunks):
        total = total + chunks_ref[c].sum(axis=-1, keepdims=True)
    mu = total / width
    sq = jnp.square(chunks_ref[0] - mu).sum(axis=-1, keepdims=True)
    for c in range(1, n_chunks):
        sq = sq + jnp.square(chunks_ref[c] - mu).sum(axis=-1, keepdims=True)
    return mu, lax.rsqrt(sq / width + LN_EPS)


def _outproj_kernel(oa_ref, ob_ref, ga_ref, gb_ref, wa_ref, wb_ref, x_ref, bo_ref,
                    lg_ref, lb_ref, y_ref, mu_ref, rs_ref, x1s_ref,
                    ma_ref, mb_ref, acc_ref):
    j = pl.program_id(1)
    n_chunks = acc_ref.shape[0]
    tn = acc_ref.shape[2]

    @pl.when(j == 0)
    def _():
        for o_ref, g_ref, m_ref in ((oa_ref, ga_ref, ma_ref), (ob_ref, gb_ref, mb_ref)):
            width = o_ref.shape[1]
            cols = [slice(c * tn, (c + 1) * tn) for c in range(width // tn)]
            sq = sum(jnp.square(o_ref[:, sl]).sum(axis=-1, keepdims=True) for sl in cols)
            r = lax.rsqrt(sq / width + RMS_EPS)
            for sl in cols:
                m_ref[:, sl] = (o_ref[:, sl] * r * g_ref[:, sl]).astype(BF16)

    mix = (jnp.dot(ma_ref[...], wa_ref[...], preferred_element_type=F32)
           + jnp.dot(mb_ref[...], wb_ref[...], preferred_element_type=F32))
    y = DEEPNORM_ALPHA * x_ref[...] + (mix + bo_ref[...])
    y_ref[...] = y
    acc_ref[j] = y

    @pl.when(j == n_chunks - 1)
    def _():
        mu, rs = _row_stats(acc_ref, n_chunks, n_chunks * tn)
        mu_ref[...] = mu
        rs_ref[...] = rs

        def normed(c):
            sl = slice(c * tn, (c + 1) * tn)
            return (acc_ref[c] - mu) * rs * lg_ref[:, sl] + lb_ref[:, sl]

        tm = acc_ref.shape[1]
        for c in range(n_chunks // 2):
            words = _pack_halves(normed(c), normed(c + n_chunks // 2))
            for q in range(tn // LANES):
                x1s_ref[pl.ds(c * (tn // LANES) + q, tm, stride=WORD_ROWS), :] = (
                    words[:, q * LANES:(q + 1) * LANES])


def _out_projection(o_a, o_b, g_a, g_b, w_b, x2, b_out, ln_g, ln_b):
    m, d = x2.shape
    tm, tn = TM_PROJ, TN_PROJ
    half = o_a.shape[1]
    once = dict(pipeline_mode=pl.Buffered(1))
    return pl.pallas_call(
        _outproj_kernel,
        out_shape=(jax.ShapeDtypeStruct((m, d), F32),
                   jax.ShapeDtypeStruct((m, 1), F32),
                   jax.ShapeDtypeStruct((m, 1), F32),
                   jax.ShapeDtypeStruct((m * WORD_ROWS, LANES), I32)),
        grid=(m // tm, d // tn),
        in_specs=[
            pl.BlockSpec((tm, half), lambda i, j: (i, 0), **once),
            pl.BlockSpec((tm, half), lambda i, j: (i, 0), **once),
            pl.BlockSpec((1, half), lambda i, j: (0, 0)),
            pl.BlockSpec((1, half), lambda i, j: (0, 0)),
            pl.BlockSpec((half, tn), lambda i, j: (0, j)),
            pl.BlockSpec((half, tn), lambda i, j: (1, j)),
            pl.BlockSpec((tm, tn), lambda i, j: (i, j)),
            pl.BlockSpec((1, tn), lambda i, j: (0, j)),
            pl.BlockSpec((1, d), lambda i, j: (0, 0)),
            pl.BlockSpec((1, d), lambda i, j: (0, 0)),
        ],
        out_specs=(pl.BlockSpec((tm, tn), lambda i, j: (i, j)),
                   pl.BlockSpec((tm, 1), lambda i, j: (i, 0)),
                   pl.BlockSpec((tm, 1), lambda i, j: (i, 0)),
                   pl.BlockSpec((tm * WORD_ROWS, LANES), lambda i, j: (i, 0))),
        scratch_shapes=[pltpu.VMEM((tm, half), BF16), pltpu.VMEM((tm, half), BF16),
                        pltpu.VMEM((d // tn, tm, tn), F32)],
        compiler_params=_cparams(("arbitrary", "arbitrary")),
        name="out_projection_ln1",
    )(o_a, o_b, g_a, g_b, w_b, w_b, x2, b_out, ln_g, ln_b)


ROUTER_COLS = LANES
EXPERT_LANE0 = N_GROUPS


def _router_kernel(x_ref, w_ref, b_ref, ids_ref, wts_ref, cnt_ref, carry_ref):
    i = pl.program_id(0)
    tm = ids_ref.shape[0]

    @pl.when(i == 0)
    def _():
        carry_ref[...] = jnp.zeros_like(carry_ref)

    logits = _dot_slab(x_ref, tm, [w_ref])[0] + b_ref[...]
    lane = lax.broadcasted_iota(I32, (tm, ROUTER_COLS), 1)
    big = jnp.int32(ROUTER_COLS)

    def first_argmax(vals):
        top = jnp.max(vals, axis=-1, keepdims=True)
        idx = jnp.min(jnp.where(vals == top, lane, big), axis=-1, keepdims=True)
        return top, idx

    is_group = lane < N_GROUPS
    g_logits = jnp.where(is_group, logits, -jnp.inf)
    g_top, g_idx = first_argmax(g_logits)
    g_w = 1.0 / jnp.sum(jnp.exp(g_logits - g_top), axis=-1, keepdims=True)

    first = EXPERT_LANE0 + g_idx * EXPERTS_PER_GROUP
    in_group = (lane >= first) & (lane < first + EXPERTS_PER_GROUP)
    e_logits = jnp.where(in_group, logits, -jnp.inf)
    top1, idx1 = first_argmax(e_logits)
    top2, idx2 = first_argmax(jnp.where(lane == idx1, -jnp.inf, e_logits))
    e2 = jnp.exp(top2 - top1)
    w1 = g_w / (1.0 + e2)
    w2 = g_w * e2 / (1.0 + e2)

    hit1 = lane == idx1
    hit2 = lane == idx2
    onehot = (hit1 | hit2).astype(BF16)
    r = lax.broadcasted_iota(I32, (tm, tm), 0)
    c = lax.broadcasted_iota(I32, (tm, tm), 1)
    before = (c < r).astype(BF16)
    prior = jnp.dot(before, onehot, preferred_element_type=F32) + carry_ref[0:1, :]
    rank1 = jnp.sum(jnp.where(hit1, prior, 0.0), axis=-1, keepdims=True)
    rank2 = jnp.sum(jnp.where(hit2, prior, 0.0), axis=-1, keepdims=True)
    counts = carry_ref[0:1, :] + jnp.sum(onehot.astype(F32), axis=0, keepdims=True)
    carry_ref[...] = jnp.broadcast_to(counts, carry_ref.shape)
    cnt_ref[...] = jnp.broadcast_to(counts, cnt_ref.shape).astype(I32)

    ids = jnp.where(lane == 0, idx1 - EXPERT_LANE0,
          jnp.where(lane == 1, idx2 - EXPERT_LANE0,
          jnp.where(lane == 2, rank1.astype(I32),
          jnp.where(lane == 3, rank2.astype(I32), 0))))
    ids_ref[...] = ids
    wts_ref[...] = jnp.where(lane == 0, w1, jnp.where(lane == 1, w2, 0.0))


def _router(x1s, w_r, b_r):
    m = x1s.shape[0] // WORD_ROWS
    tm = TM_PROJ
    return pl.pallas_call(
        _router_kernel,
        out_shape=(jax.ShapeDtypeStruct((m, ROUTER_COLS), I32),
                   jax.ShapeDtypeStruct((m, ROUTER_COLS), F32),
                   jax.ShapeDtypeStruct((8, ROUTER_COLS), I32)),
        grid=(m // tm,),
        in_specs=[
            pl.BlockSpec((tm * WORD_ROWS, LANES), lambda i: (i, 0)),
            pl.BlockSpec((D_MODEL, ROUTER_COLS), lambda i: (0, 0)),
            pl.BlockSpec((1, ROUTER_COLS), lambda i: (0, 0)),
        ],
        out_specs=(pl.BlockSpec((tm, ROUTER_COLS), lambda i: (i, 0)),
                   pl.BlockSpec((tm, ROUTER_COLS), lambda i: (i, 0)),
                   pl.BlockSpec((8, ROUTER_COLS), lambda i: (0, 0))),
        scratch_shapes=[pltpu.VMEM((8, ROUTER_COLS), F32)],
        compiler_params=_cparams(("arbitrary",)),
        name="router",
    )(x1s, w_r, b_r)


def _start_row_gather(idx_ref, first, n_items, span, src_ref, dst_ref, sem):
    def body(c, _):
        for u in range(ROW_CHUNK):
            r = c * ROW_CHUNK + u
            src_row = pl.multiple_of(idx_ref[first + r], span)
            dst_row = pl.multiple_of(r * span, span)
            pltpu.make_async_copy(src_ref.at[pl.ds(src_row, span)],
                                  dst_ref.at[pl.ds(dst_row, span)], sem).start()
        return 0

    lax.fori_loop(0, n_items // ROW_CHUNK, body, 0)


def _wait_row_gather(n_items, span, src_ref, dst_ref, sem):
    def body(c, _):
        for _u in range(ROW_CHUNK):
            pltpu.make_async_copy(src_ref.at[pl.ds(0, span)], dst_ref.at[pl.ds(0, span)],
                                  sem).wait()
        return 0

    lax.fori_loop(0, n_items // ROW_CHUNK, body, 0)


def _expert_changed(te_ref, i):
    return jnp.logical_or(i == 0, te_ref[i] != te_ref[jnp.maximum(i - 1, 0)])


def _tile_row(i, nu):
    return jnp.minimum(i, nu[0] - 1)


def _gateup_kernel(te_ref, nu_ref, tos_ref, x_hbm, wg_ref, wu_ref, a_ref,
                   xbuf_ref, wgb_ref, wub_ref, sems):
    i = pl.program_id(0)
    n_used = nu_ref[0]
    slot = lax.rem(i, 2)

    def start(tile, buf):
        _start_row_gather(tos_ref, tile * TM_MOE, TM_MOE, WORD_ROWS, x_hbm,
                          xbuf_ref.at[buf], sems.at[buf])

    @pl.when(i == 0)
    def _():
        start(0, 0)

    @pl.when(i + 1 < n_used)
    def _():
        start(i + 1, 1 - slot)

    @pl.when(_expert_changed(te_ref, i))
    def _():
        wgb_ref[...] = wg_ref[0].astype(BF16)
        wub_ref[...] = wu_ref[0].astype(BF16)

    @pl.when(i < n_used)
    def _():
        _wait_row_gather(TM_MOE, WORD_ROWS, x_hbm, xbuf_ref.at[slot], sems.at[slot])
        gate, up = _dot_slab(xbuf_ref.at[slot], TM_MOE, [wgb_ref, wub_ref])
        a_ref[...] = (gate * jax.nn.sigmoid(gate) * up).astype(a_ref.dtype)

    @pl.when(i >= n_used)
    def _():
        a_ref[...] = jnp.zeros_like(a_ref)


def _grouped_gate_up(tile_expert, n_used, slab_of_slot, x1s, w_gate, w_up):
    p_rows = slab_of_slot.shape[0]
    _, d, f = w_gate.shape
    return pl.pallas_call(
        _gateup_kernel,
        out_shape=jax.ShapeDtypeStruct((p_rows, f), BF16),
        grid_spec=pltpu.PrefetchScalarGridSpec(
            num_scalar_prefetch=3,
            grid=(p_rows // TM_MOE,),
            in_specs=[
                pl.BlockSpec(memory_space=pl.ANY),
                pl.BlockSpec((1, d, f), lambda i, te, nu, tos: (te[i], 0, 0)),
                pl.BlockSpec((1, d, f), lambda i, te, nu, tos: (te[i], 0, 0)),
            ],
            out_specs=pl.BlockSpec((TM_MOE, f), lambda i, te, nu, tos: (i, 0)),
            scratch_shapes=[pltpu.VMEM((2, TM_MOE * WORD_ROWS, LANES), I32),
                            pltpu.VMEM((d, f), BF16), pltpu.VMEM((d, f), BF16),
                            pltpu.SemaphoreType.DMA((2,))],
        ),
        compiler_params=_cparams(("arbitrary",)),
        name="moe_gate_up",
    )(tile_expert, n_used, slab_of_slot, x1s, w_gate, w_up)


def _down_kernel(te_ref, nu_ref, a_ref, wd_ref, y_ref, wdb_ref):
    i = pl.program_id(0)

    @pl.when(_expert_changed(te_ref, i))
    def _():
        wdb_ref[...] = wd_ref[0].astype(BF16)

    @pl.when(i < nu_ref[0])
    def _():
        y_ref[:, 0, :] = jnp.dot(a_ref[...], wdb_ref[...], preferred_element_type=F32)

    @pl.when(i >= nu_ref[0])
    def _():
        y_ref[...] = jnp.zeros_like(y_ref)


def _grouped_down(tile_expert, n_used, act, w_down):
    p_rows, f = act.shape
    d = w_down.shape[2]
    return pl.pallas_call(
        _down_kernel,
        out_shape=jax.ShapeDtypeStruct((p_rows, 1, d), F32),
        grid_spec=pltpu.PrefetchScalarGridSpec(
            num_scalar_prefetch=2,
            grid=(p_rows // TM_MOE,),
            in_specs=[
                pl.BlockSpec((TM_MOE, f), lambda i, te, nu: (_tile_row(i, nu), 0)),
                pl.BlockSpec((1, f, d), lambda i, te, nu: (te[i], 0, 0)),
            ],
            out_specs=pl.BlockSpec((TM_MOE, 1, d), lambda i, te, nu: (i, 0, 0)),
            scratch_shapes=[pltpu.VMEM((f, d), BF16)],
        ),
        compiler_params=_cparams(("arbitrary",)),
        name="moe_down",
    )(tile_expert, n_used, act, w_down)


def _combine_kernel(slot_ref, y_hbm, wts_ref, o_ref, buf_ref, sems):
    i = pl.program_id(0)
    tm = o_ref.shape[0]
    n_tokens = slot_ref.shape[0] // EXPERT_TOP_K
    slot = lax.rem(i, 2)

    def start(tile, buf):
        for k in range(EXPERT_TOP_K):
            _start_row_gather(slot_ref, k * n_tokens + tile * tm, tm, 1, y_hbm,
                              buf_ref.at[buf, k], sems.at[buf])

    @pl.when(i == 0)
    def _():
        start(0, 0)

    @pl.when(i + 1 < pl.num_programs(0))
    def _():
        start(i + 1, 1 - slot)

    _wait_row_gather(EXPERT_TOP_K * tm, 1, y_hbm, buf_ref.at[slot, 0], sems.at[slot])
    for c in range(o_ref.shape[1] // TN_PROJ):
        sl = slice(c * TN_PROJ, (c + 1) * TN_PROJ)
        o_ref[:, sl] = sum(wts_ref[:, k:k + 1] * buf_ref[slot, k, :, 0, sl]
                           for k in range(EXPERT_TOP_K))


def _combine_experts(slot_kt, y_sorted, wts):
    d = y_sorted.shape[2]
    m = slot_kt.shape[0] // EXPERT_TOP_K
    tm = TM_COMBINE
    return pl.pallas_call(
        _combine_kernel,
        out_shape=jax.ShapeDtypeStruct((m, d), F32),
        grid_spec=pltpu.PrefetchScalarGridSpec(
            num_scalar_prefetch=1,
            grid=(m // tm,),
            in_specs=[
                pl.BlockSpec(memory_space=pl.ANY),
                pl.BlockSpec((tm, ROUTER_COLS), lambda i, sl: (i, 0)),
            ],
            out_specs=pl.BlockSpec((tm, d), lambda i, sl: (i, 0)),
            scratch_shapes=[pltpu.VMEM((2, EXPERT_TOP_K, tm, 1, d), F32),
                            pltpu.SemaphoreType.DMA((2,))],
        ),
        compiler_params=_cparams(("arbitrary",)),
        name="moe_combine",
    )(slot_kt, y_sorted, wts)


def _final_kernel(x1s_ref, wg_ref, bg_ref, p_ref, wp_ref, y1_ref, mu_ref, rs_ref,
                  l1g_ref, l1b_ref, moe_ref, l2g_ref, l2b_ref,
                  o_ref, lo_ref, hi_ref, acc_ref, mu2_ref, rs2_ref):
    j = pl.program_id(1)
    n_chunks = acc_ref.shape[0]
    tn = acc_ref.shape[2]

    @pl.when(j == 0)
    def _():
        for s in range(WORD_ROWS):
            sl = slice(s * LANES, (s + 1) * LANES)
            lo_ref[:, sl], hi_ref[:, sl] = _unpack_halves(
                x1s_ref[pl.ds(s, lo_ref.shape[0], stride=WORD_ROWS), :])

    @pl.when(j < n_chunks)
    def _():
        gate = _dot_halves(lo_ref[...], hi_ref[...], wg_ref) + bg_ref[...]
        emb = jnp.dot(p_ref[...].astype(BF16), wp_ref[...].astype(BF16),
                      preferred_element_type=F32)
        x1 = (y1_ref[...] - mu_ref[...]) * rs_ref[...] * l1g_ref[...] + l1b_ref[...]
        acc_ref[j] = DEEPNORM_ALPHA * x1 + moe_ref[...] + jax.nn.sigmoid(gate) * emb

    @pl.when(j == n_chunks)
    def _():
        mu2_ref[...], rs2_ref[...] = _row_stats(acc_ref, n_chunks, n_chunks * tn)

    @pl.when(j >= n_chunks)
    def _():
        o_ref[...] = ((acc_ref[j - n_chunks] - mu2_ref[...]) * rs2_ref[...]
                      * l2g_ref[...] + l2b_ref[...])


def _final_stage(x1s, w_pg_b, b_pg, p2, w_pp, y1, mu1, rs1, ln1_g, ln1_b,
                 moe, ln2_g, ln2_b):
    m, d = y1.shape
    tm, tn = TM_PROJ, TN_PROJ
    ple = p2.shape[1]
    n_chunks = d // tn

    def build(j):
        return jnp.minimum(j, n_chunks - 1)

    def emit(j):
        return jnp.maximum(j - n_chunks, 0)

    return pl.pallas_call(
        _final_kernel,
        out_shape=jax.ShapeDtypeStruct((m, d), F32),
        grid=(m // tm, 2 * n_chunks),
        in_specs=[
            pl.BlockSpec((tm * WORD_ROWS, LANES), lambda i, j: (i, 0),
                         pipeline_mode=pl.Buffered(1)),
            pl.BlockSpec((d, tn), lambda i, j: (0, build(j))),
            pl.BlockSpec((1, tn), lambda i, j: (0, build(j))),
            pl.BlockSpec((tm, ple), lambda i, j: (i, 0)),
            pl.BlockSpec((ple, tn), lambda i, j: (0, build(j))),
            pl.BlockSpec((tm, tn), lambda i, j: (i, build(j))),
            pl.BlockSpec((tm, 1), lambda i, j: (i, 0)),
            pl.BlockSpec((tm, 1), lambda i, j: (i, 0)),
            pl.BlockSpec((1, tn), lambda i, j: (0, build(j))),
            pl.BlockSpec((1, tn), lambda i, j: (0, build(j))),
            pl.BlockSpec((tm, tn), lambda i, j: (i, build(j))),
            pl.BlockSpec((1, tn), lambda i, j: (0, emit(j))),
            pl.BlockSpec((1, tn), lambda i, j: (0, emit(j))),
        ],
        out_specs=pl.BlockSpec((tm, tn), lambda i, j: (i, emit(j))),
        scratch_shapes=[pltpu.VMEM((tm, d // 2), BF16), pltpu.VMEM((tm, d // 2), BF16),
                        pltpu.VMEM((n_chunks, tm, tn), F32),
                        pltpu.VMEM((tm, 1), F32), pltpu.VMEM((tm, 1), F32)],
        compiler_params=_cparams(("arbitrary", "arbitrary")),
        name="ple_moe_ln2",
    )(x1s, w_pg_b, b_pg, p2, w_pp, y1, mu1, rs1, ln1_g, ln1_b, moe, ln2_g, ln2_b)


def _routing_tables(ids, counts_row):
    counts = counts_row[EXPERT_LANE0:EXPERT_LANE0 + N_EXPERTS]
    tiles = (counts + TM_MOE - 1) // TM_MOE
    tile_end = jnp.cumsum(tiles)
    offsets = (tile_end - tiles) * TM_MOE
    n_used = tile_end[-1:]
    experts = ids[:, 0:EXPERT_TOP_K]
    slots = offsets[experts] + ids[:, EXPERT_TOP_K:2 * EXPERT_TOP_K]
    n_tiles = (EXPERT_TOP_K * ids.shape[0]) // TM_MOE + N_EXPERTS
    tile_ids = jnp.minimum(jnp.arange(n_tiles, dtype=I32), n_used - 1)
    tile_expert = jnp.sum(tile_end[None, :] <= tile_ids[:, None], axis=1).astype(I32)
    slab_of_slot = jnp.zeros((n_tiles * TM_MOE,), I32).at[slots.reshape(-1)].set(
        jnp.arange(slots.size, dtype=I32) // EXPERT_TOP_K * WORD_ROWS, unique_indices=True)
    return slots.astype(I32), slab_of_slot, tile_expert, n_used.astype(I32)


def kernel(x, p, positions, w_in, b_in, sinks, g_norm_a, g_norm_b, w_out, b_out,
           ln1_g, ln1_b, w_group, b_group, w_er, b_er, w_gate, w_up, w_down,
           w_ple_gate, b_ple_gate, w_ple_proj, ln2_g, ln2_b):
    batch, seq, d = x.shape
    m = batch * seq
    row = lambda v: v.reshape(1, -1)
    x2 = x.reshape(m, d)
    for i in range(DEPTH):
        proj = _in_projection(x2, _mixer_b_first(w_in[i]).astype(BF16),
                              row(_mixer_b_first(b_in[i])))
        inv_freq = ROPE_THETA ** (-jnp.arange(0, SWA_HEAD_DIM, 2, dtype=F32) / SWA_HEAD_DIM)
        inv_freq = jnp.tile(inv_freq, LANES // inv_freq.shape[0]).reshape(1, LANES)
        o_a = _swa_attention(proj, positions.reshape(m, 1), inv_freq, sinks[i], batch, seq)
        o_b = _sb_attention(proj, batch, seq)
        y1, mu1, rs1, x1s = _out_projection(
            o_a, o_b, row(g_norm_a[i]), row(g_norm_b[i]), w_out[i].astype(BF16), x2,
            row(b_out[i]), row(ln1_g[i]), row(ln1_b[i]))
        pad = ROUTER_COLS - N_GROUPS - N_EXPERTS
        w_r = jnp.concatenate(
            [w_group[i], w_er[i].transpose(1, 0, 2).reshape(d, N_EXPERTS),
             jnp.zeros((d, pad), F32)], axis=1).astype(BF16)
        b_r = jnp.concatenate([b_group[i], b_er[i].reshape(-1), jnp.zeros((pad,), F32)])
        ids, wts, counts = _router(x1s, w_r, row(b_r))
        slots, slab_of_slot, tile_expert, n_used = _routing_tables(ids, counts[0])
        act = _grouped_gate_up(tile_expert, n_used, slab_of_slot, x1s, w_gate[i], w_up[i])
        y_sorted = _grouped_down(tile_expert, n_used, act, w_down[i])
        moe = _combine_experts(slots.T.reshape(-1), y_sorted, wts)
        x2 = _final_stage(x1s, w_ple_gate[i].astype(BF16), row(b_ple_gate[i]),
                          p[i].reshape(m, PLE_DIM), w_ple_proj[i], y1, mu1, rs1,
                          row(ln1_g[i]), row(ln1_b[i]), moe, row(ln2_g[i]), row(ln2_b[i]))
    return x2.reshape(batch, seq, d)
```

```python
import math

import jax
import jax.numpy as jnp
from jax import lax
from jax.experimental import pallas as pl
from jax.experimental.pallas import tpu as pltpu

F32 = jnp.float32
BF16 = jnp.bfloat16
I32 = jnp.int32
U32 = jnp.uint32

D_MODEL = 4096
PLE_DIM = 256
BLOCK = 128
ROPE_THETA = 10000.0
LN_EPS = 1e-5
RMS_EPS = 1e-6
NEG_INF = -1e30

SWA_HEAD_DIM = 64
SWA_WIDTH = D_MODEL // 2
SWA_HEADS = SWA_WIDTH // SWA_HEAD_DIM
SWA_KV_HEADS = SWA_HEADS // 8
SWA_GROUP = SWA_HEADS // SWA_KV_HEADS
SWA_KV_WIDTH = SWA_KV_HEADS * SWA_HEAD_DIM

SB_HEAD_DIM = 128
SB_WIDTH = D_MODEL - SWA_WIDTH
SB_HEADS = SB_WIDTH // SB_HEAD_DIM

IN_WIDTH = SWA_WIDTH + 2 * SWA_KV_WIDTH + 3 * SB_WIDTH
A_WIDTH = SWA_WIDTH + 2 * SWA_KV_WIDTH
Q_B_COL = 0
K_B_COL = Q_B_COL + SB_WIDTH
V_B_COL = K_B_COL + SB_WIDTH
Q_A_COL = V_B_COL + SB_WIDTH
K_A_COL = Q_A_COL + SWA_WIDTH
V_A_COL = K_A_COL + SWA_KV_WIDTH

N_GROUPS = 4
EXPERTS_PER_GROUP = 8
N_EXPERTS = N_GROUPS * EXPERTS_PER_GROUP
EXPERT_TOP_K = 2
DEPTH = 1
DEEPNORM_ALPHA = (2.0 * DEPTH) ** 0.25

LANES = 128
VMEM_LIMIT_BYTES = 56 * 1024 * 1024

EXP_ZERO_LOG = -104.0 - 2.0

TM_IN = 1024
TM_FINAL = 512
TM_RES = 256
TN_RES = 1024
TM_PROJ = 512
TN_PROJ = 512
TM_MOE = 256
TM_COMBINE = 128
SB_HEADS_PER_STEP = 8
ROW_CHUNK = 16


def _cparams(sem):
    return pltpu.CompilerParams(dimension_semantics=sem,
                                vmem_limit_bytes=VMEM_LIMIT_BYTES)


def _pack_halves(lo, hi):
    return lax.bitcast_convert_type(
        pltpu.pack_elementwise([lo, hi], packed_dtype=BF16), U32)


def _unpack_halves(words):
    return tuple(
        pltpu.unpack_elementwise(words, index=k, packed_dtype=BF16,
                                 unpacked_dtype=F32).astype(BF16) for k in (0, 1))


def _dot_halves(lo, hi, w_ref):
    half = lo.shape[1]
    return (jnp.dot(lo, w_ref[:half], preferred_element_type=F32)
            + jnp.dot(hi, w_ref[half:], preferred_element_type=F32))


WORD_ROWS = (D_MODEL // 2) // LANES
SLAB_PAIR = 2 * LANES


def _slab_words(slab_ref, pair, tokens):
    return jnp.concatenate(
        [slab_ref[pl.ds(2 * pair + k, tokens, stride=WORD_ROWS), :] for k in (0, 1)],
        axis=1)


def _dot_slab(slab_ref, tokens, w_refs):
    half = D_MODEL // 2
    outs = [None] * len(w_refs)
    for pair in range(half // SLAB_PAIR):
        lo, hi = _unpack_halves(_slab_words(slab_ref, pair, tokens))
        rows_lo = slice(pair * SLAB_PAIR, (pair + 1) * SLAB_PAIR)
        rows_hi = slice(half + pair * SLAB_PAIR, half + (pair + 1) * SLAB_PAIR)
        for n, w_ref in enumerate(w_refs):
            part = (jnp.dot(lo, w_ref[rows_lo], preferred_element_type=F32)
                    + jnp.dot(hi, w_ref[rows_hi], preferred_element_type=F32))
            outs[n] = part if outs[n] is None else outs[n] + part
    return outs


def _dot_nt(a, b):
    return lax.dot_general(a, b, (((1,), (1,)), ((), ())),
                           preferred_element_type=F32)


def _inproj_kernel(x_ref, w_ref, b_ref, o_ref, xb_ref):
    @pl.when(pl.program_id(1) == 0)
    def _():
        xb_ref[...] = x_ref[...].astype(BF16)

    acc = jnp.dot(xb_ref[...], w_ref[...], preferred_element_type=F32)
    o_ref[...] = (acc + b_ref[...]).astype(o_ref.dtype)


def _in_projection(x2, w_b, b):
    m, k = x2.shape
    n = w_b.shape[1]
    n_blocks = n // TN_PROJ
    assert A_WIDTH % TN_PROJ == 0
    a_blocks = A_WIDTH // TN_PROJ

    def src(j):
        return lax.rem(j + a_blocks, n_blocks)

    return pl.pallas_call(
        _inproj_kernel,
        out_shape=jax.ShapeDtypeStruct((m, n), BF16),
        grid=(m // TM_IN, n_blocks),
        in_specs=[
            pl.BlockSpec((TM_IN, k), lambda i, j: (i, 0)),
            pl.BlockSpec((k, TN_PROJ), lambda i, j: (0, src(j))),
            pl.BlockSpec((1, TN_PROJ), lambda i, j: (0, src(j))),
        ],
        out_specs=pl.BlockSpec((TM_IN, TN_PROJ), lambda i, j: (i, j)),
        scratch_shapes=[pltpu.VMEM((TM_IN, k), BF16)],
        compiler_params=_cparams(("arbitrary", "arbitrary")),
        name="in_projection",
    )(x2, w_b, b)


def _swa_kernel(sinks_ref, q_ref, kc_ref, kp_ref, vc_ref, vp_ref,
                posc_ref, posp_ref, invf_ref, o_ref):
    n = pl.program_id(1)
    lane = lax.broadcasted_iota(I32, (1, LANES), 1)
    first_half = (lane % SWA_HEAD_DIM) < (SWA_HEAD_DIM // 2)

    def tables(pos_ref):
        ang = pos_ref[...].astype(F32) * invf_ref[...]
        sin = jnp.sin(ang)
        return jnp.cos(ang), jnp.where(first_half, -sin, sin)

    def rope(x, cos, sin_signed):
        partner = jnp.where(first_half,
                            pltpu.roll(x, LANES - SWA_HEAD_DIM // 2, 1),
                            pltpu.roll(x, SWA_HEAD_DIM // 2, 1))
        return x * cos + partner * sin_signed

    cos_c, sin_c = tables(posc_ref)
    cos_p, sin_p = tables(posp_ref)

    def rope_block(ref, cos, sin_signed):
        width = ref.shape[1]
        return [rope(ref[:, c * LANES:(c + 1) * LANES].astype(F32), cos,
                     sin_signed).astype(BF16) for c in range(width // LANES)]

    q_chunks = rope_block(q_ref, cos_c, sin_c)
    k_chunks = [jnp.concatenate([p_, c_], axis=0)
                for p_, c_ in zip(rope_block(kp_ref, cos_p, sin_p),
                                  rope_block(kc_ref, cos_c, sin_c))]
    v_all = jnp.concatenate([vp_ref[...], vc_ref[...]], axis=0)

    qi = lax.broadcasted_iota(I32, (BLOCK, 2 * BLOCK), 0)
    kj = lax.broadcasted_iota(I32, (BLOCK, 2 * BLOCK), 1)
    rel = qi - (kj - BLOCK)
    valid = (rel >= 0) & (rel < BLOCK) & ((kj >= BLOCK) | (n > 0))

    def head_slice(chunks, head):
        half = head % 2
        return chunks[head // 2][:, half * SWA_HEAD_DIM:(half + 1) * SWA_HEAD_DIM]

    scale = 1.0 / math.sqrt(SWA_HEAD_DIM)
    for h in range(SWA_KV_HEADS):
        k_h = head_slice(k_chunks, h)
        v_h = v_all[:, h * SWA_HEAD_DIM:(h + 1) * SWA_HEAD_DIM]
        q_h = jnp.concatenate(
            [head_slice(q_chunks, h * SWA_GROUP + g) for g in range(SWA_GROUP)],
            axis=0)
        s = _dot_nt(q_h, k_h) * scale
        probs = []
        for g in range(SWA_GROUP):
            sink = sinks_ref[h * SWA_GROUP + g]
            s_g = jnp.where(valid, s[g * BLOCK:(g + 1) * BLOCK], NEG_INF)
            m = jnp.maximum(jnp.max(s_g, axis=-1, keepdims=True), sink)
            e = jnp.exp(s_g - m)
            den = jnp.sum(e, axis=-1, keepdims=True) + jnp.exp(sink - m)
            probs.append((e / den).astype(BF16))
        o_h = jnp.dot(jnp.concatenate(probs, axis=0), v_h,
                      preferred_element_type=F32)
        for g in range(SWA_GROUP):
            col = (h * SWA_GROUP + g) * SWA_HEAD_DIM
            o_ref[:, col:col + SWA_HEAD_DIM] = o_h[g * BLOCK:(g + 1) * BLOCK]


def _swa_attention(proj, pos2, inv_freq, sinks, batch, seq):
    nb = seq // BLOCK
    assert Q_A_COL % SWA_WIDTH == 0 and K_A_COL % SWA_KV_WIDTH == 0
    assert V_A_COL % SWA_KV_WIDTH == 0
    qcol = Q_A_COL // SWA_WIDTH
    kcol = K_A_COL // SWA_KV_WIDTH
    vcol = V_A_COL // SWA_KV_WIDTH

    def cur(b, n):
        return b * nb + n

    def prev(b, n):
        return b * nb + jnp.maximum(n - 1, 0)

    return pl.pallas_call(
        _swa_kernel,
        out_shape=jax.ShapeDtypeStruct((batch * seq, SWA_WIDTH), F32),
        grid=(batch, nb),
        in_specs=[
            pl.BlockSpec(memory_space=pltpu.SMEM),
            pl.BlockSpec((BLOCK, SWA_WIDTH), lambda b, n: (cur(b, n), qcol)),
            pl.BlockSpec((BLOCK, SWA_KV_WIDTH), lambda b, n: (cur(b, n), kcol)),
            pl.BlockSpec((BLOCK, SWA_KV_WIDTH), lambda b, n: (prev(b, n), kcol)),
            pl.BlockSpec((BLOCK, SWA_KV_WIDTH), lambda b, n: (cur(b, n), vcol)),
            pl.BlockSpec((BLOCK, SWA_KV_WIDTH), lambda b, n: (prev(b, n), vcol)),
            pl.BlockSpec((BLOCK, 1), lambda b, n: (cur(b, n), 0)),
            pl.BlockSpec((BLOCK, 1), lambda b, n: (prev(b, n), 0)),
            pl.BlockSpec((1, LANES), lambda b, n: (0, 0)),
        ],
        out_specs=pl.BlockSpec((BLOCK, SWA_WIDTH), lambda b, n: (cur(b, n), 0)),
        compiler_params=_cparams(("arbitrary", "arbitrary")),
        name="swa_attention",
    )(sinks, proj, proj, proj, proj, proj, pos2, pos2, inv_freq)


def _sb_kernel(q_ref, k_ref, v_ref, o_ref):
    n = pl.program_id(2)
    heads = [slice(h * SB_HEAD_DIM, (h + 1) * SB_HEAD_DIM)
             for h in range(SB_HEADS_PER_STEP)]
    rows = SB_HEADS_PER_STEP * BLOCK
    scale = 1.0 / math.sqrt(SB_HEAD_DIM)
    key_j = lax.broadcasted_iota(I32, (BLOCK, 2 * BLOCK), 0)
    out_c = lax.broadcasted_iota(I32, (BLOCK, 2 * BLOCK), 1)
    later_total = ((key_j > out_c) | (out_c >= BLOCK)).astype(BF16)
    q_row = lax.broadcasted_iota(I32, (rows, BLOCK), 0) % BLOCK
    k_col = lax.broadcasted_iota(I32, (rows, BLOCK), 1)
    causal = k_col < q_row

    def block(kb, carry, acc, mask):
        start = pl.multiple_of(kb * BLOCK, BLOCK)
        z = jnp.concatenate(
            [_dot_nt(q_ref[:, hd], k_ref[pl.ds(start, BLOCK), hd]) for hd in heads],
            axis=0) * scale
        t = jnp.log(1.0 + jnp.exp(-jnp.abs(z)))
        log_not = jnp.minimum(-z, 0.0) - t
        log_beta = jnp.minimum(z, 0.0) - t
        if mask is not None:
            log_not = jnp.where(mask, log_not, 0.0)
        hi = log_not.astype(BF16)
        r1 = log_not - hi.astype(F32)
        mid = r1.astype(BF16)
        lo = (r1 - mid.astype(F32)).astype(BF16)
        parts = jnp.dot(jnp.concatenate([hi, mid, lo], axis=0), later_total,
                        preferred_element_type=F32)
        sums = (parts[:rows] + parts[rows:2 * rows]) + parts[2 * rows:]
        a = jnp.exp(log_beta + sums[:, :BLOCK] + carry)
        if mask is not None:
            a = jnp.where(mask, a, 0.0)
        a = a.astype(BF16)
        pv = jnp.concatenate(
            [jnp.dot(a[h * BLOCK:(h + 1) * BLOCK], v_ref[pl.ds(start, BLOCK), hd],
                     preferred_element_type=F32) for h, hd in enumerate(heads)], axis=0)
        return carry + sums[:, BLOCK:], acc + pv

    zeros = jnp.zeros((rows, BLOCK), F32)
    carry, acc = block(n, zeros, zeros, causal)

    def cond(state):
        kb, carry, _ = state
        return jnp.logical_and(kb >= 0, jnp.max(carry) > EXP_ZERO_LOG)

    def body(state):
        kb, carry, acc = state
        carry, acc = block(kb, carry, acc, None)
        return kb - 1, carry, acc

    _, _, acc = lax.while_loop(cond, body, (n - 1, carry, acc))
    for h, hd in enumerate(heads):
        o_ref[:, hd] = acc[h * BLOCK:(h + 1) * BLOCK]


def _sb_attention(proj, batch, seq):
    nb = seq // BLOCK
    width = SB_HEADS_PER_STEP * SB_HEAD_DIM
    assert Q_B_COL % width == 0 and K_B_COL % width == 0 and V_B_COL % width == 0
    qcol = Q_B_COL // width
    kcol = K_B_COL // width
    vcol = V_B_COL // width
    return pl.pallas_call(
        _sb_kernel,
        out_shape=jax.ShapeDtypeStruct((batch * seq, SB_WIDTH), F32),
        grid=(batch, SB_HEADS // SB_HEADS_PER_STEP, nb),
        in_specs=[
            pl.BlockSpec((BLOCK, width), lambda b, h, n: (b * nb + n, qcol + h)),
            pl.BlockSpec((seq, width), lambda b, h, n: (b, kcol + h)),
            pl.BlockSpec((seq, width), lambda b, h, n: (b, vcol + h)),
        ],
        out_specs=pl.BlockSpec((BLOCK, width), lambda b, h, n: (b * nb + n, h)),
        compiler_params=_cparams(("arbitrary", "arbitrary", "arbitrary")),
        name="sb_attention",
    )(proj, proj, proj)


def _row_stats(chunks_ref, n_chunks, width):
    total = chunks_ref[0].sum(axis=-1, keepdims=True)
    for c in range(1, n_chunks):
        total = total + chunks_ref[c].sum(axis=-1, keepdims=True)
    mu = total / width
    sq = jnp.square(chunks_ref[0] - mu).sum(axis=-1, keepdims=True)
    for c in range(1, n_chunks):
        sq = sq + jnp.square(chunks_ref[c] - mu).sum(axis=-1, keepdims=True)
    return mu, lax.rsqrt(sq / width + LN_EPS)


def _outproj_kernel(oa_ref, ob_ref, ga_ref, gb_ref, w_ref, x_ref, bo_ref,
                    lg_ref, lb_ref, y_ref, mu_ref, rs_ref, x1s_ref,
                    ma_ref, mb_ref, acc_ref):
    j = pl.program_id(1)
    n_chunks = acc_ref.shape[0]
    tn = acc_ref.shape[2]

    @pl.when(j == 0)
    def _():
        for o_ref, g_ref, m_ref in ((oa_ref, ga_ref, ma_ref), (ob_ref, gb_ref, mb_ref)):
            width = o_ref.shape[1]
            cols = [slice(c * tn, (c + 1) * tn) for c in range(width // tn)]
            sq = sum(jnp.square(o_ref[:, sl]).sum(axis=-1, keepdims=True) for sl in cols)
            r = lax.rsqrt(sq / width + RMS_EPS)
            for sl in cols:
                m_ref[:, sl] = (o_ref[:, sl] * r * g_ref[:, sl]).astype(BF16)

    half = ma_ref.shape[1]
    mix = (jnp.dot(ma_ref[...], w_ref[j, :half, :], preferred_element_type=F32)
           + jnp.dot(mb_ref[...], w_ref[j, half:, :], preferred_element_type=F32))
    y = DEEPNORM_ALPHA * x_ref[...] + (mix + bo_ref[...])
    y_ref[...] = y
    acc_ref[j] = y

    @pl.when(j == n_chunks - 1)
    def _():
        mu, rs = _row_stats(acc_ref, n_chunks, n_chunks * tn)
        mu_ref[...] = mu
        rs_ref[...] = rs

        def normed(c):
            sl = slice(c * tn, (c + 1) * tn)
            return (acc_ref[c] - mu) * rs * lg_ref[:, sl] + lb_ref[:, sl]

        tm = acc_ref.shape[1]
        for c in range(n_chunks // 2):
            words = _pack_halves(normed(c), normed(c + n_chunks // 2))
            for q in range(tn // LANES):
                x1s_ref[pl.ds(c * (tn // LANES) + q, tm, stride=WORD_ROWS), :] = (
                    words[:, q * LANES:(q + 1) * LANES])


def _column_chunks(w, tn):
    k, n = w.shape
    return w.astype(BF16).reshape(k, n // tn, tn).transpose(1, 0, 2)


def _out_projection(o_a, o_b, g_a, g_b, w_chunks, x2, b_out, ln_g, ln_b):
    m, d = x2.shape
    tm, tn = TM_RES, w_chunks.shape[2]
    half = o_a.shape[1]
    once = dict(pipeline_mode=pl.Buffered(1))
    return pl.pallas_call(
        _outproj_kernel,
        out_shape=(jax.ShapeDtypeStruct((m, d), F32),
                   jax.ShapeDtypeStruct((m, 1), F32),
                   jax.ShapeDtypeStruct((m, 1), F32),
                   jax.ShapeDtypeStruct((m * WORD_ROWS, LANES), U32)),
        grid=(m // tm, d // tn),
        in_specs=[
            pl.BlockSpec((tm, half), lambda i, j: (i, 0), **once),
            pl.BlockSpec((tm, half), lambda i, j: (i, 0), **once),
            pl.BlockSpec((1, half), lambda i, j: (0, 0)),
            pl.BlockSpec((1, half), lambda i, j: (0, 0)),
            pl.BlockSpec((d // tn, d, tn), lambda i, j: (0, 0, 0), **once),
            pl.BlockSpec((tm, tn), lambda i, j: (i, j)),
            pl.BlockSpec((1, tn), lambda i, j: (0, j)),
            pl.BlockSpec((1, d), lambda i, j: (0, 0)),
            pl.BlockSpec((1, d), lambda i, j: (0, 0)),
        ],
        out_specs=(pl.BlockSpec((tm, tn), lambda i, j: (i, j)),
                   pl.BlockSpec((tm, 1), lambda i, j: (i, 0)),
                   pl.BlockSpec((tm, 1), lambda i, j: (i, 0)),
                   pl.BlockSpec((tm * WORD_ROWS, LANES), lambda i, j: (i, 0))),
        scratch_shapes=[pltpu.VMEM((tm, half), BF16), pltpu.VMEM((tm, half), BF16),
                        pltpu.VMEM((d // tn, tm, tn), F32)],
        compiler_params=_cparams(("arbitrary", "arbitrary")),
        name="out_projection_ln1",
    )(o_a, o_b, g_a, g_b, w_chunks, x2, b_out, ln_g, ln_b)


ROUTER_COLS = LANES
EXPERT_LANE0 = N_GROUPS


def _router_kernel(x_ref, w_ref, b_ref, ids_ref, wts_ref, cnt_ref, carry_ref):
    i = pl.program_id(0)
    tm = wts_ref.shape[0]

    @pl.when(i == 0)
    def _():
        carry_ref[...] = jnp.zeros_like(carry_ref)

    logits = _dot_slab(x_ref, tm, [w_ref])[0] + b_ref[...]
    lane = lax.broadcasted_iota(I32, (tm, ROUTER_COLS), 1)
    big = jnp.int32(ROUTER_COLS)

    def first_argmax(vals):
        top = jnp.max(vals, axis=-1, keepdims=True)
        idx = jnp.min(jnp.where(vals == top, lane, big), axis=-1, keepdims=True)
        return top, idx

    is_group = lane < N_GROUPS
    g_logits = jnp.where(is_group, logits, -jnp.inf)
    g_top, g_idx = first_argmax(g_logits)
    g_w = 1.0 / jnp.sum(jnp.exp(g_logits - g_top), axis=-1, keepdims=True)

    first = EXPERT_LANE0 + g_idx * EXPERTS_PER_GROUP
    in_group = (lane >= first) & (lane < first + EXPERTS_PER_GROUP)
    e_logits = jnp.where(in_group, logits, -jnp.inf)
    top1, idx1 = first_argmax(e_logits)
    top2, idx2 = first_argmax(jnp.where(lane == idx1, -jnp.inf, e_logits))
    e2 = jnp.exp(top2 - top1)
    w1 = g_w / (1.0 + e2)
    w2 = g_w * e2 / (1.0 + e2)

    hit1 = lane == idx1
    hit2 = lane == idx2
    onehot = (hit1 | hit2).astype(BF16)
    r = lax.broadcasted_iota(I32, (tm, tm), 0)
    c = lax.broadcasted_iota(I32, (tm, tm), 1)
    before = (c < r).astype(BF16)
    prior = jnp.dot(before, onehot, preferred_element_type=F32) + carry_ref[0:1, :]
    rank1 = jnp.sum(jnp.where(hit1, prior, 0.0), axis=-1, keepdims=True)
    rank2 = jnp.sum(jnp.where(hit2, prior, 0.0), axis=-1, keepdims=True)
    counts = carry_ref[0:1, :] + jnp.sum(onehot.astype(F32), axis=0, keepdims=True)
    carry_ref[...] = jnp.broadcast_to(counts, carry_ref.shape)
    cnt_ref[...] = jnp.broadcast_to(counts, cnt_ref.shape).astype(I32)

    ids = jnp.where(lane == 0, idx1 - EXPERT_LANE0,
          jnp.where(lane == 1, idx2 - EXPERT_LANE0,
          jnp.where(lane == 2, rank1.astype(I32),
          jnp.where(lane == 3, rank2.astype(I32), 0))))
    ids_ref[...] = ids.T[:ids_ref.shape[0]]
    wts_ref[...] = jnp.where(lane == 0, w1, jnp.where(lane == 1, w2, 0.0))


def _router(x1s, w_r, b_r):
    m = x1s.shape[0] // WORD_ROWS
    tm = TM_PROJ
    return pl.pallas_call(
        _router_kernel,
        out_shape=(jax.ShapeDtypeStruct((8, m), I32),
                   jax.ShapeDtypeStruct((m, ROUTER_COLS), F32),
                   jax.ShapeDtypeStruct((8, ROUTER_COLS), I32)),
        grid=(m // tm,),
        in_specs=[
            pl.BlockSpec((tm * WORD_ROWS, LANES), lambda i: (i, 0)),
            pl.BlockSpec((D_MODEL, ROUTER_COLS), lambda i: (0, 0)),
            pl.BlockSpec((1, ROUTER_COLS), lambda i: (0, 0)),
        ],
        out_specs=(pl.BlockSpec((8, tm), lambda i: (0, i)),
                   pl.BlockSpec((tm, ROUTER_COLS), lambda i: (i, 0)),
                   pl.BlockSpec((8, ROUTER_COLS), lambda i: (0, 0))),
        scratch_shapes=[pltpu.VMEM((8, ROUTER_COLS), F32)],
        compiler_params=_cparams(("arbitrary",)),
        name="router",
    )(x1s, w_r, b_r)


def _start_row_gather(idx_ref, first, n_items, span, src_ref, dst_ref, sem):
    def body(c, _):
        for u in range(ROW_CHUNK):
            r = c * ROW_CHUNK + u
            src_row = pl.multiple_of(idx_ref[first + r], span)
            dst_row = pl.multiple_of(r * span, span)
            pltpu.make_async_copy(src_ref.at[pl.ds(src_row, span)],
                                  dst_ref.at[pl.ds(dst_row, span)], sem).start()
        return 0

    lax.fori_loop(0, n_items // ROW_CHUNK, body, 0)


def _wait_row_gather(n_items, span, src_ref, dst_ref, sem):
    def body(c, _):
        for _u in range(ROW_CHUNK):
            pltpu.make_async_copy(src_ref.at[pl.ds(0, span)], dst_ref.at[pl.ds(0, span)],
                                  sem).wait()
        return 0

    lax.fori_loop(0, n_items // ROW_CHUNK, body, 0)


def _expert_changed(te_ref, i):
    return jnp.logical_or(i == 0, te_ref[i] != te_ref[jnp.maximum(i - 1, 0)])


def _tile_row(i, nu):
    return jnp.minimum(i, nu[0] - 1)


def _gateup_kernel(te_ref, nu_ref, tos_ref, x_hbm, wg_ref, wu_ref, a_ref,
                   xbuf_ref, wgb_ref, wub_ref, sems):
    i = pl.program_id(0)
    n_used = nu_ref[0]
    slot = lax.rem(i, 2)

    def start(tile, buf):
        _start_row_gather(tos_ref, tile * TM_MOE, TM_MOE, WORD_ROWS, x_hbm,
                          xbuf_ref.at[buf], sems.at[buf])

    @pl.when(i == 0)
    def _():
        start(0, 0)

    @pl.when(i + 1 < n_used)
    def _():
        start(i + 1, 1 - slot)

    @pl.when(_expert_changed(te_ref, i))
    def _():
        wgb_ref[...] = wg_ref[0].astype(BF16)
        wub_ref[...] = wu_ref[0].astype(BF16)

    @pl.when(i < n_used)
    def _():
        _wait_row_gather(TM_MOE, WORD_ROWS, x_hbm, xbuf_ref.at[slot], sems.at[slot])
        gate, up = _dot_slab(xbuf_ref.at[slot], TM_MOE, [wgb_ref, wub_ref])
        a_ref[...] = (gate * jax.nn.sigmoid(gate) * up).astype(a_ref.dtype)

    @pl.when(i >= n_used)
    def _():
        a_ref[...] = jnp.zeros_like(a_ref)


def _grouped_gate_up(tile_expert, n_used, slab_of_slot, x1s, w_gate, w_up):
    p_rows = slab_of_slot.shape[0]
    _, d, f = w_gate.shape
    return pl.pallas_call(
        _gateup_kernel,
        out_shape=jax.ShapeDtypeStruct((p_rows, f), BF16),
        grid_spec=pltpu.PrefetchScalarGridSpec(
            num_scalar_prefetch=3,
            grid=(p_rows // TM_MOE,),
            in_specs=[
                pl.BlockSpec(memory_space=pl.ANY),
                pl.BlockSpec((1, d, f), lambda i, te, nu, tos: (te[i], 0, 0)),
                pl.BlockSpec((1, d, f), lambda i, te, nu, tos: (te[i], 0, 0)),
            ],
            out_specs=pl.BlockSpec((TM_MOE, f), lambda i, te, nu, tos: (i, 0)),
            scratch_shapes=[pltpu.VMEM((2, TM_MOE * WORD_ROWS, LANES), U32),
                            pltpu.VMEM((d, f), BF16), pltpu.VMEM((d, f), BF16),
                            pltpu.SemaphoreType.DMA((2,))],
        ),
        compiler_params=_cparams(("arbitrary",)),
        name="moe_gate_up",
    )(tile_expert, n_used, slab_of_slot, x1s, w_gate, w_up)


def _down_kernel(te_ref, nu_ref, a_ref, wd_ref, y_ref, wdb_ref):
    i = pl.program_id(0)

    @pl.when(_expert_changed(te_ref, i))
    def _():
        wdb_ref[...] = wd_ref[0].astype(BF16)

    @pl.when(i < nu_ref[0])
    def _():
        y_ref[:, 0, :] = jnp.dot(a_ref[...], wdb_ref[...], preferred_element_type=F32)

    @pl.when(i >= nu_ref[0])
    def _():
        y_ref[...] = jnp.zeros_like(y_ref)


def _grouped_down(tile_expert, n_used, act, w_down):
    p_rows, f = act.shape
    d = w_down.shape[2]
    return pl.pallas_call(
        _down_kernel,
        out_shape=jax.ShapeDtypeStruct((p_rows, 1, d), F32),
        grid_spec=pltpu.PrefetchScalarGridSpec(
            num_scalar_prefetch=2,
            grid=(p_rows // TM_MOE,),
            in_specs=[
                pl.BlockSpec((TM_MOE, f), lambda i, te, nu: (_tile_row(i, nu), 0)),
                pl.BlockSpec((1, f, d), lambda i, te, nu: (te[i], 0, 0)),
            ],
            out_specs=pl.BlockSpec((TM_MOE, 1, d), lambda i, te, nu: (i, 0, 0)),
            scratch_shapes=[pltpu.VMEM((f, d), BF16)],
        ),
        compiler_params=_cparams(("arbitrary",)),
        name="moe_down",
    )(tile_expert, n_used, act, w_down)


def _combine_kernel(slot_ref, y_hbm, wts_ref, o_ref, buf_ref, sems):
    i = pl.program_id(0)
    tm = o_ref.shape[0]
    n_tokens = slot_ref.shape[0] // EXPERT_TOP_K
    slot = lax.rem(i, 2)

    def start(tile, buf):
        for k in range(EXPERT_TOP_K):
            _start_row_gather(slot_ref, k * n_tokens + tile * tm, tm, 1, y_hbm,
                              buf_ref.at[buf, k], sems.at[buf])

    @pl.when(i == 0)
    def _():
        start(0, 0)

    @pl.when(i + 1 < pl.num_programs(0))
    def _():
        start(i + 1, 1 - slot)

    _wait_row_gather(EXPERT_TOP_K * tm, 1, y_hbm, buf_ref.at[slot, 0], sems.at[slot])
    for c in range(o_ref.shape[1] // TN_PROJ):
        sl = slice(c * TN_PROJ, (c + 1) * TN_PROJ)
        o_ref[:, sl] = sum(wts_ref[:, k:k + 1] * buf_ref[slot, k, :, 0, sl]
                           for k in range(EXPERT_TOP_K))


def _combine_experts(slot_kt, y_sorted, wts):
    d = y_sorted.shape[2]
    m = slot_kt.shape[0] // EXPERT_TOP_K
    tm = TM_COMBINE
    return pl.pallas_call(
        _combine_kernel,
        out_shape=jax.ShapeDtypeStruct((m, d), F32),
        grid_spec=pltpu.PrefetchScalarGridSpec(
            num_scalar_prefetch=1,
            grid=(m // tm,),
            in_specs=[
                pl.BlockSpec(memory_space=pl.ANY),
                pl.BlockSpec((tm, ROUTER_COLS), lambda i, sl: (i, 0)),
            ],
            out_specs=pl.BlockSpec((tm, d), lambda i, sl: (i, 0)),
            scratch_shapes=[pltpu.VMEM((2, EXPERT_TOP_K, tm, 1, d), F32),
                            pltpu.SemaphoreType.DMA((2,))],
        ),
        compiler_params=_cparams(("arbitrary",)),
        name="moe_combine",
    )(slot_kt, y_sorted, wts)


def _final_kernel(x1s_ref, wg_ref, bg_ref, p_ref, wp_ref, y1_ref, mu_ref, rs_ref,
                  l1g_ref, l1b_ref, moe_ref, l2g_ref, l2b_ref,
                  o_ref, lo_ref, hi_ref, acc_ref, mu2_ref, rs2_ref):
    j = pl.program_id(1)
    n_chunks = acc_ref.shape[0]
    tn = acc_ref.shape[2]

    @pl.when(j == 0)
    def _():
        for s in range(WORD_ROWS):
            sl = slice(s * LANES, (s + 1) * LANES)
            lo_ref[:, sl], hi_ref[:, sl] = _unpack_halves(
                x1s_ref[pl.ds(s, lo_ref.shape[0], stride=WORD_ROWS), :])

    @pl.when(j < n_chunks)
    def _():
        gate = _dot_halves(lo_ref[...], hi_ref[...], wg_ref.at[j]) + bg_ref[...]
        emb = jnp.dot(p_ref[...].astype(BF16), wp_ref[...].astype(BF16),
                      preferred_element_type=F32)
        x1 = (y1_ref[...] - mu_ref[...]) * rs_ref[...] * l1g_ref[...] + l1b_ref[...]
        acc_ref[j] = DEEPNORM_ALPHA * x1 + moe_ref[...] + jax.nn.sigmoid(gate) * emb

    @pl.when(j == n_chunks)
    def _():
        mu2_ref[...], rs2_ref[...] = _row_stats(acc_ref, n_chunks, n_chunks * tn)

    @pl.when(j >= n_chunks)
    def _():
        o_ref[...] = ((acc_ref[j - n_chunks] - mu2_ref[...]) * rs2_ref[...]
                      * l2g_ref[...] + l2b_ref[...])


def _final_stage(x1s, w_pg_chunks, b_pg, p2, w_pp, y1, mu1, rs1, ln1_g, ln1_b,
                 moe, ln2_g, ln2_b):
    m, d = y1.shape
    tm, tn = TM_RES, w_pg_chunks.shape[2]
    ple = p2.shape[1]
    n_chunks = d // tn

    def build(j):
        return jnp.minimum(j, n_chunks - 1)

    def emit(j):
        return jnp.maximum(j - n_chunks, 0)

    return pl.pallas_call(
        _final_kernel,
        out_shape=jax.ShapeDtypeStruct((m, d), F32),
        grid=(m // tm, 2 * n_chunks),
        in_specs=[
            pl.BlockSpec((tm * WORD_ROWS, LANES), lambda i, j: (i, 0),
                         pipeline_mode=pl.Buffered(1)),
            pl.BlockSpec((n_chunks, d, tn), lambda i, j: (0, 0, 0),
                         pipeline_mode=pl.Buffered(1)),
            pl.BlockSpec((1, tn), lambda i, j: (0, build(j))),
            pl.BlockSpec((tm, ple), lambda i, j: (i, 0)),
            pl.BlockSpec((ple, tn), lambda i, j: (0, build(j))),
            pl.BlockSpec((tm, tn), lambda i, j: (i, build(j))),
            pl.BlockSpec((tm, 1), lambda i, j: (i, 0)),
            pl.BlockSpec((tm, 1), lambda i, j: (i, 0)),
            pl.BlockSpec((1, tn), lambda i, j: (0, build(j))),
            pl.BlockSpec((1, tn), lambda i, j: (0, build(j))),
            pl.BlockSpec((tm, tn), lambda i, j: (i, build(j))),
            pl.BlockSpec((1, tn), lambda i, j: (0, emit(j))),
            pl.BlockSpec((1, tn), lambda i, j: (0, emit(j))),
        ],
        out_specs=pl.BlockSpec((tm, tn), lambda i, j: (i, emit(j))),
        scratch_shapes=[pltpu.VMEM((tm, d // 2), BF16), pltpu.VMEM((tm, d // 2), BF16),
                        pltpu.VMEM((n_chunks, tm, tn), F32),
                        pltpu.VMEM((tm, 1), F32), pltpu.VMEM((tm, 1), F32)],
        compiler_params=_cparams(("arbitrary", "arbitrary")),
        name="ple_moe_ln2",
    )(x1s, w_pg_chunks, b_pg, p2, w_pp, y1, mu1, rs1, ln1_g, ln1_b, moe, ln2_g, ln2_b)


def _routing_tables(ids_t, counts_row):
    counts = counts_row[EXPERT_LANE0:EXPERT_LANE0 + N_EXPERTS]
    tiles = (counts + TM_MOE - 1) // TM_MOE
    tile_end = jnp.cumsum(tiles)
    offsets = (tile_end - tiles) * TM_MOE
    n_used = tile_end[-1:]
    n_tokens = ids_t.shape[1]
    experts = ids_t[0:EXPERT_TOP_K]
    slots = (offsets[experts] + ids_t[EXPERT_TOP_K:2 * EXPERT_TOP_K]).reshape(-1)
    n_tiles = (EXPERT_TOP_K * n_tokens) // TM_MOE + N_EXPERTS
    tile_ids = jnp.minimum(jnp.arange(n_tiles, dtype=I32), n_used - 1)
    tile_expert = jnp.sum(tile_end[None, :] <= tile_ids[:, None], axis=1).astype(I32)
    slab_of_slot = jnp.zeros((n_tiles * TM_MOE,), I32).at[slots].set(
        jnp.arange(slots.size, dtype=I32) % n_tokens * WORD_ROWS, unique_indices=True)
    return slots.astype(I32), slab_of_slot, tile_expert, n_used.astype(I32)


def kernel(x, p, positions, w_in, b_in, sinks, g_norm_a, g_norm_b, w_out, b_out,
           ln1_g, ln1_b, w_group, b_group, w_er, b_er, w_gate, w_up, w_down,
           w_ple_gate, b_ple_gate, w_ple_proj, ln2_g, ln2_b):
    batch, seq, d = x.shape
    m = batch * seq
    row = lambda v: v.reshape(1, -1)
    x2 = x.reshape(m, d)
    for i in range(DEPTH):
        proj = _in_projection(x2, w_in[i].astype(BF16), row(b_in[i]))
        inv_freq = ROPE_THETA ** (-jnp.arange(0, SWA_HEAD_DIM, 2, dtype=F32) / SWA_HEAD_DIM)
        inv_freq = jnp.tile(inv_freq, LANES // inv_freq.shape[0]).reshape(1, LANES)
        o_a = _swa_attention(proj, positions.reshape(m, 1), inv_freq, sinks[i], batch, seq)
        o_b = _sb_attention(proj, batch, seq)
        y1, mu1, rs1, x1s = _out_projection(
            o_a, o_b, row(g_norm_a[i]), row(g_norm_b[i]), _column_chunks(w_out[i], TN_RES), x2,
            row(b_out[i]), row(ln1_g[i]), row(ln1_b[i]))
        pad = ROUTER_COLS - N_GROUPS - N_EXPERTS
        w_r = jnp.concatenate(
            [w_group[i], w_er[i].transpose(1, 0, 2).reshape(d, N_EXPERTS),
             jnp.zeros((d, pad), F32)], axis=1).astype(BF16)
        b_r = jnp.concatenate([b_group[i], b_er[i].reshape(-1), jnp.zeros((pad,), F32)])
        ids, wts, counts = _router(x1s, w_r, row(b_r))
        slots, slab_of_slot, tile_expert, n_used = _routing_tables(ids, counts[0])
        act = _grouped_gate_up(tile_expert, n_used, slab_of_slot, x1s, w_gate[i], w_up[i])
        y_sorted = _grouped_down(tile_expert, n_used, act, w_down[i])
        moe = _combine_experts(slots, y_sorted, wts)
        x2 = _final_stage(x1s, _column_chunks(w_ple_gate[i], TN_RES), row(b_ple_gate[i]),
                          p[i].reshape(m, PLE_DIM), w_ple_proj[i], y1, mu1, rs1,
                          row(ln1_g[i]), row(ln1_b[i]), moe, row(ln2_g[i]), row(ln2_b[i]))
    return x2.reshape(batch, seq, d)
```

```python
import math

import jax
import jax.numpy as jnp
from jax import lax
from jax.experimental import pallas as pl
from jax.experimental.pallas import tpu as pltpu

F32 = jnp.float32
BF16 = jnp.bfloat16
I32 = jnp.int32
U32 = jnp.uint32

D_MODEL = 4096
PLE_DIM = 256
BLOCK = 128
ROPE_THETA = 10000.0
LN_EPS = 1e-5
RMS_EPS = 1e-6
NEG_INF = -1e30

SWA_HEAD_DIM = 64
SWA_WIDTH = D_MODEL // 2
SWA_HEADS = SWA_WIDTH // SWA_HEAD_DIM
SWA_KV_HEADS = SWA_HEADS // 8
SWA_GROUP = SWA_HEADS // SWA_KV_HEADS
SWA_KV_WIDTH = SWA_KV_HEADS * SWA_HEAD_DIM

SB_HEAD_DIM = 128
SB_WIDTH = D_MODEL - SWA_WIDTH
SB_HEADS = SB_WIDTH // SB_HEAD_DIM

IN_WIDTH = SWA_WIDTH + 2 * SWA_KV_WIDTH + 3 * SB_WIDTH
A_WIDTH = SWA_WIDTH + 2 * SWA_KV_WIDTH
Q_B_COL = 0
K_B_COL = Q_B_COL + SB_WIDTH
V_B_COL = K_B_COL + SB_WIDTH
Q_A_COL = V_B_COL + SB_WIDTH
K_A_COL = Q_A_COL + SWA_WIDTH
V_A_COL = K_A_COL + SWA_KV_WIDTH

N_GROUPS = 4
EXPERTS_PER_GROUP = 8
N_EXPERTS = N_GROUPS * EXPERTS_PER_GROUP
EXPERT_TOP_K = 2
DEPTH = 1
DEEPNORM_ALPHA = (2.0 * DEPTH) ** 0.25

LANES = 128
VMEM_LIMIT_BYTES = 56 * 1024 * 1024

EXP_ZERO_LOG = -104.0 - 2.0

TM_IN = 1024
TM_PROJ = 512
TN_PROJ = 512
TM_MOE = 256
TM_COMBINE = 128
SB_HEADS_PER_STEP = 8
ROW_CHUNK = 16


def _cparams(sem):
    return pltpu.CompilerParams(dimension_semantics=sem,
                                vmem_limit_bytes=VMEM_LIMIT_BYTES)


def _pack_halves(lo, hi):
    return lax.bitcast_convert_type(
        pltpu.pack_elementwise([lo, hi], packed_dtype=BF16), U32)


def _unpack_halves(words):
    return tuple(
        pltpu.unpack_elementwise(words, index=k, packed_dtype=BF16,
                                 unpacked_dtype=F32).astype(BF16) for k in (0, 1))


def _dot_halves(lo, hi, w_ref):
    half = lo.shape[1]
    return (jnp.dot(lo, w_ref[:half], preferred_element_type=F32)
            + jnp.dot(hi, w_ref[half:], preferred_element_type=F32))


WORD_ROWS = (D_MODEL // 2) // LANES
SLAB_PAIR = 2 * LANES


def _slab_words(slab_ref, pair, tokens):
    return jnp.concatenate(
        [slab_ref[pl.ds(2 * pair + k, tokens, stride=WORD_ROWS), :] for k in (0, 1)],
        axis=1)


def _dot_slab(slab_ref, tokens, w_refs):
    half = D_MODEL // 2
    outs = [None] * len(w_refs)
    for pair in range(half // SLAB_PAIR):
        lo, hi = _unpack_halves(_slab_words(slab_ref, pair, tokens))
        rows_lo = slice(pair * SLAB_PAIR, (pair + 1) * SLAB_PAIR)
        rows_hi = slice(half + pair * SLAB_PAIR, half + (pair + 1) * SLAB_PAIR)
        for n, w_ref in enumerate(w_refs):
            part = (jnp.dot(lo, w_ref[rows_lo], preferred_element_type=F32)
                    + jnp.dot(hi, w_ref[rows_hi], preferred_element_type=F32))
            outs[n] = part if outs[n] is None else outs[n] + part
    return outs


def _dot_nt(a, b):
    return lax.dot_general(a, b, (((1,), (1,)), ((), ())),
                           preferred_element_type=F32)


def _inproj_kernel(x_ref, w_ref, b_ref, o_ref, xb_ref):
    @pl.when(pl.program_id(1) == 0)
    def _():
        xb_ref[...] = x_ref[...].astype(BF16)

    acc = jnp.dot(xb_ref[...], w_ref[...], preferred_element_type=F32)
    o_ref[...] = (acc + b_ref[...]).astype(o_ref.dtype)


def _in_projection(x2, w_b, b):
    m, k = x2.shape
    n = w_b.shape[1]
    n_blocks = n // TN_PROJ
    assert A_WIDTH % TN_PROJ == 0
    a_blocks = A_WIDTH // TN_PROJ

    def src(j):
        return lax.rem(j + a_blocks, n_blocks)

    return pl.pallas_call(
        _inproj_kernel,
        out_shape=jax.ShapeDtypeStruct((m, n), BF16),
        grid=(m // TM_IN, n_blocks),
        in_specs=[
            pl.BlockSpec((TM_IN, k), lambda i, j: (i, 0)),
            pl.BlockSpec((k, TN_PROJ), lambda i, j: (0, src(j))),
            pl.BlockSpec((1, TN_PROJ), lambda i, j: (0, src(j))),
        ],
        out_specs=pl.BlockSpec((TM_IN, TN_PROJ), lambda i, j: (i, j)),
        scratch_shapes=[pltpu.VMEM((TM_IN, k), BF16)],
        compiler_params=_cparams(("arbitrary", "arbitrary")),
        name="in_projection",
    )(x2, w_b, b)


def _swa_kernel(sinks_ref, q_ref, kc_ref, kp_ref, vc_ref, vp_ref,
                posc_ref, posp_ref, invf_ref, o_ref):
    n = pl.program_id(1)
    lane = lax.broadcasted_iota(I32, (1, LANES), 1)
    first_half = (lane % SWA_HEAD_DIM) < (SWA_HEAD_DIM // 2)

    def tables(pos_ref):
        ang = pos_ref[...].astype(F32) * invf_ref[...]
        sin = jnp.sin(ang)
        return jnp.cos(ang), jnp.where(first_half, -sin, sin)

    def rope(x, cos, sin_signed):
        partner = jnp.where(first_half,
                            pltpu.roll(x, LANES - SWA_HEAD_DIM // 2, 1),
                            pltpu.roll(x, SWA_HEAD_DIM // 2, 1))
        return x * cos + partner * sin_signed

    cos_c, sin_c = tables(posc_ref)
    cos_p, sin_p = tables(posp_ref)

    def rope_block(ref, cos, sin_signed):
        width = ref.shape[1]
        return [rope(ref[:, c * LANES:(c + 1) * LANES].astype(F32), cos,
                     sin_signed).astype(BF16) for c in range(width // LANES)]

    q_chunks = rope_block(q_ref, cos_c, sin_c)
    k_chunks = [jnp.concatenate([p_, c_], axis=0)
                for p_, c_ in zip(rope_block(kp_ref, cos_p, sin_p),
                                  rope_block(kc_ref, cos_c, sin_c))]
    v_all = jnp.concatenate([vp_ref[...], vc_ref[...]], axis=0)

    qi = lax.broadcasted_iota(I32, (BLOCK, 2 * BLOCK), 0)
    kj = lax.broadcasted_iota(I32, (BLOCK, 2 * BLOCK), 1)
    rel = qi - (kj - BLOCK)
    valid = (rel >= 0) & (rel < BLOCK) & ((kj >= BLOCK) | (n > 0))

    def head_slice(chunks, head):
        half = head % 2
        return chunks[head // 2][:, half * SWA_HEAD_DIM:(half + 1) * SWA_HEAD_DIM]

    scale = 1.0 / math.sqrt(SWA_HEAD_DIM)
    for h in range(SWA_KV_HEADS):
        k_h = head_slice(k_chunks, h)
        v_h = v_all[:, h * SWA_HEAD_DIM:(h + 1) * SWA_HEAD_DIM]
        q_h = jnp.concatenate(
            [head_slice(q_chunks, h * SWA_GROUP + g) for g in range(SWA_GROUP)],
            axis=0)
        s = _dot_nt(q_h, k_h) * scale
        probs = []
        for g in range(SWA_GROUP):
            sink = sinks_ref[h * SWA_GROUP + g]
            s_g = jnp.where(valid, s[g * BLOCK:(g + 1) * BLOCK], NEG_INF)
            m = jnp.maximum(jnp.max(s_g, axis=-1, keepdims=True), sink)
            e = jnp.exp(s_g - m)
            den = jnp.sum(e, axis=-1, keepdims=True) + jnp.exp(sink - m)
            probs.append((e / den).astype(BF16))
        o_h = jnp.dot(jnp.concatenate(probs, axis=0), v_h,
                      preferred_element_type=F32)
        for g in range(SWA_GROUP):
            col = (h * SWA_GROUP + g) * SWA_HEAD_DIM
            o_ref[:, col:col + SWA_HEAD_DIM] = o_h[g * BLOCK:(g + 1) * BLOCK]


def _swa_attention(proj, pos2, inv_freq, sinks, batch, seq):
    nb = seq // BLOCK
    assert Q_A_COL % SWA_WIDTH == 0 and K_A_COL % SWA_KV_WIDTH == 0
    assert V_A_COL % SWA_KV_WIDTH == 0
    qcol = Q_A_COL // SWA_WIDTH
    kcol = K_A_COL // SWA_KV_WIDTH
    vcol = V_A_COL // SWA_KV_WIDTH

    def cur(b, n):
        return b * nb + n

    def prev(b, n):
        return b * nb + jnp.maximum(n - 1, 0)

    return pl.pallas_call(
        _swa_kernel,
        out_shape=jax.ShapeDtypeStruct((batch * seq, SWA_WIDTH), F32),
        grid=(batch, nb),
        in_specs=[
            pl.BlockSpec(memory_space=pltpu.SMEM),
            pl.BlockSpec((BLOCK, SWA_WIDTH), lambda b, n: (cur(b, n), qcol)),
            pl.BlockSpec((BLOCK, SWA_KV_WIDTH), lambda b, n: (cur(b, n), kcol)),
            pl.BlockSpec((BLOCK, SWA_KV_WIDTH), lambda b, n: (prev(b, n), kcol)),
            pl.BlockSpec((BLOCK, SWA_KV_WIDTH), lambda b, n: (cur(b, n), vcol)),
            pl.BlockSpec((BLOCK, SWA_KV_WIDTH), lambda b, n: (prev(b, n), vcol)),
            pl.BlockSpec((BLOCK, 1), lambda b, n: (cur(b, n), 0)),
            pl.BlockSpec((BLOCK, 1), lambda b, n: (prev(b, n), 0)),
            pl.BlockSpec((1, LANES), lambda b, n: (0, 0)),
        ],
        out_specs=pl.BlockSpec((BLOCK, SWA_WIDTH), lambda b, n: (cur(b, n), 0)),
        compiler_params=_cparams(("arbitrary", "arbitrary")),
        name="swa_attention",
    )(sinks, proj, proj, proj, proj, proj, pos2, pos2, inv_freq)


def _sb_kernel(q_ref, k_ref, v_ref, o_ref):
    n = pl.program_id(2)
    heads = [slice(h * SB_HEAD_DIM, (h + 1) * SB_HEAD_DIM)
             for h in range(SB_HEADS_PER_STEP)]
    rows = SB_HEADS_PER_STEP * BLOCK
    scale = 1.0 / math.sqrt(SB_HEAD_DIM)
    key_j = lax.broadcasted_iota(I32, (BLOCK, 2 * BLOCK), 0)
    out_c = lax.broadcasted_iota(I32, (BLOCK, 2 * BLOCK), 1)
    later_total = ((key_j > out_c) | (out_c >= BLOCK)).astype(BF16)
    q_row = lax.broadcasted_iota(I32, (rows, BLOCK), 0) % BLOCK
    k_col = lax.broadcasted_iota(I32, (rows, BLOCK), 1)
    causal = k_col < q_row

    def block(kb, carry, acc, mask):
        start = pl.multiple_of(kb * BLOCK, BLOCK)
        z = jnp.concatenate(
            [_dot_nt(q_ref[:, hd], k_ref[pl.ds(start, BLOCK), hd]) for hd in heads],
            axis=0) * scale
        t = jnp.log(1.0 + jnp.exp(-jnp.abs(z)))
        log_not = jnp.minimum(-z, 0.0) - t
        log_beta = jnp.minimum(z, 0.0) - t
        if mask is not None:
            log_not = jnp.where(mask, log_not, 0.0)
        hi = log_not.astype(BF16)
        r1 = log_not - hi.astype(F32)
        mid = r1.astype(BF16)
        lo = (r1 - mid.astype(F32)).astype(BF16)
        parts = jnp.dot(jnp.concatenate([hi, mid, lo], axis=0), later_total,
                        preferred_element_type=F32)
        sums = (parts[:rows] + parts[rows:2 * rows]) + parts[2 * rows:]
        a = jnp.exp(log_beta + sums[:, :BLOCK] + carry)
        if mask is not None:
            a = jnp.where(mask, a, 0.0)
        a = a.astype(BF16)
        pv = jnp.concatenate(
            [jnp.dot(a[h * BLOCK:(h + 1) * BLOCK], v_ref[pl.ds(start, BLOCK), hd],
                     preferred_element_type=F32) for h, hd in enumerate(heads)], axis=0)
        return carry + sums[:, BLOCK:], acc + pv

    zeros = jnp.zeros((rows, BLOCK), F32)
    carry, acc = block(n, zeros, zeros, causal)

    def cond(state):
        kb, carry, _ = state
        return jnp.logical_and(kb >= 0, jnp.max(carry) > EXP_ZERO_LOG)

    def body(state):
        kb, carry, acc = state
        carry, acc = block(kb, carry, acc, None)
        return kb - 1, carry, acc

    _, _, acc = lax.while_loop(cond, body, (n - 1, carry, acc))
    for h, hd in enumerate(heads):
        o_ref[:, hd] = acc[h * BLOCK:(h + 1) * BLOCK]


def _sb_attention(proj, batch, seq):
    nb = seq // BLOCK
    width = SB_HEADS_PER_STEP * SB_HEAD_DIM
    assert Q_B_COL % width == 0 and K_B_COL % width == 0 and V_B_COL % width == 0
    qcol = Q_B_COL // width
    kcol = K_B_COL // width
    vcol = V_B_COL // width
    return pl.pallas_call(
        _sb_kernel,
        out_shape=jax.ShapeDtypeStruct((batch * seq, SB_WIDTH), F32),
        grid=(batch, SB_HEADS // SB_HEADS_PER_STEP, nb),
        in_specs=[
            pl.BlockSpec((BLOCK, width), lambda b, h, n: (b * nb + n, qcol + h)),
            pl.BlockSpec((seq, width), lambda b, h, n: (b, kcol + h)),
            pl.BlockSpec((seq, width), lambda b, h, n: (b, vcol + h)),
        ],
        out_specs=pl.BlockSpec((BLOCK, width), lambda b, h, n: (b * nb + n, h)),
        compiler_params=_cparams(("arbitrary", "arbitrary", "arbitrary")),
        name="sb_attention",
    )(proj, proj, proj)


def _row_stats(chunks_ref, n_chunks, width):
    total = chunks_ref[0].sum(axis=-1, keepdims=True)
    for c in range(1, n_chunks):
        total = total + chunks_ref[c].sum(axis=-1, keepdims=True)
    mu = total / width
    sq = jnp.square(chunks_ref[0] - mu).sum(axis=-1, keepdims=True)
    for c in range(1, n_chunks):
        sq = sq + jnp.square(chunks_ref[c] - mu).sum(axis=-1, keepdims=True)
    return mu, lax.rsqrt(sq / width + LN_EPS)


def _outproj_kernel(oa_ref, ob_ref, ga_ref, gb_ref, wa_ref, wb_ref, x_ref, bo_ref,
                    lg_ref, lb_ref, y_ref, mu_ref, rs_ref, x1s_ref,
                    ma_ref, mb_ref, acc_ref):
    j = pl.program_id(1)
    n_chunks = acc_ref.shape[0]
    tn = acc_ref.shape[2]

    @pl.when(j == 0)
    def _():
        for o_ref, g_ref, m_ref in ((oa_ref, ga_ref, ma_ref), (ob_ref, gb_ref, mb_ref)):
            width = o_ref.shape[1]
            cols = [slice(c * tn, (c + 1) * tn) for c in range(width // tn)]
            sq = sum(jnp.square(o_ref[:, sl]).sum(axis=-1, keepdims=True) for sl in cols)
            r = lax.rsqrt(sq / width + RMS_EPS)
            for sl in cols:
                m_ref[:, sl] = (o_ref[:, sl] * r * g_ref[:, sl]).astype(BF16)

    mix = (jnp.dot(ma_ref[...], wa_ref[...], preferred_element_type=F32)
           + jnp.dot(mb_ref[...], wb_ref[...], preferred_element_type=F32))
    y = DEEPNORM_ALPHA * x_ref[...] + (mix + bo_ref[...])
    y_ref[...] = y
    acc_ref[j] = y

    @pl.when(j == n_chunks - 1)
    def _():
        mu, rs = _row_stats(acc_ref, n_chunks, n_chunks * tn)
        mu_ref[...] = mu
        rs_ref[...] = rs

        def normed(c):
            sl = slice(c * tn, (c + 1) * tn)
            return (acc_ref[c] - mu) * rs * lg_ref[:, sl] + lb_ref[:, sl]

        tm = acc_ref.shape[1]
        for c in range(n_chunks // 2):
            words = _pack_halves(normed(c), normed(c + n_chunks // 2))
            for q in range(tn // LANES):
                x1s_ref[pl.ds(c * (tn // LANES) + q, tm, stride=WORD_ROWS), :] = (
                    words[:, q * LANES:(q + 1) * LANES])


def _out_projection(o_a, o_b, g_a, g_b, w_b, x2, b_out, ln_g, ln_b):
    m, d = x2.shape
    tm, tn = TM_PROJ, TN_PROJ
    half = o_a.shape[1]
    once = dict(pipeline_mode=pl.Buffered(1))
    return pl.pallas_call(
        _outproj_kernel,
        out_shape=(jax.ShapeDtypeStruct((m, d), F32),
                   jax.ShapeDtypeStruct((m, 1), F32),
                   jax.ShapeDtypeStruct((m, 1), F32),
                   jax.ShapeDtypeStruct((m * WORD_ROWS, LANES), U32)),
        grid=(m // tm, d // tn),
        in_specs=[
            pl.BlockSpec((tm, half), lambda i, j: (i, 0), **once),
            pl.BlockSpec((tm, half), lambda i, j: (i, 0), **once),
            pl.BlockSpec((1, half), lambda i, j: (0, 0)),
            pl.BlockSpec((1, half), lambda i, j: (0, 0)),
            pl.BlockSpec((half, tn), lambda i, j: (0, j)),
            pl.BlockSpec((half, tn), lambda i, j: (1, j)),
            pl.BlockSpec((tm, tn), lambda i, j: (i, j)),
            pl.BlockSpec((1, tn), lambda i, j: (0, j)),
            pl.BlockSpec((1, d), lambda i, j: (0, 0)),
            pl.BlockSpec((1, d), lambda i, j: (0, 0)),
        ],
        out_specs=(pl.BlockSpec((tm, tn), lambda i, j: (i, j)),
                   pl.BlockSpec((tm, 1), lambda i, j: (i, 0)),
                   pl.BlockSpec((tm, 1), lambda i, j: (i, 0)),
                   pl.BlockSpec((tm * WORD_ROWS, LANES), lambda i, j: (i, 0))),
        scratch_shapes=[pltpu.VMEM((tm, half), BF16), pltpu.VMEM((tm, half), BF16),
                        pltpu.VMEM((d // tn, tm, tn), F32)],
        compiler_params=_cparams(("arbitrary", "arbitrary")),
        name="out_projection_ln1",
    )(o_a, o_b, g_a, g_b, w_b, w_b, x2, b_out, ln_g, ln_b)


ROUTER_COLS = LANES
EXPERT_LANE0 = N_GROUPS


def _router_kernel(x_ref, w_ref, b_ref, ids_ref, wts_ref, cnt_ref, carry_ref):
    i = pl.program_id(0)
    tm = wts_ref.shape[0]

    @pl.when(i == 0)
    def _():
        carry_ref[...] = jnp.zeros_like(carry_ref)

    logits = _dot_slab(x_ref, tm, [w_ref])[0] + b_ref[...]
    lane = lax.broadcasted_iota(I32, (tm, ROUTER_COLS), 1)
    big = jnp.int32(ROUTER_COLS)

    def first_argmax(vals):
        top = jnp.max(vals, axis=-1, keepdims=True)
        idx = jnp.min(jnp.where(vals == top, lane, big), axis=-1, keepdims=True)
        return top, idx

    is_group = lane < N_GROUPS
    g_logits = jnp.where(is_group, logits, -jnp.inf)
    g_top, g_idx = first_argmax(g_logits)
    g_w = 1.0 / jnp.sum(jnp.exp(g_logits - g_top), axis=-1, keepdims=True)

    first = EXPERT_LANE0 + g_idx * EXPERTS_PER_GROUP
    in_group = (lane >= first) & (lane < first + EXPERTS_PER_GROUP)
    e_logits = jnp.where(in_group, logits, -jnp.inf)
    top1, idx1 = first_argmax(e_logits)
    top2, idx2 = first_argmax(jnp.where(lane == idx1, -jnp.inf, e_logits))
    e2 = jnp.exp(top2 - top1)
    w1 = g_w / (1.0 + e2)
    w2 = g_w * e2 / (1.0 + e2)

    hit1 = lane == idx1
    hit2 = lane == idx2
    onehot = (hit1 | hit2).astype(BF16)
    r = lax.broadcasted_iota(I32, (tm, tm), 0)
    c = lax.broadcasted_iota(I32, (tm, tm), 1)
    before = (c < r).astype(BF16)
    prior = jnp.dot(before, onehot, preferred_element_type=F32) + carry_ref[0:1, :]
    rank1 = jnp.sum(jnp.where(hit1, prior, 0.0), axis=-1, keepdims=True)
    rank2 = jnp.sum(jnp.where(hit2, prior, 0.0), axis=-1, keepdims=True)
    counts = carry_ref[0:1, :] + jnp.sum(onehot.astype(F32), axis=0, keepdims=True)
    carry_ref[...] = jnp.broadcast_to(counts, carry_ref.shape)
    cnt_ref[...] = jnp.broadcast_to(counts, cnt_ref.shape).astype(I32)

    ids = jnp.where(lane == 0, idx1 - EXPERT_LANE0,
          jnp.where(lane == 1, idx2 - EXPERT_LANE0,
          jnp.where(lane == 2, rank1.astype(I32),
          jnp.where(lane == 3, rank2.astype(I32), 0))))
    ids_ref[...] = ids.T[:ids_ref.shape[0]]
    wts_ref[...] = jnp.where(lane == 0, w1, jnp.where(lane == 1, w2, 0.0))


def _router(x1s, w_r, b_r):
    m = x1s.shape[0] // WORD_ROWS
    tm = TM_PROJ
    return pl.pallas_call(
        _router_kernel,
        out_shape=(jax.ShapeDtypeStruct((8, m), I32),
                   jax.ShapeDtypeStruct((m, ROUTER_COLS), F32),
                   jax.ShapeDtypeStruct((8, ROUTER_COLS), I32)),
        grid=(m // tm,),
        in_specs=[
            pl.BlockSpec((tm * WORD_ROWS, LANES), lambda i: (i, 0)),
            pl.BlockSpec((D_MODEL, ROUTER_COLS), lambda i: (0, 0)),
            pl.BlockSpec((1, ROUTER_COLS), lambda i: (0, 0)),
        ],
        out_specs=(pl.BlockSpec((8, tm), lambda i: (0, i)),
                   pl.BlockSpec((tm, ROUTER_COLS), lambda i: (i, 0)),
                   pl.BlockSpec((8, ROUTER_COLS), lambda i: (0, 0))),
        scratch_shapes=[pltpu.VMEM((8, ROUTER_COLS), F32)],
        compiler_params=_cparams(("arbitrary",)),
        name="router",
    )(x1s, w_r, b_r)


def _start_row_gather(idx_ref, first, n_items, span, src_ref, dst_ref, sem):
    def body(c, _):
        for u in range(ROW_CHUNK):
            r = c * ROW_CHUNK + u
            src_row = pl.multiple_of(idx_ref[first + r], span)
            dst_row = pl.multiple_of(r * span, span)
            pltpu.make_async_copy(src_ref.at[pl.ds(src_row, span)],
                                  dst_ref.at[pl.ds(dst_row, span)], sem).start()
        return 0

    lax.fori_loop(0, n_items // ROW_CHUNK, body, 0)


def _wait_row_gather(n_items, span, src_ref, dst_ref, sem):
    def body(c, _):
        for _u in range(ROW_CHUNK):
            pltpu.make_async_copy(src_ref.at[pl.ds(0, span)], dst_ref.at[pl.ds(0, span)],
                                  sem).wait()
        return 0

    lax.fori_loop(0, n_items // ROW_CHUNK, body, 0)


def _expert_changed(te_ref, i):
    return jnp.logical_or(i == 0, te_ref[i] != te_ref[jnp.maximum(i - 1, 0)])


def _tile_row(i, nu):
    return jnp.minimum(i, nu[0] - 1)


def _stage_expert_weights(i, te_ref, nxt_ref, ws_ref, w_hbms, stage_ref, bf_refs, sem):
    expert = te_ref[i]
    slot = ws_ref[expert]

    def copies(e, s):
        return [pltpu.make_async_copy(w.at[e], stage_ref.at[s, n], sem.at[s])
                for n, w in enumerate(w_hbms)]

    @pl.when(i == 0)
    def _():
        for c in copies(expert, slot):
            c.start()

    @pl.when(_expert_changed(te_ref, i))
    def _():
        for c in copies(expert, slot):
            c.wait()
        nxt = nxt_ref[expert]

        @pl.when(nxt < N_EXPERTS)
        def _():
            for c in copies(nxt, 1 - slot):
                c.start()

        for n, bf_ref in enumerate(bf_refs):
            bf_ref[...] = stage_ref[slot, n].astype(BF16)


def _gateup_kernel(te_ref, nu_ref, tos_ref, nxt_ref, ws_ref, x_hbm, wg_hbm, wu_hbm, a_ref,
                   xbuf_ref, stage_ref, wgb_ref, wub_ref, sems, wsem):
    i = pl.program_id(0)
    n_used = nu_ref[0]
    slot = lax.rem(i, 2)

    def start(tile, buf):
        _start_row_gather(tos_ref, tile * TM_MOE, TM_MOE, WORD_ROWS, x_hbm,
                          xbuf_ref.at[buf], sems.at[buf])

    @pl.when(i == 0)
    def _():
        start(0, 0)

    @pl.when(i + 1 < n_used)
    def _():
        start(i + 1, 1 - slot)

    _stage_expert_weights(i, te_ref, nxt_ref, ws_ref, [wg_hbm, wu_hbm], stage_ref,
                          [wgb_ref, wub_ref], wsem)

    @pl.when(i < n_used)
    def _():
        _wait_row_gather(TM_MOE, WORD_ROWS, x_hbm, xbuf_ref.at[slot], sems.at[slot])
        gate, up = _dot_slab(xbuf_ref.at[slot], TM_MOE, [wgb_ref, wub_ref])
        a_ref[...] = (gate * jax.nn.sigmoid(gate) * up).astype(a_ref.dtype)

    @pl.when(i >= n_used)
    def _():
        a_ref[...] = jnp.zeros_like(a_ref)


def _grouped_gate_up(tile_expert, n_used, slab_of_slot, next_expert, stage_slot,
                     x1s, w_gate, w_up):
    p_rows = slab_of_slot.shape[0]
    _, d, f = w_gate.shape
    hbm = pl.BlockSpec(memory_space=pl.ANY)
    return pl.pallas_call(
        _gateup_kernel,
        out_shape=jax.ShapeDtypeStruct((p_rows, f), BF16),
        grid_spec=pltpu.PrefetchScalarGridSpec(
            num_scalar_prefetch=5,
            grid=(p_rows // TM_MOE,),
            in_specs=[hbm, hbm, hbm],
            out_specs=pl.BlockSpec((TM_MOE, f), lambda i, *_: (i, 0)),
            scratch_shapes=[pltpu.VMEM((2, TM_MOE * WORD_ROWS, LANES), U32),
                            pltpu.VMEM((2, 2, d, f), F32),
                            pltpu.VMEM((d, f), BF16), pltpu.VMEM((d, f), BF16),
                            pltpu.SemaphoreType.DMA((2,)), pltpu.SemaphoreType.DMA((2,))],
        ),
        compiler_params=_cparams(("arbitrary",)),
        name="moe_gate_up",
    )(tile_expert, n_used, slab_of_slot, next_expert, stage_slot, x1s, w_gate, w_up)


def _down_kernel(te_ref, nu_ref, nxt_ref, ws_ref, a_ref, wd_hbm, y_ref,
                 stage_ref, wdb_ref, wsem):
    i = pl.program_id(0)
    _stage_expert_weights(i, te_ref, nxt_ref, ws_ref, [wd_hbm], stage_ref, [wdb_ref], wsem)

    @pl.when(i < nu_ref[0])
    def _():
        y_ref[:, 0, :] = jnp.dot(a_ref[...], wdb_ref[...], preferred_element_type=F32)

    @pl.when(i >= nu_ref[0])
    def _():
        y_ref[...] = jnp.zeros_like(y_ref)


def _grouped_down(tile_expert, n_used, next_expert, stage_slot, act, w_down):
    p_rows, f = act.shape
    d = w_down.shape[2]
    return pl.pallas_call(
        _down_kernel,
        out_shape=jax.ShapeDtypeStruct((p_rows, 1, d), F32),
        grid_spec=pltpu.PrefetchScalarGridSpec(
            num_scalar_prefetch=4,
            grid=(p_rows // TM_MOE,),
            in_specs=[
                pl.BlockSpec((TM_MOE, f), lambda i, te, nu, *_: (_tile_row(i, nu), 0)),
                pl.BlockSpec(memory_space=pl.ANY),
            ],
            out_specs=pl.BlockSpec((TM_MOE, 1, d), lambda i, *_: (i, 0, 0)),
            scratch_shapes=[pltpu.VMEM((2, 1, f, d), F32), pltpu.VMEM((f, d), BF16),
                            pltpu.SemaphoreType.DMA((2,))],
        ),
        compiler_params=_cparams(("arbitrary",)),
        name="moe_down",
    )(tile_expert, n_used, next_expert, stage_slot, act, w_down)


def _combine_kernel(slot_ref, y_hbm, wts_ref, o_ref, buf_ref, sems):
    i = pl.program_id(0)
    tm = o_ref.shape[0]
    n_tokens = slot_ref.shape[0] // EXPERT_TOP_K
    slot = lax.rem(i, 2)

    def start(tile, buf):
        for k in range(EXPERT_TOP_K):
            _start_row_gather(slot_ref, k * n_tokens + tile * tm, tm, 1, y_hbm,
                              buf_ref.at[buf, k], sems.at[buf])

    @pl.when(i == 0)
    def _():
        start(0, 0)

    @pl.when(i + 1 < pl.num_programs(0))
    def _():
        start(i + 1, 1 - slot)

    _wait_row_gather(EXPERT_TOP_K * tm, 1, y_hbm, buf_ref.at[slot, 0], sems.at[slot])
    for c in range(o_ref.shape[1] // TN_PROJ):
        sl = slice(c * TN_PROJ, (c + 1) * TN_PROJ)
        o_ref[:, sl] = sum(wts_ref[:, k:k + 1] * buf_ref[slot, k, :, 0, sl]
                           for k in range(EXPERT_TOP_K))


def _combine_experts(slot_kt, y_sorted, wts):
    d = y_sorted.shape[2]
    m = slot_kt.shape[0] // EXPERT_TOP_K
    tm = TM_COMBINE
    return pl.pallas_call(
        _combine_kernel,
        out_shape=jax.ShapeDtypeStruct((m, d), F32),
        grid_spec=pltpu.PrefetchScalarGridSpec(
            num_scalar_prefetch=1,
            grid=(m // tm,),
            in_specs=[
                pl.BlockSpec(memory_space=pl.ANY),
                pl.BlockSpec((tm, ROUTER_COLS), lambda i, sl: (i, 0)),
            ],
            out_specs=pl.BlockSpec((tm, d), lambda i, sl: (i, 0)),
            scratch_shapes=[pltpu.VMEM((2, EXPERT_TOP_K, tm, 1, d), F32),
                            pltpu.SemaphoreType.DMA((2,))],
        ),
        compiler_params=_cparams(("arbitrary",)),
        name="moe_combine",
    )(slot_kt, y_sorted, wts)


def _final_kernel(x1s_ref, wg_ref, bg_ref, p_ref, wp_ref, y1_ref, mu_ref, rs_ref,
                  l1g_ref, l1b_ref, moe_ref, l2g_ref, l2b_ref,
                  o_ref, lo_ref, hi_ref, acc_ref, mu2_ref, rs2_ref):
    j = pl.program_id(1)
    n_chunks = acc_ref.shape[0]
    tn = acc_ref.shape[2]

    @pl.when(j == 0)
    def _():
        for s in range(WORD_ROWS):
            sl = slice(s * LANES, (s + 1) * LANES)
            lo_ref[:, sl], hi_ref[:, sl] = _unpack_halves(
                x1s_ref[pl.ds(s, lo_ref.shape[0], stride=WORD_ROWS), :])

    @pl.when(j < n_chunks)
    def _():
        gate = _dot_halves(lo_ref[...], hi_ref[...], wg_ref) + bg_ref[...]
        emb = jnp.dot(p_ref[...].astype(BF16), wp_ref[...].astype(BF16),
                      preferred_element_type=F32)
        x1 = (y1_ref[...] - mu_ref[...]) * rs_ref[...] * l1g_ref[...] + l1b_ref[...]
        acc_ref[j] = DEEPNORM_ALPHA * x1 + moe_ref[...] + jax.nn.sigmoid(gate) * emb

    @pl.when(j == n_chunks)
    def _():
        mu2_ref[...], rs2_ref[...] = _row_stats(acc_ref, n_chunks, n_chunks * tn)

    @pl.when(j >= n_chunks)
    def _():
        o_ref[...] = ((acc_ref[j - n_chunks] - mu2_ref[...]) * rs2_ref[...]
                      * l2g_ref[...] + l2b_ref[...])


def _final_stage(x1s, w_pg_b, b_pg, p2, w_pp, y1, mu1, rs1, ln1_g, ln1_b,
                 moe, ln2_g, ln2_b):
    m, d = y1.shape
    tm, tn = TM_PROJ, TN_PROJ
    ple = p2.shape[1]
    n_chunks = d // tn

    def build(j):
        return jnp.minimum(j, n_chunks - 1)

    def emit(j):
        return jnp.maximum(j - n_chunks, 0)

    return pl.pallas_call(
        _final_kernel,
        out_shape=jax.ShapeDtypeStruct((m, d), F32),
        grid=(m // tm, 2 * n_chunks),
        in_specs=[
            pl.BlockSpec((tm * WORD_ROWS, LANES), lambda i, j: (i, 0),
                         pipeline_mode=pl.Buffered(1)),
            pl.BlockSpec((d, tn), lambda i, j: (0, build(j))),
            pl.BlockSpec((1, tn), lambda i, j: (0, build(j))),
            pl.BlockSpec((tm, ple), lambda i, j: (i, 0)),
            pl.BlockSpec((ple, tn), lambda i, j: (0, build(j))),
            pl.BlockSpec((tm, tn), lambda i, j: (i, build(j))),
            pl.BlockSpec((tm, 1), lambda i, j: (i, 0)),
            pl.BlockSpec((tm, 1), lambda i, j: (i, 0)),
            pl.BlockSpec((1, tn), lambda i, j: (0, build(j))),
            pl.BlockSpec((1, tn), lambda i, j: (0, build(j))),
            pl.BlockSpec((tm, tn), lambda i, j: (i, build(j))),
            pl.BlockSpec((1, tn), lambda i, j: (0, emit(j))),
            pl.BlockSpec((1, tn), lambda i, j: (0, emit(j))),
        ],
        out_specs=pl.BlockSpec((tm, tn), lambda i, j: (i, emit(j))),
        scratch_shapes=[pltpu.VMEM((tm, d // 2), BF16), pltpu.VMEM((tm, d // 2), BF16),
                        pltpu.VMEM((n_chunks, tm, tn), F32),
                        pltpu.VMEM((tm, 1), F32), pltpu.VMEM((tm, 1), F32)],
        compiler_params=_cparams(("arbitrary", "arbitrary")),
        name="ple_moe_ln2",
    )(x1s, w_pg_b, b_pg, p2, w_pp, y1, mu1, rs1, ln1_g, ln1_b, moe, ln2_g, ln2_b)


def _routing_tables(ids_t, counts_row):
    counts = counts_row[EXPERT_LANE0:EXPERT_LANE0 + N_EXPERTS]
    tiles = (counts + TM_MOE - 1) // TM_MOE
    tile_end = jnp.cumsum(tiles)
    offsets = (tile_end - tiles) * TM_MOE
    n_used = tile_end[-1:]
    n_tokens = ids_t.shape[1]
    experts = ids_t[0:EXPERT_TOP_K]
    hit = experts[None] == jnp.arange(N_EXPERTS, dtype=I32)[:, None, None]
    first_slot = jnp.sum(jnp.where(hit, offsets[:, None, None], 0), axis=0)
    slots = (first_slot + ids_t[EXPERT_TOP_K:2 * EXPERT_TOP_K]).reshape(-1)
    n_tiles = (EXPERT_TOP_K * n_tokens) // TM_MOE + N_EXPERTS
    tile_ids = jnp.minimum(jnp.arange(n_tiles, dtype=I32), n_used - 1)
    tile_expert = jnp.sum(tile_end[None, :] <= tile_ids[:, None], axis=1).astype(I32)
    slab_of_slot = jnp.zeros((n_tiles * TM_MOE,), I32).at[slots].set(
        jnp.arange(slots.size, dtype=I32) % n_tokens * WORD_ROWS, unique_indices=True)
    eid = jnp.arange(N_EXPERTS, dtype=I32)
    later_owner = (eid[None, :] > eid[:, None]) & (tiles[None, :] > 0)
    next_expert = jnp.min(jnp.where(later_owner, eid[None, :], N_EXPERTS), axis=1).astype(I32)
    stage_slot = ((jnp.cumsum(tiles > 0) - 1) % 2).astype(I32)
    return (slots.astype(I32), slab_of_slot, tile_expert, n_used.astype(I32),
            next_expert, stage_slot)


def kernel(x, p, positions, w_in, b_in, sinks, g_norm_a, g_norm_b, w_out, b_out,
           ln1_g, ln1_b, w_group, b_group, w_er, b_er, w_gate, w_up, w_down,
           w_ple_gate, b_ple_gate, w_ple_proj, ln2_g, ln2_b):
    batch, seq, d = x.shape
    m = batch * seq
    row = lambda v: v.reshape(1, -1)
    x2 = x.reshape(m, d)
    for i in range(DEPTH):
        proj = _in_projection(x2, w_in[i].astype(BF16), row(b_in[i]))
        inv_freq = ROPE_THETA ** (-jnp.arange(0, SWA_HEAD_DIM, 2, dtype=F32) / SWA_HEAD_DIM)
        inv_freq = jnp.tile(inv_freq, LANES // inv_freq.shape[0]).reshape(1, LANES)
        o_a = _swa_attention(proj, positions.reshape(m, 1), inv_freq, sinks[i], batch, seq)
        o_b = _sb_attention(proj, batch, seq)
        y1, mu1, rs1, x1s = _out_projection(
            o_a, o_b, row(g_norm_a[i]), row(g_norm_b[i]), w_out[i].astype(BF16), x2,
            row(b_out[i]), row(ln1_g[i]), row(ln1_b[i]))
        pad = ROUTER_COLS - N_GROUPS - N_EXPERTS
        w_r = jnp.concatenate(
            [w_group[i], w_er[i].transpose(1, 0, 2).reshape(d, N_EXPERTS),
             jnp.zeros((d, pad), F32)], axis=1).astype(BF16)
        b_r = jnp.concatenate([b_group[i], b_er[i].reshape(-1), jnp.zeros((pad,), F32)])
        ids, wts, counts = _router(x1s, w_r, row(b_r))
        (slots, slab_of_slot, tile_expert, n_used, next_expert,
         stage_slot) = _routing_tables(ids, counts[0])
        act = _grouped_gate_up(tile_expert, n_used, slab_of_slot, next_expert, stage_slot,
                               x1s, w_gate[i], w_up[i])
        y_sorted = _grouped_down(tile_expert, n_used, next_expert, stage_slot, act, w_down[i])
        moe = _combine_experts(slots, y_sorted, wts)
        x2 = _final_stage(x1s, w_ple_gate[i].astype(BF16), row(b_ple_gate[i]),
                          p[i].reshape(m, PLE_DIM), w_ple_proj[i], y1, mu1, rs1,
                          row(ln1_g[i]), row(ln1_b[i]), moe, row(ln2_g[i]), row(ln2_b[i]))
    return x2.reshape(batch, seq, d)
```

```python
import math

import jax
import jax.numpy as jnp
from jax import lax
from jax.experimental import pallas as pl
from jax.experimental.pallas import tpu as pltpu

F32 = jnp.float32
BF16 = jnp.bfloat16
I32 = jnp.int32
U32 = jnp.uint32

D_MODEL = 4096
PLE_DIM = 256
BLOCK = 128
ROPE_THETA = 10000.0
LN_EPS = 1e-5
RMS_EPS = 1e-6
NEG_INF = -1e30

SWA_HEAD_DIM = 64
SWA_WIDTH = D_MODEL // 2
SWA_HEADS = SWA_WIDTH // SWA_HEAD_DIM
SWA_KV_HEADS = SWA_HEADS // 8
SWA_GROUP = SWA_HEADS // SWA_KV_HEADS
SWA_KV_WIDTH = SWA_KV_HEADS * SWA_HEAD_DIM

SB_HEAD_DIM = 128
SB_WIDTH = D_MODEL - SWA_WIDTH
SB_HEADS = SB_WIDTH // SB_HEAD_DIM

IN_WIDTH = SWA_WIDTH + 2 * SWA_KV_WIDTH + 3 * SB_WIDTH
A_WIDTH = SWA_WIDTH + 2 * SWA_KV_WIDTH
Q_B_COL = 0
K_B_COL = Q_B_COL + SB_WIDTH
V_B_COL = K_B_COL + SB_WIDTH
Q_A_COL = V_B_COL + SB_WIDTH
K_A_COL = Q_A_COL + SWA_WIDTH
V_A_COL = K_A_COL + SWA_KV_WIDTH

N_GROUPS = 4
EXPERTS_PER_GROUP = 8
N_EXPERTS = N_GROUPS * EXPERTS_PER_GROUP
EXPERT_TOP_K = 2
DEPTH = 1
DEEPNORM_ALPHA = (2.0 * DEPTH) ** 0.25

LANES = 128
VMEM_LIMIT_BYTES = 56 * 1024 * 1024

EXP_ZERO_LOG = -104.0 - 2.0

TM_IN = 1024
TM_PROJ = 512
TN_PROJ = 512
TM_MOE = 256
TM_COMBINE = 128
SB_HEADS_PER_STEP = 8
ROW_CHUNK = 16
WEIGHT_DMA_PRIORITY = 1


def _cparams(sem):
    return pltpu.CompilerParams(dimension_semantics=sem,
                                vmem_limit_bytes=VMEM_LIMIT_BYTES)


def _pack_halves(lo, hi):
    return lax.bitcast_convert_type(
        pltpu.pack_elementwise([lo, hi], packed_dtype=BF16), U32)


def _unpack_halves(words):
    return tuple(
        pltpu.unpack_elementwise(words, index=k, packed_dtype=BF16,
                                 unpacked_dtype=F32).astype(BF16) for k in (0, 1))


def _dot_halves(lo, hi, w_ref):
    half = lo.shape[1]
    return (jnp.dot(lo, w_ref[:half], preferred_element_type=F32)
            + jnp.dot(hi, w_ref[half:], preferred_element_type=F32))


WORD_ROWS = (D_MODEL // 2) // LANES
SLAB_PAIR = 2 * LANES


def _slab_words(slab_ref, pair, tokens):
    return jnp.concatenate(
        [slab_ref[pl.ds(2 * pair + k, tokens, stride=WORD_ROWS), :] for k in (0, 1)],
        axis=1)


def _dot_slab(slab_ref, tokens, w_refs):
    half = D_MODEL // 2
    outs = [None] * len(w_refs)
    for pair in range(half // SLAB_PAIR):
        lo, hi = _unpack_halves(_slab_words(slab_ref, pair, tokens))
        rows_lo = slice(pair * SLAB_PAIR, (pair + 1) * SLAB_PAIR)
        rows_hi = slice(half + pair * SLAB_PAIR, half + (pair + 1) * SLAB_PAIR)
        for n, w_ref in enumerate(w_refs):
            part = (jnp.dot(lo, w_ref[rows_lo], preferred_element_type=F32)
                    + jnp.dot(hi, w_ref[rows_hi], preferred_element_type=F32))
            outs[n] = part if outs[n] is None else outs[n] + part
    return outs


def _dot_nt(a, b):
    return lax.dot_general(a, b, (((1,), (1,)), ((), ())),
                           preferred_element_type=F32)


def _inproj_kernel(x_ref, w_ref, b_ref, o_ref, xb_ref):
    @pl.when(pl.program_id(1) == 0)
    def _():
        xb_ref[...] = x_ref[...].astype(BF16)

    acc = jnp.dot(xb_ref[...], w_ref[...], preferred_element_type=F32)
    o_ref[...] = (acc + b_ref[...]).astype(o_ref.dtype)


def _in_projection(x2, w_b, b):
    m, k = x2.shape
    n = w_b.shape[1]
    n_blocks = n // TN_PROJ
    assert A_WIDTH % TN_PROJ == 0
    a_blocks = A_WIDTH // TN_PROJ

    def src(j):
        return lax.rem(j + a_blocks, n_blocks)

    return pl.pallas_call(
        _inproj_kernel,
        out_shape=jax.ShapeDtypeStruct((m, n), BF16),
        grid=(m // TM_IN, n_blocks),
        in_specs=[
            pl.BlockSpec((TM_IN, k), lambda i, j: (i, 0)),
            pl.BlockSpec((k, TN_PROJ), lambda i, j: (0, src(j))),
            pl.BlockSpec((1, TN_PROJ), lambda i, j: (0, src(j))),
        ],
        out_specs=pl.BlockSpec((TM_IN, TN_PROJ), lambda i, j: (i, j)),
        scratch_shapes=[pltpu.VMEM((TM_IN, k), BF16)],
        compiler_params=_cparams(("arbitrary", "arbitrary")),
        name="in_projection",
    )(x2, w_b, b)


def _swa_kernel(sinks_ref, q_ref, kc_ref, kp_ref, vc_ref, vp_ref,
                posc_ref, posp_ref, invf_ref, o_ref):
    n = pl.program_id(1)
    lane = lax.broadcasted_iota(I32, (1, LANES), 1)
    first_half = (lane % SWA_HEAD_DIM) < (SWA_HEAD_DIM // 2)

    def tables(pos_ref):
        ang = pos_ref[...].astype(F32) * invf_ref[...]
        sin = jnp.sin(ang)
        return jnp.cos(ang), jnp.where(first_half, -sin, sin)

    def rope(x, cos, sin_signed):
        partner = jnp.where(first_half,
                            pltpu.roll(x, LANES - SWA_HEAD_DIM // 2, 1),
                            pltpu.roll(x, SWA_HEAD_DIM // 2, 1))
        return x * cos + partner * sin_signed

    cos_c, sin_c = tables(posc_ref)
    cos_p, sin_p = tables(posp_ref)

    def rope_block(ref, cos, sin_signed):
        width = ref.shape[1]
        return [rope(ref[:, c * LANES:(c + 1) * LANES].astype(F32), cos,
                     sin_signed).astype(BF16) for c in range(width // LANES)]

    q_chunks = rope_block(q_ref, cos_c, sin_c)
    k_chunks = [jnp.concatenate([p_, c_], axis=0)
                for p_, c_ in zip(rope_block(kp_ref, cos_p, sin_p),
                                  rope_block(kc_ref, cos_c, sin_c))]
    v_all = jnp.concatenate([vp_ref[...], vc_ref[...]], axis=0)

    qi = lax.broadcasted_iota(I32, (BLOCK, 2 * BLOCK), 0)
    kj = lax.broadcasted_iota(I32, (BLOCK, 2 * BLOCK), 1)
    rel = qi - (kj - BLOCK)
    valid = (rel >= 0) & (rel < BLOCK) & ((kj >= BLOCK) | (n > 0))

    def head_slice(chunks, head):
        half = head % 2
        return chunks[head // 2][:, half * SWA_HEAD_DIM:(half + 1) * SWA_HEAD_DIM]

    scale = 1.0 / math.sqrt(SWA_HEAD_DIM)
    for h in range(SWA_KV_HEADS):
        k_h = head_slice(k_chunks, h)
        v_h = v_all[:, h * SWA_HEAD_DIM:(h + 1) * SWA_HEAD_DIM]
        q_h = jnp.concatenate(
            [head_slice(q_chunks, h * SWA_GROUP + g) for g in range(SWA_GROUP)],
            axis=0)
        s = _dot_nt(q_h, k_h) * scale
        probs = []
        for g in range(SWA_GROUP):
            sink = sinks_ref[h * SWA_GROUP + g]
            s_g = jnp.where(valid, s[g * BLOCK:(g + 1) * BLOCK], NEG_INF)
            m = jnp.maximum(jnp.max(s_g, axis=-1, keepdims=True), sink)
            e = jnp.exp(s_g - m)
            den = jnp.sum(e, axis=-1, keepdims=True) + jnp.exp(sink - m)
            probs.append((e / den).astype(BF16))
        o_h = jnp.dot(jnp.concatenate(probs, axis=0), v_h,
                      preferred_element_type=F32)
        for g in range(SWA_GROUP):
            col = (h * SWA_GROUP + g) * SWA_HEAD_DIM
            o_ref[:, col:col + SWA_HEAD_DIM] = o_h[g * BLOCK:(g + 1) * BLOCK]


def _swa_attention(proj, pos2, inv_freq, sinks, batch, seq):
    nb = seq // BLOCK
    assert Q_A_COL % SWA_WIDTH == 0 and K_A_COL % SWA_KV_WIDTH == 0
    assert V_A_COL % SWA_KV_WIDTH == 0
    qcol = Q_A_COL // SWA_WIDTH
    kcol = K_A_COL // SWA_KV_WIDTH
    vcol = V_A_COL // SWA_KV_WIDTH

    def cur(b, n):
        return b * nb + n

    def prev(b, n):
        return b * nb + jnp.maximum(n - 1, 0)

    return pl.pallas_call(
        _swa_kernel,
        out_shape=jax.ShapeDtypeStruct((batch * seq, SWA_WIDTH), F32),
        grid=(batch, nb),
        in_specs=[
            pl.BlockSpec(memory_space=pltpu.SMEM),
            pl.BlockSpec((BLOCK, SWA_WIDTH), lambda b, n: (cur(b, n), qcol)),
            pl.BlockSpec((BLOCK, SWA_KV_WIDTH), lambda b, n: (cur(b, n), kcol)),
            pl.BlockSpec((BLOCK, SWA_KV_WIDTH), lambda b, n: (prev(b, n), kcol)),
            pl.BlockSpec((BLOCK, SWA_KV_WIDTH), lambda b, n: (cur(b, n), vcol)),
            pl.BlockSpec((BLOCK, SWA_KV_WIDTH), lambda b, n: (prev(b, n), vcol)),
            pl.BlockSpec((BLOCK, 1), lambda b, n: (cur(b, n), 0)),
            pl.BlockSpec((BLOCK, 1), lambda b, n: (prev(b, n), 0)),
            pl.BlockSpec((1, LANES), lambda b, n: (0, 0)),
        ],
        out_specs=pl.BlockSpec((BLOCK, SWA_WIDTH), lambda b, n: (cur(b, n), 0)),
        compiler_params=_cparams(("arbitrary", "arbitrary")),
        name="swa_attention",
    )(sinks, proj, proj, proj, proj, proj, pos2, pos2, inv_freq)


def _sb_kernel(q_ref, k_ref, v_ref, o_ref):
    n = pl.program_id(2)
    heads = [slice(h * SB_HEAD_DIM, (h + 1) * SB_HEAD_DIM)
             for h in range(SB_HEADS_PER_STEP)]
    rows = SB_HEADS_PER_STEP * BLOCK
    scale = 1.0 / math.sqrt(SB_HEAD_DIM)
    key_j = lax.broadcasted_iota(I32, (BLOCK, 2 * BLOCK), 0)
    out_c = lax.broadcasted_iota(I32, (BLOCK, 2 * BLOCK), 1)
    later_total = ((key_j > out_c) | (out_c >= BLOCK)).astype(BF16)
    q_row = lax.broadcasted_iota(I32, (rows, BLOCK), 0) % BLOCK
    k_col = lax.broadcasted_iota(I32, (rows, BLOCK), 1)
    causal = k_col < q_row

    def block(kb, carry, acc, mask):
        start = pl.multiple_of(kb * BLOCK, BLOCK)
        z = jnp.concatenate(
            [_dot_nt(q_ref[:, hd], k_ref[pl.ds(start, BLOCK), hd]) for hd in heads],
            axis=0) * scale
        t = jnp.log(1.0 + jnp.exp(-jnp.abs(z)))
        log_not = jnp.minimum(-z, 0.0) - t
        log_beta = jnp.minimum(z, 0.0) - t
        if mask is not None:
            log_not = jnp.where(mask, log_not, 0.0)
        hi = log_not.astype(BF16)
        r1 = log_not - hi.astype(F32)
        mid = r1.astype(BF16)
        lo = (r1 - mid.astype(F32)).astype(BF16)
        parts = jnp.dot(jnp.concatenate([hi, mid, lo], axis=0), later_total,
                        preferred_element_type=F32)
        sums = (parts[:rows] + parts[rows:2 * rows]) + parts[2 * rows:]
        a = jnp.exp(log_beta + sums[:, :BLOCK] + carry)
        if mask is not None:
            a = jnp.where(mask, a, 0.0)
        a = a.astype(BF16)
        pv = jnp.concatenate(
            [jnp.dot(a[h * BLOCK:(h + 1) * BLOCK], v_ref[pl.ds(start, BLOCK), hd],
                     preferred_element_type=F32) for h, hd in enumerate(heads)], axis=0)
        return carry + sums[:, BLOCK:], acc + pv

    zeros = jnp.zeros((rows, BLOCK), F32)
    carry, acc = block(n, zeros, zeros, causal)

    def cond(state):
        kb, carry, _ = state
        return jnp.logical_and(kb >= 0, jnp.max(carry) > EXP_ZERO_LOG)

    def body(state):
        kb, carry, acc = state
        carry, acc = block(kb, carry, acc, None)
        return kb - 1, carry, acc

    _, _, acc = lax.while_loop(cond, body, (n - 1, carry, acc))
    for h, hd in enumerate(heads):
        o_ref[:, hd] = acc[h * BLOCK:(h + 1) * BLOCK]


def _sb_attention(proj, batch, seq):
    nb = seq // BLOCK
    width = SB_HEADS_PER_STEP * SB_HEAD_DIM
    assert Q_B_COL % width == 0 and K_B_COL % width == 0 and V_B_COL % width == 0
    qcol = Q_B_COL // width
    kcol = K_B_COL // width
    vcol = V_B_COL // width
    return pl.pallas_call(
        _sb_kernel,
        out_shape=jax.ShapeDtypeStruct((batch * seq, SB_WIDTH), F32),
        grid=(batch, SB_HEADS // SB_HEADS_PER_STEP, nb),
        in_specs=[
            pl.BlockSpec((BLOCK, width), lambda b, h, n: (b * nb + n, qcol + h)),
            pl.BlockSpec((seq, width), lambda b, h, n: (b, kcol + h)),
            pl.BlockSpec((seq, width), lambda b, h, n: (b, vcol + h)),
        ],
        out_specs=pl.BlockSpec((BLOCK, width), lambda b, h, n: (b * nb + n, h)),
        compiler_params=_cparams(("arbitrary", "arbitrary", "arbitrary")),
        name="sb_attention",
    )(proj, proj, proj)


def _row_stats(chunks_ref, n_chunks, width):
    total = chunks_ref[0].sum(axis=-1, keepdims=True)
    for c in range(1, n_chunks):
        total = total + chunks_ref[c].sum(axis=-1, keepdims=True)
    mu = total / width
    sq = jnp.square(chunks_ref[0] - mu).sum(axis=-1, keepdims=True)
    for c in range(1, n_chunks):
        sq = sq + jnp.square(chunks_ref[c] - mu).sum(axis=-1, keepdims=True)
    return mu, lax.rsqrt(sq / width + LN_EPS)


def _outproj_kernel(oa_ref, ob_ref, ga_ref, gb_ref, wa_ref, wb_ref, x_ref, bo_ref,
                    lg_ref, lb_ref, y_ref, mu_ref, rs_ref, x1s_ref,
                    ma_ref, mb_ref, acc_ref):
    j = pl.program_id(1)
    n_chunks = acc_ref.shape[0]
    tn = acc_ref.shape[2]

    @pl.when(j == 0)
    def _():
        for o_ref, g_ref, m_ref in ((oa_ref, ga_ref, ma_ref), (ob_ref, gb_ref, mb_ref)):
            width = o_ref.shape[1]
            cols = [slice(c * tn, (c + 1) * tn) for c in range(width // tn)]
            sq = sum(jnp.square(o_ref[:, sl]).sum(axis=-1, keepdims=True) for sl in cols)
            r = lax.rsqrt(sq / width + RMS_EPS)
            for sl in cols:
                m_ref[:, sl] = (o_ref[:, sl] * r * g_ref[:, sl]).astype(BF16)

    mix = (jnp.dot(ma_ref[...], wa_ref[...], preferred_element_type=F32)
           + jnp.dot(mb_ref[...], wb_ref[...], preferred_element_type=F32))
    y = DEEPNORM_ALPHA * x_ref[...] + (mix + bo_ref[...])
    y_ref[...] = y
    acc_ref[j] = y

    @pl.when(j == n_chunks - 1)
    def _():
        mu, rs = _row_stats(acc_ref, n_chunks, n_chunks * tn)
        mu_ref[...] = mu
        rs_ref[...] = rs

        def normed(c):
            sl = slice(c * tn, (c + 1) * tn)
            return (acc_ref[c] - mu) * rs * lg_ref[:, sl] + lb_ref[:, sl]

        tm = acc_ref.shape[1]
        for c in range(n_chunks // 2):
            words = _pack_halves(normed(c), normed(c + n_chunks // 2))
            for q in range(tn // LANES):
                x1s_ref[pl.ds(c * (tn // LANES) + q, tm, stride=WORD_ROWS), :] = (
                    words[:, q * LANES:(q + 1) * LANES])


def _out_projection(o_a, o_b, g_a, g_b, w_b, x2, b_out, ln_g, ln_b):
    m, d = x2.shape
    tm, tn = TM_PROJ, TN_PROJ
    half = o_a.shape[1]
    once = dict(pipeline_mode=pl.Buffered(1))
    return pl.pallas_call(
        _outproj_kernel,
        out_shape=(jax.ShapeDtypeStruct((m, d), F32),
                   jax.ShapeDtypeStruct((m, 1), F32),
                   jax.ShapeDtypeStruct((m, 1), F32),
                   jax.ShapeDtypeStruct((m * WORD_ROWS, LANES), U32)),
        grid=(m // tm, d // tn),
        in_specs=[
            pl.BlockSpec((tm, half), lambda i, j: (i, 0), **once),
            pl.BlockSpec((tm, half), lambda i, j: (i, 0), **once),
            pl.BlockSpec((1, half), lambda i, j: (0, 0)),
            pl.BlockSpec((1, half), lambda i, j: (0, 0)),
            pl.BlockSpec((half, tn), lambda i, j: (0, j)),
            pl.BlockSpec((half, tn), lambda i, j: (1, j)),
            pl.BlockSpec((tm, tn), lambda i, j: (i, j)),
            pl.BlockSpec((1, tn), lambda i, j: (0, j)),
            pl.BlockSpec((1, d), lambda i, j: (0, 0)),
            pl.BlockSpec((1, d), lambda i, j: (0, 0)),
        ],
        out_specs=(pl.BlockSpec((tm, tn), lambda i, j: (i, j)),
                   pl.BlockSpec((tm, 1), lambda i, j: (i, 0)),
                   pl.BlockSpec((tm, 1), lambda i, j: (i, 0)),
                   pl.BlockSpec((tm * WORD_ROWS, LANES), lambda i, j: (i, 0))),
        scratch_shapes=[pltpu.VMEM((tm, half), BF16), pltpu.VMEM((tm, half), BF16),
                        pltpu.VMEM((d // tn, tm, tn), F32)],
        compiler_params=_cparams(("arbitrary", "arbitrary")),
        name="out_projection_ln1",
    )(o_a, o_b, g_a, g_b, w_b, w_b, x2, b_out, ln_g, ln_b)


ROUTER_COLS = LANES
EXPERT_LANE0 = N_GROUPS


def _router_kernel(x_ref, w_ref, b_ref, ids_ref, wts_ref, cnt_ref, carry_ref):
    i = pl.program_id(0)
    tm = wts_ref.shape[0]

    @pl.when(i == 0)
    def _():
        carry_ref[...] = jnp.zeros_like(carry_ref)

    logits = _dot_slab(x_ref, tm, [w_ref])[0] + b_ref[...]
    lane = lax.broadcasted_iota(I32, (tm, ROUTER_COLS), 1)
    big = jnp.int32(ROUTER_COLS)

    def first_argmax(vals):
        top = jnp.max(vals, axis=-1, keepdims=True)
        idx = jnp.min(jnp.where(vals == top, lane, big), axis=-1, keepdims=True)
        return top, idx

    is_group = lane < N_GROUPS
    g_logits = jnp.where(is_group, logits, -jnp.inf)
    g_top, g_idx = first_argmax(g_logits)
    g_w = 1.0 / jnp.sum(jnp.exp(g_logits - g_top), axis=-1, keepdims=True)

    first = EXPERT_LANE0 + g_idx * EXPERTS_PER_GROUP
    in_group = (lane >= first) & (lane < first + EXPERTS_PER_GROUP)
    e_logits = jnp.where(in_group, logits, -jnp.inf)
    top1, idx1 = first_argmax(e_logits)
    top2, idx2 = first_argmax(jnp.where(lane == idx1, -jnp.inf, e_logits))
    e2 = jnp.exp(top2 - top1)
    w1 = g_w / (1.0 + e2)
    w2 = g_w * e2 / (1.0 + e2)

    hit1 = lane == idx1
    hit2 = lane == idx2
    onehot = (hit1 | hit2).astype(BF16)
    r = lax.broadcasted_iota(I32, (tm, tm), 0)
    c = lax.broadcasted_iota(I32, (tm, tm), 1)
    before = (c < r).astype(BF16)
    prior = jnp.dot(before, onehot, preferred_element_type=F32) + carry_ref[0:1, :]
    rank1 = jnp.sum(jnp.where(hit1, prior, 0.0), axis=-1, keepdims=True)
    rank2 = jnp.sum(jnp.where(hit2, prior, 0.0), axis=-1, keepdims=True)
    counts = carry_ref[0:1, :] + jnp.sum(onehot.astype(F32), axis=0, keepdims=True)
    carry_ref[...] = jnp.broadcast_to(counts, carry_ref.shape)
    cnt_ref[...] = jnp.broadcast_to(counts, cnt_ref.shape).astype(I32)

    ids = jnp.where(lane == 0, idx1 - EXPERT_LANE0,
          jnp.where(lane == 1, idx2 - EXPERT_LANE0,
          jnp.where(lane == 2, rank1.astype(I32),
          jnp.where(lane == 3, rank2.astype(I32), 0))))
    ids_ref[...] = ids.T[:ids_ref.shape[0]]
    wts_ref[...] = jnp.where(lane == 0, w1, jnp.where(lane == 1, w2, 0.0))


def _router(x1s, w_r, b_r):
    m = x1s.shape[0] // WORD_ROWS
    tm = TM_PROJ
    return pl.pallas_call(
        _router_kernel,
        out_shape=(jax.ShapeDtypeStruct((8, m), I32),
                   jax.ShapeDtypeStruct((m, ROUTER_COLS), F32),
                   jax.ShapeDtypeStruct((8, ROUTER_COLS), I32)),
        grid=(m // tm,),
        in_specs=[
            pl.BlockSpec((tm * WORD_ROWS, LANES), lambda i: (i, 0)),
            pl.BlockSpec((D_MODEL, ROUTER_COLS), lambda i: (0, 0)),
            pl.BlockSpec((1, ROUTER_COLS), lambda i: (0, 0)),
        ],
        out_specs=(pl.BlockSpec((8, tm), lambda i: (0, i)),
                   pl.BlockSpec((tm, ROUTER_COLS), lambda i: (i, 0)),
                   pl.BlockSpec((8, ROUTER_COLS), lambda i: (0, 0))),
        scratch_shapes=[pltpu.VMEM((8, ROUTER_COLS), F32)],
        compiler_params=_cparams(("arbitrary",)),
        name="router",
    )(x1s, w_r, b_r)


def _start_row_gather(idx_ref, first, n_items, span, src_ref, dst_ref, sem):
    def body(c, _):
        for u in range(ROW_CHUNK):
            r = c * ROW_CHUNK + u
            src_row = pl.multiple_of(idx_ref[first + r], span)
            dst_row = pl.multiple_of(r * span, span)
            pltpu.make_async_copy(src_ref.at[pl.ds(src_row, span)],
                                  dst_ref.at[pl.ds(dst_row, span)], sem).start()
        return 0

    lax.fori_loop(0, n_items // ROW_CHUNK, body, 0)


def _wait_row_gather(n_items, span, src_ref, dst_ref, sem):
    def body(c, _):
        for _u in range(ROW_CHUNK):
            pltpu.make_async_copy(src_ref.at[pl.ds(0, span)], dst_ref.at[pl.ds(0, span)],
                                  sem).wait()
        return 0

    lax.fori_loop(0, n_items // ROW_CHUNK, body, 0)


def _expert_changed(te_ref, i):
    return jnp.logical_or(i == 0, te_ref[i] != te_ref[jnp.maximum(i - 1, 0)])


def _tile_row(i, nu):
    return jnp.minimum(i, nu[0] - 1)


def _stage_expert_weights(i, te_ref, nxt_ref, ws_ref, w_hbms, stage_ref, bf_refs, sem):
    expert = te_ref[i]
    slot = ws_ref[expert]

    def copies(e, s):
        return [pltpu.make_async_copy(w.at[e], stage_ref.at[s, n], sem.at[s])
                for n, w in enumerate(w_hbms)]

    @pl.when(i == 0)
    def _():
        for c in copies(expert, slot):
            c.start(priority=WEIGHT_DMA_PRIORITY)

    @pl.when(_expert_changed(te_ref, i))
    def _():
        for c in copies(expert, slot):
            c.wait()
        nxt = nxt_ref[expert]

        @pl.when(nxt < N_EXPERTS)
        def _():
            for c in copies(nxt, 1 - slot):
                c.start(priority=WEIGHT_DMA_PRIORITY)

        for n, bf_ref in enumerate(bf_refs):
            bf_ref[...] = stage_ref[slot, n].astype(BF16)


def _gateup_kernel(te_ref, nu_ref, tos_ref, nxt_ref, ws_ref, x_hbm, wg_hbm, wu_hbm, a_ref,
                   xbuf_ref, stage_ref, wgb_ref, wub_ref, sems, wsem):
    i = pl.program_id(0)
    n_used = nu_ref[0]
    slot = lax.rem(i, 2)

    def start(tile, buf):
        _start_row_gather(tos_ref, tile * TM_MOE, TM_MOE, WORD_ROWS, x_hbm,
                          xbuf_ref.at[buf], sems.at[buf])

    @pl.when(i == 0)
    def _():
        start(0, 0)

    @pl.when(i + 1 < n_used)
    def _():
        start(i + 1, 1 - slot)

    _stage_expert_weights(i, te_ref, nxt_ref, ws_ref, [wg_hbm, wu_hbm], stage_ref,
                          [wgb_ref, wub_ref], wsem)

    @pl.when(i < n_used)
    def _():
        _wait_row_gather(TM_MOE, WORD_ROWS, x_hbm, xbuf_ref.at[slot], sems.at[slot])
        gate, up = _dot_slab(xbuf_ref.at[slot], TM_MOE, [wgb_ref, wub_ref])
        a_ref[...] = (gate * jax.nn.sigmoid(gate) * up).astype(a_ref.dtype)

    @pl.when(i >= n_used)
    def _():
        a_ref[...] = jnp.zeros_like(a_ref)


def _grouped_gate_up(tile_expert, n_used, slab_of_slot, next_expert, stage_slot,
                     x1s, w_gate, w_up):
    p_rows = slab_of_slot.shape[0]
    _, d, f = w_gate.shape
    hbm = pl.BlockSpec(memory_space=pl.ANY)
    return pl.pallas_call(
        _gateup_kernel,
        out_shape=jax.ShapeDtypeStruct((p_rows, f), BF16),
        grid_spec=pltpu.PrefetchScalarGridSpec(
            num_scalar_prefetch=5,
            grid=(p_rows // TM_MOE,),
            in_specs=[hbm, hbm, hbm],
            out_specs=pl.BlockSpec((TM_MOE, f), lambda i, *_: (i, 0)),
            scratch_shapes=[pltpu.VMEM((2, TM_MOE * WORD_ROWS, LANES), U32),
                            pltpu.VMEM((2, 2, d, f), F32),
                            pltpu.VMEM((d, f), BF16), pltpu.VMEM((d, f), BF16),
                            pltpu.SemaphoreType.DMA((2,)), pltpu.SemaphoreType.DMA((2,))],
        ),
        compiler_params=_cparams(("arbitrary",)),
        name="moe_gate_up",
    )(tile_expert, n_used, slab_of_slot, next_expert, stage_slot, x1s, w_gate, w_up)


def _down_kernel(te_ref, nu_ref, nxt_ref, ws_ref, a_ref, wd_hbm, y_ref,
                 stage_ref, wdb_ref, wsem):
    i = pl.program_id(0)
    _stage_expert_weights(i, te_ref, nxt_ref, ws_ref, [wd_hbm], stage_ref, [wdb_ref], wsem)

    @pl.when(i < nu_ref[0])
    def _():
        y_ref[:, 0, :] = jnp.dot(a_ref[...], wdb_ref[...], preferred_element_type=F32)

    @pl.when(i >= nu_ref[0])
    def _():
        y_ref[...] = jnp.zeros_like(y_ref)


def _grouped_down(tile_expert, n_used, next_expert, stage_slot, act, w_down):
    p_rows, f = act.shape
    d = w_down.shape[2]
    return pl.pallas_call(
        _down_kernel,
        out_shape=jax.ShapeDtypeStruct((p_rows, 1, d), F32),
        grid_spec=pltpu.PrefetchScalarGridSpec(
            num_scalar_prefetch=4,
            grid=(p_rows // TM_MOE,),
            in_specs=[
                pl.BlockSpec((TM_MOE, f), lambda i, te, nu, *_: (_tile_row(i, nu), 0)),
                pl.BlockSpec(memory_space=pl.ANY),
            ],
            out_specs=pl.BlockSpec((TM_MOE, 1, d), lambda i, *_: (i, 0, 0)),
            scratch_shapes=[pltpu.VMEM((2, 1, f, d), F32), pltpu.VMEM((f, d), BF16),
                            pltpu.SemaphoreType.DMA((2,))],
        ),
        compiler_params=_cparams(("arbitrary",)),
        name="moe_down",
    )(tile_expert, n_used, next_expert, stage_slot, act, w_down)


def _combine_kernel(slot_ref, y_hbm, wts_ref, o_ref, buf_ref, sems):
    i = pl.program_id(0)
    tm = o_ref.shape[0]
    n_tokens = slot_ref.shape[0] // EXPERT_TOP_K
    slot = lax.rem(i, 2)

    def start(tile, buf):
        for k in range(EXPERT_TOP_K):
            _start_row_gather(slot_ref, k * n_tokens + tile * tm, tm, 1, y_hbm,
                              buf_ref.at[buf, k], sems.at[buf])

    @pl.when(i == 0)
    def _():
        start(0, 0)

    @pl.when(i + 1 < pl.num_programs(0))
    def _():
        start(i + 1, 1 - slot)

    _wait_row_gather(EXPERT_TOP_K * tm, 1, y_hbm, buf_ref.at[slot, 0], sems.at[slot])
    for c in range(o_ref.shape[1] // TN_PROJ):
        sl = slice(c * TN_PROJ, (c + 1) * TN_PROJ)
        o_ref[:, sl] = sum(wts_ref[:, k:k + 1] * buf_ref[slot, k, :, 0, sl]
                           for k in range(EXPERT_TOP_K))


def _combine_experts(slot_kt, y_sorted, wts):
    d = y_sorted.shape[2]
    m = slot_kt.shape[0] // EXPERT_TOP_K
    tm = TM_COMBINE
    return pl.pallas_call(
        _combine_kernel,
        out_shape=jax.ShapeDtypeStruct((m, d), F32),
        grid_spec=pltpu.PrefetchScalarGridSpec(
            num_scalar_prefetch=1,
            grid=(m // tm,),
            in_specs=[
                pl.BlockSpec(memory_space=pl.ANY),
                pl.BlockSpec((tm, ROUTER_COLS), lambda i, sl: (i, 0)),
            ],
            out_specs=pl.BlockSpec((tm, d), lambda i, sl: (i, 0)),
            scratch_shapes=[pltpu.VMEM((2, EXPERT_TOP_K, tm, 1, d), F32),
                            pltpu.SemaphoreType.DMA((2,))],
        ),
        compiler_params=_cparams(("arbitrary",)),
        name="moe_combine",
    )(slot_kt, y_sorted, wts)


def _final_kernel(x1s_ref, wg_ref, bg_ref, p_ref, wp_ref, y1_ref, mu_ref, rs_ref,
                  l1g_ref, l1b_ref, moe_ref, l2g_ref, l2b_ref,
                  o_ref, lo_ref, hi_ref, acc_ref, mu2_ref, rs2_ref):
    i = pl.program_id(0)
    j = pl.program_id(1)
    n_tiles = pl.num_programs(0) - 1
    n_chunks = acc_ref.shape[1]
    tn = acc_ref.shape[3]
    cur = lax.rem(i, 2)
    prv = 1 - cur

    @pl.when(jnp.logical_and(j == 0, i < n_tiles))
    def _():
        for s in range(WORD_ROWS):
            sl = slice(s * LANES, (s + 1) * LANES)
            lo_ref[:, sl], hi_ref[:, sl] = _unpack_halves(
                x1s_ref[pl.ds(s, lo_ref.shape[0], stride=WORD_ROWS), :])

    def build():
        gate = _dot_halves(lo_ref[...], hi_ref[...], wg_ref) + bg_ref[...]
        emb = jnp.dot(p_ref[...].astype(BF16), wp_ref[...].astype(BF16),
                      preferred_element_type=F32)
        x1 = (y1_ref[...] - mu_ref[...]) * rs_ref[...] * l1g_ref[...] + l1b_ref[...]
        acc_ref[cur, j] = DEEPNORM_ALPHA * x1 + moe_ref[...] + jax.nn.sigmoid(gate) * emb

    def emit():
        o_ref[...] = ((acc_ref[prv, j] - mu2_ref[prv]) * rs2_ref[prv]
                      * l2g_ref[...] + l2b_ref[...])

    @pl.when(i == 0)
    def _():
        build()

    @pl.when(jnp.logical_and(i > 0, i < n_tiles))
    def _():
        emit()
        build()

    @pl.when(i == n_tiles)
    def _():
        emit()

    @pl.when(jnp.logical_and(j == n_chunks - 1, i < n_tiles))
    def _():
        mu2_ref[cur], rs2_ref[cur] = _row_stats(acc_ref.at[cur], n_chunks, n_chunks * tn)


def _final_stage(x1s, w_pg_b, b_pg, p2, w_pp, y1, mu1, rs1, ln1_g, ln1_b,
                 moe, ln2_g, ln2_b):
    m, d = y1.shape
    tm, tn = TM_PROJ, TN_PROJ
    ple = p2.shape[1]
    n_chunks = d // tn

    n_tiles = m // tm

    def built(i):
        return jnp.minimum(i, n_tiles - 1)

    def emitted(i):
        return jnp.maximum(i - 1, 0)

    return pl.pallas_call(
        _final_kernel,
        out_shape=jax.ShapeDtypeStruct((m, d), F32),
        grid=(n_tiles + 1, n_chunks),
        in_specs=[
            pl.BlockSpec((tm * WORD_ROWS, LANES), lambda i, j: (built(i), 0),
                         pipeline_mode=pl.Buffered(1)),
            pl.BlockSpec((d, tn), lambda i, j: (0, j)),
            pl.BlockSpec((1, tn), lambda i, j: (0, j)),
            pl.BlockSpec((tm, ple), lambda i, j: (built(i), 0)),
            pl.BlockSpec((ple, tn), lambda i, j: (0, j)),
            pl.BlockSpec((tm, tn), lambda i, j: (built(i), j)),
            pl.BlockSpec((tm, 1), lambda i, j: (built(i), 0)),
            pl.BlockSpec((tm, 1), lambda i, j: (built(i), 0)),
            pl.BlockSpec((1, tn), lambda i, j: (0, j)),
            pl.BlockSpec((1, tn), lambda i, j: (0, j)),
            pl.BlockSpec((tm, tn), lambda i, j: (built(i), j)),
            pl.BlockSpec((1, tn), lambda i, j: (0, j)),
            pl.BlockSpec((1, tn), lambda i, j: (0, j)),
        ],
        out_specs=pl.BlockSpec((tm, tn), lambda i, j: (emitted(i), jnp.where(i == 0, 0, j))),
        scratch_shapes=[pltpu.VMEM((tm, d // 2), BF16), pltpu.VMEM((tm, d // 2), BF16),
                        pltpu.VMEM((2, n_chunks, tm, tn), F32),
                        pltpu.VMEM((2, tm, 1), F32), pltpu.VMEM((2, tm, 1), F32)],
        compiler_params=_cparams(("arbitrary", "arbitrary")),
        name="ple_moe_ln2",
    )(x1s, w_pg_b, b_pg, p2, w_pp, y1, mu1, rs1, ln1_g, ln1_b, moe, ln2_g, ln2_b)


def _routing_tables(ids_t, counts_row):
    counts = counts_row[EXPERT_LANE0:EXPERT_LANE0 + N_EXPERTS]
    tiles = (counts + TM_MOE - 1) // TM_MOE
    tile_end = jnp.cumsum(tiles)
    offsets = (tile_end - tiles) * TM_MOE
    n_used = tile_end[-1:]
    n_tokens = ids_t.shape[1]
    experts = ids_t[0:EXPERT_TOP_K]
    hit = experts[None] == jnp.arange(N_EXPERTS, dtype=I32)[:, None, None]
    first_slot = jnp.sum(jnp.where(hit, offsets[:, None, None], 0), axis=0)
    slots = (first_slot + ids_t[EXPERT_TOP_K:2 * EXPERT_TOP_K]).reshape(-1)
    n_tiles = (EXPERT_TOP_K * n_tokens) // TM_MOE + N_EXPERTS
    tile_ids = jnp.minimum(jnp.arange(n_tiles, dtype=I32), n_used - 1)
    tile_expert = jnp.sum(tile_end[None, :] <= tile_ids[:, None], axis=1).astype(I32)
    slab_of_slot = jnp.zeros((n_tiles * TM_MOE,), I32).at[slots].set(
        jnp.arange(slots.size, dtype=I32) % n_tokens * WORD_ROWS, unique_indices=True)
    eid = jnp.arange(N_EXPERTS, dtype=I32)
    later_owner = (eid[None, :] > eid[:, None]) & (tiles[None, :] > 0)
    next_expert = jnp.min(jnp.where(later_owner, eid[None, :], N_EXPERTS), axis=1).astype(I32)
    stage_slot = ((jnp.cumsum(tiles > 0) - 1) % 2).astype(I32)
    return (slots.astype(I32), slab_of_slot, tile_expert, n_used.astype(I32),
            next_expert, stage_slot)


def kernel(x, p, positions, w_in, b_in, sinks, g_norm_a, g_norm_b, w_out, b_out,
           ln1_g, ln1_b, w_group, b_group, w_er, b_er, w_gate, w_up, w_down,
           w_ple_gate, b_ple_gate, w_ple_proj, ln2_g, ln2_b):
    batch, seq, d = x.shape
    m = batch * seq
    row = lambda v: v.reshape(1, -1)
    x2 = x.reshape(m, d)
    for i in range(DEPTH):
        proj = _in_projection(x2, w_in[i].astype(BF16), row(b_in[i]))
        inv_freq = ROPE_THETA ** (-jnp.arange(0, SWA_HEAD_DIM, 2, dtype=F32) / SWA_HEAD_DIM)
        inv_freq = jnp.tile(inv_freq, LANES // inv_freq.shape[0]).reshape(1, LANES)
        o_a = _swa_attention(proj, positions.reshape(m, 1), inv_freq, sinks[i], batch, seq)
        o_b = _sb_attention(proj, batch, seq)
        y1, mu1, rs1, x1s = _out_projection(
            o_a, o_b, row(g_norm_a[i]), row(g_norm_b[i]), w_out[i].astype(BF16), x2,
            row(b_out[i]), row(ln1_g[i]), row(ln1_b[i]))
        pad = ROUTER_COLS - N_GROUPS - N_EXPERTS
        w_r = jnp.concatenate(
            [w_group[i], w_er[i].transpose(1, 0, 2).reshape(d, N_EXPERTS),
             jnp.zeros((d, pad), F32)], axis=1).astype(BF16)
        b_r = jnp.concatenate([b_group[i], b_er[i].reshape(-1), jnp.zeros((pad,), F32)])
        ids, wts, counts = _router(x1s, w_r, row(b_r))
        (slots, slab_of_slot, tile_expert, n_used, next_expert,
         stage_slot) = _routing_tables(ids, counts[0])
        act = _grouped_gate_up(tile_expert, n_used, slab_of_slot, next_expert, stage_slot,
                               x1s, w_gate[i], w_up[i])
        y_sorted = _grouped_down(tile_expert, n_used, next_expert, stage_slot, act, w_down[i])
        moe = _combine_experts(slots, y_sorted, wts)
        x2 = _final_stage(x1s, w_ple_gate[i].astype(BF16), row(b_ple_gate[i]),
                          p[i].reshape(m, PLE_DIM), w_ple_proj[i], y1, mu1, rs1,
                          row(ln1_g[i]), row(ln1_b[i]), moe, row(ln2_g[i]), row(ln2_b[i]))
    return x2.reshape(batch, seq, d)
```

```python
import math

import jax
import jax.numpy as jnp
from jax import lax
from jax.experimental import pallas as pl
from jax.experimental.pallas import tpu as pltpu

F32 = jnp.float32
BF16 = jnp.bfloat16
I32 = jnp.int32
U32 = jnp.uint32

D_MODEL = 4096
PLE_DIM = 256
BLOCK = 128
ROPE_THETA = 10000.0
LN_EPS = 1e-5
RMS_EPS = 1e-6
NEG_INF = -1e30

SWA_HEAD_DIM = 64
SWA_WIDTH = D_MODEL // 2
SWA_HEADS = SWA_WIDTH // SWA_HEAD_DIM
SWA_KV_HEADS = SWA_HEADS // 8
SWA_GROUP = SWA_HEADS // SWA_KV_HEADS
SWA_KV_WIDTH = SWA_KV_HEADS * SWA_HEAD_DIM

SB_HEAD_DIM = 128
SB_WIDTH = D_MODEL - SWA_WIDTH
SB_HEADS = SB_WIDTH // SB_HEAD_DIM

IN_WIDTH = SWA_WIDTH + 2 * SWA_KV_WIDTH + 3 * SB_WIDTH
A_WIDTH = SWA_WIDTH + 2 * SWA_KV_WIDTH
Q_B_COL = 0
K_B_COL = Q_B_COL + SB_WIDTH
V_B_COL = K_B_COL + SB_WIDTH
Q_A_COL = V_B_COL + SB_WIDTH
K_A_COL = Q_A_COL + SWA_WIDTH
V_A_COL = K_A_COL + SWA_KV_WIDTH

N_GROUPS = 4
EXPERTS_PER_GROUP = 8
N_EXPERTS = N_GROUPS * EXPERTS_PER_GROUP
EXPERT_TOP_K = 2
DEPTH = 1
DEEPNORM_ALPHA = (2.0 * DEPTH) ** 0.25

LANES = 128
VMEM_LIMIT_BYTES = 56 * 1024 * 1024

EXP_ZERO_LOG = -104.0 - 2.0

TM_IN = 1024
TM_PROJ = 512
TN_PROJ = 512
TM_MOE = 256
TM_COMBINE = 128
SB_HEADS_PER_STEP = 8
ROW_CHUNK = 16
WEIGHT_DMA_PRIORITY = 1


def _cparams(sem):
    return pltpu.CompilerParams(dimension_semantics=sem,
                                vmem_limit_bytes=VMEM_LIMIT_BYTES)


def _pack_halves(lo, hi):
    return lax.bitcast_convert_type(
        pltpu.pack_elementwise([lo, hi], packed_dtype=BF16), U32)


def _unpack_halves(words):
    return tuple(
        pltpu.unpack_elementwise(words, index=k, packed_dtype=BF16,
                                 unpacked_dtype=F32).astype(BF16) for k in (0, 1))


def _dot_halves(lo, hi, w_ref):
    half = lo.shape[1]
    return (jnp.dot(lo, w_ref[:half], preferred_element_type=F32)
            + jnp.dot(hi, w_ref[half:], preferred_element_type=F32))


WORD_ROWS = (D_MODEL // 2) // LANES
SLAB_PAIR = 2 * LANES


def _slab_words(slab_ref, pair, tokens):
    return jnp.concatenate(
        [slab_ref[pl.ds(2 * pair + k, tokens, stride=WORD_ROWS), :] for k in (0, 1)],
        axis=1)


def _dot_slab(slab_ref, tokens, w_refs):
    half = D_MODEL // 2
    outs = [None] * len(w_refs)
    for pair in range(half // SLAB_PAIR):
        lo, hi = _unpack_halves(_slab_words(slab_ref, pair, tokens))
        rows_lo = slice(pair * SLAB_PAIR, (pair + 1) * SLAB_PAIR)
        rows_hi = slice(half + pair * SLAB_PAIR, half + (pair + 1) * SLAB_PAIR)
        for n, w_ref in enumerate(w_refs):
            part = (jnp.dot(lo, w_ref[rows_lo], preferred_element_type=F32)
                    + jnp.dot(hi, w_ref[rows_hi], preferred_element_type=F32))
            outs[n] = part if outs[n] is None else outs[n] + part
    return outs


def _dot_nt(a, b):
    return lax.dot_general(a, b, (((1,), (1,)), ((), ())),
                           preferred_element_type=F32)


def _inproj_kernel(x_ref, w_ref, b_ref, o_ref, xb_ref):
    @pl.when(pl.program_id(1) == 0)
    def _():
        xb_ref[...] = x_ref[...].astype(BF16)

    acc = jnp.dot(xb_ref[...], w_ref[...], preferred_element_type=F32)
    o_ref[...] = (acc + b_ref[...]).astype(o_ref.dtype)


def _in_projection(x2, w_b, b):
    m, k = x2.shape
    n = w_b.shape[1]
    n_blocks = n // TN_PROJ
    assert A_WIDTH % TN_PROJ == 0
    a_blocks = A_WIDTH // TN_PROJ

    def src(j):
        return lax.rem(j + a_blocks, n_blocks)

    return pl.pallas_call(
        _inproj_kernel,
        out_shape=jax.ShapeDtypeStruct((m, n), BF16),
        grid=(m // TM_IN, n_blocks),
        in_specs=[
            pl.BlockSpec((TM_IN, k), lambda i, j: (i, 0)),
            pl.BlockSpec((k, TN_PROJ), lambda i, j: (0, src(j))),
            pl.BlockSpec((1, TN_PROJ), lambda i, j: (0, src(j))),
        ],
        out_specs=pl.BlockSpec((TM_IN, TN_PROJ), lambda i, j: (i, j)),
        scratch_shapes=[pltpu.VMEM((TM_IN, k), BF16)],
        compiler_params=_cparams(("arbitrary", "arbitrary")),
        name="in_projection",
    )(x2, w_b, b)


def _swa_kernel(sinks_ref, q_ref, kc_ref, kp_ref, vc_ref, vp_ref,
                posc_ref, posp_ref, invf_ref, o_ref):
    n = pl.program_id(1)
    lane = lax.broadcasted_iota(I32, (1, LANES), 1)
    first_half = (lane % SWA_HEAD_DIM) < (SWA_HEAD_DIM // 2)

    def tables(pos_ref):
        ang = pos_ref[...].astype(F32) * invf_ref[...]
        sin = jnp.sin(ang)
        return jnp.cos(ang), jnp.where(first_half, -sin, sin)

    def rope(x, cos, sin_signed):
        partner = jnp.where(first_half,
                            pltpu.roll(x, LANES - SWA_HEAD_DIM // 2, 1),
                            pltpu.roll(x, SWA_HEAD_DIM // 2, 1))
        return x * cos + partner * sin_signed

    cos_c, sin_c = tables(posc_ref)
    cos_p, sin_p = tables(posp_ref)

    def rope_block(ref, cos, sin_signed):
        width = ref.shape[1]
        return [rope(ref[:, c * LANES:(c + 1) * LANES].astype(F32), cos,
                     sin_signed).astype(BF16) for c in range(width // LANES)]

    q_chunks = rope_block(q_ref, cos_c, sin_c)
    k_chunks = [jnp.concatenate([p_, c_], axis=0)
                for p_, c_ in zip(rope_block(kp_ref, cos_p, sin_p),
                                  rope_block(kc_ref, cos_c, sin_c))]
    v_all = jnp.concatenate([vp_ref[...], vc_ref[...]], axis=0)

    qi = lax.broadcasted_iota(I32, (BLOCK, 2 * BLOCK), 0)
    kj = lax.broadcasted_iota(I32, (BLOCK, 2 * BLOCK), 1)
    rel = qi - (kj - BLOCK)
    valid = (rel >= 0) & (rel < BLOCK) & ((kj >= BLOCK) | (n > 0))

    def head_slice(chunks, head):
        half = head % 2
        return chunks[head // 2][:, half * SWA_HEAD_DIM:(half + 1) * SWA_HEAD_DIM]

    scale = 1.0 / math.sqrt(SWA_HEAD_DIM)
    for h in range(SWA_KV_HEADS):
        k_h = head_slice(k_chunks, h)
        v_h = v_all[:, h * SWA_HEAD_DIM:(h + 1) * SWA_HEAD_DIM]
        q_h = jnp.concatenate(
            [head_slice(q_chunks, h * SWA_GROUP + g) for g in range(SWA_GROUP)],
            axis=0)
        s = _dot_nt(q_h, k_h) * scale
        probs = []
        for g in range(SWA_GROUP):
            sink = sinks_ref[h * SWA_GROUP + g]
            s_g = jnp.where(valid, s[g * BLOCK:(g + 1) * BLOCK], NEG_INF)
            m = jnp.maximum(jnp.max(s_g, axis=-1, keepdims=True), sink)
            e = jnp.exp(s_g - m)
            den = jnp.sum(e, axis=-1, keepdims=True) + jnp.exp(sink - m)
            probs.append((e / den).astype(BF16))
        o_h = jnp.dot(jnp.concatenate(probs, axis=0), v_h,
                      preferred_element_type=F32)
        for g in range(SWA_GROUP):
            col = (h * SWA_GROUP + g) * SWA_HEAD_DIM
            o_ref[:, col:col + SWA_HEAD_DIM] = o_h[g * BLOCK:(g + 1) * BLOCK]


def _swa_attention(proj, pos2, inv_freq, sinks, batch, seq):
    nb = seq // BLOCK
    assert Q_A_COL % SWA_WIDTH == 0 and K_A_COL % SWA_KV_WIDTH == 0
    assert V_A_COL % SWA_KV_WIDTH == 0
    qcol = Q_A_COL // SWA_WIDTH
    kcol = K_A_COL // SWA_KV_WIDTH
    vcol = V_A_COL // SWA_KV_WIDTH

    def cur(b, n):
        return b * nb + n

    def prev(b, n):
        return b * nb + jnp.maximum(n - 1, 0)

    return pl.pallas_call(
        _swa_kernel,
        out_shape=jax.ShapeDtypeStruct((batch * seq, SWA_WIDTH), F32),
        grid=(batch, nb),
        in_specs=[
            pl.BlockSpec(memory_space=pltpu.SMEM),
            pl.BlockSpec((BLOCK, SWA_WIDTH), lambda b, n: (cur(b, n), qcol)),
            pl.BlockSpec((BLOCK, SWA_KV_WIDTH), lambda b, n: (cur(b, n), kcol)),
            pl.BlockSpec((BLOCK, SWA_KV_WIDTH), lambda b, n: (prev(b, n), kcol)),
            pl.BlockSpec((BLOCK, SWA_KV_WIDTH), lambda b, n: (cur(b, n), vcol)),
            pl.BlockSpec((BLOCK, SWA_KV_WIDTH), lambda b, n: (prev(b, n), vcol)),
            pl.BlockSpec((BLOCK, 1), lambda b, n: (cur(b, n), 0)),
            pl.BlockSpec((BLOCK, 1), lambda b, n: (prev(b, n), 0)),
            pl.BlockSpec((1, LANES), lambda b, n: (0, 0)),
        ],
        out_specs=pl.BlockSpec((BLOCK, SWA_WIDTH), lambda b, n: (cur(b, n), 0)),
        compiler_params=_cparams(("arbitrary", "arbitrary")),
        name="swa_attention",
    )(sinks, proj, proj, proj, proj, proj, pos2, pos2, inv_freq)


def _sb_kernel(q_ref, k_ref, v_ref, o_ref):
    n = pl.program_id(2)
    heads = [slice(h * SB_HEAD_DIM, (h + 1) * SB_HEAD_DIM)
             for h in range(SB_HEADS_PER_STEP)]
    rows = SB_HEADS_PER_STEP * BLOCK
    scale = 1.0 / math.sqrt(SB_HEAD_DIM)
    key_j = lax.broadcasted_iota(I32, (BLOCK, 2 * BLOCK), 0)
    out_c = lax.broadcasted_iota(I32, (BLOCK, 2 * BLOCK), 1)
    later_total = ((key_j > out_c) | (out_c >= BLOCK)).astype(BF16)
    q_row = lax.broadcasted_iota(I32, (rows, BLOCK), 0) % BLOCK
    k_col = lax.broadcasted_iota(I32, (rows, BLOCK), 1)
    causal = k_col < q_row

    def block(kb, carry, acc, mask):
        start = pl.multiple_of(kb * BLOCK, BLOCK)
        z = jnp.concatenate(
            [_dot_nt(q_ref[:, hd], k_ref[pl.ds(start, BLOCK), hd]) for hd in heads],
            axis=0) * scale
        t = jnp.log(1.0 + jnp.exp(-jnp.abs(z)))
        log_not = jnp.minimum(-z, 0.0) - t
        log_beta = jnp.minimum(z, 0.0) - t
        if mask is not None:
            log_not = jnp.where(mask, log_not, 0.0)
        hi = log_not.astype(BF16)
        r1 = log_not - hi.astype(F32)
        mid = r1.astype(BF16)
        lo = (r1 - mid.astype(F32)).astype(BF16)
        parts = jnp.dot(jnp.concatenate([hi, mid, lo], axis=0), later_total,
                        preferred_element_type=F32)
        sums = (parts[:rows] + parts[rows:2 * rows]) + parts[2 * rows:]
        a = jnp.exp(log_beta + sums[:, :BLOCK] + carry)
        if mask is not None:
            a = jnp.where(mask, a, 0.0)
        a = a.astype(BF16)
        pv = jnp.concatenate(
            [jnp.dot(a[h * BLOCK:(h + 1) * BLOCK], v_ref[pl.ds(start, BLOCK), hd],
                     preferred_element_type=F32) for h, hd in enumerate(heads)], axis=0)
        return carry + sums[:, BLOCK:], acc + pv

    zeros = jnp.zeros((rows, BLOCK), F32)
    carry, acc = block(n, zeros, zeros, causal)

    def cond(state):
        kb, carry, _ = state
        return jnp.logical_and(kb >= 0, jnp.max(carry) > EXP_ZERO_LOG)

    def body(state):
        kb, carry, acc = state
        carry, acc = block(kb, carry, acc, None)
        return kb - 1, carry, acc

    _, _, acc = lax.while_loop(cond, body, (n - 1, carry, acc))
    for h, hd in enumerate(heads):
        o_ref[:, hd] = acc[h * BLOCK:(h + 1) * BLOCK]


def _sb_attention(proj, batch, seq):
    nb = seq // BLOCK
    width = SB_HEADS_PER_STEP * SB_HEAD_DIM
    assert Q_B_COL % width == 0 and K_B_COL % width == 0 and V_B_COL % width == 0
    qcol = Q_B_COL // width
    kcol = K_B_COL // width
    vcol = V_B_COL // width
    return pl.pallas_call(
        _sb_kernel,
        out_shape=jax.ShapeDtypeStruct((batch * seq, SB_WIDTH), F32),
        grid=(batch, SB_HEADS // SB_HEADS_PER_STEP, nb),
        in_specs=[
            pl.BlockSpec((BLOCK, width), lambda b, h, n: (b * nb + n, qcol + h)),
            pl.BlockSpec((seq, width), lambda b, h, n: (b, kcol + h)),
            pl.BlockSpec((seq, width), lambda b, h, n: (b, vcol + h)),
        ],
        out_specs=pl.BlockSpec((BLOCK, width), lambda b, h, n: (b * nb + n, h)),
        compiler_params=_cparams(("arbitrary", "arbitrary", "arbitrary")),
        name="sb_attention",
    )(proj, proj, proj)


def _row_stats(chunks_ref, n_chunks, width):
    total = chunks_ref[0].sum(axis=-1, keepdims=True)
    for c in range(1, n_chunks):
        total = total + chunks_ref[c].sum(axis=-1, keepdims=True)
    mu = total / width
    sq = jnp.square(chunks_ref[0] - mu).sum(axis=-1, keepdims=True)
    for c in range(1, n_chunks):
        sq = sq + jnp.square(chunks_ref[c] - mu).sum(axis=-1, keepdims=True)
    return mu, lax.rsqrt(sq / width + LN_EPS)


def _outproj_kernel(oa_ref, ob_ref, ga_ref, gb_ref, wa_ref, wb_ref, x_ref, bo_ref,
                    lg_ref, lb_ref, y_ref, mu_ref, rs_ref, x1s_ref,
                    ma_ref, mb_ref, acc_ref):
    j = pl.program_id(1)
    n_chunks = acc_ref.shape[0]
    tn = acc_ref.shape[2]

    @pl.when(j == 0)
    def _():
        for o_ref, g_ref, m_ref in ((oa_ref, ga_ref, ma_ref), (ob_ref, gb_ref, mb_ref)):
            width = o_ref.shape[1]
            cols = [slice(c * tn, (c + 1) * tn) for c in range(width // tn)]
            sq = sum(jnp.square(o_ref[:, sl]).sum(axis=-1, keepdims=True) for sl in cols)
            r = lax.rsqrt(sq / width + RMS_EPS)
            for sl in cols:
                m_ref[:, sl] = (o_ref[:, sl] * r * g_ref[:, sl]).astype(BF16)

    mix = (jnp.dot(ma_ref[...], wa_ref[...], preferred_element_type=F32)
           + jnp.dot(mb_ref[...], wb_ref[...], preferred_element_type=F32))
    y = DEEPNORM_ALPHA * x_ref[...] + (mix + bo_ref[...])
    y_ref[...] = y
    acc_ref[j] = y

    @pl.when(j == n_chunks - 1)
    def _():
        mu, rs = _row_stats(acc_ref, n_chunks, n_chunks * tn)
        mu_ref[...] = mu
        rs_ref[...] = rs

        def normed(c):
            sl = slice(c * tn, (c + 1) * tn)
            return (acc_ref[c] - mu) * rs * lg_ref[:, sl] + lb_ref[:, sl]

        tm = acc_ref.shape[1]
        for c in range(n_chunks // 2):
            words = _pack_halves(normed(c), normed(c + n_chunks // 2))
            for q in range(tn // LANES):
                x1s_ref[pl.ds(c * (tn // LANES) + q, tm, stride=WORD_ROWS), :] = (
                    words[:, q * LANES:(q + 1) * LANES])


def _out_projection(o_a, o_b, g_a, g_b, w_b, x2, b_out, ln_g, ln_b):
    m, d = x2.shape
    tm, tn = TM_PROJ, TN_PROJ
    half = o_a.shape[1]
    once = dict(pipeline_mode=pl.Buffered(1))
    return pl.pallas_call(
        _outproj_kernel,
        out_shape=(jax.ShapeDtypeStruct((m, d), F32),
                   jax.ShapeDtypeStruct((m, 1), F32),
                   jax.ShapeDtypeStruct((m, 1), F32),
                   jax.ShapeDtypeStruct((m * WORD_ROWS, LANES), U32)),
        grid=(m // tm, d // tn),
        in_specs=[
            pl.BlockSpec((tm, half), lambda i, j: (i, 0), **once),
            pl.BlockSpec((tm, half), lambda i, j: (i, 0), **once),
            pl.BlockSpec((1, half), lambda i, j: (0, 0)),
            pl.BlockSpec((1, half), lambda i, j: (0, 0)),
            pl.BlockSpec((half, tn), lambda i, j: (0, j)),
            pl.BlockSpec((half, tn), lambda i, j: (1, j)),
            pl.BlockSpec((tm, tn), lambda i, j: (i, j)),
            pl.BlockSpec((1, tn), lambda i, j: (0, j)),
            pl.BlockSpec((1, d), lambda i, j: (0, 0)),
            pl.BlockSpec((1, d), lambda i, j: (0, 0)),
        ],
        out_specs=(pl.BlockSpec((tm, tn), lambda i, j: (i, j)),
                   pl.BlockSpec((tm, 1), lambda i, j: (i, 0)),
                   pl.BlockSpec((tm, 1), lambda i, j: (i, 0)),
                   pl.BlockSpec((tm * WORD_ROWS, LANES), lambda i, j: (i, 0))),
        scratch_shapes=[pltpu.VMEM((tm, half), BF16), pltpu.VMEM((tm, half), BF16),
                        pltpu.VMEM((d // tn, tm, tn), F32)],
        compiler_params=_cparams(("arbitrary", "arbitrary")),
        name="out_projection_ln1",
    )(o_a, o_b, g_a, g_b, w_b, w_b, x2, b_out, ln_g, ln_b)


ROUTER_COLS = LANES
EXPERT_LANE0 = N_GROUPS


def _router_kernel(x_ref, w_ref, b_ref, ids_ref, wts_ref, cnt_ref, carry_ref):
    i = pl.program_id(0)
    tm = wts_ref.shape[0]

    @pl.when(i == 0)
    def _():
        carry_ref[...] = jnp.zeros_like(carry_ref)

    logits = _dot_slab(x_ref, tm, [w_ref])[0] + b_ref[...]
    lane = lax.broadcasted_iota(I32, (tm, ROUTER_COLS), 1)
    big = jnp.int32(ROUTER_COLS)

    def first_argmax(vals):
        top = jnp.max(vals, axis=-1, keepdims=True)
        idx = jnp.min(jnp.where(vals == top, lane, big), axis=-1, keepdims=True)
        return top, idx

    is_group = lane < N_GROUPS
    g_logits = jnp.where(is_group, logits, -jnp.inf)
    g_top, g_idx = first_argmax(g_logits)
    g_w = 1.0 / jnp.sum(jnp.exp(g_logits - g_top), axis=-1, keepdims=True)

    first = EXPERT_LANE0 + g_idx * EXPERTS_PER_GROUP
    in_group = (lane >= first) & (lane < first + EXPERTS_PER_GROUP)
    e_logits = jnp.where(in_group, logits, -jnp.inf)
    top1, idx1 = first_argmax(e_logits)
    top2, idx2 = first_argmax(jnp.where(lane == idx1, -jnp.inf, e_logits))
    e2 = jnp.exp(top2 - top1)
    w1 = g_w / (1.0 + e2)
    w2 = g_w * e2 / (1.0 + e2)

    hit1 = lane == idx1
    hit2 = lane == idx2
    onehot = (hit1 | hit2).astype(BF16)
    r = lax.broadcasted_iota(I32, (tm, tm), 0)
    c = lax.broadcasted_iota(I32, (tm, tm), 1)
    before = (c < r).astype(BF16)
    prior = jnp.dot(before, onehot, preferred_element_type=F32) + carry_ref[0:1, :]
    rank1 = jnp.sum(jnp.where(hit1, prior, 0.0), axis=-1, keepdims=True)
    rank2 = jnp.sum(jnp.where(hit2, prior, 0.0), axis=-1, keepdims=True)
    counts = carry_ref[0:1, :] + jnp.sum(onehot.astype(F32), axis=0, keepdims=True)
    carry_ref[...] = jnp.broadcast_to(counts, carry_ref.shape)
    cnt_ref[...] = jnp.broadcast_to(counts, cnt_ref.shape).astype(I32)

    ids = jnp.where(lane == 0, idx1 - EXPERT_LANE0,
          jnp.where(lane == 1, idx2 - EXPERT_LANE0,
          jnp.where(lane == 2, rank1.astype(I32),
          jnp.where(lane == 3, rank2.astype(I32), 0))))
    ids_ref[...] = ids.T[:ids_ref.shape[0]]
    wts_ref[...] = jnp.where(lane == 0, w1, jnp.where(lane == 1, w2, 0.0))


def _router(x1s, w_r, b_r):
    m = x1s.shape[0] // WORD_ROWS
    tm = TM_PROJ
    return pl.pallas_call(
        _router_kernel,
        out_shape=(jax.ShapeDtypeStruct((8, m), I32),
                   jax.ShapeDtypeStruct((m, ROUTER_COLS), F32),
                   jax.ShapeDtypeStruct((8, ROUTER_COLS), I32)),
        grid=(m // tm,),
        in_specs=[
            pl.BlockSpec((tm * WORD_ROWS, LANES), lambda i: (i, 0)),
            pl.BlockSpec((D_MODEL, ROUTER_COLS), lambda i: (0, 0)),
            pl.BlockSpec((1, ROUTER_COLS), lambda i: (0, 0)),
        ],
        out_specs=(pl.BlockSpec((8, tm), lambda i: (0, i)),
                   pl.BlockSpec((tm, ROUTER_COLS), lambda i: (i, 0)),
                   pl.BlockSpec((8, ROUTER_COLS), lambda i: (0, 0))),
        scratch_shapes=[pltpu.VMEM((8, ROUTER_COLS), F32)],
        compiler_params=_cparams(("arbitrary",)),
        name="router",
    )(x1s, w_r, b_r)


def _start_row_gather(idx_ref, first, n_items, span, src_ref, dst_ref, sem, inline=False):
    def start(r):
        src_row = pl.multiple_of(idx_ref[first + r], span)
        dst_row = pl.multiple_of(r * span, span)
        pltpu.make_async_copy(src_ref.at[pl.ds(src_row, span)],
                              dst_ref.at[pl.ds(dst_row, span)], sem).start()

    if inline:
        for r in range(n_items):
            start(r)
        return

    def body(c, _):
        for u in range(ROW_CHUNK):
            start(c * ROW_CHUNK + u)
        return 0

    lax.fori_loop(0, n_items // ROW_CHUNK, body, 0)


def _wait_row_gather(n_items, span, src_ref, dst_ref, sem):
    for _ in range(n_items):
        pltpu.make_async_copy(src_ref.at[pl.ds(0, span)], dst_ref.at[pl.ds(0, span)],
                              sem).wait()


def _expert_changed(te_ref, i):
    return jnp.logical_or(i == 0, te_ref[i] != te_ref[jnp.maximum(i - 1, 0)])


def _tile_row(i, nu):
    return jnp.minimum(i, nu[0] - 1)


def _stage_expert_weights(i, te_ref, nxt_ref, ws_ref, w_hbms, stage_ref, bf_refs, sem):
    expert = te_ref[i]
    slot = ws_ref[expert]

    def copies(e, s):
        return [pltpu.make_async_copy(w.at[e], stage_ref.at[s, n], sem.at[s])
                for n, w in enumerate(w_hbms)]

    @pl.when(i == 0)
    def _():
        for c in copies(expert, slot):
            c.start(priority=WEIGHT_DMA_PRIORITY)

    @pl.when(_expert_changed(te_ref, i))
    def _():
        for c in copies(expert, slot):
            c.wait()
        nxt = nxt_ref[expert]

        @pl.when(nxt < N_EXPERTS)
        def _():
            for c in copies(nxt, 1 - slot):
                c.start(priority=WEIGHT_DMA_PRIORITY)

        for n, bf_ref in enumerate(bf_refs):
            bf_ref[...] = stage_ref[slot, n].astype(BF16)


def _gateup_kernel(te_ref, nu_ref, tos_ref, nxt_ref, ws_ref, x_hbm, wg_hbm, wu_hbm, a_ref,
                   xbuf0_ref, xbuf1_ref, stage_ref, wgb_ref, wub_ref, sems, wsem):
    i = pl.program_id(0)
    n_used = nu_ref[0]
    xbufs = (xbuf0_ref, xbuf1_ref)

    def start(tile, buf, inline=False):
        _start_row_gather(tos_ref, tile * TM_MOE, TM_MOE, WORD_ROWS, x_hbm,
                          xbufs[buf], sems.at[buf], inline=inline)

    @pl.when(i == 0)
    def _():
        start(0, 0)

    _stage_expert_weights(i, te_ref, nxt_ref, ws_ref, [wg_hbm, wu_hbm], stage_ref,
                          [wgb_ref, wub_ref], wsem)

    def compute(buf, prefetch):
        _wait_row_gather(TM_MOE, WORD_ROWS, x_hbm, xbufs[buf], sems.at[buf])
        if prefetch:
            start(i + 1, 1 - buf, inline=True)
        gate, up = _dot_slab(xbufs[buf], TM_MOE, [wgb_ref, wub_ref])
        a_ref[...] = (gate * jax.nn.sigmoid(gate) * up).astype(a_ref.dtype)

    for buf in (0, 1):
        mine = lax.rem(i, 2) == buf

        @pl.when(jnp.logical_and(mine, i + 1 < n_used))
        def _(buf=buf):
            compute(buf, prefetch=True)

        @pl.when(jnp.logical_and(mine, i + 1 == n_used))
        def _(buf=buf):
            compute(buf, prefetch=False)

    @pl.when(i >= n_used)
    def _():
        a_ref[...] = jnp.zeros_like(a_ref)


def _grouped_gate_up(tile_expert, n_used, slab_of_slot, next_expert, stage_slot,
                     x1s, w_gate, w_up):
    p_rows = slab_of_slot.shape[0]
    _, d, f = w_gate.shape
    hbm = pl.BlockSpec(memory_space=pl.ANY)
    return pl.pallas_call(
        _gateup_kernel,
        out_shape=jax.ShapeDtypeStruct((p_rows, f), BF16),
        grid_spec=pltpu.PrefetchScalarGridSpec(
            num_scalar_prefetch=5,
            grid=(p_rows // TM_MOE,),
            in_specs=[hbm, hbm, hbm],
            out_specs=pl.BlockSpec((TM_MOE, f), lambda i, *_: (i, 0)),
            scratch_shapes=[pltpu.VMEM((TM_MOE * WORD_ROWS, LANES), U32),
                            pltpu.VMEM((TM_MOE * WORD_ROWS, LANES), U32),
                            pltpu.VMEM((2, 2, d, f), F32),
                            pltpu.VMEM((d, f), BF16), pltpu.VMEM((d, f), BF16),
                            pltpu.SemaphoreType.DMA((2,)), pltpu.SemaphoreType.DMA((2,))],
        ),
        compiler_params=_cparams(("arbitrary",)),
        name="moe_gate_up",
    )(tile_expert, n_used, slab_of_slot, next_expert, stage_slot, x1s, w_gate, w_up)


def _down_kernel(te_ref, nu_ref, nxt_ref, ws_ref, a_ref, wd_hbm, y_ref,
                 stage_ref, wdb_ref, wsem):
    i = pl.program_id(0)
    _stage_expert_weights(i, te_ref, nxt_ref, ws_ref, [wd_hbm], stage_ref, [wdb_ref], wsem)

    @pl.when(i < nu_ref[0])
    def _():
        y_ref[:, 0, :] = jnp.dot(a_ref[...], wdb_ref[...], preferred_element_type=F32)

    @pl.when(i >= nu_ref[0])
    def _():
        y_ref[...] = jnp.zeros_like(y_ref)


def _grouped_down(tile_expert, n_used, next_expert, stage_slot, act, w_down):
    p_rows, f = act.shape
    d = w_down.shape[2]
    return pl.pallas_call(
        _down_kernel,
        out_shape=jax.ShapeDtypeStruct((p_rows, 1, d), F32),
        grid_spec=pltpu.PrefetchScalarGridSpec(
            num_scalar_prefetch=4,
            grid=(p_rows // TM_MOE,),
            in_specs=[
                pl.BlockSpec((TM_MOE, f), lambda i, te, nu, *_: (_tile_row(i, nu), 0)),
                pl.BlockSpec(memory_space=pl.ANY),
            ],
            out_specs=pl.BlockSpec((TM_MOE, 1, d), lambda i, *_: (i, 0, 0)),
            scratch_shapes=[pltpu.VMEM((2, 1, f, d), F32), pltpu.VMEM((f, d), BF16),
                            pltpu.SemaphoreType.DMA((2,))],
        ),
        compiler_params=_cparams(("arbitrary",)),
        name="moe_down",
    )(tile_expert, n_used, next_expert, stage_slot, act, w_down)


def _combine_kernel(slot_ref, y_hbm, wts_ref, o_ref, buf0_ref, buf1_ref, sems):
    i = pl.program_id(0)
    tm = o_ref.shape[0]
    n_tokens = slot_ref.shape[0] // EXPERT_TOP_K
    bufs = (buf0_ref, buf1_ref)

    def start(tile, buf, inline=False):
        for k in range(EXPERT_TOP_K):
            _start_row_gather(slot_ref, k * n_tokens + tile * tm, tm, 1, y_hbm,
                              bufs[buf].at[k], sems.at[buf], inline=inline)

    @pl.when(i == 0)
    def _():
        start(0, 0)

    def combine(buf, prefetch):
        _wait_row_gather(EXPERT_TOP_K * tm, 1, y_hbm, bufs[buf].at[0], sems.at[buf])
        if prefetch:
            start(i + 1, 1 - buf, inline=True)
        for c in range(o_ref.shape[1] // TN_PROJ):
            sl = slice(c * TN_PROJ, (c + 1) * TN_PROJ)
            o_ref[:, sl] = sum(wts_ref[:, k:k + 1] * bufs[buf][k, :, 0, sl]
                               for k in range(EXPERT_TOP_K))

    for buf in (0, 1):
        mine = lax.rem(i, 2) == buf

        @pl.when(jnp.logical_and(mine, i + 1 < pl.num_programs(0)))
        def _(buf=buf):
            combine(buf, prefetch=True)

        @pl.when(jnp.logical_and(mine, i + 1 == pl.num_programs(0)))
        def _(buf=buf):
            combine(buf, prefetch=False)


def _combine_experts(slot_kt, y_sorted, wts):
    d = y_sorted.shape[2]
    m = slot_kt.shape[0] // EXPERT_TOP_K
    tm = TM_COMBINE
    return pl.pallas_call(
        _combine_kernel,
        out_shape=jax.ShapeDtypeStruct((m, d), F32),
        grid_spec=pltpu.PrefetchScalarGridSpec(
            num_scalar_prefetch=1,
            grid=(m // tm,),
            in_specs=[
                pl.BlockSpec(memory_space=pl.ANY),
                pl.BlockSpec((tm, ROUTER_COLS), lambda i, sl: (i, 0)),
            ],
            out_specs=pl.BlockSpec((tm, d), lambda i, sl: (i, 0)),
            scratch_shapes=[pltpu.VMEM((EXPERT_TOP_K, tm, 1, d), F32),
                            pltpu.VMEM((EXPERT_TOP_K, tm, 1, d), F32),
                            pltpu.SemaphoreType.DMA((2,))],
        ),
        compiler_params=_cparams(("arbitrary",)),
        name="moe_combine",
    )(slot_kt, y_sorted, wts)


def _final_kernel(x1s_ref, wg_ref, bg_ref, p_ref, wp_ref, y1_ref, mu_ref, rs_ref,
                  l1g_ref, l1b_ref, moe_ref, l2g_ref, l2b_ref,
                  o_ref, lo_ref, hi_ref, acc_ref, mu2_ref, rs2_ref):
    i = pl.program_id(0)
    j = pl.program_id(1)
    n_tiles = pl.num_programs(0) - 1
    n_chunks = acc_ref.shape[1]
    tn = acc_ref.shape[3]
    cur = lax.rem(i, 2)
    prv = 1 - cur

    @pl.when(jnp.logical_and(j == 0, i < n_tiles))
    def _():
        for s in range(WORD_ROWS):
            sl = slice(s * LANES, (s + 1) * LANES)
            lo_ref[:, sl], hi_ref[:, sl] = _unpack_halves(
                x1s_ref[pl.ds(s, lo_ref.shape[0], stride=WORD_ROWS), :])

    def build():
        gate = _dot_halves(lo_ref[...], hi_ref[...], wg_ref) + bg_ref[...]
        emb = jnp.dot(p_ref[...].astype(BF16), wp_ref[...].astype(BF16),
                      preferred_element_type=F32)
        x1 = (y1_ref[...] - mu_ref[...]) * rs_ref[...] * l1g_ref[...] + l1b_ref[...]
        acc_ref[cur, j] = DEEPNORM_ALPHA * x1 + moe_ref[...] + jax.nn.sigmoid(gate) * emb

    def emit():
        o_ref[...] = ((acc_ref[prv, j] - mu2_ref[prv]) * rs2_ref[prv]
                      * l2g_ref[...] + l2b_ref[...])

    @pl.when(i == 0)
    def _():
        build()

    @pl.when(jnp.logical_and(i > 0, i < n_tiles))
    def _():
        emit()
        build()

    @pl.when(i == n_tiles)
    def _():
        emit()

    @pl.when(jnp.logical_and(j == n_chunks - 1, i < n_tiles))
    def _():
        mu2_ref[cur], rs2_ref[cur] = _row_stats(acc_ref.at[cur], n_chunks, n_chunks * tn)


def _final_stage(x1s, w_pg_b, b_pg, p2, w_pp, y1, mu1, rs1, ln1_g, ln1_b,
                 moe, ln2_g, ln2_b):
    m, d = y1.shape
    tm, tn = TM_PROJ, TN_PROJ
    ple = p2.shape[1]
    n_chunks = d // tn

    n_tiles = m // tm

    def built(i):
        return jnp.minimum(i, n_tiles - 1)

    def emitted(i):
        return jnp.maximum(i - 1, 0)

    return pl.pallas_call(
        _final_kernel,
        out_shape=jax.ShapeDtypeStruct((m, d), F32),
        grid=(n_tiles + 1, n_chunks),
        in_specs=[
            pl.BlockSpec((tm * WORD_ROWS, LANES), lambda i, j: (built(i), 0),
                         pipeline_mode=pl.Buffered(1)),
            pl.BlockSpec((d, tn), lambda i, j: (0, j)),
            pl.BlockSpec((1, tn), lambda i, j: (0, j)),
            pl.BlockSpec((tm, ple), lambda i, j: (built(i), 0)),
            pl.BlockSpec((ple, tn), lambda i, j: (0, j)),
            pl.BlockSpec((tm, tn), lambda i, j: (built(i), j)),
            pl.BlockSpec((tm, 1), lambda i, j: (built(i), 0)),
            pl.BlockSpec((tm, 1), lambda i, j: (built(i), 0)),
            pl.BlockSpec((1, tn), lambda i, j: (0, j)),
            pl.BlockSpec((1, tn), lambda i, j: (0, j)),
            pl.BlockSpec((tm, tn), lambda i, j: (built(i), j)),
            pl.BlockSpec((1, tn), lambda i, j: (0, j)),
            pl.BlockSpec((1, tn), lambda i, j: (0, j)),
        ],
        out_specs=pl.BlockSpec((tm, tn), lambda i, j: (emitted(i), jnp.where(i == 0, 0, j))),
        scratch_shapes=[pltpu.VMEM((tm, d // 2), BF16), pltpu.VMEM((tm, d // 2), BF16),
                        pltpu.VMEM((2, n_chunks, tm, tn), F32),
                        pltpu.VMEM((2, tm, 1), F32), pltpu.VMEM((2, tm, 1), F32)],
        compiler_params=_cparams(("arbitrary", "arbitrary")),
        name="ple_moe_ln2",
    )(x1s, w_pg_b, b_pg, p2, w_pp, y1, mu1, rs1, ln1_g, ln1_b, moe, ln2_g, ln2_b)


def _routing_tables(ids_t, counts_row):
    counts = counts_row[EXPERT_LANE0:EXPERT_LANE0 + N_EXPERTS]
    tiles = (counts + TM_MOE - 1) // TM_MOE
    tile_end = jnp.cumsum(tiles)
    offsets = (tile_end - tiles) * TM_MOE
    n_used = tile_end[-1:]
    n_tokens = ids_t.shape[1]
    experts = ids_t[0:EXPERT_TOP_K]
    hit = experts[None] == jnp.arange(N_EXPERTS, dtype=I32)[:, None, None]
    first_slot = jnp.sum(jnp.where(hit, offsets[:, None, None], 0), axis=0)
    slots = (first_slot + ids_t[EXPERT_TOP_K:2 * EXPERT_TOP_K]).reshape(-1)
    n_tiles = (EXPERT_TOP_K * n_tokens) // TM_MOE + N_EXPERTS
    tile_ids = jnp.minimum(jnp.arange(n_tiles, dtype=I32), n_used - 1)
    tile_expert = jnp.sum(tile_end[None, :] <= tile_ids[:, None], axis=1).astype(I32)
    slab_of_slot = jnp.zeros((n_tiles * TM_MOE,), I32).at[slots].set(
        jnp.arange(slots.size, dtype=I32) % n_tokens * WORD_ROWS, unique_indices=True)
    eid = jnp.arange(N_EXPERTS, dtype=I32)
    later_owner = (eid[None, :] > eid[:, None]) & (tiles[None, :] > 0)
    next_expert = jnp.min(jnp.where(later_owner, eid[None, :], N_EXPERTS), axis=1).astype(I32)
    stage_slot = ((jnp.cumsum(tiles > 0) - 1) % 2).astype(I32)
    return (slots.astype(I32), slab_of_slot, tile_expert, n_used.astype(I32),
            next_expert, stage_slot)


def kernel(x, p, positions, w_in, b_in, sinks, g_norm_a, g_norm_b, w_out, b_out,
           ln1_g, ln1_b, w_group, b_group, w_er, b_er, w_gate, w_up, w_down,
           w_ple_gate, b_ple_gate, w_ple_proj, ln2_g, ln2_b):
    batch, seq, d = x.shape
    m = batch * seq
    row = lambda v: v.reshape(1, -1)
    x2 = x.reshape(m, d)
    for i in range(DEPTH):
        proj = _in_projection(x2, w_in[i].astype(BF16), row(b_in[i]))
        inv_freq = ROPE_THETA ** (-jnp.arange(0, SWA_HEAD_DIM, 2, dtype=F32) / SWA_HEAD_DIM)
        inv_freq = jnp.tile(inv_freq, LANES // inv_freq.shape[0]).reshape(1, LANES)
        o_a = _swa_attention(proj, positions.reshape(m, 1), inv_freq, sinks[i], batch, seq)
        o_b = _sb_attention(proj, batch, seq)
        y1, mu1, rs1, x1s = _out_projection(
            o_a, o_b, row(g_norm_a[i]), row(g_norm_b[i]), w_out[i].astype(BF16), x2,
            row(b_out[i]), row(ln1_g[i]), row(ln1_b[i]))
        pad = ROUTER_COLS - N_GROUPS - N_EXPERTS
        w_r = jnp.concatenate(
            [w_group[i], w_er[i].transpose(1, 0, 2).reshape(d, N_EXPERTS),
             jnp.zeros((d, pad), F32)], axis=1).astype(BF16)
        b_r = jnp.concatenate([b_group[i], b_er[i].reshape(-1), jnp.zeros((pad,), F32)])
        ids, wts, counts = _router(x1s, w_r, row(b_r))
        (slots, slab_of_slot, tile_expert, n_used, next_expert,
         stage_slot) = _routing_tables(ids, counts[0])
        act = _grouped_gate_up(tile_expert, n_used, slab_of_slot, next_expert, stage_slot,
                               x1s, w_gate[i], w_up[i])
        y_sorted = _grouped_down(tile_expert, n_used, next_expert, stage_slot, act, w_down[i])
        moe = _combine_experts(slots, y_sorted, wts)
        x2 = _final_stage(x1s, w_ple_gate[i].astype(BF16), row(b_ple_gate[i]),
                          p[i].reshape(m, PLE_DIM), w_ple_proj[i], y1, mu1, rs1,
                          row(ln1_g[i]), row(ln1_b[i]), moe, row(ln2_g[i]), row(ln2_b[i]))
    return x2.reshape(batch, seq, d)
```

```python
import math

import jax
import jax.numpy as jnp
from jax import lax
from jax.experimental import pallas as pl
from jax.experimental.pallas import tpu as pltpu

F32 = jnp.float32
BF16 = jnp.bfloat16
I32 = jnp.int32
U32 = jnp.uint32

D_MODEL = 4096
PLE_DIM = 256
BLOCK = 128
ROPE_THETA = 10000.0
LN_EPS = 1e-5
RMS_EPS = 1e-6
NEG_INF = -1e30

SWA_HEAD_DIM = 64
SWA_WIDTH = D_MODEL // 2
SWA_HEADS = SWA_WIDTH // SWA_HEAD_DIM
SWA_KV_HEADS = SWA_HEADS // 8
SWA_GROUP = SWA_HEADS // SWA_KV_HEADS
SWA_KV_WIDTH = SWA_KV_HEADS * SWA_HEAD_DIM

SB_HEAD_DIM = 128
SB_WIDTH = D_MODEL - SWA_WIDTH
SB_HEADS = SB_WIDTH // SB_HEAD_DIM

IN_WIDTH = SWA_WIDTH + 2 * SWA_KV_WIDTH + 3 * SB_WIDTH
A_WIDTH = SWA_WIDTH + 2 * SWA_KV_WIDTH
Q_B_COL = 0
K_B_COL = Q_B_COL + SB_WIDTH
V_B_COL = K_B_COL + SB_WIDTH
Q_A_COL = V_B_COL + SB_WIDTH
K_A_COL = Q_A_COL + SWA_WIDTH
V_A_COL = K_A_COL + SWA_KV_WIDTH

N_GROUPS = 4
EXPERTS_PER_GROUP = 8
N_EXPERTS = N_GROUPS * EXPERTS_PER_GROUP
EXPERT_TOP_K = 2
DEPTH = 1
DEEPNORM_ALPHA = (2.0 * DEPTH) ** 0.25

LANES = 128
VMEM_LIMIT_BYTES = 56 * 1024 * 1024

EXP_ZERO_LOG = -104.0 - 2.0

TM_IN = 1024
TM_PROJ = 512
TN_PROJ = 512
TM_MOE = 256
TM_COMBINE = 128
SB_HEADS_PER_STEP = 16
SUFFIX_TERMS = 2
ROW_CHUNK = 16
WEIGHT_DMA_PRIORITY = 1


def _cparams(sem):
    return pltpu.CompilerParams(dimension_semantics=sem,
                                vmem_limit_bytes=VMEM_LIMIT_BYTES)


def _pack_halves(lo, hi):
    return lax.bitcast_convert_type(
        pltpu.pack_elementwise([lo, hi], packed_dtype=BF16), U32)


def _unpack_halves(words):
    return tuple(
        pltpu.unpack_elementwise(words, index=k, packed_dtype=BF16,
                                 unpacked_dtype=F32).astype(BF16) for k in (0, 1))


def _dot_halves(lo, hi, w_ref):
    half = lo.shape[1]
    return (jnp.dot(lo, w_ref[:half], preferred_element_type=F32)
            + jnp.dot(hi, w_ref[half:], preferred_element_type=F32))


WORD_ROWS = (D_MODEL // 2) // LANES
SLAB_PAIR = 2 * LANES


def _slab_words(slab_ref, pair, tokens):
    return jnp.concatenate(
        [slab_ref[pl.ds(2 * pair + k, tokens, stride=WORD_ROWS), :] for k in (0, 1)],
        axis=1)


def _dot_slab(slab_ref, tokens, w_refs):
    half = D_MODEL // 2
    outs = [None] * len(w_refs)
    for pair in range(half // SLAB_PAIR):
        lo, hi = _unpack_halves(_slab_words(slab_ref, pair, tokens))
        rows_lo = slice(pair * SLAB_PAIR, (pair + 1) * SLAB_PAIR)
        rows_hi = slice(half + pair * SLAB_PAIR, half + (pair + 1) * SLAB_PAIR)
        for n, w_ref in enumerate(w_refs):
            part = (jnp.dot(lo, w_ref[rows_lo], preferred_element_type=F32)
                    + jnp.dot(hi, w_ref[rows_hi], preferred_element_type=F32))
            outs[n] = part if outs[n] is None else outs[n] + part
    return outs


def _dot_nt(a, b):
    return lax.dot_general(a, b, (((1,), (1,)), ((), ())),
                           preferred_element_type=F32)


def _inproj_kernel(x_ref, w_ref, b_ref, o_ref, xb_ref):
    @pl.when(pl.program_id(1) == 0)
    def _():
        xb_ref[...] = x_ref[...].astype(BF16)

    acc = jnp.dot(xb_ref[...], w_ref[...], preferred_element_type=F32)
    o_ref[...] = (acc + b_ref[...]).astype(o_ref.dtype)


def _in_projection(x2, w_b, b):
    m, k = x2.shape
    n = w_b.shape[1]
    n_blocks = n // TN_PROJ
    assert A_WIDTH % TN_PROJ == 0
    a_blocks = A_WIDTH // TN_PROJ

    def src(j):
        return lax.rem(j + a_blocks, n_blocks)

    return pl.pallas_call(
        _inproj_kernel,
        out_shape=jax.ShapeDtypeStruct((m, n), BF16),
        grid=(m // TM_IN, n_blocks),
        in_specs=[
            pl.BlockSpec((TM_IN, k), lambda i, j: (i, 0)),
            pl.BlockSpec((k, TN_PROJ), lambda i, j: (0, src(j))),
            pl.BlockSpec((1, TN_PROJ), lambda i, j: (0, src(j))),
        ],
        out_specs=pl.BlockSpec((TM_IN, TN_PROJ), lambda i, j: (i, j)),
        scratch_shapes=[pltpu.VMEM((TM_IN, k), BF16)],
        compiler_params=_cparams(("arbitrary", "arbitrary")),
        name="in_projection",
    )(x2, w_b, b)


def _swa_kernel(sinks_ref, q_ref, kc_ref, kp_ref, vc_ref, vp_ref,
                posc_ref, posp_ref, invf_ref, o_ref):
    n = pl.program_id(1)
    lane = lax.broadcasted_iota(I32, (1, LANES), 1)
    first_half = (lane % SWA_HEAD_DIM) < (SWA_HEAD_DIM // 2)

    def tables(pos_ref):
        ang = pos_ref[...].astype(F32) * invf_ref[...]
        sin = jnp.sin(ang)
        return jnp.cos(ang), jnp.where(first_half, -sin, sin)

    def rope(x, cos, sin_signed):
        partner = jnp.where(first_half,
                            pltpu.roll(x, LANES - SWA_HEAD_DIM // 2, 1),
                            pltpu.roll(x, SWA_HEAD_DIM // 2, 1))
        return x * cos + partner * sin_signed

    cos_c, sin_c = tables(posc_ref)
    cos_p, sin_p = tables(posp_ref)

    def rope_block(ref, cos, sin_signed):
        width = ref.shape[1]
        return [rope(ref[:, c * LANES:(c + 1) * LANES].astype(F32), cos,
                     sin_signed).astype(BF16) for c in range(width // LANES)]

    q_chunks = rope_block(q_ref, cos_c, sin_c)
    k_chunks = [jnp.concatenate([p_, c_], axis=0)
                for p_, c_ in zip(rope_block(kp_ref, cos_p, sin_p),
                                  rope_block(kc_ref, cos_c, sin_c))]
    v_all = jnp.concatenate([vp_ref[...], vc_ref[...]], axis=0)

    qi = lax.broadcasted_iota(I32, (BLOCK, 2 * BLOCK), 0)
    kj = lax.broadcasted_iota(I32, (BLOCK, 2 * BLOCK), 1)
    rel = qi - (kj - BLOCK)
    valid = (rel >= 0) & (rel < BLOCK) & ((kj >= BLOCK) | (n > 0))

    def head_slice(chunks, head):
        half = head % 2
        return chunks[head // 2][:, half * SWA_HEAD_DIM:(half + 1) * SWA_HEAD_DIM]

    scale = 1.0 / math.sqrt(SWA_HEAD_DIM)
    for h in range(SWA_KV_HEADS):
        k_h = head_slice(k_chunks, h)
        v_h = v_all[:, h * SWA_HEAD_DIM:(h + 1) * SWA_HEAD_DIM]
        q_h = jnp.concatenate(
            [head_slice(q_chunks, h * SWA_GROUP + g) for g in range(SWA_GROUP)],
            axis=0)
        s = _dot_nt(q_h, k_h) * scale
        probs = []
        for g in range(SWA_GROUP):
            sink = sinks_ref[h * SWA_GROUP + g]
            s_g = jnp.where(valid, s[g * BLOCK:(g + 1) * BLOCK], NEG_INF)
            m = jnp.maximum(jnp.max(s_g, axis=-1, keepdims=True), sink)
            e = jnp.exp(s_g - m)
            den = jnp.sum(e, axis=-1, keepdims=True) + jnp.exp(sink - m)
            probs.append((e / den).astype(BF16))
        o_h = jnp.dot(jnp.concatenate(probs, axis=0), v_h,
                      preferred_element_type=F32)
        for g in range(SWA_GROUP):
            col = (h * SWA_GROUP + g) * SWA_HEAD_DIM
            o_ref[:, col:col + SWA_HEAD_DIM] = o_h[g * BLOCK:(g + 1) * BLOCK]


def _swa_attention(proj, pos2, inv_freq, sinks, batch, seq):
    nb = seq // BLOCK
    assert Q_A_COL % SWA_WIDTH == 0 and K_A_COL % SWA_KV_WIDTH == 0
    assert V_A_COL % SWA_KV_WIDTH == 0
    qcol = Q_A_COL // SWA_WIDTH
    kcol = K_A_COL // SWA_KV_WIDTH
    vcol = V_A_COL // SWA_KV_WIDTH

    def cur(b, n):
        return b * nb + n

    def prev(b, n):
        return b * nb + jnp.maximum(n - 1, 0)

    return pl.pallas_call(
        _swa_kernel,
        out_shape=jax.ShapeDtypeStruct((batch * seq, SWA_WIDTH), F32),
        grid=(batch, nb),
        in_specs=[
            pl.BlockSpec(memory_space=pltpu.SMEM),
            pl.BlockSpec((BLOCK, SWA_WIDTH), lambda b, n: (cur(b, n), qcol)),
            pl.BlockSpec((BLOCK, SWA_KV_WIDTH), lambda b, n: (cur(b, n), kcol)),
            pl.BlockSpec((BLOCK, SWA_KV_WIDTH), lambda b, n: (prev(b, n), kcol)),
            pl.BlockSpec((BLOCK, SWA_KV_WIDTH), lambda b, n: (cur(b, n), vcol)),
            pl.BlockSpec((BLOCK, SWA_KV_WIDTH), lambda b, n: (prev(b, n), vcol)),
            pl.BlockSpec((BLOCK, 1), lambda b, n: (cur(b, n), 0)),
            pl.BlockSpec((BLOCK, 1), lambda b, n: (prev(b, n), 0)),
            pl.BlockSpec((1, LANES), lambda b, n: (0, 0)),
        ],
        out_specs=pl.BlockSpec((BLOCK, SWA_WIDTH), lambda b, n: (cur(b, n), 0)),
        compiler_params=_cparams(("arbitrary", "arbitrary")),
        name="swa_attention",
    )(sinks, proj, proj, proj, proj, proj, pos2, pos2, inv_freq)


def _sb_kernel(q_ref, k_ref, v_ref, o_ref):
    n = pl.program_id(2)
    heads = [slice(h * SB_HEAD_DIM, (h + 1) * SB_HEAD_DIM)
             for h in range(SB_HEADS_PER_STEP)]
    rows = SB_HEADS_PER_STEP * BLOCK
    scale = 1.0 / math.sqrt(SB_HEAD_DIM)
    key_j = lax.broadcasted_iota(I32, (BLOCK, 2 * BLOCK), 0)
    out_c = lax.broadcasted_iota(I32, (BLOCK, 2 * BLOCK), 1)
    later_total = ((key_j > out_c) | (out_c >= BLOCK)).astype(BF16)
    q_row = lax.broadcasted_iota(I32, (rows, BLOCK), 0) % BLOCK
    k_col = lax.broadcasted_iota(I32, (rows, BLOCK), 1)
    causal = k_col < q_row

    def block(kb, carry, acc, mask):
        start = pl.multiple_of(kb * BLOCK, BLOCK)
        z = jnp.concatenate(
            [_dot_nt(q_ref[:, hd], k_ref[pl.ds(start, BLOCK), hd]) for hd in heads],
            axis=0) * scale
        t = jnp.log(1.0 + jnp.exp(-jnp.abs(z)))
        log_not = jnp.minimum(-z, 0.0) - t
        log_beta = jnp.minimum(z, 0.0) - t
        if mask is not None:
            log_not = jnp.where(mask, log_not, 0.0)
        terms, rest = [], log_not
        for _ in range(SUFFIX_TERMS):
            terms.append(rest.astype(BF16))
            rest = rest - terms[-1].astype(F32)
        parts = jnp.dot(jnp.concatenate(terms, axis=0), later_total,
                        preferred_element_type=F32)
        sums = parts[:rows]
        for t in range(1, SUFFIX_TERMS):
            sums = sums + parts[t * rows:(t + 1) * rows]
        a = jnp.exp(log_beta + sums[:, :BLOCK] + carry)
        if mask is not None:
            a = jnp.where(mask, a, 0.0)
        a = a.astype(BF16)
        pv = jnp.concatenate(
            [jnp.dot(a[h * BLOCK:(h + 1) * BLOCK], v_ref[pl.ds(start, BLOCK), hd],
                     preferred_element_type=F32) for h, hd in enumerate(heads)], axis=0)
        return carry + sums[:, BLOCK:], acc + pv

    zeros = jnp.zeros((rows, BLOCK), F32)
    carry, acc = block(n, zeros, zeros, causal)

    def cond(state):
        kb, carry, _ = state
        return jnp.logical_and(kb >= 0, jnp.max(carry) > EXP_ZERO_LOG)

    def body(state):
        kb, carry, acc = state
        carry, acc = block(kb, carry, acc, None)
        return kb - 1, carry, acc

    _, _, acc = lax.while_loop(cond, body, (n - 1, carry, acc))
    for h, hd in enumerate(heads):
        o_ref[:, hd] = acc[h * BLOCK:(h + 1) * BLOCK]


def _sb_attention(proj, batch, seq):
    nb = seq // BLOCK
    width = SB_HEADS_PER_STEP * SB_HEAD_DIM
    assert Q_B_COL % width == 0 and K_B_COL % width == 0 and V_B_COL % width == 0
    qcol = Q_B_COL // width
    kcol = K_B_COL // width
    vcol = V_B_COL // width
    return pl.pallas_call(
        _sb_kernel,
        out_shape=jax.ShapeDtypeStruct((batch * seq, SB_WIDTH), F32),
        grid=(batch, SB_HEADS // SB_HEADS_PER_STEP, nb),
        in_specs=[
            pl.BlockSpec((BLOCK, width), lambda b, h, n: (b * nb + n, qcol + h)),
            pl.BlockSpec((seq, width), lambda b, h, n: (b, kcol + h),
                         pipeline_mode=pl.Buffered(1)),
            pl.BlockSpec((seq, width), lambda b, h, n: (b, vcol + h),
                         pipeline_mode=pl.Buffered(1)),
        ],
        out_specs=pl.BlockSpec((BLOCK, width), lambda b, h, n: (b * nb + n, h)),
        compiler_params=_cparams(("arbitrary", "arbitrary", "arbitrary")),
        name="sb_attention",
    )(proj, proj, proj)


def _row_stats(chunks_ref, n_chunks, width):
    total = chunks_ref[0].sum(axis=-1, keepdims=True)
    for c in range(1, n_chunks):
        total = total + chunks_ref[c].sum(axis=-1, keepdims=True)
    mu = total / width
    sq = jnp.square(chunks_ref[0] - mu).sum(axis=-1, keepdims=True)
    for c in range(1, n_chunks):
        sq = sq + jnp.square(chunks_ref[c] - mu).sum(axis=-1, keepdims=True)
    return mu, lax.rsqrt(sq / width + LN_EPS)


def _outproj_kernel(oa_ref, ob_ref, ga_ref, gb_ref, wa_ref, wb_ref, x_ref, bo_ref,
                    lg_ref, lb_ref, y_ref, mu_ref, rs_ref, x1s_ref,
                    ma_ref, mb_ref, acc_ref):
    j = pl.program_id(1)
    n_chunks = acc_ref.shape[0]
    tn = acc_ref.shape[2]

    @pl.when(j == 0)
    def _():
        for o_ref, g_ref, m_ref in ((oa_ref, ga_ref, ma_ref), (ob_ref, gb_ref, mb_ref)):
            width = o_ref.shape[1]
            cols = [slice(c * tn, (c + 1) * tn) for c in range(width // tn)]
            sq = sum(jnp.square(o_ref[:, sl]).sum(axis=-1, keepdims=True) for sl in cols)
            r = lax.rsqrt(sq / width + RMS_EPS)
            for sl in cols:
                m_ref[:, sl] = (o_ref[:, sl] * r * g_ref[:, sl]).astype(BF16)

    mix = (jnp.dot(ma_ref[...], wa_ref[...], preferred_element_type=F32)
           + jnp.dot(mb_ref[...], wb_ref[...], preferred_element_type=F32))
    y = DEEPNORM_ALPHA * x_ref[...] + (mix + bo_ref[...])
    y_ref[...] = y
    acc_ref[j] = y

    @pl.when(j == n_chunks - 1)
    def _():
        mu, rs = _row_stats(acc_ref, n_chunks, n_chunks * tn)
        mu_ref[...] = mu
        rs_ref[...] = rs

        def normed(c):
            sl = slice(c * tn, (c + 1) * tn)
            return (acc_ref[c] - mu) * rs * lg_ref[:, sl] + lb_ref[:, sl]

        tm = acc_ref.shape[1]
        for c in range(n_chunks // 2):
            words = _pack_halves(normed(c), normed(c + n_chunks // 2))
            for q in range(tn // LANES):
                x1s_ref[pl.ds(c * (tn // LANES) + q, tm, stride=WORD_ROWS), :] = (
                    words[:, q * LANES:(q + 1) * LANES])


def _out_projection(o_a, o_b, g_a, g_b, w_b, x2, b_out, ln_g, ln_b):
    m, d = x2.shape
    tm, tn = TM_PROJ, TN_PROJ
    half = o_a.shape[1]
    once = dict(pipeline_mode=pl.Buffered(1))
    return pl.pallas_call(
        _outproj_kernel,
        out_shape=(jax.ShapeDtypeStruct((m, d), F32),
                   jax.ShapeDtypeStruct((m, 1), F32),
                   jax.ShapeDtypeStruct((m, 1), F32),
                   jax.ShapeDtypeStruct((m * WORD_ROWS, LANES), U32)),
        grid=(m // tm, d // tn),
        in_specs=[
            pl.BlockSpec((tm, half), lambda i, j: (i, 0), **once),
            pl.BlockSpec((tm, half), lambda i, j: (i, 0), **once),
            pl.BlockSpec((1, half), lambda i, j: (0, 0)),
            pl.BlockSpec((1, half), lambda i, j: (0, 0)),
            pl.BlockSpec((half, tn), lambda i, j: (0, j)),
            pl.BlockSpec((half, tn), lambda i, j: (1, j)),
            pl.BlockSpec((tm, tn), lambda i, j: (i, j)),
            pl.BlockSpec((1, tn), lambda i, j: (0, j)),
            pl.BlockSpec((1, d), lambda i, j: (0, 0)),
            pl.BlockSpec((1, d), lambda i, j: (0, 0)),
        ],
        out_specs=(pl.BlockSpec((tm, tn), lambda i, j: (i, j)),
                   pl.BlockSpec((tm, 1), lambda i, j: (i, 0)),
                   pl.BlockSpec((tm, 1), lambda i, j: (i, 0)),
                   pl.BlockSpec((tm * WORD_ROWS, LANES), lambda i, j: (i, 0))),
        scratch_shapes=[pltpu.VMEM((tm, half), BF16), pltpu.VMEM((tm, half), BF16),
                        pltpu.VMEM((d // tn, tm, tn), F32)],
        compiler_params=_cparams(("arbitrary", "arbitrary")),
        name="out_projection_ln1",
    )(o_a, o_b, g_a, g_b, w_b, w_b, x2, b_out, ln_g, ln_b)


ROUTER_COLS = LANES
EXPERT_LANE0 = N_GROUPS


def _router_kernel(x_ref, w_ref, b_ref, ids_ref, wts_ref, cnt_ref, carry_ref):
    i = pl.program_id(0)
    tm = wts_ref.shape[0]

    @pl.when(i == 0)
    def _():
        carry_ref[...] = jnp.zeros_like(carry_ref)

    logits = _dot_slab(x_ref, tm, [w_ref])[0] + b_ref[...]
    lane = lax.broadcasted_iota(I32, (tm, ROUTER_COLS), 1)
    big = jnp.int32(ROUTER_COLS)

    def first_argmax(vals):
        top = jnp.max(vals, axis=-1, keepdims=True)
        idx = jnp.min(jnp.where(vals == top, lane, big), axis=-1, keepdims=True)
        return top, idx

    is_group = lane < N_GROUPS
    g_logits = jnp.where(is_group, logits, -jnp.inf)
    g_top, g_idx = first_argmax(g_logits)
    g_w = 1.0 / jnp.sum(jnp.exp(g_logits - g_top), axis=-1, keepdims=True)

    first = EXPERT_LANE0 + g_idx * EXPERTS_PER_GROUP
    in_group = (lane >= first) & (lane < first + EXPERTS_PER_GROUP)
    e_logits = jnp.where(in_group, logits, -jnp.inf)
    top1, idx1 = first_argmax(e_logits)
    top2, idx2 = first_argmax(jnp.where(lane == idx1, -jnp.inf, e_logits))
    e2 = jnp.exp(top2 - top1)
    w1 = g_w / (1.0 + e2)
    w2 = g_w * e2 / (1.0 + e2)

    hit1 = lane == idx1
    hit2 = lane == idx2
    onehot = (hit1 | hit2).astype(BF16)
    r = lax.broadcasted_iota(I32, (tm, tm), 0)
    c = lax.broadcasted_iota(I32, (tm, tm), 1)
    before = (c < r).astype(BF16)
    prior = jnp.dot(before, onehot, preferred_element_type=F32) + carry_ref[0:1, :]
    rank1 = jnp.sum(jnp.where(hit1, prior, 0.0), axis=-1, keepdims=True)
    rank2 = jnp.sum(jnp.where(hit2, prior, 0.0), axis=-1, keepdims=True)
    counts = carry_ref[0:1, :] + jnp.sum(onehot.astype(F32), axis=0, keepdims=True)
    carry_ref[...] = jnp.broadcast_to(counts, carry_ref.shape)
    cnt_ref[...] = jnp.broadcast_to(counts, cnt_ref.shape).astype(I32)

    ids = jnp.where(lane == 0, idx1 - EXPERT_LANE0,
          jnp.where(lane == 1, idx2 - EXPERT_LANE0,
          jnp.where(lane == 2, rank1.astype(I32),
          jnp.where(lane == 3, rank2.astype(I32), 0))))
    ids_ref[...] = ids.T[:ids_ref.shape[0]]
    wts_ref[...] = jnp.where(lane == 0, w1, jnp.where(lane == 1, w2, 0.0))


def _router(x1s, w_r, b_r):
    m = x1s.shape[0] // WORD_ROWS
    tm = TM_PROJ
    return pl.pallas_call(
        _router_kernel,
        out_shape=(jax.ShapeDtypeStruct((8, m), I32),
                   jax.ShapeDtypeStruct((m, ROUTER_COLS), F32),
                   jax.ShapeDtypeStruct((8, ROUTER_COLS), I32)),
        grid=(m // tm,),
        in_specs=[
            pl.BlockSpec((tm * WORD_ROWS, LANES), lambda i: (i, 0)),
            pl.BlockSpec((D_MODEL, ROUTER_COLS), lambda i: (0, 0)),
            pl.BlockSpec((1, ROUTER_COLS), lambda i: (0, 0)),
        ],
        out_specs=(pl.BlockSpec((8, tm), lambda i: (0, i)),
                   pl.BlockSpec((tm, ROUTER_COLS), lambda i: (i, 0)),
                   pl.BlockSpec((8, ROUTER_COLS), lambda i: (0, 0))),
        scratch_shapes=[pltpu.VMEM((8, ROUTER_COLS), F32)],
        compiler_params=_cparams(("arbitrary",)),
        name="router",
    )(x1s, w_r, b_r)


def _start_row_gather(idx_ref, first, n_items, span, src_ref, dst_ref, sem):
    def body(c, _):
        for u in range(ROW_CHUNK):
            r = c * ROW_CHUNK + u
            src_row = pl.multiple_of(idx_ref[first + r], span)
            dst_row = pl.multiple_of(r * span, span)
            pltpu.make_async_copy(src_ref.at[pl.ds(src_row, span)],
                                  dst_ref.at[pl.ds(dst_row, span)], sem).start()
        return 0

    lax.fori_loop(0, n_items // ROW_CHUNK, body, 0)


def _wait_row_gather(n_items, span, src_ref, dst_ref, sem):
    def body(c, _):
        for _u in range(ROW_CHUNK):
            pltpu.make_async_copy(src_ref.at[pl.ds(0, span)], dst_ref.at[pl.ds(0, span)],
                                  sem).wait()
        return 0

    lax.fori_loop(0, n_items // ROW_CHUNK, body, 0)


def _expert_changed(te_ref, i):
    return jnp.logical_or(i == 0, te_ref[i] != te_ref[jnp.maximum(i - 1, 0)])


def _tile_row(i, nu):
    return jnp.minimum(i, nu[0] - 1)


def _stage_expert_weights(i, te_ref, nxt_ref, ws_ref, w_hbms, stage_ref, bf_refs, sem):
    expert = te_ref[i]
    slot = ws_ref[expert]

    def copies(e, s):
        return [pltpu.make_async_copy(w.at[e], stage_ref.at[s, n], sem.at[s])
                for n, w in enumerate(w_hbms)]

    @pl.when(i == 0)
    def _():
        for c in copies(expert, slot):
            c.start(priority=WEIGHT_DMA_PRIORITY)

    @pl.when(_expert_changed(te_ref, i))
    def _():
        for c in copies(expert, slot):
            c.wait()
        nxt = nxt_ref[expert]

        @pl.when(nxt < N_EXPERTS)
        def _():
            for c in copies(nxt, 1 - slot):
                c.start(priority=WEIGHT_DMA_PRIORITY)

        for n, bf_ref in enumerate(bf_refs):
            bf_ref[...] = stage_ref[slot, n].astype(BF16)


def _gateup_kernel(te_ref, nu_ref, tos_ref, nxt_ref, ws_ref, x_hbm, wg_hbm, wu_hbm, a_ref,
                   xbuf_ref, stage_ref, wgb_ref, wub_ref, sems, wsem):
    i = pl.program_id(0)
    n_used = nu_ref[0]
    slot = lax.rem(i, 2)

    def start(tile, buf):
        _start_row_gather(tos_ref, tile * TM_MOE, TM_MOE, WORD_ROWS, x_hbm,
                          xbuf_ref.at[buf], sems.at[buf])

    @pl.when(i == 0)
    def _():
        start(0, 0)

    @pl.when(i + 1 < n_used)
    def _():
        start(i + 1, 1 - slot)

    _stage_expert_weights(i, te_ref, nxt_ref, ws_ref, [wg_hbm, wu_hbm], stage_ref,
                          [wgb_ref, wub_ref], wsem)

    @pl.when(i < n_used)
    def _():
        _wait_row_gather(TM_MOE, WORD_ROWS, x_hbm, xbuf_ref.at[slot], sems.at[slot])
        gate, up = _dot_slab(xbuf_ref.at[slot], TM_MOE, [wgb_ref, wub_ref])
        a_ref[...] = (gate * jax.nn.sigmoid(gate) * up).astype(a_ref.dtype)

    @pl.when(i >= n_used)
    def _():
        a_ref[...] = jnp.zeros_like(a_ref)


def _grouped_gate_up(tile_expert, n_used, slab_of_slot, next_expert, stage_slot,
                     x1s, w_gate, w_up):
    p_rows = slab_of_slot.shape[0]
    _, d, f = w_gate.shape
    hbm = pl.BlockSpec(memory_space=pl.ANY)
    return pl.pallas_call(
        _gateup_kernel,
        out_shape=jax.ShapeDtypeStruct((p_rows, f), BF16),
        grid_spec=pltpu.PrefetchScalarGridSpec(
            num_scalar_prefetch=5,
            grid=(p_rows // TM_MOE,),
            in_specs=[hbm, hbm, hbm],
            out_specs=pl.BlockSpec((TM_MOE, f), lambda i, *_: (i, 0)),
            scratch_shapes=[pltpu.VMEM((2, TM_MOE * WORD_ROWS, LANES), U32),
                            pltpu.VMEM((2, 2, d, f), F32),
                            pltpu.VMEM((d, f), BF16), pltpu.VMEM((d, f), BF16),
                            pltpu.SemaphoreType.DMA((2,)), pltpu.SemaphoreType.DMA((2,))],
        ),
        compiler_params=_cparams(("arbitrary",)),
        name="moe_gate_up",
    )(tile_expert, n_used, slab_of_slot, next_expert, stage_slot, x1s, w_gate, w_up)


def _down_kernel(te_ref, nu_ref, nxt_ref, ws_ref, a_ref, wd_hbm, y_ref,
                 stage_ref, wdb_ref, wsem):
    i = pl.program_id(0)
    _stage_expert_weights(i, te_ref, nxt_ref, ws_ref, [wd_hbm], stage_ref, [wdb_ref], wsem)

    @pl.when(i < nu_ref[0])
    def _():
        y = jnp.dot(a_ref[...], wdb_ref[...], preferred_element_type=F32)
        half = y.shape[1] // 2
        y_ref[:, 0, :] = _pack_halves(y[:, :half], y[:, half:])

    @pl.when(i >= nu_ref[0])
    def _():
        y_ref[...] = jnp.zeros_like(y_ref)


def _grouped_down(tile_expert, n_used, next_expert, stage_slot, act, w_down):
    p_rows, f = act.shape
    d = w_down.shape[2]
    return pl.pallas_call(
        _down_kernel,
        out_shape=jax.ShapeDtypeStruct((p_rows, 1, d // 2), U32),
        grid_spec=pltpu.PrefetchScalarGridSpec(
            num_scalar_prefetch=4,
            grid=(p_rows // TM_MOE,),
            in_specs=[
                pl.BlockSpec((TM_MOE, f), lambda i, te, nu, *_: (_tile_row(i, nu), 0)),
                pl.BlockSpec(memory_space=pl.ANY),
            ],
            out_specs=pl.BlockSpec((TM_MOE, 1, d // 2), lambda i, *_: (i, 0, 0)),
            scratch_shapes=[pltpu.VMEM((2, 1, f, d), F32), pltpu.VMEM((f, d), BF16),
                            pltpu.SemaphoreType.DMA((2,))],
        ),
        compiler_params=_cparams(("arbitrary",)),
        name="moe_down",
    )(tile_expert, n_used, next_expert, stage_slot, act, w_down)


def _combine_kernel(slot_ref, y_hbm, wts_ref, o_ref, buf_ref, sems):
    i = pl.program_id(0)
    tm = o_ref.shape[0]
    n_tokens = slot_ref.shape[0] // EXPERT_TOP_K
    slot = lax.rem(i, 2)

    def start(tile, buf):
        for k in range(EXPERT_TOP_K):
            _start_row_gather(slot_ref, k * n_tokens + tile * tm, tm, 1, y_hbm,
                              buf_ref.at[buf, k], sems.at[buf])

    @pl.when(i == 0)
    def _():
        start(0, 0)

    @pl.when(i + 1 < pl.num_programs(0))
    def _():
        start(i + 1, 1 - slot)

    _wait_row_gather(EXPERT_TOP_K * tm, 1, y_hbm, buf_ref.at[slot, 0], sems.at[slot])
    half = o_ref.shape[1] // 2
    for c in range(half // TN_PROJ):
        sl = slice(c * TN_PROJ, (c + 1) * TN_PROJ)
        words = [buf_ref[slot, k, :, 0, sl] for k in range(EXPERT_TOP_K)]
        for part, cols in ((0, sl), (1, slice(half + sl.start, half + sl.stop))):
            o_ref[:, cols] = sum(
                wts_ref[:, k:k + 1] * pltpu.unpack_elementwise(
                    words[k], index=part, packed_dtype=BF16, unpacked_dtype=F32)
                for k in range(EXPERT_TOP_K))


def _combine_experts(slot_kt, y_sorted, wts):
    d = 2 * y_sorted.shape[2]
    m = slot_kt.shape[0] // EXPERT_TOP_K
    tm = TM_COMBINE
    return pl.pallas_call(
        _combine_kernel,
        out_shape=jax.ShapeDtypeStruct((m, d), F32),
        grid_spec=pltpu.PrefetchScalarGridSpec(
            num_scalar_prefetch=1,
            grid=(m // tm,),
            in_specs=[
                pl.BlockSpec(memory_space=pl.ANY),
                pl.BlockSpec((tm, ROUTER_COLS), lambda i, sl: (i, 0)),
            ],
            out_specs=pl.BlockSpec((tm, d), lambda i, sl: (i, 0)),
            scratch_shapes=[pltpu.VMEM((2, EXPERT_TOP_K, tm, 1, d // 2), U32),
                            pltpu.SemaphoreType.DMA((2,))],
        ),
        compiler_params=_cparams(("arbitrary",)),
        name="moe_combine",
    )(slot_kt, y_sorted, wts)


def _final_kernel(x1s_ref, wg_ref, bg_ref, p_ref, wp_ref, y1_ref, mu_ref, rs_ref,
                  l1g_ref, l1b_ref, moe_ref, l2g_ref, l2b_ref,
                  o_ref, lo_ref, hi_ref, acc_ref, mu2_ref, rs2_ref):
    i = pl.program_id(0)
    j = pl.program_id(1)
    n_tiles = pl.num_programs(0) - 1
    n_chunks = acc_ref.shape[1]
    tn = acc_ref.shape[3]
    cur = lax.rem(i, 2)
    prv = 1 - cur

    @pl.when(jnp.logical_and(j == 0, i < n_tiles))
    def _():
        for s in range(WORD_ROWS):
            sl = slice(s * LANES, (s + 1) * LANES)
            lo_ref[:, sl], hi_ref[:, sl] = _unpack_halves(
                x1s_ref[pl.ds(s, lo_ref.shape[0], stride=WORD_ROWS), :])

    def build():
        gate = _dot_halves(lo_ref[...], hi_ref[...], wg_ref) + bg_ref[...]
        emb = jnp.dot(p_ref[...].astype(BF16), wp_ref[...].astype(BF16),
                      preferred_element_type=F32)
        x1 = (y1_ref[...] - mu_ref[...]) * rs_ref[...] * l1g_ref[...] + l1b_ref[...]
        acc_ref[cur, j] = DEEPNORM_ALPHA * x1 + moe_ref[...] + jax.nn.sigmoid(gate) * emb

    def emit():
        o_ref[...] = ((acc_ref[prv, j] - mu2_ref[prv]) * rs2_ref[prv]
                      * l2g_ref[...] + l2b_ref[...])

    @pl.when(i == 0)
    def _():
        build()

    @pl.when(jnp.logical_and(i > 0, i < n_tiles))
    def _():
        emit()
        build()

    @pl.when(i == n_tiles)
    def _():
        emit()

    @pl.when(jnp.logical_and(j == n_chunks - 1, i < n_tiles))
    def _():
        mu2_ref[cur], rs2_ref[cur] = _row_stats(acc_ref.at[cur], n_chunks, n_chunks * tn)


def _final_stage(x1s, w_pg_b, b_pg, p2, w_pp, y1, mu1, rs1, ln1_g, ln1_b,
                 moe, ln2_g, ln2_b):
    m, d = y1.shape
    tm, tn = TM_PROJ, TN_PROJ
    ple = p2.shape[1]
    n_chunks = d // tn

    n_tiles = m // tm

    def built(i):
        return jnp.minimum(i, n_tiles - 1)

    def emitted(i):
        return jnp.maximum(i - 1, 0)

    return pl.pallas_call(
        _final_kernel,
        out_shape=jax.ShapeDtypeStruct((m, d), F32),
        grid=(n_tiles + 1, n_chunks),
        in_specs=[
            pl.BlockSpec((tm * WORD_ROWS, LANES), lambda i, j: (built(i), 0),
                         pipeline_mode=pl.Buffered(1)),
            pl.BlockSpec((d, tn), lambda i, j: (0, j)),
            pl.BlockSpec((1, tn), lambda i, j: (0, j)),
            pl.BlockSpec((tm, ple), lambda i, j: (built(i), 0)),
            pl.BlockSpec((ple, tn), lambda i, j: (0, j)),
            pl.BlockSpec((tm, tn), lambda i, j: (built(i), j)),
            pl.BlockSpec((tm, 1), lambda i, j: (built(i), 0)),
            pl.BlockSpec((tm, 1), lambda i, j: (built(i), 0)),
            pl.BlockSpec((1, tn), lambda i, j: (0, j)),
            pl.BlockSpec((1, tn), lambda i, j: (0, j)),
            pl.BlockSpec((tm, tn), lambda i, j: (built(i), j)),
            pl.BlockSpec((1, tn), lambda i, j: (0, j)),
            pl.BlockSpec((1, tn), lambda i, j: (0, j)),
        ],
        out_specs=pl.BlockSpec((tm, tn), lambda i, j: (emitted(i), jnp.where(i == 0, 0, j))),
        scratch_shapes=[pltpu.VMEM((tm, d // 2), BF16), pltpu.VMEM((tm, d // 2), BF16),
                        pltpu.VMEM((2, n_chunks, tm, tn), F32),
                        pltpu.VMEM((2, tm, 1), F32), pltpu.VMEM((2, tm, 1), F32)],
        compiler_params=_cparams(("arbitrary", "arbitrary")),
        name="ple_moe_ln2",
    )(x1s, w_pg_b, b_pg, p2, w_pp, y1, mu1, rs1, ln1_g, ln1_b, moe, ln2_g, ln2_b)


def _routing_tables(ids_t, counts_row):
    counts = counts_row[EXPERT_LANE0:EXPERT_LANE0 + N_EXPERTS]
    tiles = (counts + TM_MOE - 1) // TM_MOE
    tile_end = jnp.cumsum(tiles)
    offsets = (tile_end - tiles) * TM_MOE
    n_used = tile_end[-1:]
    n_tokens = ids_t.shape[1]
    experts = ids_t[0:EXPERT_TOP_K]
    hit = experts[None] == jnp.arange(N_EXPERTS, dtype=I32)[:, None, None]
    first_slot = jnp.sum(jnp.where(hit, offsets[:, None, None], 0), axis=0)
    slots = (first_slot + ids_t[EXPERT_TOP_K:2 * EXPERT_TOP_K]).reshape(-1)
    n_tiles = (EXPERT_TOP_K * n_tokens) // TM_MOE + N_EXPERTS
    tile_ids = jnp.minimum(jnp.arange(n_tiles, dtype=I32), n_used - 1)
    tile_expert = jnp.sum(tile_end[None, :] <= tile_ids[:, None], axis=1).astype(I32)
    slab_of_slot = jnp.zeros((n_tiles * TM_MOE,), I32).at[slots].set(
        jnp.arange(slots.size, dtype=I32) % n_tokens * WORD_ROWS, unique_indices=True)
    eid = jnp.arange(N_EXPERTS, dtype=I32)
    later_owner = (eid[None, :] > eid[:, None]) & (tiles[None, :] > 0)
    next_expert = jnp.min(jnp.where(later_owner, eid[None, :], N_EXPERTS), axis=1).astype(I32)
    stage_slot = ((jnp.cumsum(tiles > 0) - 1) % 2).astype(I32)
    return (slots.astype(I32), slab_of_slot, tile_expert, n_used.astype(I32),
            next_expert, stage_slot)


def kernel(x, p, positions, w_in, b_in, sinks, g_norm_a, g_norm_b, w_out, b_out,
           ln1_g, ln1_b, w_group, b_group, w_er, b_er, w_gate, w_up, w_down,
           w_ple_gate, b_ple_gate, w_ple_proj, ln2_g, ln2_b):
    batch, seq, d = x.shape
    m = batch * seq
    row = lambda v: v.reshape(1, -1)
    x2 = x.reshape(m, d)
    for i in range(DEPTH):
        proj = _in_projection(x2, w_in[i].astype(BF16), row(b_in[i]))
        inv_freq = ROPE_THETA ** (-jnp.arange(0, SWA_HEAD_DIM, 2, dtype=F32) / SWA_HEAD_DIM)
        inv_freq = jnp.tile(inv_freq, LANES // inv_freq.shape[0]).reshape(1, LANES)
        o_a = _swa_attention(proj, positions.reshape(m, 1), inv_freq, sinks[i], batch, seq)
        o_b = _sb_attention(proj, batch, seq)
        y1, mu1, rs1, x1s = _out_projection(
            o_a, o_b, row(g_norm_a[i]), row(g_norm_b[i]), w_out[i].astype(BF16), x2,
            row(b_out[i]), row(ln1_g[i]), row(ln1_b[i]))
        pad = ROUTER_COLS - N_GROUPS - N_EXPERTS
        w_r = jnp.concatenate(
            [w_group[i], w_er[i].transpose(1, 0, 2).reshape(d, N_EXPERTS),
             jnp.zeros((d, pad), F32)], axis=1).astype(BF16)
        b_r = jnp.concatenate([b_group[i], b_er[i].reshape(-1), jnp.zeros((pad,), F32)])
        ids, wts, counts = _router(x1s, w_r, row(b_r))
        (slots, slab_of_slot, tile_expert, n_used, next_expert,
         stage_slot) = _routing_tables(ids, counts[0])
        act = _grouped_gate_up(tile_expert, n_used, slab_of_slot, next_expert, stage_slot,
                               x1s, w_gate[i], w_up[i])
        y_sorted = _grouped_down(tile_expert, n_used, next_expert, stage_slot, act, w_down[i])
        moe = _combine_experts(slots, y_sorted, wts)
        x2 = _final_stage(x1s, w_ple_gate[i].astype(BF16), row(b_ple_gate[i]),
                          p[i].reshape(m, PLE_DIM), w_ple_proj[i], y1, mu1, rs1,
                          row(ln1_g[i]), row(ln1_b[i]), moe, row(ln2_g[i]), row(ln2_b[i]))
    return x2.reshape(batch, seq, d)
```

```python
import math

import jax
import jax.numpy as jnp
from jax import lax
from jax.experimental import pallas as pl
from jax.experimental.pallas import tpu as pltpu

F32 = jnp.float32
BF16 = jnp.bfloat16
I32 = jnp.int32
U32 = jnp.uint32

D_MODEL = 4096
PLE_DIM = 256
BLOCK = 128
ROPE_THETA = 10000.0
LN_EPS = 1e-5
RMS_EPS = 1e-6
NEG_INF = -1e30

SWA_HEAD_DIM = 64
SWA_WIDTH = D_MODEL // 2
SWA_HEADS = SWA_WIDTH // SWA_HEAD_DIM
SWA_KV_HEADS = SWA_HEADS // 8
SWA_GROUP = SWA_HEADS // SWA_KV_HEADS
SWA_KV_WIDTH = SWA_KV_HEADS * SWA_HEAD_DIM

SB_HEAD_DIM = 128
SB_WIDTH = D_MODEL - SWA_WIDTH
SB_HEADS = SB_WIDTH // SB_HEAD_DIM

IN_WIDTH = SWA_WIDTH + 2 * SWA_KV_WIDTH + 3 * SB_WIDTH
A_WIDTH = SWA_WIDTH + 2 * SWA_KV_WIDTH
Q_B_COL = 0
K_B_COL = Q_B_COL + SB_WIDTH
V_B_COL = K_B_COL + SB_WIDTH
Q_A_COL = V_B_COL + SB_WIDTH
K_A_COL = Q_A_COL + SWA_WIDTH
V_A_COL = K_A_COL + SWA_KV_WIDTH

N_GROUPS = 4
EXPERTS_PER_GROUP = 8
N_EXPERTS = N_GROUPS * EXPERTS_PER_GROUP
EXPERT_TOP_K = 2
DEPTH = 1
DEEPNORM_ALPHA = (2.0 * DEPTH) ** 0.25

LANES = 128
VMEM_LIMIT_BYTES = 56 * 1024 * 1024

EXP_ZERO_LOG = -104.0 - 2.0

TM_IN = 1024
TM_PROJ = 512
TN_PROJ = 512
TM_MOE = 256
TM_COMBINE = 128
SB_HEADS_PER_STEP = 16
SUFFIX_TERMS = 2
ROW_CHUNK = 16
WEIGHT_DMA_PRIORITY = 1


def _cparams(sem):
    return pltpu.CompilerParams(dimension_semantics=sem,
                                vmem_limit_bytes=VMEM_LIMIT_BYTES)


def _pack_halves(lo, hi):
    return lax.bitcast_convert_type(
        pltpu.pack_elementwise([lo, hi], packed_dtype=BF16), U32)


def _unpack_halves(words):
    return tuple(
        pltpu.unpack_elementwise(words, index=k, packed_dtype=BF16,
                                 unpacked_dtype=F32).astype(BF16) for k in (0, 1))


def _dot_halves(lo, hi, w_ref):
    half = lo.shape[1]
    return (jnp.dot(lo, w_ref[:half], preferred_element_type=F32)
            + jnp.dot(hi, w_ref[half:], preferred_element_type=F32))


WORD_ROWS = (D_MODEL // 2) // LANES
SLAB_PAIR = 2 * LANES


def _slab_words(slab_ref, pair, tokens):
    return jnp.concatenate(
        [slab_ref[pl.ds(2 * pair + k, tokens, stride=WORD_ROWS), :] for k in (0, 1)],
        axis=1)


def _dot_slab(slab_ref, tokens, w_refs):
    half = D_MODEL // 2
    outs = [None] * len(w_refs)
    for pair in range(half // SLAB_PAIR):
        lo, hi = _unpack_halves(_slab_words(slab_ref, pair, tokens))
        rows_lo = slice(pair * SLAB_PAIR, (pair + 1) * SLAB_PAIR)
        rows_hi = slice(half + pair * SLAB_PAIR, half + (pair + 1) * SLAB_PAIR)
        for n, w_ref in enumerate(w_refs):
            part = (jnp.dot(lo, w_ref[rows_lo], preferred_element_type=F32)
                    + jnp.dot(hi, w_ref[rows_hi], preferred_element_type=F32))
            outs[n] = part if outs[n] is None else outs[n] + part
    return outs


def _dot_nt(a, b):
    return lax.dot_general(a, b, (((1,), (1,)), ((), ())),
                           preferred_element_type=F32)


def _inproj_kernel(x_ref, w_ref, b_ref, o_ref, xb_ref):
    @pl.when(pl.program_id(1) == 0)
    def _():
        xb_ref[...] = x_ref[...].astype(BF16)

    acc = jnp.dot(xb_ref[...], w_ref[...], preferred_element_type=F32)
    o_ref[...] = (acc + b_ref[...]).astype(o_ref.dtype)


def _in_projection(x2, w_b, b):
    m, k = x2.shape
    n = w_b.shape[1]
    n_blocks = n // TN_PROJ
    assert A_WIDTH % TN_PROJ == 0
    a_blocks = A_WIDTH // TN_PROJ

    def src(j):
        return lax.rem(j + a_blocks, n_blocks)

    return pl.pallas_call(
        _inproj_kernel,
        out_shape=jax.ShapeDtypeStruct((m, n), BF16),
        grid=(m // TM_IN, n_blocks),
        in_specs=[
            pl.BlockSpec((TM_IN, k), lambda i, j: (i, 0)),
            pl.BlockSpec((k, TN_PROJ), lambda i, j: (0, src(j))),
            pl.BlockSpec((1, TN_PROJ), lambda i, j: (0, src(j))),
        ],
        out_specs=pl.BlockSpec((TM_IN, TN_PROJ), lambda i, j: (i, j)),
        scratch_shapes=[pltpu.VMEM((TM_IN, k), BF16)],
        compiler_params=_cparams(("arbitrary", "arbitrary")),
        name="in_projection",
    )(x2, w_b, b)


def _swa_kernel(sinks_ref, q_ref, kc_ref, vc_ref, vp_ref,
                posc_ref, invf_ref, o_ref, kprev_ref):
    n = pl.program_id(1)

    @pl.when(n == 0)
    def _():
        kprev_ref[...] = jnp.zeros_like(kprev_ref)
    lane = lax.broadcasted_iota(I32, (1, LANES), 1)
    first_half = (lane % SWA_HEAD_DIM) < (SWA_HEAD_DIM // 2)

    def tables(pos_ref):
        ang = pos_ref[...].astype(F32) * invf_ref[...]
        sin = jnp.sin(ang)
        return jnp.cos(ang), jnp.where(first_half, -sin, sin)

    def rope(x, cos, sin_signed):
        partner = jnp.where(first_half,
                            pltpu.roll(x, LANES - SWA_HEAD_DIM // 2, 1),
                            pltpu.roll(x, SWA_HEAD_DIM // 2, 1))
        return x * cos + partner * sin_signed

    cos_c, sin_c = tables(posc_ref)

    def rope_block(ref, cos, sin_signed):
        width = ref.shape[1]
        return [rope(ref[:, c * LANES:(c + 1) * LANES].astype(F32), cos,
                     sin_signed).astype(BF16) for c in range(width // LANES)]

    q_chunks = rope_block(q_ref, cos_c, sin_c)
    k_cur = rope_block(kc_ref, cos_c, sin_c)
    k_chunks = [jnp.concatenate([kprev_ref[:, c * LANES:(c + 1) * LANES], k_c], axis=0)
                for c, k_c in enumerate(k_cur)]
    for c, k_c in enumerate(k_cur):
        kprev_ref[:, c * LANES:(c + 1) * LANES] = k_c
    v_all = jnp.concatenate([vp_ref[...], vc_ref[...]], axis=0)

    qi = lax.broadcasted_iota(I32, (BLOCK, 2 * BLOCK), 0)
    kj = lax.broadcasted_iota(I32, (BLOCK, 2 * BLOCK), 1)
    rel = qi - (kj - BLOCK)
    valid = (rel >= 0) & (rel < BLOCK) & ((kj >= BLOCK) | (n > 0))

    def head_slice(chunks, head):
        half = head % 2
        return chunks[head // 2][:, half * SWA_HEAD_DIM:(half + 1) * SWA_HEAD_DIM]

    scale = 1.0 / math.sqrt(SWA_HEAD_DIM)
    scores = []
    for h in range(SWA_KV_HEADS):
        k_h = head_slice(k_chunks, h)
        q_h = jnp.concatenate(
            [head_slice(q_chunks, h * SWA_GROUP + g) for g in range(SWA_GROUP)],
            axis=0)
        scores.append(_dot_nt(q_h, k_h) * scale)
    probs = []
    for h in range(SWA_KV_HEADS):
        probs_h = []
        for g in range(SWA_GROUP):
            sink = sinks_ref[h * SWA_GROUP + g]
            s_g = jnp.where(valid, scores[h][g * BLOCK:(g + 1) * BLOCK], NEG_INF)
            m = jnp.maximum(jnp.max(s_g, axis=-1, keepdims=True), sink)
            e = jnp.exp(s_g - m)
            den = jnp.sum(e, axis=-1, keepdims=True) + jnp.exp(sink - m)
            probs_h.append((e / den).astype(BF16))
        probs.append(jnp.concatenate(probs_h, axis=0))
    for h in range(SWA_KV_HEADS):
        v_h = v_all[:, h * SWA_HEAD_DIM:(h + 1) * SWA_HEAD_DIM]
        o_h = jnp.dot(probs[h], v_h, preferred_element_type=F32)
        for g in range(SWA_GROUP):
            col = (h * SWA_GROUP + g) * SWA_HEAD_DIM
            o_ref[:, col:col + SWA_HEAD_DIM] = o_h[g * BLOCK:(g + 1) * BLOCK]


def _swa_attention(proj, pos2, inv_freq, sinks, batch, seq):
    nb = seq // BLOCK
    assert Q_A_COL % SWA_WIDTH == 0 and K_A_COL % SWA_KV_WIDTH == 0
    assert V_A_COL % SWA_KV_WIDTH == 0
    qcol = Q_A_COL // SWA_WIDTH
    kcol = K_A_COL // SWA_KV_WIDTH
    vcol = V_A_COL // SWA_KV_WIDTH

    def cur(b, n):
        return b * nb + n

    def prev(b, n):
        return b * nb + jnp.maximum(n - 1, 0)

    return pl.pallas_call(
        _swa_kernel,
        out_shape=jax.ShapeDtypeStruct((batch * seq, SWA_WIDTH), F32),
        grid=(batch, nb),
        in_specs=[
            pl.BlockSpec(memory_space=pltpu.SMEM),
            pl.BlockSpec((BLOCK, SWA_WIDTH), lambda b, n: (cur(b, n), qcol)),
            pl.BlockSpec((BLOCK, SWA_KV_WIDTH), lambda b, n: (cur(b, n), kcol)),
            pl.BlockSpec((BLOCK, SWA_KV_WIDTH), lambda b, n: (cur(b, n), vcol)),
            pl.BlockSpec((BLOCK, SWA_KV_WIDTH), lambda b, n: (prev(b, n), vcol)),
            pl.BlockSpec((BLOCK, 1), lambda b, n: (cur(b, n), 0)),
            pl.BlockSpec((1, LANES), lambda b, n: (0, 0)),
        ],
        out_specs=pl.BlockSpec((BLOCK, SWA_WIDTH), lambda b, n: (cur(b, n), 0)),
        scratch_shapes=[pltpu.VMEM((BLOCK, SWA_KV_WIDTH), BF16)],
        compiler_params=_cparams(("arbitrary", "arbitrary")),
        name="swa_attention",
    )(sinks, proj, proj, proj, proj, pos2, inv_freq)


def _sb_kernel(q_ref, k_ref, v_ref, o_ref):
    n = pl.program_id(2)
    heads = [slice(h * SB_HEAD_DIM, (h + 1) * SB_HEAD_DIM)
             for h in range(SB_HEADS_PER_STEP)]
    rows = SB_HEADS_PER_STEP * BLOCK
    scale = 1.0 / math.sqrt(SB_HEAD_DIM)
    key_j = lax.broadcasted_iota(I32, (BLOCK, 2 * BLOCK), 0)
    out_c = lax.broadcasted_iota(I32, (BLOCK, 2 * BLOCK), 1)
    later_total = ((key_j > out_c) | (out_c >= BLOCK)).astype(BF16)
    q_row = lax.broadcasted_iota(I32, (rows, BLOCK), 0) % BLOCK
    k_col = lax.broadcasted_iota(I32, (rows, BLOCK), 1)
    causal = k_col < q_row

    def block(kb, carry, acc, mask):
        start = pl.multiple_of(kb * BLOCK, BLOCK)
        z = jnp.concatenate(
            [_dot_nt(q_ref[:, hd], k_ref[pl.ds(start, BLOCK), hd]) for hd in heads],
            axis=0) * scale
        t = jnp.log(1.0 + jnp.exp(-jnp.abs(z)))
        log_not = jnp.minimum(-z, 0.0) - t
        log_beta = jnp.minimum(z, 0.0) - t
        if mask is not None:
            log_not = jnp.where(mask, log_not, 0.0)
        terms, rest = [], log_not
        for _ in range(SUFFIX_TERMS):
            terms.append(rest.astype(BF16))
            rest = rest - terms[-1].astype(F32)
        parts = jnp.dot(jnp.concatenate(terms, axis=0), later_total,
                        preferred_element_type=F32)
        sums = parts[:rows]
        for t in range(1, SUFFIX_TERMS):
            sums = sums + parts[t * rows:(t + 1) * rows]
        a = jnp.exp(log_beta + sums[:, :BLOCK] + carry)
        if mask is not None:
            a = jnp.where(mask, a, 0.0)
        a = a.astype(BF16)
        pv = jnp.concatenate(
            [jnp.dot(a[h * BLOCK:(h + 1) * BLOCK], v_ref[pl.ds(start, BLOCK), hd],
                     preferred_element_type=F32) for h, hd in enumerate(heads)], axis=0)
        return carry + sums[:, BLOCK:], acc + pv

    zeros = jnp.zeros((rows, BLOCK), F32)
    carry, acc = block(n, zeros, zeros, causal)

    def cond(state):
        kb, carry, _ = state
        return jnp.logical_and(kb >= 0, jnp.max(carry) > EXP_ZERO_LOG)

    def body(state):
        kb, carry, acc = state
        carry, acc = block(kb, carry, acc, None)
        return kb - 1, carry, acc

    _, _, acc = lax.while_loop(cond, body, (n - 1, carry, acc))
    for h, hd in enumerate(heads):
        o_ref[:, hd] = acc[h * BLOCK:(h + 1) * BLOCK]


def _sb_attention(proj, batch, seq):
    nb = seq // BLOCK
    width = SB_HEADS_PER_STEP * SB_HEAD_DIM
    assert Q_B_COL % width == 0 and K_B_COL % width == 0 and V_B_COL % width == 0
    qcol = Q_B_COL // width
    kcol = K_B_COL // width
    vcol = V_B_COL // width
    return pl.pallas_call(
        _sb_kernel,
        out_shape=jax.ShapeDtypeStruct((batch * seq, SB_WIDTH), F32),
        grid=(batch, SB_HEADS // SB_HEADS_PER_STEP, nb),
        in_specs=[
            pl.BlockSpec((BLOCK, width), lambda b, h, n: (b * nb + n, qcol + h)),
            pl.BlockSpec((seq, width), lambda b, h, n: (b, kcol + h),
                         pipeline_mode=pl.Buffered(1)),
            pl.BlockSpec((seq, width), lambda b, h, n: (b, vcol + h),
                         pipeline_mode=pl.Buffered(1)),
        ],
        out_specs=pl.BlockSpec((BLOCK, width), lambda b, h, n: (b * nb + n, h)),
        compiler_params=_cparams(("arbitrary", "arbitrary", "arbitrary")),
        name="sb_attention",
    )(proj, proj, proj)


def _row_stats(chunks_ref, n_chunks, width):
    total = chunks_ref[0].sum(axis=-1, keepdims=True)
    for c in range(1, n_chunks):
        total = total + chunks_ref[c].sum(axis=-1, keepdims=True)
    mu = total / width
    sq = jnp.square(chunks_ref[0] - mu).sum(axis=-1, keepdims=True)
    for c in range(1, n_chunks):
        sq = sq + jnp.square(chunks_ref[c] - mu).sum(axis=-1, keepdims=True)
    return mu, lax.rsqrt(sq / width + LN_EPS)


def _outproj_kernel(oa_ref, ob_ref, ga_ref, gb_ref, wa_ref, wb_ref, x_ref, bo_ref,
                    lg_ref, lb_ref, y_ref, mu_ref, rs_ref, x1s_ref,
                    ma_ref, mb_ref, acc_ref):
    j = pl.program_id(1)
    n_chunks = acc_ref.shape[0]
    tn = acc_ref.shape[2]

    @pl.when(j == 0)
    def _():
        for o_ref, g_ref, m_ref in ((oa_ref, ga_ref, ma_ref), (ob_ref, gb_ref, mb_ref)):
            width = o_ref.shape[1]
            cols = [slice(c * tn, (c + 1) * tn) for c in range(width // tn)]
            sq = sum(jnp.square(o_ref[:, sl]).sum(axis=-1, keepdims=True) for sl in cols)
            r = lax.rsqrt(sq / width + RMS_EPS)
            for sl in cols:
                m_ref[:, sl] = (o_ref[:, sl] * r * g_ref[:, sl]).astype(BF16)

    mix = (jnp.dot(ma_ref[...], wa_ref[...], preferred_element_type=F32)
           + jnp.dot(mb_ref[...], wb_ref[...], preferred_element_type=F32))
    y = DEEPNORM_ALPHA * x_ref[...] + (mix + bo_ref[...])
    y_ref[...] = y
    acc_ref[j] = y

    @pl.when(j == n_chunks - 1)
    def _():
        mu, rs = _row_stats(acc_ref, n_chunks, n_chunks * tn)
        mu_ref[...] = mu
        rs_ref[...] = rs

        def normed(c):
            sl = slice(c * tn, (c + 1) * tn)
            return (acc_ref[c] - mu) * rs * lg_ref[:, sl] + lb_ref[:, sl]

        tm = acc_ref.shape[1]
        for c in range(n_chunks // 2):
            words = _pack_halves(normed(c), normed(c + n_chunks // 2))
            for q in range(tn // LANES):
                x1s_ref[pl.ds(c * (tn // LANES) + q, tm, stride=WORD_ROWS), :] = (
                    words[:, q * LANES:(q + 1) * LANES])


def _out_projection(o_a, o_b, g_a, g_b, w_b, x2, b_out, ln_g, ln_b):
    m, d = x2.shape
    tm, tn = TM_PROJ, TN_PROJ
    half = o_a.shape[1]
    once = dict(pipeline_mode=pl.Buffered(1))
    return pl.pallas_call(
        _outproj_kernel,
        out_shape=(jax.ShapeDtypeStruct((m, d), F32),
                   jax.ShapeDtypeStruct((m, 1), F32),
                   jax.ShapeDtypeStruct((m, 1), F32),
                   jax.ShapeDtypeStruct((m * WORD_ROWS, LANES), U32)),
        grid=(m // tm, d // tn),
        in_specs=[
            pl.BlockSpec((tm, half), lambda i, j: (i, 0), **once),
            pl.BlockSpec((tm, half), lambda i, j: (i, 0), **once),
            pl.BlockSpec((1, half), lambda i, j: (0, 0)),
            pl.BlockSpec((1, half), lambda i, j: (0, 0)),
            pl.BlockSpec((half, tn), lambda i, j: (0, j)),
            pl.BlockSpec((half, tn), lambda i, j: (1, j)),
            pl.BlockSpec((tm, tn), lambda i, j: (i, j)),
            pl.BlockSpec((1, tn), lambda i, j: (0, j)),
            pl.BlockSpec((1, d), lambda i, j: (0, 0)),
            pl.BlockSpec((1, d), lambda i, j: (0, 0)),
        ],
        out_specs=(pl.BlockSpec((tm, tn), lambda i, j: (i, j)),
                   pl.BlockSpec((tm, 1), lambda i, j: (i, 0)),
                   pl.BlockSpec((tm, 1), lambda i, j: (i, 0)),
                   pl.BlockSpec((tm * WORD_ROWS, LANES), lambda i, j: (i, 0))),
        scratch_shapes=[pltpu.VMEM((tm, half), BF16), pltpu.VMEM((tm, half), BF16),
                        pltpu.VMEM((d // tn, tm, tn), F32)],
        compiler_params=_cparams(("arbitrary", "arbitrary")),
        name="out_projection_ln1",
    )(o_a, o_b, g_a, g_b, w_b, w_b, x2, b_out, ln_g, ln_b)


ROUTER_COLS = LANES
EXPERT_LANE0 = N_GROUPS


def _router_kernel(x_ref, w_ref, b_ref, ids_ref, wts_ref, cnt_ref, carry_ref):
    i = pl.program_id(0)
    tm = wts_ref.shape[0]

    @pl.when(i == 0)
    def _():
        carry_ref[...] = jnp.zeros_like(carry_ref)

    logits = _dot_slab(x_ref, tm, [w_ref])[0] + b_ref[...]
    lane = lax.broadcasted_iota(I32, (tm, ROUTER_COLS), 1)
    big = jnp.int32(ROUTER_COLS)

    def first_argmax(vals):
        top = jnp.max(vals, axis=-1, keepdims=True)
        idx = jnp.min(jnp.where(vals == top, lane, big), axis=-1, keepdims=True)
        return top, idx

    is_group = lane < N_GROUPS
    g_logits = jnp.where(is_group, logits, -jnp.inf)
    g_top, g_idx = first_argmax(g_logits)
    g_w = 1.0 / jnp.sum(jnp.exp(g_logits - g_top), axis=-1, keepdims=True)

    first = EXPERT_LANE0 + g_idx * EXPERTS_PER_GROUP
    in_group = (lane >= first) & (lane < first + EXPERTS_PER_GROUP)
    e_logits = jnp.where(in_group, logits, -jnp.inf)
    top1, idx1 = first_argmax(e_logits)
    top2, idx2 = first_argmax(jnp.where(lane == idx1, -jnp.inf, e_logits))
    e2 = jnp.exp(top2 - top1)
    w1 = g_w / (1.0 + e2)
    w2 = g_w * e2 / (1.0 + e2)

    hit1 = lane == idx1
    hit2 = lane == idx2
    onehot = (hit1 | hit2).astype(BF16)
    r = lax.broadcasted_iota(I32, (tm, tm), 0)
    c = lax.broadcasted_iota(I32, (tm, tm), 1)
    before = (c < r).astype(BF16)
    prior = jnp.dot(before, onehot, preferred_element_type=F32) + carry_ref[0:1, :]
    rank1 = jnp.sum(jnp.where(hit1, prior, 0.0), axis=-1, keepdims=True)
    rank2 = jnp.sum(jnp.where(hit2, prior, 0.0), axis=-1, keepdims=True)
    counts = carry_ref[0:1, :] + jnp.sum(onehot.astype(F32), axis=0, keepdims=True)
    carry_ref[...] = jnp.broadcast_to(counts, carry_ref.shape)
    cnt_ref[...] = jnp.broadcast_to(counts, cnt_ref.shape).astype(I32)

    ids = jnp.where(lane == 0, idx1 - EXPERT_LANE0,
          jnp.where(lane == 1, idx2 - EXPERT_LANE0,
          jnp.where(lane == 2, rank1.astype(I32),
          jnp.where(lane == 3, rank2.astype(I32), 0))))
    ids_ref[...] = ids.T[:ids_ref.shape[0]]
    wts_ref[...] = jnp.where(lane == 0, w1, jnp.where(lane == 1, w2, 0.0))


def _router(x1s, w_r, b_r):
    m = x1s.shape[0] // WORD_ROWS
    tm = TM_PROJ
    return pl.pallas_call(
        _router_kernel,
        out_shape=(jax.ShapeDtypeStruct((8, m), I32),
                   jax.ShapeDtypeStruct((m, ROUTER_COLS), F32),
                   jax.ShapeDtypeStruct((8, ROUTER_COLS), I32)),
        grid=(m // tm,),
        in_specs=[
            pl.BlockSpec((tm * WORD_ROWS, LANES), lambda i: (i, 0)),
            pl.BlockSpec((D_MODEL, ROUTER_COLS), lambda i: (0, 0)),
            pl.BlockSpec((1, ROUTER_COLS), lambda i: (0, 0)),
        ],
        out_specs=(pl.BlockSpec((8, tm), lambda i: (0, i)),
                   pl.BlockSpec((tm, ROUTER_COLS), lambda i: (i, 0)),
                   pl.BlockSpec((8, ROUTER_COLS), lambda i: (0, 0))),
        scratch_shapes=[pltpu.VMEM((8, ROUTER_COLS), F32)],
        compiler_params=_cparams(("arbitrary",)),
        name="router",
    )(x1s, w_r, b_r)


def _start_row_gather(idx_ref, first, n_items, span, src_ref, dst_ref, sem):
    def body(c, _):
        for u in range(ROW_CHUNK):
            r = c * ROW_CHUNK + u
            src_row = pl.multiple_of(idx_ref[first + r], span)
            dst_row = pl.multiple_of(r * span, span)
            pltpu.make_async_copy(src_ref.at[pl.ds(src_row, span)],
                                  dst_ref.at[pl.ds(dst_row, span)], sem).start()
        return 0

    lax.fori_loop(0, n_items // ROW_CHUNK, body, 0)


def _wait_row_gather(n_items, span, src_ref, dst_ref, sem):
    def body(c, _):
        for _u in range(ROW_CHUNK):
            pltpu.make_async_copy(src_ref.at[pl.ds(0, span)], dst_ref.at[pl.ds(0, span)],
                                  sem).wait()
        return 0

    lax.fori_loop(0, n_items // ROW_CHUNK, body, 0)


def _expert_changed(te_ref, i):
    return jnp.logical_or(i == 0, te_ref[i] != te_ref[jnp.maximum(i - 1, 0)])


def _tile_row(i, nu):
    return jnp.minimum(i, nu[0] - 1)


def _stage_expert_weights(i, te_ref, nxt_ref, ws_ref, w_hbms, stage_ref, bf_refs, sem):
    expert = te_ref[i]
    slot = ws_ref[expert]

    def copies(e, s):
        return [pltpu.make_async_copy(w.at[e], stage_ref.at[s, n], sem.at[s])
                for n, w in enumerate(w_hbms)]

    @pl.when(i == 0)
    def _():
        for c in copies(expert, slot):
            c.start(priority=WEIGHT_DMA_PRIORITY)

    @pl.when(_expert_changed(te_ref, i))
    def _():
        for c in copies(expert, slot):
            c.wait()
        nxt = nxt_ref[expert]

        @pl.when(nxt < N_EXPERTS)
        def _():
            for c in copies(nxt, 1 - slot):
                c.start(priority=WEIGHT_DMA_PRIORITY)

        for n, bf_ref in enumerate(bf_refs):
            bf_ref[...] = stage_ref[slot, n].astype(BF16)


def _gateup_kernel(te_ref, nu_ref, tos_ref, nxt_ref, ws_ref, x_hbm, wg_hbm, wu_hbm, a_ref,
                   xbuf_ref, stage_ref, wgb_ref, wub_ref, sems, wsem):
    i = pl.program_id(0)
    n_used = nu_ref[0]
    slot = lax.rem(i, 2)

    def start(tile, buf):
        _start_row_gather(tos_ref, tile * TM_MOE, TM_MOE, WORD_ROWS, x_hbm,
                          xbuf_ref.at[buf], sems.at[buf])

    @pl.when(i == 0)
    def _():
        start(0, 0)

    @pl.when(i + 1 < n_used)
    def _():
        start(i + 1, 1 - slot)

    _stage_expert_weights(i, te_ref, nxt_ref, ws_ref, [wg_hbm, wu_hbm], stage_ref,
                          [wgb_ref, wub_ref], wsem)

    @pl.when(i < n_used)
    def _():
        _wait_row_gather(TM_MOE, WORD_ROWS, x_hbm, xbuf_ref.at[slot], sems.at[slot])
        gate, up = _dot_slab(xbuf_ref.at[slot], TM_MOE, [wgb_ref, wub_ref])
        a_ref[...] = (gate * jax.nn.sigmoid(gate) * up).astype(a_ref.dtype)

    @pl.when(i >= n_used)
    def _():
        a_ref[...] = jnp.zeros_like(a_ref)


def _grouped_gate_up(tile_expert, n_used, slab_of_slot, next_expert, stage_slot,
                     x1s, w_gate, w_up):
    p_rows = slab_of_slot.shape[0]
    _, d, f = w_gate.shape
    hbm = pl.BlockSpec(memory_space=pl.ANY)
    return pl.pallas_call(
        _gateup_kernel,
        out_shape=jax.ShapeDtypeStruct((p_rows, f), BF16),
        grid_spec=pltpu.PrefetchScalarGridSpec(
            num_scalar_prefetch=5,
            grid=(p_rows // TM_MOE,),
            in_specs=[hbm, hbm, hbm],
            out_specs=pl.BlockSpec((TM_MOE, f), lambda i, *_: (i, 0)),
            scratch_shapes=[pltpu.VMEM((2, TM_MOE * WORD_ROWS, LANES), U32),
                            pltpu.VMEM((2, 2, d, f), F32),
                            pltpu.VMEM((d, f), BF16), pltpu.VMEM((d, f), BF16),
                            pltpu.SemaphoreType.DMA((2,)), pltpu.SemaphoreType.DMA((2,))],
        ),
        compiler_params=_cparams(("arbitrary",)),
        name="moe_gate_up",
    )(tile_expert, n_used, slab_of_slot, next_expert, stage_slot, x1s, w_gate, w_up)


def _down_kernel(te_ref, nu_ref, nxt_ref, ws_ref, a_ref, wd_hbm, y_ref,
                 stage_ref, wdb_ref, wsem):
    i = pl.program_id(0)
    _stage_expert_weights(i, te_ref, nxt_ref, ws_ref, [wd_hbm], stage_ref, [wdb_ref], wsem)

    @pl.when(i < nu_ref[0])
    def _():
        y_ref[:, 0, :] = jnp.dot(a_ref[...], wdb_ref[...], preferred_element_type=F32)

    @pl.when(i >= nu_ref[0])
    def _():
        y_ref[...] = jnp.zeros_like(y_ref)


def _grouped_down(tile_expert, n_used, next_expert, stage_slot, act, w_down):
    p_rows, f = act.shape
    d = w_down.shape[2]
    return pl.pallas_call(
        _down_kernel,
        out_shape=jax.ShapeDtypeStruct((p_rows, 1, d), F32),
        grid_spec=pltpu.PrefetchScalarGridSpec(
            num_scalar_prefetch=4,
            grid=(p_rows // TM_MOE,),
            in_specs=[
                pl.BlockSpec((TM_MOE, f), lambda i, te, nu, *_: (_tile_row(i, nu), 0)),
                pl.BlockSpec(memory_space=pl.ANY),
            ],
            out_specs=pl.BlockSpec((TM_MOE, 1, d), lambda i, *_: (i, 0, 0)),
            scratch_shapes=[pltpu.VMEM((2, 1, f, d), F32), pltpu.VMEM((f, d), BF16),
                            pltpu.SemaphoreType.DMA((2,))],
        ),
        compiler_params=_cparams(("arbitrary",)),
        name="moe_down",
    )(tile_expert, n_used, next_expert, stage_slot, act, w_down)


def _combine_kernel(slot_ref, y_hbm, wts_ref, o_ref, buf_ref, sems):
    i = pl.program_id(0)
    tm = o_ref.shape[0]
    n_tokens = slot_ref.shape[0] // EXPERT_TOP_K
    slot = lax.rem(i, 2)

    def start(tile, buf):
        for k in range(EXPERT_TOP_K):
            _start_row_gather(slot_ref, k * n_tokens + tile * tm, tm, 1, y_hbm,
                              buf_ref.at[buf, k], sems.at[buf])

    @pl.when(i == 0)
    def _():
        start(0, 0)

    @pl.when(i + 1 < pl.num_programs(0))
    def _():
        start(i + 1, 1 - slot)

    _wait_row_gather(EXPERT_TOP_K * tm, 1, y_hbm, buf_ref.at[slot, 0], sems.at[slot])
    for c in range(o_ref.shape[1] // TN_PROJ):
        sl = slice(c * TN_PROJ, (c + 1) * TN_PROJ)
        o_ref[:, sl] = sum(wts_ref[:, k:k + 1] * buf_ref[slot, k, :, 0, sl]
                           for k in range(EXPERT_TOP_K))


def _combine_experts(slot_kt, y_sorted, wts):
    d = y_sorted.shape[2]
    m = slot_kt.shape[0] // EXPERT_TOP_K
    tm = TM_COMBINE
    return pl.pallas_call(
        _combine_kernel,
        out_shape=jax.ShapeDtypeStruct((m, d), F32),
        grid_spec=pltpu.PrefetchScalarGridSpec(
            num_scalar_prefetch=1,
            grid=(m // tm,),
            in_specs=[
                pl.BlockSpec(memory_space=pl.ANY),
                pl.BlockSpec((tm, ROUTER_COLS), lambda i, sl: (i, 0)),
            ],
            out_specs=pl.BlockSpec((tm, d), lambda i, sl: (i, 0)),
            scratch_shapes=[pltpu.VMEM((2, EXPERT_TOP_K, tm, 1, d), F32),
                            pltpu.SemaphoreType.DMA((2,))],
        ),
        compiler_params=_cparams(("arbitrary",)),
        name="moe_combine",
    )(slot_kt, y_sorted, wts)


def _final_kernel(x1s_ref, wg_ref, bg_ref, p_ref, wp_ref, y1_ref, mu_ref, rs_ref,
                  l1g_ref, l1b_ref, moe_ref, l2g_ref, l2b_ref,
                  o_ref, lo_ref, hi_ref, acc_ref, mu2_ref, rs2_ref):
    i = pl.program_id(0)
    j = pl.program_id(1)
    n_tiles = pl.num_programs(0) - 1
    n_chunks = acc_ref.shape[1]
    tn = acc_ref.shape[3]
    cur = lax.rem(i, 2)
    prv = 1 - cur

    @pl.when(jnp.logical_and(j == 0, i < n_tiles))
    def _():
        for s in range(WORD_ROWS):
            sl = slice(s * LANES, (s + 1) * LANES)
            lo_ref[:, sl], hi_ref[:, sl] = _unpack_halves(
                x1s_ref[pl.ds(s, lo_ref.shape[0], stride=WORD_ROWS), :])

    def build():
        gate = _dot_halves(lo_ref[...], hi_ref[...], wg_ref) + bg_ref[...]
        emb = jnp.dot(p_ref[...].astype(BF16), wp_ref[...].astype(BF16),
                      preferred_element_type=F32)
        x1 = (y1_ref[...] - mu_ref[...]) * rs_ref[...] * l1g_ref[...] + l1b_ref[...]
        acc_ref[cur, j] = DEEPNORM_ALPHA * x1 + moe_ref[...] + jax.nn.sigmoid(gate) * emb

    def emit():
        o_ref[...] = ((acc_ref[prv, j] - mu2_ref[prv]) * rs2_ref[prv]
                      * l2g_ref[...] + l2b_ref[...])

    @pl.when(i == 0)
    def _():
        build()

    @pl.when(jnp.logical_and(i > 0, i < n_tiles))
    def _():
        emit()
        build()

    @pl.when(i == n_tiles)
    def _():
        emit()

    @pl.when(jnp.logical_and(j == n_chunks - 1, i < n_tiles))
    def _():
        mu2_ref[cur], rs2_ref[cur] = _row_stats(acc_ref.at[cur], n_chunks, n_chunks * tn)


def _final_stage(x1s, w_pg_b, b_pg, p2, w_pp, y1, mu1, rs1, ln1_g, ln1_b,
                 moe, ln2_g, ln2_b):
    m, d = y1.shape
    tm, tn = TM_PROJ, TN_PROJ
    ple = p2.shape[1]
    n_chunks = d // tn

    n_tiles = m // tm

    def built(i):
        return jnp.minimum(i, n_tiles - 1)

    def emitted(i):
        return jnp.maximum(i - 1, 0)

    return pl.pallas_call(
        _final_kernel,
        out_shape=jax.ShapeDtypeStruct((m, d), F32),
        grid=(n_tiles + 1, n_chunks),
        in_specs=[
            pl.BlockSpec((tm * WORD_ROWS, LANES), lambda i, j: (built(i), 0),
                         pipeline_mode=pl.Buffered(1)),
            pl.BlockSpec((d, tn), lambda i, j: (0, j)),
            pl.BlockSpec((1, tn), lambda i, j: (0, j)),
            pl.BlockSpec((tm, ple), lambda i, j: (built(i), 0)),
            pl.BlockSpec((ple, tn), lambda i, j: (0, j)),
            pl.BlockSpec((tm, tn), lambda i, j: (built(i), j)),
            pl.BlockSpec((tm, 1), lambda i, j: (built(i), 0)),
            pl.BlockSpec((tm, 1), lambda i, j: (built(i), 0)),
            pl.BlockSpec((1, tn), lambda i, j: (0, j)),
            pl.BlockSpec((1, tn), lambda i, j: (0, j)),
            pl.BlockSpec((tm, tn), lambda i, j: (built(i), j)),
            pl.BlockSpec((1, tn), lambda i, j: (0, j)),
            pl.BlockSpec((1, tn), lambda i, j: (0, j)),
        ],
        out_specs=pl.BlockSpec((tm, tn), lambda i, j: (emitted(i), jnp.where(i == 0, 0, j))),
        scratch_shapes=[pltpu.VMEM((tm, d // 2), BF16), pltpu.VMEM((tm, d // 2), BF16),
                        pltpu.VMEM((2, n_chunks, tm, tn), F32),
                        pltpu.VMEM((2, tm, 1), F32), pltpu.VMEM((2, tm, 1), F32)],
        compiler_params=_cparams(("arbitrary", "arbitrary")),
        name="ple_moe_ln2",
    )(x1s, w_pg_b, b_pg, p2, w_pp, y1, mu1, rs1, ln1_g, ln1_b, moe, ln2_g, ln2_b)


def _routing_tables(ids_t, counts_row):
    counts = counts_row[EXPERT_LANE0:EXPERT_LANE0 + N_EXPERTS]
    tiles = (counts + TM_MOE - 1) // TM_MOE
    tile_end = jnp.cumsum(tiles)
    offsets = (tile_end - tiles) * TM_MOE
    n_used = tile_end[-1:]
    n_tokens = ids_t.shape[1]
    experts = ids_t[0:EXPERT_TOP_K]
    hit = experts[None] == jnp.arange(N_EXPERTS, dtype=I32)[:, None, None]
    first_slot = jnp.sum(jnp.where(hit, offsets[:, None, None], 0), axis=0)
    slots = (first_slot + ids_t[EXPERT_TOP_K:2 * EXPERT_TOP_K]).reshape(-1)
    n_tiles = (EXPERT_TOP_K * n_tokens) // TM_MOE + N_EXPERTS
    tile_ids = jnp.minimum(jnp.arange(n_tiles, dtype=I32), n_used - 1)
    tile_expert = jnp.sum(tile_end[None, :] <= tile_ids[:, None], axis=1).astype(I32)
    slab_of_slot = jnp.zeros((n_tiles * TM_MOE,), I32).at[slots].set(
        jnp.arange(slots.size, dtype=I32) % n_tokens * WORD_ROWS, unique_indices=True)
    eid = jnp.arange(N_EXPERTS, dtype=I32)
    later_owner = (eid[None, :] > eid[:, None]) & (tiles[None, :] > 0)
    next_expert = jnp.min(jnp.where(later_owner, eid[None, :], N_EXPERTS), axis=1).astype(I32)
    stage_slot = ((jnp.cumsum(tiles > 0) - 1) % 2).astype(I32)
    return (slots.astype(I32), slab_of_slot, tile_expert, n_used.astype(I32),
            next_expert, stage_slot)


def kernel(x, p, positions, w_in, b_in, sinks, g_norm_a, g_norm_b, w_out, b_out,
           ln1_g, ln1_b, w_group, b_group, w_er, b_er, w_gate, w_up, w_down,
           w_ple_gate, b_ple_gate, w_ple_proj, ln2_g, ln2_b):
    batch, seq, d = x.shape
    m = batch * seq
    row = lambda v: v.reshape(1, -1)
    x2 = x.reshape(m, d)
    for i in range(DEPTH):
        proj = _in_projection(x2, w_in[i].astype(BF16), row(b_in[i]))
        inv_freq = ROPE_THETA ** (-jnp.arange(0, SWA_HEAD_DIM, 2, dtype=F32) / SWA_HEAD_DIM)
        inv_freq = jnp.tile(inv_freq, LANES // inv_freq.shape[0]).reshape(1, LANES)
        o_a = _swa_attention(proj, positions.reshape(m, 1), inv_freq, sinks[i], batch, seq)
        o_b = _sb_attention(proj, batch, seq)
        y1, mu1, rs1, x1s = _out_projection(
            o_a, o_b, row(g_norm_a[i]), row(g_norm_b[i]), w_out[i].astype(BF16), x2,
            row(b_out[i]), row(ln1_g[i]), row(ln1_b[i]))
        pad = ROUTER_COLS - N_GROUPS - N_EXPERTS
        w_r = jnp.concatenate(
            [w_group[i], w_er[i].transpose(1, 0, 2).reshape(d, N_EXPERTS),
             jnp.zeros((d, pad), F32)], axis=1).astype(BF16)
        b_r = jnp.concatenate([b_group[i], b_er[i].reshape(-1), jnp.zeros((pad,), F32)])
        ids, wts, counts = _router(x1s, w_r, row(b_r))
        (slots, slab_of_slot, tile_expert, n_used, next_expert,
         stage_slot) = _routing_tables(ids, counts[0])
        act = _grouped_gate_up(tile_expert, n_used, slab_of_slot, next_expert, stage_slot,
                               x1s, w_gate[i], w_up[i])
        y_sorted = _grouped_down(tile_expert, n_used, next_expert, stage_slot, act, w_down[i])
        moe = _combine_experts(slots, y_sorted, wts)
        x2 = _final_stage(x1s, w_ple_gate[i].astype(BF16), row(b_ple_gate[i]),
                          p[i].reshape(m, PLE_DIM), w_ple_proj[i], y1, mu1, rs1,
                          row(ln1_g[i]), row(ln1_b[i]), moe, row(ln2_g[i]), row(ln2_b[i]))
    return x2.reshape(batch, seq, d)
```

```python
import math

import jax
import jax.numpy as jnp
from jax import lax
from jax.experimental import pallas as pl
from jax.experimental.pallas import tpu as pltpu

F32 = jnp.float32
BF16 = jnp.bfloat16
I32 = jnp.int32
U32 = jnp.uint32

D_MODEL = 4096
PLE_DIM = 256
BLOCK = 128
ROPE_THETA = 10000.0
LN_EPS = 1e-5
RMS_EPS = 1e-6
NEG_INF = -1e30

SWA_HEAD_DIM = 64
SWA_WIDTH = D_MODEL // 2
SWA_HEADS = SWA_WIDTH // SWA_HEAD_DIM
SWA_KV_HEADS = SWA_HEADS // 8
SWA_GROUP = SWA_HEADS // SWA_KV_HEADS
SWA_KV_WIDTH = SWA_KV_HEADS * SWA_HEAD_DIM

SB_HEAD_DIM = 128
SB_WIDTH = D_MODEL - SWA_WIDTH
SB_HEADS = SB_WIDTH // SB_HEAD_DIM

IN_WIDTH = SWA_WIDTH + 2 * SWA_KV_WIDTH + 3 * SB_WIDTH
A_WIDTH = SWA_WIDTH + 2 * SWA_KV_WIDTH
Q_B_COL = 0
K_B_COL = Q_B_COL + SB_WIDTH
V_B_COL = K_B_COL + SB_WIDTH
Q_A_COL = V_B_COL + SB_WIDTH
K_A_COL = Q_A_COL + SWA_WIDTH
V_A_COL = K_A_COL + SWA_KV_WIDTH

N_GROUPS = 4
EXPERTS_PER_GROUP = 8
N_EXPERTS = N_GROUPS * EXPERTS_PER_GROUP
EXPERT_TOP_K = 2
DEPTH = 1
DEEPNORM_ALPHA = (2.0 * DEPTH) ** 0.25

LANES = 128
SUBLANES = 8
VMEM_LIMIT_BYTES = 56 * 1024 * 1024

EXP_ZERO_LOG = -126 * math.log(2.0) - 2.0

TM_IN = 1024
TM_PROJ = 512
TN_PROJ = 512
TM_MOE = 256
TM_COMBINE = 128
SB_HEADS_PER_STEP = 16
SUFFIX_TERMS = 2
ROW_CHUNK = 16
WEIGHT_DMA_PRIORITY = 1


def _cparams(sem):
    return pltpu.CompilerParams(dimension_semantics=sem,
                                vmem_limit_bytes=VMEM_LIMIT_BYTES)


def _pack_halves(lo, hi):
    return lax.bitcast_convert_type(
        pltpu.pack_elementwise([lo, hi], packed_dtype=BF16), U32)


def _unpack_halves(words):
    return tuple(
        pltpu.unpack_elementwise(words, index=k, packed_dtype=BF16,
                                 unpacked_dtype=F32).astype(BF16) for k in (0, 1))


def _dot_halves(lo, hi, w_ref):
    half = lo.shape[1]
    return (jnp.dot(lo, w_ref[:half], preferred_element_type=F32)
            + jnp.dot(hi, w_ref[half:], preferred_element_type=F32))


WORD_ROWS = (D_MODEL // 2) // LANES
SLAB_PAIR = 2 * LANES


def _slab_words(slab_ref, pair, tokens):
    return jnp.concatenate(
        [slab_ref[pl.ds(2 * pair + k, tokens, stride=WORD_ROWS), :] for k in (0, 1)],
        axis=1)


def _dot_slab(slab_ref, tokens, w_refs):
    half = D_MODEL // 2
    outs = [None] * len(w_refs)
    for pair in range(half // SLAB_PAIR):
        lo, hi = _unpack_halves(_slab_words(slab_ref, pair, tokens))
        rows_lo = slice(pair * SLAB_PAIR, (pair + 1) * SLAB_PAIR)
        rows_hi = slice(half + pair * SLAB_PAIR, half + (pair + 1) * SLAB_PAIR)
        for n, w_ref in enumerate(w_refs):
            part = (jnp.dot(lo, w_ref[rows_lo], preferred_element_type=F32)
                    + jnp.dot(hi, w_ref[rows_hi], preferred_element_type=F32))
            outs[n] = part if outs[n] is None else outs[n] + part
    return outs


def _dot_nt(a, b):
    return lax.dot_general(a, b, (((1,), (1,)), ((), ())),
                           preferred_element_type=F32)


def _inproj_kernel(x_ref, w_ref, b_ref, o_ref, xb_ref):
    @pl.when(pl.program_id(1) == 0)
    def _():
        xb_ref[...] = x_ref[...].astype(BF16)

    acc = jnp.dot(xb_ref[...], w_ref[...], preferred_element_type=F32)
    o_ref[...] = (acc + b_ref[...]).astype(o_ref.dtype)


def _in_projection(x2, w_b, b):
    m, k = x2.shape
    n = w_b.shape[1]
    n_blocks = n // TN_PROJ
    assert A_WIDTH % TN_PROJ == 0
    a_blocks = A_WIDTH // TN_PROJ

    def src(j):
        return lax.rem(j + a_blocks, n_blocks)

    return pl.pallas_call(
        _inproj_kernel,
        out_shape=jax.ShapeDtypeStruct((m, n), BF16),
        grid=(m // TM_IN, n_blocks),
        in_specs=[
            pl.BlockSpec((TM_IN, k), lambda i, j: (i, 0)),
            pl.BlockSpec((k, TN_PROJ), lambda i, j: (0, src(j))),
            pl.BlockSpec((1, TN_PROJ), lambda i, j: (0, src(j))),
        ],
        out_specs=pl.BlockSpec((TM_IN, TN_PROJ), lambda i, j: (i, j)),
        scratch_shapes=[pltpu.VMEM((TM_IN, k), BF16)],
        compiler_params=_cparams(("arbitrary", "arbitrary")),
        name="in_projection",
    )(x2, w_b, b)


def _swa_kernel(sinks_ref, q_ref, kc_ref, vc_ref, vp_ref,
                posc_ref, invf_ref, o_ref, kprev_ref):
    n = pl.program_id(1)

    @pl.when(n == 0)
    def _():
        kprev_ref[...] = jnp.zeros_like(kprev_ref)
    lane = lax.broadcasted_iota(I32, (1, LANES), 1)
    first_half = (lane % SWA_HEAD_DIM) < (SWA_HEAD_DIM // 2)

    def tables(pos_ref):
        ang = pos_ref[...].astype(F32) * invf_ref[...]
        sin = jnp.sin(ang)
        return jnp.cos(ang), jnp.where(first_half, -sin, sin)

    def rope(x, cos, sin_signed):
        partner = jnp.where(first_half,
                            pltpu.roll(x, LANES - SWA_HEAD_DIM // 2, 1),
                            pltpu.roll(x, SWA_HEAD_DIM // 2, 1))
        return x * cos + partner * sin_signed

    cos_c, sin_c = tables(posc_ref)

    def rope_block(ref, cos, sin_signed):
        width = ref.shape[1]
        return [rope(ref[:, c * LANES:(c + 1) * LANES].astype(F32), cos,
                     sin_signed).astype(BF16) for c in range(width // LANES)]

    q_chunks = rope_block(q_ref, cos_c, sin_c)
    k_cur = rope_block(kc_ref, cos_c, sin_c)
    k_chunks = [jnp.concatenate([kprev_ref[:, c * LANES:(c + 1) * LANES], k_c], axis=0)
                for c, k_c in enumerate(k_cur)]
    for c, k_c in enumerate(k_cur):
        kprev_ref[:, c * LANES:(c + 1) * LANES] = k_c
    v_all = jnp.concatenate([vp_ref[...], vc_ref[...]], axis=0)

    qi = lax.broadcasted_iota(I32, (BLOCK, 2 * BLOCK), 0)
    kj = lax.broadcasted_iota(I32, (BLOCK, 2 * BLOCK), 1)
    rel = qi - (kj - BLOCK)
    valid = (rel >= 0) & (rel < BLOCK) & ((kj >= BLOCK) | (n > 0))

    def head_slice(chunks, head):
        half = head % 2
        return chunks[head // 2][:, half * SWA_HEAD_DIM:(half + 1) * SWA_HEAD_DIM]

    scale = 1.0 / math.sqrt(SWA_HEAD_DIM)
    scores = []
    for h in range(SWA_KV_HEADS):
        k_h = head_slice(k_chunks, h)
        q_h = jnp.concatenate(
            [head_slice(q_chunks, h * SWA_GROUP + g) for g in range(SWA_GROUP)],
            axis=0)
        scores.append(_dot_nt(q_h, k_h) * scale)
    probs = []
    for h in range(SWA_KV_HEADS):
        probs_h = []
        for g in range(SWA_GROUP):
            sink = sinks_ref[h * SWA_GROUP + g]
            s_g = jnp.where(valid, scores[h][g * BLOCK:(g + 1) * BLOCK], NEG_INF)
            m = jnp.maximum(jnp.max(s_g, axis=-1, keepdims=True), sink)
            e = jnp.exp(s_g - m)
            den = jnp.sum(e, axis=-1, keepdims=True) + jnp.exp(sink - m)
            probs_h.append((e / den).astype(BF16))
        probs.append(jnp.concatenate(probs_h, axis=0))
    for h in range(SWA_KV_HEADS):
        v_h = v_all[:, h * SWA_HEAD_DIM:(h + 1) * SWA_HEAD_DIM]
        o_h = jnp.dot(probs[h], v_h, preferred_element_type=F32)
        for g in range(SWA_GROUP):
            col = (h * SWA_GROUP + g) * SWA_HEAD_DIM
            o_ref[:, col:col + SWA_HEAD_DIM] = o_h[g * BLOCK:(g + 1) * BLOCK]


def _swa_attention(proj, pos2, inv_freq, sinks, batch, seq):
    nb = seq // BLOCK
    assert Q_A_COL % SWA_WIDTH == 0 and K_A_COL % SWA_KV_WIDTH == 0
    assert V_A_COL % SWA_KV_WIDTH == 0
    qcol = Q_A_COL // SWA_WIDTH
    kcol = K_A_COL // SWA_KV_WIDTH
    vcol = V_A_COL // SWA_KV_WIDTH

    def cur(b, n):
        return b * nb + n

    def prev(b, n):
        return b * nb + jnp.maximum(n - 1, 0)

    return pl.pallas_call(
        _swa_kernel,
        out_shape=jax.ShapeDtypeStruct((batch * seq, SWA_WIDTH), F32),
        grid=(batch, nb),
        in_specs=[
            pl.BlockSpec(memory_space=pltpu.SMEM),
            pl.BlockSpec((BLOCK, SWA_WIDTH), lambda b, n: (cur(b, n), qcol)),
            pl.BlockSpec((BLOCK, SWA_KV_WIDTH), lambda b, n: (cur(b, n), kcol)),
            pl.BlockSpec((BLOCK, SWA_KV_WIDTH), lambda b, n: (cur(b, n), vcol)),
            pl.BlockSpec((BLOCK, SWA_KV_WIDTH), lambda b, n: (prev(b, n), vcol)),
            pl.BlockSpec((BLOCK, 1), lambda b, n: (cur(b, n), 0)),
            pl.BlockSpec((1, LANES), lambda b, n: (0, 0)),
        ],
        out_specs=pl.BlockSpec((BLOCK, SWA_WIDTH), lambda b, n: (cur(b, n), 0)),
        scratch_shapes=[pltpu.VMEM((BLOCK, SWA_KV_WIDTH), BF16)],
        compiler_params=_cparams(("arbitrary", "arbitrary")),
        name="swa_attention",
    )(sinks, proj, proj, proj, proj, pos2, inv_freq)


def _sb_kernel(q_ref, k_ref, v_ref, o_ref):
    n = pl.program_id(2)
    heads = [slice(h * SB_HEAD_DIM, (h + 1) * SB_HEAD_DIM)
             for h in range(SB_HEADS_PER_STEP)]
    rows = SB_HEADS_PER_STEP * BLOCK
    scale = 1.0 / math.sqrt(SB_HEAD_DIM)
    key_j = lax.broadcasted_iota(I32, (BLOCK, 2 * BLOCK), 0)
    out_c = lax.broadcasted_iota(I32, (BLOCK, 2 * BLOCK), 1)
    later_total = ((key_j > out_c) | (out_c >= BLOCK)).astype(BF16)
    q_row = lax.broadcasted_iota(I32, (rows, BLOCK), 0) % BLOCK
    k_col = lax.broadcasted_iota(I32, (rows, BLOCK), 1)
    causal = k_col < q_row

    def block(kb, carry, acc, mask):
        start = pl.multiple_of(kb * BLOCK, BLOCK)
        z = jnp.concatenate(
            [_dot_nt(q_ref[:, hd], k_ref[pl.ds(start, BLOCK), hd]) for hd in heads],
            axis=0) * scale
        t = jnp.log(1.0 + jnp.exp(-jnp.abs(z)))
        log_not = jnp.minimum(-z, 0.0) - t
        log_beta = jnp.minimum(z, 0.0) - t
        if mask is not None:
            log_not = jnp.where(mask, log_not, 0.0)
        terms, rest = [], log_not
        for _ in range(SUFFIX_TERMS):
            terms.append(rest.astype(BF16))
            rest = rest - terms[-1].astype(F32)
        parts = jnp.dot(jnp.concatenate(terms, axis=0), later_total,
                        preferred_element_type=F32)
        sums = parts[:rows]
        for t in range(1, SUFFIX_TERMS):
            sums = sums + parts[t * rows:(t + 1) * rows]
        a = jnp.exp(log_beta + sums[:, :BLOCK] + carry)
        if mask is not None:
            a = jnp.where(mask, a, 0.0)
        a = a.astype(BF16)
        pv = jnp.concatenate(
            [jnp.dot(a[h * BLOCK:(h + 1) * BLOCK], v_ref[pl.ds(start, BLOCK), hd],
                     preferred_element_type=F32) for h, hd in enumerate(heads)], axis=0)
        return carry + sums[:, BLOCK:], acc + pv

    zeros = jnp.zeros((rows, BLOCK), F32)
    carry, acc = block(n, zeros, zeros, causal)

    def cond(state):
        kb, carry, _ = state
        return jnp.logical_and(kb >= 0, jnp.max(carry) > EXP_ZERO_LOG)

    def body(state):
        kb, carry, acc = state
        carry, acc = block(kb, carry, acc, None)
        return kb - 1, carry, acc

    _, _, acc = lax.while_loop(cond, body, (n - 1, carry, acc))
    for h, hd in enumerate(heads):
        o_ref[:, hd] = acc[h * BLOCK:(h + 1) * BLOCK]


def _sb_attention(proj, batch, seq):
    nb = seq // BLOCK
    width = SB_HEADS_PER_STEP * SB_HEAD_DIM
    assert Q_B_COL % width == 0 and K_B_COL % width == 0 and V_B_COL % width == 0
    qcol = Q_B_COL // width
    kcol = K_B_COL // width
    vcol = V_B_COL // width
    return pl.pallas_call(
        _sb_kernel,
        out_shape=jax.ShapeDtypeStruct((batch * seq, SB_WIDTH), F32),
        grid=(batch, SB_HEADS // SB_HEADS_PER_STEP, nb),
        in_specs=[
            pl.BlockSpec((BLOCK, width), lambda b, h, n: (b * nb + n, qcol + h)),
            pl.BlockSpec((seq, width), lambda b, h, n: (b, kcol + h),
                         pipeline_mode=pl.Buffered(1)),
            pl.BlockSpec((seq, width), lambda b, h, n: (b, vcol + h),
                         pipeline_mode=pl.Buffered(1)),
        ],
        out_specs=pl.BlockSpec((BLOCK, width), lambda b, h, n: (b * nb + n, h)),
        compiler_params=_cparams(("arbitrary", "arbitrary", "arbitrary")),
        name="sb_attention",
    )(proj, proj, proj)


def _row_stats(chunks_ref, n_chunks, width):
    total = chunks_ref[0].sum(axis=-1, keepdims=True)
    for c in range(1, n_chunks):
        total = total + chunks_ref[c].sum(axis=-1, keepdims=True)
    mu = total / width
    sq = jnp.square(chunks_ref[0] - mu).sum(axis=-1, keepdims=True)
    for c in range(1, n_chunks):
        sq = sq + jnp.square(chunks_ref[c] - mu).sum(axis=-1, keepdims=True)
    return mu, lax.rsqrt(sq / width + LN_EPS)


def _outproj_kernel(oa_ref, ob_ref, ga_ref, gb_ref, wa_ref, wb_ref, x_ref, bo_ref,
                    lg_ref, lb_ref, y_ref, mu_ref, rs_ref, x1s_ref,
                    ma_ref, mb_ref, acc_ref):
    j = pl.program_id(1)
    n_chunks = acc_ref.shape[0]
    tn = acc_ref.shape[2]

    @pl.when(j == 0)
    def _():
        for o_ref, g_ref, m_ref in ((oa_ref, ga_ref, ma_ref), (ob_ref, gb_ref, mb_ref)):
            width = o_ref.shape[1]
            cols = [slice(c * tn, (c + 1) * tn) for c in range(width // tn)]
            sq = sum(jnp.square(o_ref[:, sl]).sum(axis=-1, keepdims=True) for sl in cols)
            r = lax.rsqrt(sq / width + RMS_EPS)
            for sl in cols:
                m_ref[:, sl] = (o_ref[:, sl] * r * g_ref[:, sl]).astype(BF16)

    mix = (jnp.dot(ma_ref[...], wa_ref[...], preferred_element_type=F32)
           + jnp.dot(mb_ref[...], wb_ref[...], preferred_element_type=F32))
    y = DEEPNORM_ALPHA * x_ref[...] + (mix + bo_ref[...])
    y_ref[...] = y
    acc_ref[j] = y

    @pl.when(j == n_chunks - 1)
    def _():
        mu, rs = _row_stats(acc_ref, n_chunks, n_chunks * tn)
        mu_ref[...] = mu
        rs_ref[...] = rs

        def normed(c):
            sl = slice(c * tn, (c + 1) * tn)
            return (acc_ref[c] - mu) * rs * lg_ref[:, sl] + lb_ref[:, sl]

        tm = acc_ref.shape[1]
        for c in range(n_chunks // 2):
            words = _pack_halves(normed(c), normed(c + n_chunks // 2))
            for q in range(tn // LANES):
                x1s_ref[pl.ds(c * (tn // LANES) + q, tm, stride=WORD_ROWS), :] = (
                    words[:, q * LANES:(q + 1) * LANES])


def _out_projection(o_a, o_b, g_a, g_b, w_b, x2, b_out, ln_g, ln_b):
    m, d = x2.shape
    tm, tn = TM_PROJ, TN_PROJ
    half = o_a.shape[1]
    once = dict(pipeline_mode=pl.Buffered(1))
    return pl.pallas_call(
        _outproj_kernel,
        out_shape=(jax.ShapeDtypeStruct((m, d), F32),
                   jax.ShapeDtypeStruct((m, 1), F32),
                   jax.ShapeDtypeStruct((m, 1), F32),
                   jax.ShapeDtypeStruct((m * WORD_ROWS, LANES), U32)),
        grid=(m // tm, d // tn),
        in_specs=[
            pl.BlockSpec((tm, half), lambda i, j: (i, 0), **once),
            pl.BlockSpec((tm, half), lambda i, j: (i, 0), **once),
            pl.BlockSpec((1, half), lambda i, j: (0, 0)),
            pl.BlockSpec((1, half), lambda i, j: (0, 0)),
            pl.BlockSpec((half, tn), lambda i, j: (0, j)),
            pl.BlockSpec((half, tn), lambda i, j: (1, j)),
            pl.BlockSpec((tm, tn), lambda i, j: (i, j)),
            pl.BlockSpec((1, tn), lambda i, j: (0, j)),
            pl.BlockSpec((1, d), lambda i, j: (0, 0)),
            pl.BlockSpec((1, d), lambda i, j: (0, 0)),
        ],
        out_specs=(pl.BlockSpec((tm, tn), lambda i, j: (i, j)),
                   pl.BlockSpec((tm, 1), lambda i, j: (i, 0)),
                   pl.BlockSpec((tm, 1), lambda i, j: (i, 0)),
                   pl.BlockSpec((tm * WORD_ROWS, LANES), lambda i, j: (i, 0))),
        scratch_shapes=[pltpu.VMEM((tm, half), BF16), pltpu.VMEM((tm, half), BF16),
                        pltpu.VMEM((d // tn, tm, tn), F32)],
        compiler_params=_cparams(("arbitrary", "arbitrary")),
        name="out_projection_ln1",
    )(o_a, o_b, g_a, g_b, w_b, w_b, x2, b_out, ln_g, ln_b)


ROUTER_COLS = LANES
EXPERT_LANE0 = N_GROUPS


def _router_kernel(x_ref, w_ref, b_ref, ids_ref, wts_ref, cnt_ref, carry_ref):
    i = pl.program_id(0)
    tm = wts_ref.shape[0]

    @pl.when(i == 0)
    def _():
        carry_ref[...] = jnp.zeros_like(carry_ref)

    logits = _dot_slab(x_ref, tm, [w_ref])[0] + b_ref[...]
    lane = lax.broadcasted_iota(I32, (tm, ROUTER_COLS), 1)
    big = jnp.int32(ROUTER_COLS)

    def first_argmax(vals):
        top = jnp.max(vals, axis=-1, keepdims=True)
        idx = jnp.min(jnp.where(vals == top, lane, big), axis=-1, keepdims=True)
        return top, idx

    is_group = lane < N_GROUPS
    g_logits = jnp.where(is_group, logits, -jnp.inf)
    g_top, g_idx = first_argmax(g_logits)
    g_w = 1.0 / jnp.sum(jnp.exp(g_logits - g_top), axis=-1, keepdims=True)

    first = EXPERT_LANE0 + g_idx * EXPERTS_PER_GROUP
    in_group = (lane >= first) & (lane < first + EXPERTS_PER_GROUP)
    e_logits = jnp.where(in_group, logits, -jnp.inf)
    top1, idx1 = first_argmax(e_logits)
    top2, idx2 = first_argmax(jnp.where(lane == idx1, -jnp.inf, e_logits))
    e2 = jnp.exp(top2 - top1)
    w1 = g_w / (1.0 + e2)
    w2 = g_w * e2 / (1.0 + e2)

    hit1 = lane == idx1
    hit2 = lane == idx2
    onehot = (hit1 | hit2).astype(BF16)
    r = lax.broadcasted_iota(I32, (tm, tm), 0)
    c = lax.broadcasted_iota(I32, (tm, tm), 1)
    before = (c < r).astype(BF16)
    prior = jnp.dot(before, onehot, preferred_element_type=F32) + carry_ref[0:1, :]
    rank1 = jnp.sum(jnp.where(hit1, prior, 0.0), axis=-1, keepdims=True)
    rank2 = jnp.sum(jnp.where(hit2, prior, 0.0), axis=-1, keepdims=True)
    counts = carry_ref[0:1, :] + jnp.sum(onehot.astype(F32), axis=0, keepdims=True)
    carry_ref[...] = jnp.broadcast_to(counts, carry_ref.shape)
    cnt_ref[...] = jnp.broadcast_to(counts, cnt_ref.shape).astype(I32)

    ids = jnp.where(lane == 0, idx1 - EXPERT_LANE0,
          jnp.where(lane == 1, idx2 - EXPERT_LANE0,
          jnp.where(lane == 2, rank1.astype(I32),
          jnp.where(lane == 3, rank2.astype(I32), 0))))
    ids_ref[...] = ids.T[:ids_ref.shape[0]]
    wts_ref[...] = jnp.where(lane == 0, w1, jnp.where(lane == 1, w2, 0.0))


def _router(x1s, w_r, b_r):
    m = x1s.shape[0] // WORD_ROWS
    tm = TM_PROJ
    return pl.pallas_call(
        _router_kernel,
        out_shape=(jax.ShapeDtypeStruct((SUBLANES, m), I32),
                   jax.ShapeDtypeStruct((m, ROUTER_COLS), F32),
                   jax.ShapeDtypeStruct((SUBLANES, ROUTER_COLS), I32)),
        grid=(m // tm,),
        in_specs=[
            pl.BlockSpec((tm * WORD_ROWS, LANES), lambda i: (i, 0)),
            pl.BlockSpec((D_MODEL, ROUTER_COLS), lambda i: (0, 0)),
            pl.BlockSpec((1, ROUTER_COLS), lambda i: (0, 0)),
        ],
        out_specs=(pl.BlockSpec((SUBLANES, tm), lambda i: (0, i)),
                   pl.BlockSpec((tm, ROUTER_COLS), lambda i: (i, 0)),
                   pl.BlockSpec((SUBLANES, ROUTER_COLS), lambda i: (0, 0))),
        scratch_shapes=[pltpu.VMEM((SUBLANES, ROUTER_COLS), F32)],
        compiler_params=_cparams(("arbitrary",)),
        name="router",
    )(x1s, w_r, b_r)


def _start_row_gather(idx_ref, first, n_items, span, src_ref, dst_ref, sem):
    def body(c, _):
        for u in range(ROW_CHUNK):
            r = c * ROW_CHUNK + u
            src_row = pl.multiple_of(idx_ref[first + r], span)
            dst_row = pl.multiple_of(r * span, span)
            pltpu.make_async_copy(src_ref.at[pl.ds(src_row, span)],
                                  dst_ref.at[pl.ds(dst_row, span)], sem).start()
        return 0

    lax.fori_loop(0, n_items // ROW_CHUNK, body, 0)


def _wait_row_gather(n_items, span, src_ref, dst_ref, sem):
    def body(c, _):
        for _u in range(ROW_CHUNK):
            pltpu.make_async_copy(src_ref.at[pl.ds(0, span)], dst_ref.at[pl.ds(0, span)],
                                  sem).wait()
        return 0

    lax.fori_loop(0, n_items // ROW_CHUNK, body, 0)


def _expert_changed(te_ref, i):
    return jnp.logical_or(i == 0, te_ref[i] != te_ref[jnp.maximum(i - 1, 0)])


def _tile_row(i, nu):
    return jnp.minimum(i, nu[0] - 1)


def _stage_expert_weights(i, te_ref, nxt_ref, ws_ref, w_hbms, stage_ref, bf_refs, sem):
    expert = te_ref[i]
    slot = ws_ref[expert]

    def copies(e, s):
        return [pltpu.make_async_copy(w.at[e], stage_ref.at[s, n], sem.at[s])
                for n, w in enumerate(w_hbms)]

    @pl.when(i == 0)
    def _():
        for c in copies(expert, slot):
            c.start(priority=WEIGHT_DMA_PRIORITY)

    @pl.when(_expert_changed(te_ref, i))
    def _():
        for c in copies(expert, slot):
            c.wait()
        nxt = nxt_ref[expert]

        @pl.when(nxt < N_EXPERTS)
        def _():
            for c in copies(nxt, 1 - slot):
                c.start(priority=WEIGHT_DMA_PRIORITY)

        for n, bf_ref in enumerate(bf_refs):
            bf_ref[...] = stage_ref[slot, n].astype(BF16)


def _gateup_kernel(te_ref, nu_ref, tos_ref, nxt_ref, ws_ref, x_hbm, wg_hbm, wu_hbm, a_ref,
                   xbuf_ref, stage_ref, wgb_ref, wub_ref, sems, wsem):
    i = pl.program_id(0)
    n_used = nu_ref[0]
    slot = lax.rem(i, 2)

    def start(tile, buf):
        _start_row_gather(tos_ref, tile * TM_MOE, TM_MOE, WORD_ROWS, x_hbm,
                          xbuf_ref.at[buf], sems.at[buf])

    @pl.when(i == 0)
    def _():
        start(0, 0)

    @pl.when(i + 1 < n_used)
    def _():
        start(i + 1, 1 - slot)

    _stage_expert_weights(i, te_ref, nxt_ref, ws_ref, [wg_hbm, wu_hbm], stage_ref,
                          [wgb_ref, wub_ref], wsem)

    @pl.when(i < n_used)
    def _():
        _wait_row_gather(TM_MOE, WORD_ROWS, x_hbm, xbuf_ref.at[slot], sems.at[slot])
        gate, up = _dot_slab(xbuf_ref.at[slot], TM_MOE, [wgb_ref, wub_ref])
        a_ref[...] = (gate * jax.nn.sigmoid(gate) * up).astype(a_ref.dtype)

    @pl.when(i >= n_used)
    def _():
        a_ref[...] = jnp.zeros_like(a_ref)


def _grouped_gate_up(tile_expert, n_used, slab_of_slot, next_expert, stage_slot,
                     x1s, w_gate, w_up):
    p_rows = slab_of_slot.shape[0]
    _, d, f = w_gate.shape
    hbm = pl.BlockSpec(memory_space=pl.ANY)
    return pl.pallas_call(
        _gateup_kernel,
        out_shape=jax.ShapeDtypeStruct((p_rows, f), BF16),
        grid_spec=pltpu.PrefetchScalarGridSpec(
            num_scalar_prefetch=5,
            grid=(p_rows // TM_MOE,),
            in_specs=[hbm, hbm, hbm],
            out_specs=pl.BlockSpec((TM_MOE, f), lambda i, *_: (i, 0)),
            scratch_shapes=[pltpu.VMEM((2, TM_MOE * WORD_ROWS, LANES), U32),
                            pltpu.VMEM((2, 2, d, f), F32),
                            pltpu.VMEM((d, f), BF16), pltpu.VMEM((d, f), BF16),
                            pltpu.SemaphoreType.DMA((2,)), pltpu.SemaphoreType.DMA((2,))],
        ),
        compiler_params=_cparams(("arbitrary",)),
        name="moe_gate_up",
    )(tile_expert, n_used, slab_of_slot, next_expert, stage_slot, x1s, w_gate, w_up)


def _down_kernel(te_ref, nu_ref, nxt_ref, ws_ref, a_ref, wd_hbm, y_ref,
                 stage_ref, wdb_ref, wsem):
    i = pl.program_id(0)
    _stage_expert_weights(i, te_ref, nxt_ref, ws_ref, [wd_hbm], stage_ref, [wdb_ref], wsem)

    @pl.when(i < nu_ref[0])
    def _():
        y_ref[:, 0, :] = jnp.dot(a_ref[...], wdb_ref[...], preferred_element_type=F32)

    @pl.when(i >= nu_ref[0])
    def _():
        y_ref[...] = jnp.zeros_like(y_ref)


def _grouped_down(tile_expert, n_used, next_expert, stage_slot, act, w_down):
    p_rows, f = act.shape
    d = w_down.shape[2]
    return pl.pallas_call(
        _down_kernel,
        out_shape=jax.ShapeDtypeStruct((p_rows, 1, d), F32),
        grid_spec=pltpu.PrefetchScalarGridSpec(
            num_scalar_prefetch=4,
            grid=(p_rows // TM_MOE,),
            in_specs=[
                pl.BlockSpec((TM_MOE, f), lambda i, te, nu, *_: (_tile_row(i, nu), 0)),
                pl.BlockSpec(memory_space=pl.ANY),
            ],
            out_specs=pl.BlockSpec((TM_MOE, 1, d), lambda i, *_: (i, 0, 0)),
            scratch_shapes=[pltpu.VMEM((2, 1, f, d), F32), pltpu.VMEM((f, d), BF16),
                            pltpu.SemaphoreType.DMA((2,))],
        ),
        compiler_params=_cparams(("arbitrary",)),
        name="moe_down",
    )(tile_expert, n_used, next_expert, stage_slot, act, w_down)


def _combine_kernel(slot_ref, y_hbm, wts_ref, o_ref, buf_ref, sems):
    i = pl.program_id(0)
    tm = o_ref.shape[0]
    n_tokens = slot_ref.shape[0] // EXPERT_TOP_K
    slot = lax.rem(i, 2)

    def start(tile, buf):
        for k in range(EXPERT_TOP_K):
            _start_row_gather(slot_ref, k * n_tokens + tile * tm, tm, 1, y_hbm,
                              buf_ref.at[buf, k], sems.at[buf])

    @pl.when(i == 0)
    def _():
        start(0, 0)

    @pl.when(i + 1 < pl.num_programs(0))
    def _():
        start(i + 1, 1 - slot)

    _wait_row_gather(EXPERT_TOP_K * tm, 1, y_hbm, buf_ref.at[slot, 0], sems.at[slot])
    for c in range(o_ref.shape[1] // TN_PROJ):
        sl = slice(c * TN_PROJ, (c + 1) * TN_PROJ)
        o_ref[:, sl] = sum(wts_ref[:, k:k + 1] * buf_ref[slot, k, :, 0, sl]
                           for k in range(EXPERT_TOP_K))


def _combine_experts(slot_kt, y_sorted, wts):
    d = y_sorted.shape[2]
    m = slot_kt.shape[0] // EXPERT_TOP_K
    tm = TM_COMBINE
    return pl.pallas_call(
        _combine_kernel,
        out_shape=jax.ShapeDtypeStruct((m, d), F32),
        grid_spec=pltpu.PrefetchScalarGridSpec(
            num_scalar_prefetch=1,
            grid=(m // tm,),
            in_specs=[
                pl.BlockSpec(memory_space=pl.ANY),
                pl.BlockSpec((tm, ROUTER_COLS), lambda i, sl: (i, 0)),
            ],
            out_specs=pl.BlockSpec((tm, d), lambda i, sl: (i, 0)),
            scratch_shapes=[pltpu.VMEM((2, EXPERT_TOP_K, tm, 1, d), F32),
                            pltpu.SemaphoreType.DMA((2,))],
        ),
        compiler_params=_cparams(("arbitrary",)),
        name="moe_combine",
    )(slot_kt, y_sorted, wts)


def _final_kernel(x1s_ref, wg_ref, bg_ref, p_ref, wp_ref, y1_ref, mu_ref, rs_ref,
                  l1g_ref, l1b_ref, moe_ref, l2g_ref, l2b_ref,
                  o_ref, lo_ref, hi_ref, acc_ref, mu2_ref, rs2_ref):
    i = pl.program_id(0)
    j = pl.program_id(1)
    n_tiles = pl.num_programs(0) - 1
    n_chunks = acc_ref.shape[1]
    tn = acc_ref.shape[3]
    cur = lax.rem(i, 2)
    prv = 1 - cur

    @pl.when(jnp.logical_and(j == 0, i < n_tiles))
    def _():
        for s in range(WORD_ROWS):
            sl = slice(s * LANES, (s + 1) * LANES)
            lo_ref[:, sl], hi_ref[:, sl] = _unpack_halves(
                x1s_ref[pl.ds(s, lo_ref.shape[0], stride=WORD_ROWS), :])

    def build():
        gate = _dot_halves(lo_ref[...], hi_ref[...], wg_ref) + bg_ref[...]
        emb = jnp.dot(p_ref[...].astype(BF16), wp_ref[...].astype(BF16),
                      preferred_element_type=F32)
        x1 = (y1_ref[...] - mu_ref[...]) * rs_ref[...] * l1g_ref[...] + l1b_ref[...]
        acc_ref[cur, j] = DEEPNORM_ALPHA * x1 + moe_ref[...] + jax.nn.sigmoid(gate) * emb

    def emit():
        o_ref[...] = ((acc_ref[prv, j] - mu2_ref[prv]) * rs2_ref[prv]
                      * l2g_ref[...] + l2b_ref[...])

    @pl.when(i == 0)
    def _():
        build()

    @pl.when(jnp.logical_and(i > 0, i < n_tiles))
    def _():
        emit()
        build()

    @pl.when(i == n_tiles)
    def _():
        emit()

    @pl.when(jnp.logical_and(j == n_chunks - 1, i < n_tiles))
    def _():
        mu2_ref[cur], rs2_ref[cur] = _row_stats(acc_ref.at[cur], n_chunks, n_chunks * tn)


def _final_stage(x1s, w_pg_b, b_pg, p2, w_pp, y1, mu1, rs1, ln1_g, ln1_b,
                 moe, ln2_g, ln2_b):
    m, d = y1.shape
    tm, tn = TM_PROJ, TN_PROJ
    ple = p2.shape[1]
    n_chunks = d // tn

    n_tiles = m // tm

    def built(i):
        return jnp.minimum(i, n_tiles - 1)

    def emitted(i):
        return jnp.maximum(i - 1, 0)

    return pl.pallas_call(
        _final_kernel,
        out_shape=jax.ShapeDtypeStruct((m, d), F32),
        grid=(n_tiles + 1, n_chunks),
        in_specs=[
            pl.BlockSpec((tm * WORD_ROWS, LANES), lambda i, j: (built(i), 0),
                         pipeline_mode=pl.Buffered(1)),
            pl.BlockSpec((d, tn), lambda i, j: (0, j)),
            pl.BlockSpec((1, tn), lambda i, j: (0, j)),
            pl.BlockSpec((tm, ple), lambda i, j: (built(i), 0)),
            pl.BlockSpec((ple, tn), lambda i, j: (0, j)),
            pl.BlockSpec((tm, tn), lambda i, j: (built(i), j)),
            pl.BlockSpec((tm, 1), lambda i, j: (built(i), 0)),
            pl.BlockSpec((tm, 1), lambda i, j: (built(i), 0)),
            pl.BlockSpec((1, tn), lambda i, j: (0, j)),
            pl.BlockSpec((1, tn), lambda i, j: (0, j)),
            pl.BlockSpec((tm, tn), lambda i, j: (built(i), j)),
            pl.BlockSpec((1, tn), lambda i, j: (0, j)),
            pl.BlockSpec((1, tn), lambda i, j: (0, j)),
        ],
        out_specs=pl.BlockSpec((tm, tn), lambda i, j: (emitted(i), jnp.where(i == 0, 0, j))),
        scratch_shapes=[pltpu.VMEM((tm, d // 2), BF16), pltpu.VMEM((tm, d // 2), BF16),
                        pltpu.VMEM((2, n_chunks, tm, tn), F32),
                        pltpu.VMEM((2, tm, 1), F32), pltpu.VMEM((2, tm, 1), F32)],
        compiler_params=_cparams(("arbitrary", "arbitrary")),
        name="ple_moe_ln2",
    )(x1s, w_pg_b, b_pg, p2, w_pp, y1, mu1, rs1, ln1_g, ln1_b, moe, ln2_g, ln2_b)


def _routing_tables(ids_t, counts_row):
    counts = counts_row[EXPERT_LANE0:EXPERT_LANE0 + N_EXPERTS]
    tiles = (counts + TM_MOE - 1) // TM_MOE
    tile_end = jnp.cumsum(tiles)
    offsets = (tile_end - tiles) * TM_MOE
    n_used = tile_end[-1:]
    n_tokens = ids_t.shape[1]
    experts = ids_t[0:EXPERT_TOP_K]
    hit = experts[None] == jnp.arange(N_EXPERTS, dtype=I32)[:, None, None]
    first_slot = jnp.sum(jnp.where(hit, offsets[:, None, None], 0), axis=0)
    slots = (first_slot + ids_t[EXPERT_TOP_K:2 * EXPERT_TOP_K]).reshape(-1)
    n_tiles = (EXPERT_TOP_K * n_tokens) // TM_MOE + N_EXPERTS
    tile_ids = jnp.minimum(jnp.arange(n_tiles, dtype=I32), n_used - 1)
    tile_expert = jnp.sum(tile_end[None, :] <= tile_ids[:, None], axis=1).astype(I32)
    slab_of_slot = jnp.zeros((n_tiles * TM_MOE,), I32).at[slots].set(
        jnp.arange(slots.size, dtype=I32) % n_tokens * WORD_ROWS, unique_indices=True)
    eid = jnp.arange(N_EXPERTS, dtype=I32)
    later_owner = (eid[None, :] > eid[:, None]) & (tiles[None, :] > 0)
    next_expert = jnp.min(jnp.where(later_owner, eid[None, :], N_EXPERTS), axis=1).astype(I32)
    stage_slot = ((jnp.cumsum(tiles > 0) - 1) % 2).astype(I32)
    return (slots.astype(I32), slab_of_slot, tile_expert, n_used.astype(I32),
            next_expert, stage_slot)


def kernel(x, p, positions, w_in, b_in, sinks, g_norm_a, g_norm_b, w_out, b_out,
           ln1_g, ln1_b, w_group, b_group, w_er, b_er, w_gate, w_up, w_down,
           w_ple_gate, b_ple_gate, w_ple_proj, ln2_g, ln2_b):
    batch, seq, d = x.shape
    m = batch * seq
    row = lambda v: v.reshape(1, -1)
    x2 = x.reshape(m, d)
    for i in range(DEPTH):
        proj = _in_projection(x2, w_in[i].astype(BF16), row(b_in[i]))
        inv_freq = ROPE_THETA ** (-jnp.arange(0, SWA_HEAD_DIM, 2, dtype=F32) / SWA_HEAD_DIM)
        inv_freq = jnp.tile(inv_freq, LANES // inv_freq.shape[0]).reshape(1, LANES)
        o_a = _swa_attention(proj, positions.reshape(m, 1), inv_freq, sinks[i], batch, seq)
        o_b = _sb_attention(proj, batch, seq)
        y1, mu1, rs1, x1s = _out_projection(
            o_a, o_b, row(g_norm_a[i]), row(g_norm_b[i]), w_out[i].astype(BF16), x2,
            row(b_out[i]), row(ln1_g[i]), row(ln1_b[i]))
        pad = ROUTER_COLS - N_GROUPS - N_EXPERTS
        w_r = jnp.concatenate(
            [w_group[i], w_er[i].transpose(1, 0, 2).reshape(d, N_EXPERTS),
             jnp.zeros((d, pad), F32)], axis=1).astype(BF16)
        b_r = jnp.concatenate([b_group[i], b_er[i].reshape(-1), jnp.zeros((pad,), F32)])
        ids, wts, counts = _router(x1s, w_r, row(b_r))
        (slots, slab_of_slot, tile_expert, n_used, next_expert,
         stage_slot) = _routing_tables(ids, counts[0])
        act = _grouped_gate_up(tile_expert, n_used, slab_of_slot, next_expert, stage_slot,
                               x1s, w_gate[i], w_up[i])
        y_sorted = _grouped_down(tile_expert, n_used, next_expert, stage_slot, act, w_down[i])
        moe = _combine_experts(slots, y_sorted, wts)
        x2 = _final_stage(x1s, w_ple_gate[i].astype(BF16), row(b_ple_gate[i]),
                          p[i].reshape(m, PLE_DIM), w_ple_proj[i], y1, mu1, rs1,
                          row(ln1_g[i]), row(ln1_b[i]), moe, row(ln2_g[i]), row(ln2_b[i]))
    return x2.reshape(batch, seq, d)
```

```python
import math

import jax
import jax.numpy as jnp
from jax import lax
from jax.experimental import pallas as pl
from jax.experimental.pallas import tpu as pltpu

F32 = jnp.float32
BF16 = jnp.bfloat16
I32 = jnp.int32
U32 = jnp.uint32

D_MODEL = 4096
PLE_DIM = 256
BLOCK = 128
ROPE_THETA = 10000.0
LN_EPS = 1e-5
RMS_EPS = 1e-6
NEG_INF = -1e30

SWA_HEAD_DIM = 64
SWA_WIDTH = D_MODEL // 2
SWA_HEADS = SWA_WIDTH // SWA_HEAD_DIM
SWA_KV_HEADS = SWA_HEADS // 8
SWA_GROUP = SWA_HEADS // SWA_KV_HEADS
SWA_KV_WIDTH = SWA_KV_HEADS * SWA_HEAD_DIM

SB_HEAD_DIM = 128
SB_WIDTH = D_MODEL - SWA_WIDTH
SB_HEADS = SB_WIDTH // SB_HEAD_DIM

IN_WIDTH = SWA_WIDTH + 2 * SWA_KV_WIDTH + 3 * SB_WIDTH
A_WIDTH = SWA_WIDTH + 2 * SWA_KV_WIDTH
Q_B_COL = 0
K_B_COL = Q_B_COL + SB_WIDTH
V_B_COL = K_B_COL + SB_WIDTH
Q_A_COL = V_B_COL + SB_WIDTH
K_A_COL = Q_A_COL + SWA_WIDTH
V_A_COL = K_A_COL + SWA_KV_WIDTH

N_GROUPS = 4
EXPERTS_PER_GROUP = 8
N_EXPERTS = N_GROUPS * EXPERTS_PER_GROUP
EXPERT_TOP_K = 2
DEPTH = 1
DEEPNORM_ALPHA = (2.0 * DEPTH) ** 0.25

LANES = 128
SUBLANES = 8
VMEM_LIMIT_BYTES = 56 * 1024 * 1024

EXP_ZERO_LOG = -126 * math.log(2.0) - 2.0

TM_IN = 1024
TM_PROJ = 512
TN_PROJ = 512
TM_MOE = 256
TM_COMBINE = 256
BUILD_ROWS = 128
SB_HEADS_PER_STEP = 16
SUFFIX_TERMS = 2
ROW_CHUNK = 16
WEIGHT_DMA_PRIORITY = 1


def _cparams(sem):
    return pltpu.CompilerParams(dimension_semantics=sem,
                                vmem_limit_bytes=VMEM_LIMIT_BYTES)


def _pack_halves(lo, hi):
    return lax.bitcast_convert_type(
        pltpu.pack_elementwise([lo, hi], packed_dtype=BF16), U32)


def _unpack_halves(words):
    return tuple(
        pltpu.unpack_elementwise(words, index=k, packed_dtype=BF16,
                                 unpacked_dtype=F32).astype(BF16) for k in (0, 1))


def _dot_halves(lo, hi, w_ref):
    half = lo.shape[1]
    return (jnp.dot(lo, w_ref[:half], preferred_element_type=F32)
            + jnp.dot(hi, w_ref[half:], preferred_element_type=F32))


WORD_ROWS = (D_MODEL // 2) // LANES
SLAB_PAIR = 2 * LANES


def _slab_words(slab_ref, pair, tokens):
    return jnp.concatenate(
        [slab_ref[pl.ds(2 * pair + k, tokens, stride=WORD_ROWS), :] for k in (0, 1)],
        axis=1)


def _dot_slab(slab_ref, tokens, w_refs):
    half = D_MODEL // 2
    outs = [None] * len(w_refs)
    for pair in range(half // SLAB_PAIR):
        lo, hi = _unpack_halves(_slab_words(slab_ref, pair, tokens))
        rows_lo = slice(pair * SLAB_PAIR, (pair + 1) * SLAB_PAIR)
        rows_hi = slice(half + pair * SLAB_PAIR, half + (pair + 1) * SLAB_PAIR)
        for n, w_ref in enumerate(w_refs):
            part = (jnp.dot(lo, w_ref[rows_lo], preferred_element_type=F32)
                    + jnp.dot(hi, w_ref[rows_hi], preferred_element_type=F32))
            outs[n] = part if outs[n] is None else outs[n] + part
    return outs


def _dot_nt(a, b):
    return lax.dot_general(a, b, (((1,), (1,)), ((), ())),
                           preferred_element_type=F32)


def _inproj_kernel(x_ref, w_ref, b_ref, o_ref, xb_ref):
    @pl.when(pl.program_id(1) == 0)
    def _():
        xb_ref[...] = x_ref[...].astype(BF16)

    acc = jnp.dot(xb_ref[...], w_ref[...], preferred_element_type=F32)
    o_ref[...] = (acc + b_ref[...]).astype(o_ref.dtype)


def _in_projection(x2, w_b, b):
    m, k = x2.shape
    n = w_b.shape[1]
    n_blocks = n // TN_PROJ
    assert A_WIDTH % TN_PROJ == 0
    a_blocks = A_WIDTH // TN_PROJ

    def src(j):
        return lax.rem(j + a_blocks, n_blocks)

    return pl.pallas_call(
        _inproj_kernel,
        out_shape=jax.ShapeDtypeStruct((m, n), BF16),
        grid=(m // TM_IN, n_blocks),
        in_specs=[
            pl.BlockSpec((TM_IN, k), lambda i, j: (i, 0)),
            pl.BlockSpec((k, TN_PROJ), lambda i, j: (0, src(j))),
            pl.BlockSpec((1, TN_PROJ), lambda i, j: (0, src(j))),
        ],
        out_specs=pl.BlockSpec((TM_IN, TN_PROJ), lambda i, j: (i, j)),
        scratch_shapes=[pltpu.VMEM((TM_IN, k), BF16)],
        compiler_params=_cparams(("arbitrary", "arbitrary")),
        name="in_projection",
    )(x2, w_b, b)


def _swa_kernel(sinks_ref, q_ref, kc_ref, vc_ref, vp_ref,
                posc_ref, invf_ref, o_ref, kprev_ref):
    n = pl.program_id(1)

    @pl.when(n == 0)
    def _():
        kprev_ref[...] = jnp.zeros_like(kprev_ref)
    lane = lax.broadcasted_iota(I32, (1, LANES), 1)
    first_half = (lane % SWA_HEAD_DIM) < (SWA_HEAD_DIM // 2)

    def tables(pos_ref):
        ang = pos_ref[...].astype(F32) * invf_ref[...]
        sin = jnp.sin(ang)
        return jnp.cos(ang), jnp.where(first_half, -sin, sin)

    def rope(x, cos, sin_signed):
        partner = jnp.where(first_half,
                            pltpu.roll(x, LANES - SWA_HEAD_DIM // 2, 1),
                            pltpu.roll(x, SWA_HEAD_DIM // 2, 1))
        return x * cos + partner * sin_signed

    cos_c, sin_c = tables(posc_ref)

    def rope_block(ref, cos, sin_signed):
        width = ref.shape[1]
        return [rope(ref[:, c * LANES:(c + 1) * LANES].astype(F32), cos,
                     sin_signed).astype(BF16) for c in range(width // LANES)]

    q_chunks = rope_block(q_ref, cos_c, sin_c)
    k_cur = rope_block(kc_ref, cos_c, sin_c)
    k_chunks = [jnp.concatenate([kprev_ref[:, c * LANES:(c + 1) * LANES], k_c], axis=0)
                for c, k_c in enumerate(k_cur)]
    for c, k_c in enumerate(k_cur):
        kprev_ref[:, c * LANES:(c + 1) * LANES] = k_c
    v_all = jnp.concatenate([vp_ref[...], vc_ref[...]], axis=0)

    qi = lax.broadcasted_iota(I32, (BLOCK, 2 * BLOCK), 0)
    kj = lax.broadcasted_iota(I32, (BLOCK, 2 * BLOCK), 1)
    rel = qi - (kj - BLOCK)
    valid = (rel >= 0) & (rel < BLOCK) & ((kj >= BLOCK) | (n > 0))

    def head_slice(chunks, head):
        half = head % 2
        return chunks[head // 2][:, half * SWA_HEAD_DIM:(half + 1) * SWA_HEAD_DIM]

    scale = 1.0 / math.sqrt(SWA_HEAD_DIM)
    scores = []
    for h in range(SWA_KV_HEADS):
        k_h = head_slice(k_chunks, h)
        q_h = jnp.concatenate(
            [head_slice(q_chunks, h * SWA_GROUP + g) for g in range(SWA_GROUP)],
            axis=0)
        scores.append(_dot_nt(q_h, k_h) * scale)
    weights, sink_terms = [], []
    for h in range(SWA_KV_HEADS):
        weights_h = []
        for g in range(SWA_GROUP):
            sink = sinks_ref[h * SWA_GROUP + g]
            s_g = jnp.where(valid, scores[h][g * BLOCK:(g + 1) * BLOCK], NEG_INF)
            m = jnp.maximum(jnp.max(s_g, axis=-1, keepdims=True), sink)
            weights_h.append(jnp.exp(s_g - m).astype(BF16))
            sink_terms.append(jnp.exp(sink - m))
        weights.append(jnp.concatenate(weights_h, axis=0))
    ones = jnp.ones((2 * BLOCK, SWA_HEAD_DIM), BF16)
    for h in range(SWA_KV_HEADS):
        v_h = v_all[:, h * SWA_HEAD_DIM:(h + 1) * SWA_HEAD_DIM]
        both = jnp.dot(weights[h], jnp.concatenate([v_h, ones], axis=1),
                       preferred_element_type=F32)
        for g in range(SWA_GROUP):
            rows = slice(g * BLOCK, (g + 1) * BLOCK)
            den = both[rows, SWA_HEAD_DIM:SWA_HEAD_DIM + 1] + sink_terms[h * SWA_GROUP + g]
            col = (h * SWA_GROUP + g) * SWA_HEAD_DIM
            o_ref[:, col:col + SWA_HEAD_DIM] = both[rows, :SWA_HEAD_DIM] / den


def _swa_attention(proj, pos2, inv_freq, sinks, batch, seq):
    nb = seq // BLOCK
    assert Q_A_COL % SWA_WIDTH == 0 and K_A_COL % SWA_KV_WIDTH == 0
    assert V_A_COL % SWA_KV_WIDTH == 0
    qcol = Q_A_COL // SWA_WIDTH
    kcol = K_A_COL // SWA_KV_WIDTH
    vcol = V_A_COL // SWA_KV_WIDTH

    def cur(b, n):
        return b * nb + n

    def prev(b, n):
        return b * nb + jnp.maximum(n - 1, 0)

    return pl.pallas_call(
        _swa_kernel,
        out_shape=jax.ShapeDtypeStruct((batch * seq, SWA_WIDTH), F32),
        grid=(batch, nb),
        in_specs=[
            pl.BlockSpec(memory_space=pltpu.SMEM),
            pl.BlockSpec((BLOCK, SWA_WIDTH), lambda b, n: (cur(b, n), qcol)),
            pl.BlockSpec((BLOCK, SWA_KV_WIDTH), lambda b, n: (cur(b, n), kcol)),
            pl.BlockSpec((BLOCK, SWA_KV_WIDTH), lambda b, n: (cur(b, n), vcol)),
            pl.BlockSpec((BLOCK, SWA_KV_WIDTH), lambda b, n: (prev(b, n), vcol)),
            pl.BlockSpec((BLOCK, 1), lambda b, n: (cur(b, n), 0)),
            pl.BlockSpec((1, LANES), lambda b, n: (0, 0)),
        ],
        out_specs=pl.BlockSpec((BLOCK, SWA_WIDTH), lambda b, n: (cur(b, n), 0)),
        scratch_shapes=[pltpu.VMEM((BLOCK, SWA_KV_WIDTH), BF16)],
        compiler_params=_cparams(("arbitrary", "arbitrary")),
        name="swa_attention",
    )(sinks, proj, proj, proj, proj, pos2, inv_freq)


def _sb_kernel(q_ref, k_ref, v_ref, o_ref):
    n = pl.program_id(2)
    heads = [slice(h * SB_HEAD_DIM, (h + 1) * SB_HEAD_DIM)
             for h in range(SB_HEADS_PER_STEP)]
    rows = SB_HEADS_PER_STEP * BLOCK
    scale = 1.0 / math.sqrt(SB_HEAD_DIM)
    key_j = lax.broadcasted_iota(I32, (BLOCK, 2 * BLOCK), 0)
    out_c = lax.broadcasted_iota(I32, (BLOCK, 2 * BLOCK), 1)
    later_total = ((key_j > out_c) | (out_c >= BLOCK)).astype(BF16)
    q_row = lax.broadcasted_iota(I32, (rows, BLOCK), 0) % BLOCK
    k_col = lax.broadcasted_iota(I32, (rows, BLOCK), 1)
    causal = k_col < q_row

    def block(kb, carry, acc, mask):
        start = pl.multiple_of(kb * BLOCK, BLOCK)
        z = jnp.concatenate(
            [_dot_nt(q_ref[:, hd], k_ref[pl.ds(start, BLOCK), hd]) for hd in heads],
            axis=0) * scale
        t = jnp.log(1.0 + jnp.exp(-jnp.abs(z)))
        log_not = jnp.minimum(-z, 0.0) - t
        log_beta = jnp.minimum(z, 0.0) - t
        if mask is not None:
            log_not = jnp.where(mask, log_not, 0.0)
        terms, rest = [], log_not
        for _ in range(SUFFIX_TERMS):
            terms.append(rest.astype(BF16))
            rest = rest - terms[-1].astype(F32)
        parts = jnp.dot(jnp.concatenate(terms, axis=0), later_total,
                        preferred_element_type=F32)
        sums = parts[:rows]
        for t in range(1, SUFFIX_TERMS):
            sums = sums + parts[t * rows:(t + 1) * rows]
        a = jnp.exp(log_beta + sums[:, :BLOCK] + carry)
        if mask is not None:
            a = jnp.where(mask, a, 0.0)
        a = a.astype(BF16)
        pv = jnp.concatenate(
            [jnp.dot(a[h * BLOCK:(h + 1) * BLOCK], v_ref[pl.ds(start, BLOCK), hd],
                     preferred_element_type=F32) for h, hd in enumerate(heads)], axis=0)
        return carry + sums[:, BLOCK:], acc + pv

    zeros = jnp.zeros((rows, BLOCK), F32)
    carry, acc = block(n, zeros, zeros, causal)

    def cond(state):
        kb, carry, _ = state
        return jnp.logical_and(kb >= 0, jnp.max(carry) > EXP_ZERO_LOG)

    def body(state):
        kb, carry, acc = state
        carry, acc = block(kb, carry, acc, None)
        return kb - 1, carry, acc

    _, _, acc = lax.while_loop(cond, body, (n - 1, carry, acc))
    for h, hd in enumerate(heads):
        o_ref[:, hd] = acc[h * BLOCK:(h + 1) * BLOCK]


def _sb_attention(proj, batch, seq):
    nb = seq // BLOCK
    width = SB_HEADS_PER_STEP * SB_HEAD_DIM
    assert Q_B_COL % width == 0 and K_B_COL % width == 0 and V_B_COL % width == 0
    qcol = Q_B_COL // width
    kcol = K_B_COL // width
    vcol = V_B_COL // width
    return pl.pallas_call(
        _sb_kernel,
        out_shape=jax.ShapeDtypeStruct((batch * seq, SB_WIDTH), F32),
        grid=(batch, SB_HEADS // SB_HEADS_PER_STEP, nb),
        in_specs=[
            pl.BlockSpec((BLOCK, width), lambda b, h, n: (b * nb + n, qcol + h)),
            pl.BlockSpec((seq, width), lambda b, h, n: (b, kcol + h),
                         pipeline_mode=pl.Buffered(1)),
            pl.BlockSpec((seq, width), lambda b, h, n: (b, vcol + h),
                         pipeline_mode=pl.Buffered(1)),
        ],
        out_specs=pl.BlockSpec((BLOCK, width), lambda b, h, n: (b * nb + n, h)),
        compiler_params=_cparams(("arbitrary", "arbitrary", "arbitrary")),
        name="sb_attention",
    )(proj, proj, proj)


def _row_stats(chunks_ref, n_chunks, width):
    total = chunks_ref[0].sum(axis=-1, keepdims=True)
    for c in range(1, n_chunks):
        total = total + chunks_ref[c].sum(axis=-1, keepdims=True)
    mu = total / width
    sq = jnp.square(chunks_ref[0] - mu).sum(axis=-1, keepdims=True)
    for c in range(1, n_chunks):
        sq = sq + jnp.square(chunks_ref[c] - mu).sum(axis=-1, keepdims=True)
    return mu, lax.rsqrt(sq / width + LN_EPS)


def _outproj_kernel(oa_ref, ob_ref, ga_ref, gb_ref, wa_ref, wb_ref, x_ref, bo_ref,
                    lg_ref, lb_ref, y_ref, mu_ref, rs_ref, x1s_ref,
                    ma_ref, mb_ref, acc_ref):
    j = pl.program_id(1)
    n_chunks = acc_ref.shape[0]
    tn = acc_ref.shape[2]

    @pl.when(j == 0)
    def _():
        for o_ref, g_ref, m_ref in ((oa_ref, ga_ref, ma_ref), (ob_ref, gb_ref, mb_ref)):
            width = o_ref.shape[1]
            cols = [slice(c * tn, (c + 1) * tn) for c in range(width // tn)]
            sq = sum(jnp.square(o_ref[:, sl]).sum(axis=-1, keepdims=True) for sl in cols)
            r = lax.rsqrt(sq / width + RMS_EPS)
            for sl in cols:
                m_ref[:, sl] = (o_ref[:, sl] * r * g_ref[:, sl]).astype(BF16)

    mix = (jnp.dot(ma_ref[...], wa_ref[...], preferred_element_type=F32)
           + jnp.dot(mb_ref[...], wb_ref[...], preferred_element_type=F32))
    y = DEEPNORM_ALPHA * x_ref[...] + (mix + bo_ref[...])
    y_ref[...] = y
    acc_ref[j] = y

    @pl.when(j == n_chunks - 1)
    def _():
        mu, rs = _row_stats(acc_ref, n_chunks, n_chunks * tn)
        mu_ref[...] = mu
        rs_ref[...] = rs

        def normed(c):
            sl = slice(c * tn, (c + 1) * tn)
            return (acc_ref[c] - mu) * rs * lg_ref[:, sl] + lb_ref[:, sl]

        tm = acc_ref.shape[1]
        for c in range(n_chunks // 2):
            words = _pack_halves(normed(c), normed(c + n_chunks // 2))
            for q in range(tn // LANES):
                x1s_ref[pl.ds(c * (tn // LANES) + q, tm, stride=WORD_ROWS), :] = (
                    words[:, q * LANES:(q + 1) * LANES])


def _out_projection(o_a, o_b, g_a, g_b, w_b, x2, b_out, ln_g, ln_b):
    m, d = x2.shape
    tm, tn = TM_PROJ, TN_PROJ
    half = o_a.shape[1]
    once = dict(pipeline_mode=pl.Buffered(1))
    return pl.pallas_call(
        _outproj_kernel,
        out_shape=(jax.ShapeDtypeStruct((m, d), F32),
                   jax.ShapeDtypeStruct((m, 1), F32),
                   jax.ShapeDtypeStruct((m, 1), F32),
                   jax.ShapeDtypeStruct((m * WORD_ROWS, LANES), U32)),
        grid=(m // tm, d // tn),
        in_specs=[
            pl.BlockSpec((tm, half), lambda i, j: (i, 0), **once),
            pl.BlockSpec((tm, half), lambda i, j: (i, 0), **once),
            pl.BlockSpec((1, half), lambda i, j: (0, 0)),
            pl.BlockSpec((1, half), lambda i, j: (0, 0)),
            pl.BlockSpec((half, tn), lambda i, j: (0, j)),
            pl.BlockSpec((half, tn), lambda i, j: (1, j)),
            pl.BlockSpec((tm, tn), lambda i, j: (i, j)),
            pl.BlockSpec((1, tn), lambda i, j: (0, j)),
            pl.BlockSpec((1, d), lambda i, j: (0, 0)),
            pl.BlockSpec((1, d), lambda i, j: (0, 0)),
        ],
        out_specs=(pl.BlockSpec((tm, tn), lambda i, j: (i, j)),
                   pl.BlockSpec((tm, 1), lambda i, j: (i, 0)),
                   pl.BlockSpec((tm, 1), lambda i, j: (i, 0)),
                   pl.BlockSpec((tm * WORD_ROWS, LANES), lambda i, j: (i, 0))),
        scratch_shapes=[pltpu.VMEM((tm, half), BF16), pltpu.VMEM((tm, half), BF16),
                        pltpu.VMEM((d // tn, tm, tn), F32)],
        compiler_params=_cparams(("arbitrary", "arbitrary")),
        name="out_projection_ln1",
    )(o_a, o_b, g_a, g_b, w_b, w_b, x2, b_out, ln_g, ln_b)


ROUTER_COLS = LANES
EXPERT_LANE0 = N_GROUPS


def _router_kernel(x_ref, w_ref, b_ref, ids_ref, wts_ref, cnt_ref, carry_ref):
    i = pl.program_id(0)
    tm = wts_ref.shape[0]

    @pl.when(i == 0)
    def _():
        carry_ref[...] = jnp.zeros_like(carry_ref)

    logits = _dot_slab(x_ref, tm, [w_ref])[0] + b_ref[...]
    lane = lax.broadcasted_iota(I32, (tm, ROUTER_COLS), 1)
    big = jnp.int32(ROUTER_COLS)

    def first_argmax(vals):
        top = jnp.max(vals, axis=-1, keepdims=True)
        idx = jnp.min(jnp.where(vals == top, lane, big), axis=-1, keepdims=True)
        return top, idx

    is_group = lane < N_GROUPS
    g_logits = jnp.where(is_group, logits, -jnp.inf)
    g_top, g_idx = first_argmax(g_logits)
    g_w = 1.0 / jnp.sum(jnp.exp(g_logits - g_top), axis=-1, keepdims=True)

    first = EXPERT_LANE0 + g_idx * EXPERTS_PER_GROUP
    in_group = (lane >= first) & (lane < first + EXPERTS_PER_GROUP)
    e_logits = jnp.where(in_group, logits, -jnp.inf)
    top1, idx1 = first_argmax(e_logits)
    top2, idx2 = first_argmax(jnp.where(lane == idx1, -jnp.inf, e_logits))
    e2 = jnp.exp(top2 - top1)
    w1 = g_w / (1.0 + e2)
    w2 = g_w * e2 / (1.0 + e2)

    hit1 = lane == idx1
    hit2 = lane == idx2
    onehot = (hit1 | hit2).astype(BF16)
    r = lax.broadcasted_iota(I32, (tm, tm), 0)
    c = lax.broadcasted_iota(I32, (tm, tm), 1)
    before = (c < r).astype(BF16)
    prior = jnp.dot(before, onehot, preferred_element_type=F32) + carry_ref[0:1, :]
    rank1 = jnp.sum(jnp.where(hit1, prior, 0.0), axis=-1, keepdims=True)
    rank2 = jnp.sum(jnp.where(hit2, prior, 0.0), axis=-1, keepdims=True)
    counts = carry_ref[0:1, :] + jnp.sum(onehot.astype(F32), axis=0, keepdims=True)
    carry_ref[...] = jnp.broadcast_to(counts, carry_ref.shape)
    cnt_ref[...] = jnp.broadcast_to(counts, cnt_ref.shape).astype(I32)

    ids = jnp.where(lane == 0, idx1 - EXPERT_LANE0,
          jnp.where(lane == 1, idx2 - EXPERT_LANE0,
          jnp.where(lane == 2, rank1.astype(I32),
          jnp.where(lane == 3, rank2.astype(I32), 0))))
    ids_ref[...] = ids.T[:ids_ref.shape[0]]
    wts_ref[...] = jnp.where(lane == 0, w1, jnp.where(lane == 1, w2, 0.0))


def _router(x1s, w_r, b_r):
    m = x1s.shape[0] // WORD_ROWS
    tm = TM_PROJ
    return pl.pallas_call(
        _router_kernel,
        out_shape=(jax.ShapeDtypeStruct((SUBLANES, m), I32),
                   jax.ShapeDtypeStruct((m, ROUTER_COLS), F32),
                   jax.ShapeDtypeStruct((SUBLANES, ROUTER_COLS), I32)),
        grid=(m // tm,),
        in_specs=[
            pl.BlockSpec((tm * WORD_ROWS, LANES), lambda i: (i, 0)),
            pl.BlockSpec((D_MODEL, ROUTER_COLS), lambda i: (0, 0)),
            pl.BlockSpec((1, ROUTER_COLS), lambda i: (0, 0)),
        ],
        out_specs=(pl.BlockSpec((SUBLANES, tm), lambda i: (0, i)),
                   pl.BlockSpec((tm, ROUTER_COLS), lambda i: (i, 0)),
                   pl.BlockSpec((SUBLANES, ROUTER_COLS), lambda i: (0, 0))),
        scratch_shapes=[pltpu.VMEM((SUBLANES, ROUTER_COLS), F32)],
        compiler_params=_cparams(("arbitrary",)),
        name="router",
    )(x1s, w_r, b_r)


def _start_row_gather(idx_ref, first, n_items, span, src_ref, dst_ref, sem):
    def body(c, _):
        for u in range(ROW_CHUNK):
            r = c * ROW_CHUNK + u
            src_row = pl.multiple_of(idx_ref[first + r], span)
            dst_row = pl.multiple_of(r * span, span)
            pltpu.make_async_copy(src_ref.at[pl.ds(src_row, span)],
                                  dst_ref.at[pl.ds(dst_row, span)], sem).start()
        return 0

    lax.fori_loop(0, n_items // ROW_CHUNK, body, 0)


def _wait_row_gather(n_items, span, src_ref, dst_ref, sem):
    def body(c, _):
        for _u in range(ROW_CHUNK):
            pltpu.make_async_copy(src_ref.at[pl.ds(0, span)], dst_ref.at[pl.ds(0, span)],
                                  sem).wait()
        return 0

    lax.fori_loop(0, n_items // ROW_CHUNK, body, 0)


def _expert_changed(te_ref, i):
    return jnp.logical_or(i == 0, te_ref[i] != te_ref[jnp.maximum(i - 1, 0)])


def _tile_row(i, nu):
    return jnp.minimum(i, nu[0] - 1)


def _stage_expert_weights(i, te_ref, nxt_ref, ws_ref, w_hbms, stage_ref, bf_refs, sem):
    expert = te_ref[i]
    slot = ws_ref[expert]

    def copies(e, s):
        return [pltpu.make_async_copy(w.at[e], stage_ref.at[s, n], sem.at[s])
                for n, w in enumerate(w_hbms)]

    @pl.when(i == 0)
    def _():
        for c in copies(expert, slot):
            c.start(priority=WEIGHT_DMA_PRIORITY)

    @pl.when(_expert_changed(te_ref, i))
    def _():
        for c in copies(expert, slot):
            c.wait()
        nxt = nxt_ref[expert]

        @pl.when(nxt < N_EXPERTS)
        def _():
            for c in copies(nxt, 1 - slot):
                c.start(priority=WEIGHT_DMA_PRIORITY)

        for n, bf_ref in enumerate(bf_refs):
            bf_ref[...] = stage_ref[slot, n].astype(BF16)


def _gateup_kernel(te_ref, nu_ref, tos_ref, nxt_ref, ws_ref, x_hbm, wg_hbm, wu_hbm, a_ref,
                   xbuf_ref, stage_ref, wgb_ref, wub_ref, sems, wsem):
    i = pl.program_id(0)
    n_used = nu_ref[0]
    slot = lax.rem(i, 2)

    def start(tile, buf):
        _start_row_gather(tos_ref, tile * TM_MOE, TM_MOE, WORD_ROWS, x_hbm,
                          xbuf_ref.at[buf], sems.at[buf])

    @pl.when(i == 0)
    def _():
        start(0, 0)

    @pl.when(i + 1 < n_used)
    def _():
        start(i + 1, 1 - slot)

    _stage_expert_weights(i, te_ref, nxt_ref, ws_ref, [wg_hbm, wu_hbm], stage_ref,
                          [wgb_ref, wub_ref], wsem)

    @pl.when(i < n_used)
    def _():
        _wait_row_gather(TM_MOE, WORD_ROWS, x_hbm, xbuf_ref.at[slot], sems.at[slot])
        gate, up = _dot_slab(xbuf_ref.at[slot], TM_MOE, [wgb_ref, wub_ref])
        a_ref[...] = (gate * jax.nn.sigmoid(gate) * up).astype(a_ref.dtype)

    @pl.when(i >= n_used)
    def _():
        a_ref[...] = jnp.zeros_like(a_ref)


def _grouped_gate_up(tile_expert, n_used, slab_of_slot, next_expert, stage_slot,
                     x1s, w_gate, w_up):
    p_rows = slab_of_slot.shape[0]
    _, d, f = w_gate.shape
    hbm = pl.BlockSpec(memory_space=pl.ANY)
    return pl.pallas_call(
        _gateup_kernel,
        out_shape=jax.ShapeDtypeStruct((p_rows, f), BF16),
        grid_spec=pltpu.PrefetchScalarGridSpec(
            num_scalar_prefetch=5,
            grid=(p_rows // TM_MOE,),
            in_specs=[hbm, hbm, hbm],
            out_specs=pl.BlockSpec((TM_MOE, f), lambda i, *_: (i, 0)),
            scratch_shapes=[pltpu.VMEM((2, TM_MOE * WORD_ROWS, LANES), U32),
                            pltpu.VMEM((2, 2, d, f), F32),
                            pltpu.VMEM((d, f), BF16), pltpu.VMEM((d, f), BF16),
                            pltpu.SemaphoreType.DMA((2,)), pltpu.SemaphoreType.DMA((2,))],
        ),
        compiler_params=_cparams(("arbitrary",)),
        name="moe_gate_up",
    )(tile_expert, n_used, slab_of_slot, next_expert, stage_slot, x1s, w_gate, w_up)


def _down_kernel(te_ref, nu_ref, nxt_ref, ws_ref, a_ref, wd_hbm, y_ref,
                 stage_ref, wdb_ref, wsem):
    i = pl.program_id(0)
    _stage_expert_weights(i, te_ref, nxt_ref, ws_ref, [wd_hbm], stage_ref, [wdb_ref], wsem)

    @pl.when(i < nu_ref[0])
    def _():
        y_ref[:, 0, :] = jnp.dot(a_ref[...], wdb_ref[...], preferred_element_type=F32)

    @pl.when(i >= nu_ref[0])
    def _():
        y_ref[...] = jnp.zeros_like(y_ref)


def _grouped_down(tile_expert, n_used, next_expert, stage_slot, act, w_down):
    p_rows, f = act.shape
    d = w_down.shape[2]
    return pl.pallas_call(
        _down_kernel,
        out_shape=jax.ShapeDtypeStruct((p_rows, 1, d), F32),
        grid_spec=pltpu.PrefetchScalarGridSpec(
            num_scalar_prefetch=4,
            grid=(p_rows // TM_MOE,),
            in_specs=[
                pl.BlockSpec((TM_MOE, f), lambda i, te, nu, *_: (_tile_row(i, nu), 0)),
                pl.BlockSpec(memory_space=pl.ANY),
            ],
            out_specs=pl.BlockSpec((TM_MOE, 1, d), lambda i, *_: (i, 0, 0)),
            scratch_shapes=[pltpu.VMEM((2, 1, f, d), F32), pltpu.VMEM((f, d), BF16),
                            pltpu.SemaphoreType.DMA((2,))],
        ),
        compiler_params=_cparams(("arbitrary",)),
        name="moe_down",
    )(tile_expert, n_used, next_expert, stage_slot, act, w_down)


def _combine_kernel(slot_ref, y_hbm, wts_ref, o_ref, buf_ref, sems):
    i = pl.program_id(0)
    tm = o_ref.shape[0]
    n_tokens = slot_ref.shape[0] // EXPERT_TOP_K
    slot = lax.rem(i, 2)

    def start(tile, buf):
        for k in range(EXPERT_TOP_K):
            _start_row_gather(slot_ref, k * n_tokens + tile * tm, tm, 1, y_hbm,
                              buf_ref.at[buf, k], sems.at[buf])

    @pl.when(i == 0)
    def _():
        start(0, 0)

    @pl.when(i + 1 < pl.num_programs(0))
    def _():
        start(i + 1, 1 - slot)

    _wait_row_gather(EXPERT_TOP_K * tm, 1, y_hbm, buf_ref.at[slot, 0], sems.at[slot])
    for c in range(o_ref.shape[1] // TN_PROJ):
        sl = slice(c * TN_PROJ, (c + 1) * TN_PROJ)
        o_ref[:, sl] = sum(wts_ref[:, k:k + 1] * buf_ref[slot, k, :, 0, sl]
                           for k in range(EXPERT_TOP_K))


def _combine_experts(slot_kt, y_sorted, wts):
    d = y_sorted.shape[2]
    m = slot_kt.shape[0] // EXPERT_TOP_K
    tm = TM_COMBINE
    return pl.pallas_call(
        _combine_kernel,
        out_shape=jax.ShapeDtypeStruct((m, d), F32),
        grid_spec=pltpu.PrefetchScalarGridSpec(
            num_scalar_prefetch=1,
            grid=(m // tm,),
            in_specs=[
                pl.BlockSpec(memory_space=pl.ANY),
                pl.BlockSpec((tm, ROUTER_COLS), lambda i, sl: (i, 0)),
            ],
            out_specs=pl.BlockSpec((tm, d), lambda i, sl: (i, 0)),
            scratch_shapes=[pltpu.VMEM((2, EXPERT_TOP_K, tm, 1, d), F32),
                            pltpu.SemaphoreType.DMA((2,))],
        ),
        compiler_params=_cparams(("arbitrary",)),
        name="moe_combine",
    )(slot_kt, y_sorted, wts)


def _final_kernel(x1s_ref, wg_ref, bg_ref, p_ref, wp_ref, y1_ref, mu_ref, rs_ref,
                  l1g_ref, l1b_ref, moe_ref, l2g_ref, l2b_ref,
                  o_ref, lo_ref, hi_ref, acc_ref, mu2_ref, rs2_ref):
    i = pl.program_id(0)
    j = pl.program_id(1)
    n_tiles = pl.num_programs(0) - 1
    n_chunks = acc_ref.shape[1]
    tn = acc_ref.shape[3]
    cur = lax.rem(i, 2)
    prv = 1 - cur

    @pl.when(jnp.logical_and(j == 0, i < n_tiles))
    def _():
        for s in range(WORD_ROWS):
            sl = slice(s * LANES, (s + 1) * LANES)
            lo_ref[:, sl], hi_ref[:, sl] = _unpack_halves(
                x1s_ref[pl.ds(s, lo_ref.shape[0], stride=WORD_ROWS), :])

    def build():
        tm = lo_ref.shape[0]
        wp = wp_ref[...].astype(BF16)
        for r in range(tm // BUILD_ROWS):
            rs = slice(r * BUILD_ROWS, (r + 1) * BUILD_ROWS)
            gate = _dot_halves(lo_ref[rs, :], hi_ref[rs, :], wg_ref) + bg_ref[...]
            emb = jnp.dot(p_ref[rs, :].astype(BF16), wp, preferred_element_type=F32)
            x1 = ((y1_ref[rs, :] - mu_ref[rs, :]) * rs_ref[rs, :] * l1g_ref[...]
                  + l1b_ref[...])
            acc_ref[cur, j, rs, :] = (DEEPNORM_ALPHA * x1 + moe_ref[rs, :]
                                      + jax.nn.sigmoid(gate) * emb)

    def emit():
        o_ref[...] = ((acc_ref[prv, j] - mu2_ref[prv]) * rs2_ref[prv]
                      * l2g_ref[...] + l2b_ref[...])

    @pl.when(i == 0)
    def _():
        build()

    @pl.when(jnp.logical_and(i > 0, i < n_tiles))
    def _():
        emit()
        build()

    @pl.when(i == n_tiles)
    def _():
        emit()

    @pl.when(jnp.logical_and(j == n_chunks - 1, i < n_tiles))
    def _():
        mu2_ref[cur], rs2_ref[cur] = _row_stats(acc_ref.at[cur], n_chunks, n_chunks * tn)


def _final_stage(x1s, w_pg_b, b_pg, p2, w_pp, y1, mu1, rs1, ln1_g, ln1_b,
                 moe, ln2_g, ln2_b):
    m, d = y1.shape
    tm, tn = TM_PROJ, TN_PROJ
    ple = p2.shape[1]
    n_chunks = d // tn

    n_tiles = m // tm

    def built(i):
        return jnp.minimum(i, n_tiles - 1)

    def emitted(i):
        return jnp.maximum(i - 1, 0)

    return pl.pallas_call(
        _final_kernel,
        out_shape=jax.ShapeDtypeStruct((m, d), F32),
        grid=(n_tiles + 1, n_chunks),
        in_specs=[
            pl.BlockSpec((tm * WORD_ROWS, LANES), lambda i, j: (built(i), 0),
                         pipeline_mode=pl.Buffered(1)),
            pl.BlockSpec((d, tn), lambda i, j: (0, j)),
            pl.BlockSpec((1, tn), lambda i, j: (0, j)),
            pl.BlockSpec((tm, ple), lambda i, j: (built(i), 0)),
            pl.BlockSpec((ple, tn), lambda i, j: (0, j)),
            pl.BlockSpec((tm, tn), lambda i, j: (built(i), j)),
            pl.BlockSpec((tm, 1), lambda i, j: (built(i), 0)),
            pl.BlockSpec((tm, 1), lambda i, j: (built(i), 0)),
            pl.BlockSpec((1, tn), lambda i, j: (0, j)),
            pl.BlockSpec((1, tn), lambda i, j: (0, j)),
            pl.BlockSpec((tm, tn), lambda i, j: (built(i), j)),
            pl.BlockSpec((1, tn), lambda i, j: (0, j)),
            pl.BlockSpec((1, tn), lambda i, j: (0, j)),
        ],
        out_specs=pl.BlockSpec((tm, tn), lambda i, j: (emitted(i), jnp.where(i == 0, 0, j))),
        scratch_shapes=[pltpu.VMEM((tm, d // 2), BF16), pltpu.VMEM((tm, d // 2), BF16),
                        pltpu.VMEM((2, n_chunks, tm, tn), F32),
                        pltpu.VMEM((2, tm, 1), F32), pltpu.VMEM((2, tm, 1), F32)],
        compiler_params=_cparams(("arbitrary", "arbitrary")),
        name="ple_moe_ln2",
    )(x1s, w_pg_b, b_pg, p2, w_pp, y1, mu1, rs1, ln1_g, ln1_b, moe, ln2_g, ln2_b)


def _routing_tables(ids_t, counts_row):
    counts = counts_row[EXPERT_LANE0:EXPERT_LANE0 + N_EXPERTS]
    tiles = (counts + TM_MOE - 1) // TM_MOE
    tile_end = jnp.cumsum(tiles)
    offsets = (tile_end - tiles) * TM_MOE
    n_used = tile_end[-1:]
    n_tokens = ids_t.shape[1]
    experts = ids_t[0:EXPERT_TOP_K]
    hit = experts[None] == jnp.arange(N_EXPERTS, dtype=I32)[:, None, None]
    first_slot = jnp.sum(jnp.where(hit, offsets[:, None, None], 0), axis=0)
    slots = (first_slot + ids_t[EXPERT_TOP_K:2 * EXPERT_TOP_K]).reshape(-1)
    n_tiles = (EXPERT_TOP_K * n_tokens) // TM_MOE + N_EXPERTS
    tile_ids = jnp.minimum(jnp.arange(n_tiles, dtype=I32), n_used - 1)
    tile_expert = jnp.sum(tile_end[None, :] <= tile_ids[:, None], axis=1).astype(I32)
    slab_of_slot = jnp.zeros((n_tiles * TM_MOE,), I32).at[slots].set(
        jnp.arange(slots.size, dtype=I32) % n_tokens * WORD_ROWS, unique_indices=True)
    eid = jnp.arange(N_EXPERTS, dtype=I32)
    later_owner = (eid[None, :] > eid[:, None]) & (tiles[None, :] > 0)
    next_expert = jnp.min(jnp.where(later_owner, eid[None, :], N_EXPERTS), axis=1).astype(I32)
    stage_slot = ((jnp.cumsum(tiles > 0) - 1) % 2).astype(I32)
    return (slots.astype(I32), slab_of_slot, tile_expert, n_used.astype(I32),
            next_expert, stage_slot)


def kernel(x, p, positions, w_in, b_in, sinks, g_norm_a, g_norm_b, w_out, b_out,
           ln1_g, ln1_b, w_group, b_group, w_er, b_er, w_gate, w_up, w_down,
           w_ple_gate, b_ple_gate, w_ple_proj, ln2_g, ln2_b):
    batch, seq, d = x.shape
    m = batch * seq
    row = lambda v: v.reshape(1, -1)
    x2 = x.reshape(m, d)
    for i in range(DEPTH):
        proj = _in_projection(x2, w_in[i].astype(BF16), row(b_in[i]))
        inv_freq = ROPE_THETA ** (-jnp.arange(0, SWA_HEAD_DIM, 2, dtype=F32) / SWA_HEAD_DIM)
        inv_freq = jnp.tile(inv_freq, LANES // inv_freq.shape[0]).reshape(1, LANES)
        o_a = _swa_attention(proj, positions.reshape(m, 1), inv_freq, sinks[i], batch, seq)
        o_b = _sb_attention(proj, batch, seq)
        y1, mu1, rs1, x1s = _out_projection(
            o_a, o_b, row(g_norm_a[i]), row(g_norm_b[i]), w_out[i].astype(BF16), x2,
            row(b_out[i]), row(ln1_g[i]), row(ln1_b[i]))
        pad = ROUTER_COLS - N_GROUPS - N_EXPERTS
        w_r = jnp.concatenate(
            [w_group[i], w_er[i].transpose(1, 0, 2).reshape(d, N_EXPERTS),
             jnp.zeros((d, pad), F32)], axis=1).astype(BF16)
        b_r = jnp.concatenate([b_group[i], b_er[i].reshape(-1), jnp.zeros((pad,), F32)])
        ids, wts, counts = _router(x1s, w_r, row(b_r))
        (slots, slab_of_slot, tile_expert, n_used, next_expert,
         stage_slot) = _routing_tables(ids, counts[0])
        act = _grouped_gate_up(tile_expert, n_used, slab_of_slot, next_expert, stage_slot,
                               x1s, w_gate[i], w_up[i])
        y_sorted = _grouped_down(tile_expert, n_used, next_expert, stage_slot, act, w_down[i])
        moe = _combine_experts(slots, y_sorted, wts)
        x2 = _final_stage(x1s, w_ple_gate[i].astype(BF16), row(b_ple_gate[i]),
                          p[i].reshape(m, PLE_DIM), w_ple_proj[i], y1, mu1, rs1,
                          row(ln1_g[i]), row(ln1_b[i]), moe, row(ln2_g[i]), row(ln2_b[i]))
    return x2.reshape(batch, seq, d)
```

```python
import math

import jax
import jax.numpy as jnp
from jax import lax
from jax.experimental import pallas as pl
from jax.experimental.pallas import tpu as pltpu

F32 = jnp.float32
BF16 = jnp.bfloat16
I32 = jnp.int32
U32 = jnp.uint32

D_MODEL = 4096
PLE_DIM = 256
BLOCK = 128
ROPE_THETA = 10000.0
LN_EPS = 1e-5
RMS_EPS = 1e-6
NEG_INF = -1e30

SWA_HEAD_DIM = 64
SWA_WIDTH = D_MODEL // 2
SWA_HEADS = SWA_WIDTH // SWA_HEAD_DIM
SWA_KV_HEADS = SWA_HEADS // 8
SWA_GROUP = SWA_HEADS // SWA_KV_HEADS
SWA_KV_WIDTH = SWA_KV_HEADS * SWA_HEAD_DIM

SB_HEAD_DIM = 128
SB_WIDTH = D_MODEL - SWA_WIDTH
SB_HEADS = SB_WIDTH // SB_HEAD_DIM

IN_WIDTH = SWA_WIDTH + 2 * SWA_KV_WIDTH + 3 * SB_WIDTH
A_WIDTH = SWA_WIDTH + 2 * SWA_KV_WIDTH
Q_B_COL = 0
K_B_COL = Q_B_COL + SB_WIDTH
V_B_COL = K_B_COL + SB_WIDTH
Q_A_COL = V_B_COL + SB_WIDTH
K_A_COL = Q_A_COL + SWA_WIDTH
V_A_COL = K_A_COL + SWA_KV_WIDTH

N_GROUPS = 4
EXPERTS_PER_GROUP = 8
N_EXPERTS = N_GROUPS * EXPERTS_PER_GROUP
EXPERT_TOP_K = 2
DEPTH = 1
DEEPNORM_ALPHA = (2.0 * DEPTH) ** 0.25

LANES = 128
SUBLANES = 8
VMEM_LIMIT_BYTES = 56 * 1024 * 1024

EXP_ZERO_LOG = -126 * math.log(2.0) - 2.0

TM_IN = 1024
TM_PROJ = 512
TN_PROJ = 512
TM_MOE = 256
TM_COMBINE = 256
SB_HEADS_PER_STEP = 16
SUFFIX_TERMS = 2
ROW_CHUNK = 16
WEIGHT_DMA_PRIORITY = 1


def _cparams(sem):
    return pltpu.CompilerParams(dimension_semantics=sem,
                                vmem_limit_bytes=VMEM_LIMIT_BYTES)


def _pack_halves(lo, hi):
    return lax.bitcast_convert_type(
        pltpu.pack_elementwise([lo, hi], packed_dtype=BF16), U32)


def _unpack_halves(words):
    return tuple(
        pltpu.unpack_elementwise(words, index=k, packed_dtype=BF16,
                                 unpacked_dtype=F32).astype(BF16) for k in (0, 1))


def _dot_halves(lo, hi, w_ref):
    half = lo.shape[1]
    return (jnp.dot(lo, w_ref[:half], preferred_element_type=F32)
            + jnp.dot(hi, w_ref[half:], preferred_element_type=F32))


WORD_ROWS = (D_MODEL // 2) // LANES
SLAB_PAIR = 2 * LANES


def _slab_words(slab_ref, pair, tokens):
    return jnp.concatenate(
        [slab_ref[pl.ds(2 * pair + k, tokens, stride=WORD_ROWS), :] for k in (0, 1)],
        axis=1)


def _dot_slab(slab_ref, tokens, w_refs):
    half = D_MODEL // 2
    outs = [None] * len(w_refs)
    for pair in range(half // SLAB_PAIR):
        lo, hi = _unpack_halves(_slab_words(slab_ref, pair, tokens))
        rows_lo = slice(pair * SLAB_PAIR, (pair + 1) * SLAB_PAIR)
        rows_hi = slice(half + pair * SLAB_PAIR, half + (pair + 1) * SLAB_PAIR)
        for n, w_ref in enumerate(w_refs):
            part = (jnp.dot(lo, w_ref[rows_lo], preferred_element_type=F32)
                    + jnp.dot(hi, w_ref[rows_hi], preferred_element_type=F32))
            outs[n] = part if outs[n] is None else outs[n] + part
    return outs


def _dot_nt(a, b):
    return lax.dot_general(a, b, (((1,), (1,)), ((), ())),
                           preferred_element_type=F32)


def _inproj_kernel(x_ref, w_ref, b_ref, o_ref, xb_ref):
    @pl.when(pl.program_id(1) == 0)
    def _():
        xb_ref[...] = x_ref[...].astype(BF16)

    acc = jnp.dot(xb_ref[...], w_ref[...], preferred_element_type=F32)
    o_ref[...] = (acc + b_ref[...]).astype(o_ref.dtype)


def _in_projection(x2, w_b, b):
    m, k = x2.shape
    n = w_b.shape[1]
    n_blocks = n // TN_PROJ
    assert A_WIDTH % TN_PROJ == 0
    a_blocks = A_WIDTH // TN_PROJ

    def src(j):
        return lax.rem(j + a_blocks, n_blocks)

    return pl.pallas_call(
        _inproj_kernel,
        out_shape=jax.ShapeDtypeStruct((m, n), BF16),
        grid=(m // TM_IN, n_blocks),
        in_specs=[
            pl.BlockSpec((TM_IN, k), lambda i, j: (i, 0)),
            pl.BlockSpec((k, TN_PROJ), lambda i, j: (0, src(j))),
            pl.BlockSpec((1, TN_PROJ), lambda i, j: (0, src(j))),
        ],
        out_specs=pl.BlockSpec((TM_IN, TN_PROJ), lambda i, j: (i, j)),
        scratch_shapes=[pltpu.VMEM((TM_IN, k), BF16)],
        compiler_params=_cparams(("arbitrary", "arbitrary")),
        name="in_projection",
    )(x2, w_b, b)


def _swa_kernel(sinks_ref, q_ref, kc_ref, vc_ref, vp_ref,
                posc_ref, invf_ref, o_ref, kprev_ref):
    n = pl.program_id(1)

    @pl.when(n == 0)
    def _():
        kprev_ref[...] = jnp.zeros_like(kprev_ref)
    lane = lax.broadcasted_iota(I32, (1, LANES), 1)
    first_half = (lane % SWA_HEAD_DIM) < (SWA_HEAD_DIM // 2)

    def tables(pos_ref):
        ang = pos_ref[...].astype(F32) * invf_ref[...]
        sin = jnp.sin(ang)
        return jnp.cos(ang), jnp.where(first_half, -sin, sin)

    def rope(x, cos, sin_signed):
        partner = jnp.where(first_half,
                            pltpu.roll(x, LANES - SWA_HEAD_DIM // 2, 1),
                            pltpu.roll(x, SWA_HEAD_DIM // 2, 1))
        return x * cos + partner * sin_signed

    cos_c, sin_c = tables(posc_ref)

    def rope_block(ref, cos, sin_signed):
        width = ref.shape[1]
        return [rope(ref[:, c * LANES:(c + 1) * LANES].astype(F32), cos,
                     sin_signed).astype(BF16) for c in range(width // LANES)]

    q_chunks = rope_block(q_ref, cos_c, sin_c)
    k_cur = rope_block(kc_ref, cos_c, sin_c)
    k_chunks = [jnp.concatenate([kprev_ref[:, c * LANES:(c + 1) * LANES], k_c], axis=0)
                for c, k_c in enumerate(k_cur)]
    for c, k_c in enumerate(k_cur):
        kprev_ref[:, c * LANES:(c + 1) * LANES] = k_c
    v_all = jnp.concatenate([vp_ref[...], vc_ref[...]], axis=0)

    qi = lax.broadcasted_iota(I32, (BLOCK, 2 * BLOCK), 0)
    kj = lax.broadcasted_iota(I32, (BLOCK, 2 * BLOCK), 1)
    rel = qi - (kj - BLOCK)
    valid = (rel >= 0) & (rel < BLOCK) & ((kj >= BLOCK) | (n > 0))

    def head_slice(chunks, head):
        half = head % 2
        return chunks[head // 2][:, half * SWA_HEAD_DIM:(half + 1) * SWA_HEAD_DIM]

    scale = 1.0 / math.sqrt(SWA_HEAD_DIM)
    scores = []
    for h in range(SWA_KV_HEADS):
        k_h = head_slice(k_chunks, h)
        q_h = jnp.concatenate(
            [head_slice(q_chunks, h * SWA_GROUP + g) for g in range(SWA_GROUP)],
            axis=0)
        scores.append(_dot_nt(q_h, k_h) * scale)
    weights, sink_terms = [], []
    for h in range(SWA_KV_HEADS):
        weights_h = []
        for g in range(SWA_GROUP):
            sink = sinks_ref[h * SWA_GROUP + g]
            s_g = jnp.where(valid, scores[h][g * BLOCK:(g + 1) * BLOCK], NEG_INF)
            m = jnp.maximum(jnp.max(s_g, axis=-1, keepdims=True), sink)
            weights_h.append(jnp.exp(s_g - m).astype(BF16))
            sink_terms.append(jnp.exp(sink - m))
        weights.append(jnp.concatenate(weights_h, axis=0))
    ones = jnp.ones((2 * BLOCK, SWA_HEAD_DIM), BF16)
    for h in range(SWA_KV_HEADS):
        v_h = v_all[:, h * SWA_HEAD_DIM:(h + 1) * SWA_HEAD_DIM]
        both = jnp.dot(weights[h], jnp.concatenate([v_h, ones], axis=1),
                       preferred_element_type=F32)
        for g in range(SWA_GROUP):
            rows = slice(g * BLOCK, (g + 1) * BLOCK)
            den = both[rows, SWA_HEAD_DIM:SWA_HEAD_DIM + 1] + sink_terms[h * SWA_GROUP + g]
            col = (h * SWA_GROUP + g) * SWA_HEAD_DIM
            o_ref[:, col:col + SWA_HEAD_DIM] = both[rows, :SWA_HEAD_DIM] / den


def _swa_attention(proj, pos2, inv_freq, sinks, batch, seq):
    nb = seq // BLOCK
    assert Q_A_COL % SWA_WIDTH == 0 and K_A_COL % SWA_KV_WIDTH == 0
    assert V_A_COL % SWA_KV_WIDTH == 0
    qcol = Q_A_COL // SWA_WIDTH
    kcol = K_A_COL // SWA_KV_WIDTH
    vcol = V_A_COL // SWA_KV_WIDTH

    def cur(b, n):
        return b * nb + n

    def prev(b, n):
        return b * nb + jnp.maximum(n - 1, 0)

    return pl.pallas_call(
        _swa_kernel,
        out_shape=jax.ShapeDtypeStruct((batch * seq, SWA_WIDTH), F32),
        grid=(batch, nb),
        in_specs=[
            pl.BlockSpec(memory_space=pltpu.SMEM),
            pl.BlockSpec((BLOCK, SWA_WIDTH), lambda b, n: (cur(b, n), qcol)),
            pl.BlockSpec((BLOCK, SWA_KV_WIDTH), lambda b, n: (cur(b, n), kcol)),
            pl.BlockSpec((BLOCK, SWA_KV_WIDTH), lambda b, n: (cur(b, n), vcol)),
            pl.BlockSpec((BLOCK, SWA_KV_WIDTH), lambda b, n: (prev(b, n), vcol)),
            pl.BlockSpec((BLOCK, 1), lambda b, n: (cur(b, n), 0)),
            pl.BlockSpec((1, LANES), lambda b, n: (0, 0)),
        ],
        out_specs=pl.BlockSpec((BLOCK, SWA_WIDTH), lambda b, n: (cur(b, n), 0)),
        scratch_shapes=[pltpu.VMEM((BLOCK, SWA_KV_WIDTH), BF16)],
        compiler_params=_cparams(("arbitrary", "arbitrary")),
        name="swa_attention",
    )(sinks, proj, proj, proj, proj, pos2, inv_freq)


def _sb_kernel(q_ref, k_ref, v_ref, o_ref):
    n = pl.program_id(2)
    heads = [slice(h * SB_HEAD_DIM, (h + 1) * SB_HEAD_DIM)
             for h in range(SB_HEADS_PER_STEP)]
    rows = SB_HEADS_PER_STEP * BLOCK
    scale = 1.0 / math.sqrt(SB_HEAD_DIM)
    key_j = lax.broadcasted_iota(I32, (BLOCK, 2 * BLOCK), 0)
    out_c = lax.broadcasted_iota(I32, (BLOCK, 2 * BLOCK), 1)
    later_total = ((key_j > out_c) | (out_c >= BLOCK)).astype(BF16)
    q_row = lax.broadcasted_iota(I32, (rows, BLOCK), 0) % BLOCK
    k_col = lax.broadcasted_iota(I32, (rows, BLOCK), 1)
    causal = k_col < q_row

    def block(kb, carry, acc, mask):
        start = pl.multiple_of(kb * BLOCK, BLOCK)
        z = jnp.concatenate(
            [_dot_nt(q_ref[:, hd], k_ref[pl.ds(start, BLOCK), hd]) for hd in heads],
            axis=0) * scale
        t = jnp.log(1.0 + jnp.exp(-jnp.abs(z)))
        log_not = jnp.minimum(-z, 0.0) - t
        log_beta = jnp.minimum(z, 0.0) - t
        if mask is not None:
            log_not = jnp.where(mask, log_not, 0.0)
        terms, rest = [], log_not
        for _ in range(SUFFIX_TERMS):
            terms.append(rest.astype(BF16))
            rest = rest - terms[-1].astype(F32)
        parts = jnp.dot(jnp.concatenate(terms, axis=0), later_total,
                        preferred_element_type=F32)
        sums = parts[:rows]
        for t in range(1, SUFFIX_TERMS):
            sums = sums + parts[t * rows:(t + 1) * rows]
        a = jnp.exp(log_beta + sums[:, :BLOCK] + carry)
        if mask is not None:
            a = jnp.where(mask, a, 0.0)
        a = a.astype(BF16)
        pv = jnp.concatenate(
            [jnp.dot(a[h * BLOCK:(h + 1) * BLOCK], v_ref[pl.ds(start, BLOCK), hd],
                     preferred_element_type=F32) for h, hd in enumerate(heads)], axis=0)
        return carry + sums[:, BLOCK:], acc + pv

    zeros = jnp.zeros((rows, BLOCK), F32)
    carry, acc = block(n, zeros, zeros, causal)

    def cond(state):
        kb, carry, _ = state
        return jnp.logical_and(kb >= 0, jnp.max(carry) > EXP_ZERO_LOG)

    def body(state):
        kb, carry, acc = state
        carry, acc = block(kb, carry, acc, None)
        return kb - 1, carry, acc

    _, _, acc = lax.while_loop(cond, body, (n - 1, carry, acc))
    for h, hd in enumerate(heads):
        o_ref[:, hd] = acc[h * BLOCK:(h + 1) * BLOCK]


def _sb_attention(proj, batch, seq):
    nb = seq // BLOCK
    width = SB_HEADS_PER_STEP * SB_HEAD_DIM
    assert Q_B_COL % width == 0 and K_B_COL % width == 0 and V_B_COL % width == 0
    qcol = Q_B_COL // width
    kcol = K_B_COL // width
    vcol = V_B_COL // width
    return pl.pallas_call(
        _sb_kernel,
        out_shape=jax.ShapeDtypeStruct((batch * seq, SB_WIDTH), F32),
        grid=(batch, SB_HEADS // SB_HEADS_PER_STEP, nb),
        in_specs=[
            pl.BlockSpec((BLOCK, width), lambda b, h, n: (b * nb + n, qcol + h)),
            pl.BlockSpec((seq, width), lambda b, h, n: (b, kcol + h),
                         pipeline_mode=pl.Buffered(1)),
            pl.BlockSpec((seq, width), lambda b, h, n: (b, vcol + h),
                         pipeline_mode=pl.Buffered(1)),
        ],
        out_specs=pl.BlockSpec((BLOCK, width), lambda b, h, n: (b * nb + n, h)),
        compiler_params=_cparams(("arbitrary", "arbitrary", "arbitrary")),
        name="sb_attention",
    )(proj, proj, proj)


def _row_stats(chunks_ref, n_chunks, width):
    total = chunks_ref[0].sum(axis=-1, keepdims=True)
    for c in range(1, n_chunks):
        total = total + chunks_ref[c].sum(axis=-1, keepdims=True)
    mu = total / width
    sq = jnp.square(chunks_ref[0] - mu).sum(axis=-1, keepdims=True)
    for c in range(1, n_chunks):
        sq = sq + jnp.square(chunks_ref[c] - mu).sum(axis=-1, keepdims=True)
    return mu, lax.rsqrt(sq / width + LN_EPS)


def _outproj_kernel(oa_ref, ob_ref, ga_ref, gb_ref, wa_ref, wb_ref, x_ref, bo_ref,
                    lg_ref, lb_ref, y_ref, mu_ref, rs_ref, x1s_ref,
                    ma_ref, mb_ref, acc_ref):
    j = pl.program_id(1)
    n_chunks = acc_ref.shape[0]
    tn = acc_ref.shape[2]

    @pl.when(j == 0)
    def _():
        for o_ref, g_ref, m_ref in ((oa_ref, ga_ref, ma_ref), (ob_ref, gb_ref, mb_ref)):
            width = o_ref.shape[1]
            cols = [slice(c * tn, (c + 1) * tn) for c in range(width // tn)]
            sq = sum(jnp.square(o_ref[:, sl]).sum(axis=-1, keepdims=True) for sl in cols)
            r = lax.rsqrt(sq / width + RMS_EPS)
            for sl in cols:
                m_ref[:, sl] = (o_ref[:, sl] * r * g_ref[:, sl]).astype(BF16)

    mix = (jnp.dot(ma_ref[...], wa_ref[...], preferred_element_type=F32)
           + jnp.dot(mb_ref[...], wb_ref[...], preferred_element_type=F32))
    y = DEEPNORM_ALPHA * x_ref[...] + (mix + bo_ref[...])
    y_ref[...] = y
    acc_ref[j] = y

    @pl.when(j == n_chunks - 1)
    def _():
        mu, rs = _row_stats(acc_ref, n_chunks, n_chunks * tn)
        mu_ref[...] = mu
        rs_ref[...] = rs

        def normed(c):
            sl = slice(c * tn, (c + 1) * tn)
            return (acc_ref[c] - mu) * rs * lg_ref[:, sl] + lb_ref[:, sl]

        tm = acc_ref.shape[1]
        for c in range(n_chunks // 2):
            words = _pack_halves(normed(c), normed(c + n_chunks // 2))
            for q in range(tn // LANES):
                x1s_ref[pl.ds(c * (tn // LANES) + q, tm, stride=WORD_ROWS), :] = (
                    words[:, q * LANES:(q + 1) * LANES])


def _out_projection(o_a, o_b, g_a, g_b, w_b, x2, b_out, ln_g, ln_b):
    m, d = x2.shape
    tm, tn = TM_PROJ, TN_PROJ
    half = o_a.shape[1]
    once = dict(pipeline_mode=pl.Buffered(1))
    return pl.pallas_call(
        _outproj_kernel,
        out_shape=(jax.ShapeDtypeStruct((m, d), F32),
                   jax.ShapeDtypeStruct((m, 1), F32),
                   jax.ShapeDtypeStruct((m, 1), F32),
                   jax.ShapeDtypeStruct((m * WORD_ROWS, LANES), U32)),
        grid=(m // tm, d // tn),
        in_specs=[
            pl.BlockSpec((tm, half), lambda i, j: (i, 0), **once),
            pl.BlockSpec((tm, half), lambda i, j: (i, 0), **once),
            pl.BlockSpec((1, half), lambda i, j: (0, 0)),
            pl.BlockSpec((1, half), lambda i, j: (0, 0)),
            pl.BlockSpec((half, tn), lambda i, j: (0, j)),
            pl.BlockSpec((half, tn), lambda i, j: (1, j)),
            pl.BlockSpec((tm, tn), lambda i, j: (i, j)),
            pl.BlockSpec((1, tn), lambda i, j: (0, j)),
            pl.BlockSpec((1, d), lambda i, j: (0, 0)),
            pl.BlockSpec((1, d), lambda i, j: (0, 0)),
        ],
        out_specs=(pl.BlockSpec((tm, tn), lambda i, j: (i, j)),
                   pl.BlockSpec((tm, 1), lambda i, j: (i, 0)),
                   pl.BlockSpec((tm, 1), lambda i, j: (i, 0)),
                   pl.BlockSpec((tm * WORD_ROWS, LANES), lambda i, j: (i, 0))),
        scratch_shapes=[pltpu.VMEM((tm, half), BF16), pltpu.VMEM((tm, half), BF16),
                        pltpu.VMEM((d // tn, tm, tn), F32)],
        compiler_params=_cparams(("arbitrary", "arbitrary")),
        name="out_projection_ln1",
    )(o_a, o_b, g_a, g_b, w_b, w_b, x2, b_out, ln_g, ln_b)


ROUTER_COLS = LANES
EXPERT_LANE0 = N_GROUPS


def _router_kernel(x_ref, w_ref, b_ref, ids_ref, wts_ref, cnt_ref, carry_ref):
    i = pl.program_id(0)
    tm = wts_ref.shape[0]

    @pl.when(i == 0)
    def _():
        carry_ref[...] = jnp.zeros_like(carry_ref)

    logits = _dot_slab(x_ref, tm, [w_ref])[0] + b_ref[...]
    lane = lax.broadcasted_iota(I32, (tm, ROUTER_COLS), 1)
    big = jnp.int32(ROUTER_COLS)

    def first_argmax(vals):
        top = jnp.max(vals, axis=-1, keepdims=True)
        idx = jnp.min(jnp.where(vals == top, lane, big), axis=-1, keepdims=True)
        return top, idx

    is_group = lane < N_GROUPS
    g_logits = jnp.where(is_group, logits, -jnp.inf)
    g_top, g_idx = first_argmax(g_logits)
    g_w = 1.0 / jnp.sum(jnp.exp(g_logits - g_top), axis=-1, keepdims=True)

    first = EXPERT_LANE0 + g_idx * EXPERTS_PER_GROUP
    in_group = (lane >= first) & (lane < first + EXPERTS_PER_GROUP)
    e_logits = jnp.where(in_group, logits, -jnp.inf)
    top1, idx1 = first_argmax(e_logits)
    top2, idx2 = first_argmax(jnp.where(lane == idx1, -jnp.inf, e_logits))
    e2 = jnp.exp(top2 - top1)
    w1 = g_w / (1.0 + e2)
    w2 = g_w * e2 / (1.0 + e2)

    hit1 = lane == idx1
    hit2 = lane == idx2
    onehot = (hit1 | hit2).astype(BF16)
    r = lax.broadcasted_iota(I32, (tm, tm), 0)
    c = lax.broadcasted_iota(I32, (tm, tm), 1)
    before = (c < r).astype(BF16)
    prior = jnp.dot(before, onehot, preferred_element_type=F32) + carry_ref[0:1, :]
    rank1 = jnp.sum(jnp.where(hit1, prior, 0.0), axis=-1, keepdims=True)
    rank2 = jnp.sum(jnp.where(hit2, prior, 0.0), axis=-1, keepdims=True)
    counts = carry_ref[0:1, :] + jnp.sum(onehot.astype(F32), axis=0, keepdims=True)
    carry_ref[...] = jnp.broadcast_to(counts, carry_ref.shape)
    cnt_ref[...] = jnp.broadcast_to(counts, cnt_ref.shape).astype(I32)

    ids = jnp.where(lane == 0, idx1 - EXPERT_LANE0,
          jnp.where(lane == 1, idx2 - EXPERT_LANE0,
          jnp.where(lane == 2, rank1.astype(I32),
          jnp.where(lane == 3, rank2.astype(I32), 0))))
    ids_ref[...] = ids.T[:ids_ref.shape[0]]
    wts_ref[...] = jnp.where(lane == 0, w1, jnp.where(lane == 1, w2, 0.0))


def _router(x1s, w_r, b_r):
    m = x1s.shape[0] // WORD_ROWS
    tm = TM_PROJ
    return pl.pallas_call(
        _router_kernel,
        out_shape=(jax.ShapeDtypeStruct((SUBLANES, m), I32),
                   jax.ShapeDtypeStruct((m, ROUTER_COLS), F32),
                   jax.ShapeDtypeStruct((SUBLANES, ROUTER_COLS), I32)),
        grid=(m // tm,),
        in_specs=[
            pl.BlockSpec((tm * WORD_ROWS, LANES), lambda i: (i, 0)),
            pl.BlockSpec((D_MODEL, ROUTER_COLS), lambda i: (0, 0)),
            pl.BlockSpec((1, ROUTER_COLS), lambda i: (0, 0)),
        ],
        out_specs=(pl.BlockSpec((SUBLANES, tm), lambda i: (0, i)),
                   pl.BlockSpec((tm, ROUTER_COLS), lambda i: (i, 0)),
                   pl.BlockSpec((SUBLANES, ROUTER_COLS), lambda i: (0, 0))),
        scratch_shapes=[pltpu.VMEM((SUBLANES, ROUTER_COLS), F32)],
        compiler_params=_cparams(("arbitrary",)),
        name="router",
    )(x1s, w_r, b_r)


def _start_row_gather(idx_ref, first, n_items, span, src_ref, dst_ref, sem):
    def body(c, _):
        for u in range(ROW_CHUNK):
            r = c * ROW_CHUNK + u
            src_row = pl.multiple_of(idx_ref[first + r], span)
            dst_row = pl.multiple_of(r * span, span)
            pltpu.make_async_copy(src_ref.at[pl.ds(src_row, span)],
                                  dst_ref.at[pl.ds(dst_row, span)], sem).start()
        return 0

    lax.fori_loop(0, n_items // ROW_CHUNK, body, 0)


def _wait_row_gather(n_items, span, src_ref, dst_ref, sem):
    def body(c, _):
        for _u in range(ROW_CHUNK):
            pltpu.make_async_copy(src_ref.at[pl.ds(0, span)], dst_ref.at[pl.ds(0, span)],
                                  sem).wait()
        return 0

    lax.fori_loop(0, n_items // ROW_CHUNK, body, 0)


def _expert_changed(te_ref, i):
    return jnp.logical_or(i == 0, te_ref[i] != te_ref[jnp.maximum(i - 1, 0)])


def _tile_row(i, nu):
    return jnp.minimum(i, nu[0] - 1)


def _stage_expert_weights(i, te_ref, nxt_ref, ws_ref, w_hbms, stage_ref, bf_refs, sem):
    expert = te_ref[i]
    slot = ws_ref[expert]

    def copies(e, s):
        return [pltpu.make_async_copy(w.at[e], stage_ref.at[s, n], sem.at[s])
                for n, w in enumerate(w_hbms)]

    @pl.when(i == 0)
    def _():
        for c in copies(expert, slot):
            c.start(priority=WEIGHT_DMA_PRIORITY)

    @pl.when(_expert_changed(te_ref, i))
    def _():
        for c in copies(expert, slot):
            c.wait()
        nxt = nxt_ref[expert]

        @pl.when(nxt < N_EXPERTS)
        def _():
            for c in copies(nxt, 1 - slot):
                c.start(priority=WEIGHT_DMA_PRIORITY)

        for n, bf_ref in enumerate(bf_refs):
            bf_ref[...] = stage_ref[slot, n].astype(BF16)


def _gateup_kernel(te_ref, nu_ref, tos_ref, nxt_ref, ws_ref, x_hbm, wg_hbm, wu_hbm, a_ref,
                   xbuf_ref, stage_ref, wgb_ref, wub_ref, sems, wsem):
    i = pl.program_id(0)
    n_used = nu_ref[0]
    slot = lax.rem(i, 2)

    def start(tile, buf):
        _start_row_gather(tos_ref, tile * TM_MOE, TM_MOE, WORD_ROWS, x_hbm,
                          xbuf_ref.at[buf], sems.at[buf])

    @pl.when(i == 0)
    def _():
        start(0, 0)

    @pl.when(i + 1 < n_used)
    def _():
        start(i + 1, 1 - slot)

    _stage_expert_weights(i, te_ref, nxt_ref, ws_ref, [wg_hbm, wu_hbm], stage_ref,
                          [wgb_ref, wub_ref], wsem)

    @pl.when(i < n_used)
    def _():
        _wait_row_gather(TM_MOE, WORD_ROWS, x_hbm, xbuf_ref.at[slot], sems.at[slot])
        gate, up = _dot_slab(xbuf_ref.at[slot], TM_MOE, [wgb_ref, wub_ref])
        a_ref[...] = (gate * jax.nn.sigmoid(gate) * up).astype(a_ref.dtype)

    @pl.when(i >= n_used)
    def _():
        a_ref[...] = jnp.zeros_like(a_ref)


def _grouped_gate_up(tile_expert, n_used, slab_of_slot, next_expert, stage_slot,
                     x1s, w_gate, w_up):
    p_rows = slab_of_slot.shape[0]
    _, d, f = w_gate.shape
    hbm = pl.BlockSpec(memory_space=pl.ANY)
    return pl.pallas_call(
        _gateup_kernel,
        out_shape=jax.ShapeDtypeStruct((p_rows, f), BF16),
        grid_spec=pltpu.PrefetchScalarGridSpec(
            num_scalar_prefetch=5,
            grid=(p_rows // TM_MOE,),
            in_specs=[hbm, hbm, hbm],
            out_specs=pl.BlockSpec((TM_MOE, f), lambda i, *_: (i, 0)),
            scratch_shapes=[pltpu.VMEM((2, TM_MOE * WORD_ROWS, LANES), U32),
                            pltpu.VMEM((2, 2, d, f), F32),
                            pltpu.VMEM((d, f), BF16), pltpu.VMEM((d, f), BF16),
                            pltpu.SemaphoreType.DMA((2,)), pltpu.SemaphoreType.DMA((2,))],
        ),
        compiler_params=_cparams(("arbitrary",)),
        name="moe_gate_up",
    )(tile_expert, n_used, slab_of_slot, next_expert, stage_slot, x1s, w_gate, w_up)


def _down_kernel(te_ref, nu_ref, nxt_ref, ws_ref, a_ref, wd_hbm, y_ref,
                 stage_ref, wdb_ref, wsem):
    i = pl.program_id(0)
    _stage_expert_weights(i, te_ref, nxt_ref, ws_ref, [wd_hbm], stage_ref, [wdb_ref], wsem)

    @pl.when(i < nu_ref[0])
    def _():
        y_ref[:, 0, :] = jnp.dot(a_ref[...], wdb_ref[...], preferred_element_type=F32)

    @pl.when(i >= nu_ref[0])
    def _():
        y_ref[...] = jnp.zeros_like(y_ref)


def _grouped_down(tile_expert, n_used, next_expert, stage_slot, act, w_down):
    p_rows, f = act.shape
    d = w_down.shape[2]
    return pl.pallas_call(
        _down_kernel,
        out_shape=jax.ShapeDtypeStruct((p_rows, 1, d), F32),
        grid_spec=pltpu.PrefetchScalarGridSpec(
            num_scalar_prefetch=4,
            grid=(p_rows // TM_MOE,),
            in_specs=[
                pl.BlockSpec((TM_MOE, f), lambda i, te, nu, *_: (_tile_row(i, nu), 0)),
                pl.BlockSpec(memory_space=pl.ANY),
            ],
            out_specs=pl.BlockSpec((TM_MOE, 1, d), lambda i, *_: (i, 0, 0)),
            scratch_shapes=[pltpu.VMEM((2, 1, f, d), F32), pltpu.VMEM((f, d), BF16),
                            pltpu.SemaphoreType.DMA((2,))],
        ),
        compiler_params=_cparams(("arbitrary",)),
        name="moe_down",
    )(tile_expert, n_used, next_expert, stage_slot, act, w_down)


def _combine_kernel(slot_ref, y_hbm, wts_ref, o_ref, buf_ref, sems):
    i = pl.program_id(0)
    tm = o_ref.shape[0]
    n_tokens = slot_ref.shape[0] // EXPERT_TOP_K
    slot = lax.rem(i, 2)

    def start(tile, buf):
        for k in range(EXPERT_TOP_K):
            _start_row_gather(slot_ref, k * n_tokens + tile * tm, tm, 1, y_hbm,
                              buf_ref.at[buf, k], sems.at[buf])

    @pl.when(i == 0)
    def _():
        start(0, 0)

    @pl.when(i + 1 < pl.num_programs(0))
    def _():
        start(i + 1, 1 - slot)

    _wait_row_gather(EXPERT_TOP_K * tm, 1, y_hbm, buf_ref.at[slot, 0], sems.at[slot])
    for c in range(o_ref.shape[1] // TN_PROJ):
        sl = slice(c * TN_PROJ, (c + 1) * TN_PROJ)
        o_ref[:, sl] = sum(wts_ref[:, k:k + 1] * buf_ref[slot, k, :, 0, sl]
                           for k in range(EXPERT_TOP_K))


def _combine_experts(slot_kt, y_sorted, wts):
    d = y_sorted.shape[2]
    m = slot_kt.shape[0] // EXPERT_TOP_K
    tm = TM_COMBINE
    return pl.pallas_call(
        _combine_kernel,
        out_shape=jax.ShapeDtypeStruct((m, d), F32),
        grid_spec=pltpu.PrefetchScalarGridSpec(
            num_scalar_prefetch=1,
            grid=(m // tm,),
            in_specs=[
                pl.BlockSpec(memory_space=pl.ANY),
                pl.BlockSpec((tm, ROUTER_COLS), lambda i, sl: (i, 0)),
            ],
            out_specs=pl.BlockSpec((tm, d), lambda i, sl: (i, 0)),
            scratch_shapes=[pltpu.VMEM((2, EXPERT_TOP_K, tm, 1, d), F32),
                            pltpu.SemaphoreType.DMA((2,))],
        ),
        compiler_params=_cparams(("arbitrary",)),
        name="moe_combine",
    )(slot_kt, y_sorted, wts)


def _final_kernel(x1s_ref, wg_ref, bg_ref, p_ref, wp_ref, y1_ref, mu_ref, rs_ref,
                  l1g_ref, l1b_ref, moe_ref, l2g_ref, l2b_ref,
                  o_ref, lo_ref, hi_ref, acc_ref, mu2_ref, rs2_ref):
    i = pl.program_id(0)
    j = pl.program_id(1)
    n_tiles = pl.num_programs(0) - 1
    n_chunks = acc_ref.shape[1]
    tn = acc_ref.shape[3]
    cur = lax.rem(i, 2)
    prv = 1 - cur

    @pl.when(jnp.logical_and(j == 0, i < n_tiles))
    def _():
        for s in range(WORD_ROWS):
            sl = slice(s * LANES, (s + 1) * LANES)
            lo_ref[:, sl], hi_ref[:, sl] = _unpack_halves(
                x1s_ref[pl.ds(s, lo_ref.shape[0], stride=WORD_ROWS), :])

    def build():
        gate = _dot_halves(lo_ref[...], hi_ref[...], wg_ref) + bg_ref[...]
        emb = jnp.dot(p_ref[...].astype(BF16), wp_ref[...].astype(BF16),
                      preferred_element_type=F32)
        x1 = (y1_ref[...] - mu_ref[...]) * rs_ref[...] * l1g_ref[...] + l1b_ref[...]
        acc_ref[cur, j] = DEEPNORM_ALPHA * x1 + moe_ref[...] + jax.nn.sigmoid(gate) * emb

    def emit():
        o_ref[...] = ((acc_ref[prv, j] - mu2_ref[prv]) * rs2_ref[prv]
                      * l2g_ref[...] + l2b_ref[...])

    @pl.when(i == 0)
    def _():
        build()

    @pl.when(jnp.logical_and(i > 0, i < n_tiles))
    def _():
        emit()
        build()

    @pl.when(i == n_tiles)
    def _():
        emit()

    @pl.when(jnp.logical_and(j == n_chunks - 1, i < n_tiles))
    def _():
        mu2_ref[cur], rs2_ref[cur] = _row_stats(acc_ref.at[cur], n_chunks, n_chunks * tn)


def _final_stage(x1s, w_pg_b, b_pg, p2, w_pp, y1, mu1, rs1, ln1_g, ln1_b,
                 moe, ln2_g, ln2_b):
    m, d = y1.shape
    tm, tn = TM_PROJ, TN_PROJ
    ple = p2.shape[1]
    n_chunks = d // tn

    n_tiles = m // tm

    def built(i):
        return jnp.minimum(i, n_tiles - 1)

    def emitted(i):
        return jnp.maximum(i - 1, 0)

    return pl.pallas_call(
        _final_kernel,
        out_shape=jax.ShapeDtypeStruct((m, d), F32),
        grid=(n_tiles + 1, n_chunks),
        in_specs=[
            pl.BlockSpec((tm * WORD_ROWS, LANES), lambda i, j: (built(i), 0),
                         pipeline_mode=pl.Buffered(1)),
            pl.BlockSpec((d, tn), lambda i, j: (0, j)),
            pl.BlockSpec((1, tn), lambda i, j: (0, j)),
            pl.BlockSpec((tm, ple), lambda i, j: (built(i), 0)),
            pl.BlockSpec((ple, tn), lambda i, j: (0, j)),
            pl.BlockSpec((tm, tn), lambda i, j: (built(i), j)),
            pl.BlockSpec((tm, 1), lambda i, j: (built(i), 0)),
            pl.BlockSpec((tm, 1), lambda i, j: (built(i), 0)),
            pl.BlockSpec((1, tn), lambda i, j: (0, j)),
            pl.BlockSpec((1, tn), lambda i, j: (0, j)),
            pl.BlockSpec((tm, tn), lambda i, j: (built(i), j)),
            pl.BlockSpec((1, tn), lambda i, j: (0, j)),
            pl.BlockSpec((1, tn), lambda i, j: (0, j)),
        ],
        out_specs=pl.BlockSpec((tm, tn), lambda i, j: (emitted(i), jnp.where(i == 0, 0, j))),
        scratch_shapes=[pltpu.VMEM((tm, d // 2), BF16), pltpu.VMEM((tm, d // 2), BF16),
                        pltpu.VMEM((2, n_chunks, tm, tn), F32),
                        pltpu.VMEM((2, tm, 1), F32), pltpu.VMEM((2, tm, 1), F32)],
        compiler_params=_cparams(("arbitrary", "arbitrary")),
        name="ple_moe_ln2",
    )(x1s, w_pg_b, b_pg, p2, w_pp, y1, mu1, rs1, ln1_g, ln1_b, moe, ln2_g, ln2_b)


def _routing_tables(ids_t, counts_row):
    counts = counts_row[EXPERT_LANE0:EXPERT_LANE0 + N_EXPERTS]
    tiles = (counts + TM_MOE - 1) // TM_MOE
    tile_end = jnp.cumsum(tiles)
    offsets = (tile_end - tiles) * TM_MOE
    n_used = tile_end[-1:]
    n_tokens = ids_t.shape[1]
    experts = ids_t[0:EXPERT_TOP_K]
    hit = experts[None] == jnp.arange(N_EXPERTS, dtype=I32)[:, None, None]
    first_slot = jnp.sum(jnp.where(hit, offsets[:, None, None], 0), axis=0)
    slots = (first_slot + ids_t[EXPERT_TOP_K:2 * EXPERT_TOP_K]).reshape(-1)
    n_tiles = (EXPERT_TOP_K * n_tokens) // TM_MOE + N_EXPERTS
    tile_ids = jnp.minimum(jnp.arange(n_tiles, dtype=I32), n_used - 1)
    tile_expert = jnp.sum(tile_end[None, :] <= tile_ids[:, None], axis=1).astype(I32)
    slab_of_slot = jnp.zeros((n_tiles * TM_MOE,), I32).at[slots].set(
        jnp.arange(slots.size, dtype=I32) % n_tokens * WORD_ROWS, unique_indices=True)
    eid = jnp.arange(N_EXPERTS, dtype=I32)
    later_owner = (eid[None, :] > eid[:, None]) & (tiles[None, :] > 0)
    next_expert = jnp.min(jnp.where(later_owner, eid[None, :], N_EXPERTS), axis=1).astype(I32)
    stage_slot = ((jnp.cumsum(tiles > 0) - 1) % 2).astype(I32)
    return (slots.astype(I32), slab_of_slot, tile_expert, n_used.astype(I32),
            next_expert, stage_slot)


def kernel(x, p, positions, w_in, b_in, sinks, g_norm_a, g_norm_b, w_out, b_out,
           ln1_g, ln1_b, w_group, b_group, w_er, b_er, w_gate, w_up, w_down,
           w_ple_gate, b_ple_gate, w_ple_proj, ln2_g, ln2_b):
    batch, seq, d = x.shape
    m = batch * seq
    row = lambda v: v.reshape(1, -1)
    x2 = x.reshape(m, d)
    for i in range(DEPTH):
        proj = _in_projection(x2, w_in[i].astype(BF16), row(b_in[i]))
        inv_freq = ROPE_THETA ** (-jnp.arange(0, SWA_HEAD_DIM, 2, dtype=F32) / SWA_HEAD_DIM)
        inv_freq = jnp.tile(inv_freq, LANES // inv_freq.shape[0]).reshape(1, LANES)
        o_a = _swa_attention(proj, positions.reshape(m, 1), inv_freq, sinks[i], batch, seq)
        o_b = _sb_attention(proj, batch, seq)
        y1, mu1, rs1, x1s = _out_projection(
            o_a, o_b, row(g_norm_a[i]), row(g_norm_b[i]), w_out[i].astype(BF16), x2,
            row(b_out[i]), row(ln1_g[i]), row(ln1_b[i]))
        pad = ROUTER_COLS - N_GROUPS - N_EXPERTS
        w_r = jnp.concatenate(
            [w_group[i], w_er[i].transpose(1, 0, 2).reshape(d, N_EXPERTS),
             jnp.zeros((d, pad), F32)], axis=1).astype(BF16)
        b_r = jnp.concatenate([b_group[i], b_er[i].reshape(-1), jnp.zeros((pad,), F32)])
        ids, wts, counts = _router(x1s, w_r, row(b_r))
        (slots, slab_of_slot, tile_expert, n_used, next_expert,
         stage_slot) = _routing_tables(ids, counts[0])
        act = _grouped_gate_up(tile_expert, n_used, slab_of_slot, next_expert, stage_slot,
                               x1s, w_gate[i], w_up[i])
        y_sorted = _grouped_down(tile_expert, n_used, next_expert, stage_slot, act, w_down[i])
        moe = _combine_experts(slots, y_sorted, wts)
        x2 = _final_stage(x1s, w_ple_gate[i].astype(BF16), row(b_ple_gate[i]),
                          p[i].reshape(m, PLE_DIM), w_ple_proj[i], y1, mu1, rs1,
                          row(ln1_g[i]), row(ln1_b[i]), moe, row(ln2_g[i]), row(ln2_b[i]))
    return x2.reshape(batch, seq, d)
```

```python
import math

import jax
import jax.numpy as jnp
from jax import lax
from jax.experimental import pallas as pl
from jax.experimental.pallas import tpu as pltpu

F32 = jnp.float32
BF16 = jnp.bfloat16
I32 = jnp.int32
U32 = jnp.uint32

D_MODEL = 4096
PLE_DIM = 256
BLOCK = 128
ROPE_THETA = 10000.0
LN_EPS = 1e-5
RMS_EPS = 1e-6
NEG_INF = -1e30

SWA_HEAD_DIM = 64
SWA_WIDTH = D_MODEL // 2
SWA_HEADS = SWA_WIDTH // SWA_HEAD_DIM
SWA_KV_HEADS = SWA_HEADS // 8
SWA_GROUP = SWA_HEADS // SWA_KV_HEADS
SWA_KV_WIDTH = SWA_KV_HEADS * SWA_HEAD_DIM

SB_HEAD_DIM = 128
SB_WIDTH = D_MODEL - SWA_WIDTH
SB_HEADS = SB_WIDTH // SB_HEAD_DIM

IN_WIDTH = SWA_WIDTH + 2 * SWA_KV_WIDTH + 3 * SB_WIDTH
A_WIDTH = SWA_WIDTH + 2 * SWA_KV_WIDTH
Q_B_COL = 0
K_B_COL = Q_B_COL + SB_WIDTH
V_B_COL = K_B_COL + SB_WIDTH
Q_A_COL = V_B_COL + SB_WIDTH
K_A_COL = Q_A_COL + SWA_WIDTH
V_A_COL = K_A_COL + SWA_KV_WIDTH

N_GROUPS = 4
EXPERTS_PER_GROUP = 8
N_EXPERTS = N_GROUPS * EXPERTS_PER_GROUP
EXPERT_TOP_K = 2
DEPTH = 1
DEEPNORM_ALPHA = (2.0 * DEPTH) ** 0.25

LANES = 128
SUBLANES = 8
VMEM_LIMIT_BYTES = 56 * 1024 * 1024

EXP_ZERO_LOG = -126 * math.log(2.0) - 2.0

TM_IN = 1024
TM_PROJ = 512
TN_PROJ = 512
TM_MOE = 256
TM_COMBINE = 256
SB_HEADS_PER_STEP = 16
SUFFIX_TERMS = 2
ROW_CHUNK = 16
WEIGHT_DMA_PRIORITY = 1


def _cparams(sem):
    return pltpu.CompilerParams(dimension_semantics=sem,
                                vmem_limit_bytes=VMEM_LIMIT_BYTES)


def _pack_halves(lo, hi):
    return lax.bitcast_convert_type(
        pltpu.pack_elementwise([lo, hi], packed_dtype=BF16), U32)


def _unpack_halves(words):
    return tuple(
        pltpu.unpack_elementwise(words, index=k, packed_dtype=BF16,
                                 unpacked_dtype=F32).astype(BF16) for k in (0, 1))


def _dot_halves(lo, hi, w_ref):
    half = lo.shape[1]
    return (jnp.dot(lo, w_ref[:half], preferred_element_type=F32)
            + jnp.dot(hi, w_ref[half:], preferred_element_type=F32))


WORD_ROWS = (D_MODEL // 2) // LANES
SLAB_PAIR = 2 * LANES


def _slab_words(slab_ref, pair, tokens):
    return jnp.concatenate(
        [slab_ref[pl.ds(2 * pair + k, tokens, stride=WORD_ROWS), :] for k in (0, 1)],
        axis=1)


def _dot_slab(slab_ref, tokens, w_refs):
    half = D_MODEL // 2
    outs = [None] * len(w_refs)
    for pair in range(half // SLAB_PAIR):
        lo, hi = _unpack_halves(_slab_words(slab_ref, pair, tokens))
        rows_lo = slice(pair * SLAB_PAIR, (pair + 1) * SLAB_PAIR)
        rows_hi = slice(half + pair * SLAB_PAIR, half + (pair + 1) * SLAB_PAIR)
        for n, w_ref in enumerate(w_refs):
            part = (jnp.dot(lo, w_ref[rows_lo], preferred_element_type=F32)
                    + jnp.dot(hi, w_ref[rows_hi], preferred_element_type=F32))
            outs[n] = part if outs[n] is None else outs[n] + part
    return outs


def _dot_nt(a, b):
    return lax.dot_general(a, b, (((1,), (1,)), ((), ())),
                           preferred_element_type=F32)


def _inproj_kernel(x_ref, w_ref, b_ref, o_ref, xb_ref):
    @pl.when(pl.program_id(1) == 0)
    def _():
        xb_ref[...] = x_ref[...].astype(BF16)

    acc = jnp.dot(xb_ref[...], w_ref[...], preferred_element_type=F32)
    o_ref[...] = (acc + b_ref[...]).astype(o_ref.dtype)


def _in_projection(x2, w_b, b):
    m, k = x2.shape
    n = w_b.shape[1]
    n_blocks = n // TN_PROJ
    assert A_WIDTH % TN_PROJ == 0
    a_blocks = A_WIDTH // TN_PROJ

    def src(j):
        return lax.rem(j + a_blocks, n_blocks)

    return pl.pallas_call(
        _inproj_kernel,
        out_shape=jax.ShapeDtypeStruct((m, n), BF16),
        grid=(m // TM_IN, n_blocks),
        in_specs=[
            pl.BlockSpec((TM_IN, k), lambda i, j: (i, 0)),
            pl.BlockSpec((k, TN_PROJ), lambda i, j: (0, src(j))),
            pl.BlockSpec((1, TN_PROJ), lambda i, j: (0, src(j))),
        ],
        out_specs=pl.BlockSpec((TM_IN, TN_PROJ), lambda i, j: (i, j)),
        scratch_shapes=[pltpu.VMEM((TM_IN, k), BF16)],
        compiler_params=_cparams(("arbitrary", "arbitrary")),
        name="in_projection",
    )(x2, w_b, b)


def _swa_kernel(sinks_ref, q_ref, kc_ref, vc_ref, vp_ref,
                posc_ref, invf_ref, o_ref, kprev_ref):
    n = pl.program_id(1)

    @pl.when(n == 0)
    def _():
        kprev_ref[...] = jnp.zeros_like(kprev_ref)
    lane = lax.broadcasted_iota(I32, (1, LANES), 1)
    first_half = (lane % SWA_HEAD_DIM) < (SWA_HEAD_DIM // 2)

    def tables(pos_ref):
        ang = pos_ref[...].astype(F32) * invf_ref[...]
        sin = jnp.sin(ang)
        return jnp.cos(ang), jnp.where(first_half, -sin, sin)

    def rope(x, cos, sin_signed):
        partner = jnp.where(first_half,
                            pltpu.roll(x, LANES - SWA_HEAD_DIM // 2, 1),
                            pltpu.roll(x, SWA_HEAD_DIM // 2, 1))
        return x * cos + partner * sin_signed

    cos_c, sin_c = tables(posc_ref)

    def rope_block(ref, cos, sin_signed):
        width = ref.shape[1]
        return [rope(ref[:, c * LANES:(c + 1) * LANES].astype(F32), cos,
                     sin_signed).astype(BF16) for c in range(width // LANES)]

    q_chunks = rope_block(q_ref, cos_c, sin_c)
    k_cur = rope_block(kc_ref, cos_c, sin_c)
    k_chunks = [jnp.concatenate([kprev_ref[:, c * LANES:(c + 1) * LANES], k_c], axis=0)
                for c, k_c in enumerate(k_cur)]
    for c, k_c in enumerate(k_cur):
        kprev_ref[:, c * LANES:(c + 1) * LANES] = k_c
    v_all = jnp.concatenate([vp_ref[...], vc_ref[...]], axis=0)

    qi = lax.broadcasted_iota(I32, (BLOCK, 2 * BLOCK), 0)
    kj = lax.broadcasted_iota(I32, (BLOCK, 2 * BLOCK), 1)
    rel = qi - (kj - BLOCK)
    valid = (rel >= 0) & (rel < BLOCK) & ((kj >= BLOCK) | (n > 0))

    def head_slice(chunks, head):
        half = head % 2
        return chunks[head // 2][:, half * SWA_HEAD_DIM:(half + 1) * SWA_HEAD_DIM]

    scale = 1.0 / math.sqrt(SWA_HEAD_DIM)
    scores = []
    for h in range(SWA_KV_HEADS):
        k_h = head_slice(k_chunks, h)
        q_h = jnp.concatenate(
            [head_slice(q_chunks, h * SWA_GROUP + g) for g in range(SWA_GROUP)],
            axis=0)
        scores.append(_dot_nt(q_h, k_h) * scale)
    weights, sink_terms = [], []
    for h in range(SWA_KV_HEADS):
        weights_h = []
        for g in range(SWA_GROUP):
            sink = sinks_ref[h * SWA_GROUP + g]
            s_g = jnp.where(valid, scores[h][g * BLOCK:(g + 1) * BLOCK], NEG_INF)
            m = jnp.maximum(jnp.max(s_g, axis=-1, keepdims=True), sink)
            weights_h.append(jnp.exp(s_g - m).astype(BF16))
            sink_terms.append(jnp.exp(sink - m))
        weights.append(jnp.concatenate(weights_h, axis=0))
    ones = jnp.ones((2 * BLOCK, SWA_HEAD_DIM), BF16)
    for h in range(SWA_KV_HEADS):
        v_h = v_all[:, h * SWA_HEAD_DIM:(h + 1) * SWA_HEAD_DIM]
        both = jnp.dot(weights[h], jnp.concatenate([v_h, ones], axis=1),
                       preferred_element_type=F32)
        for g in range(SWA_GROUP):
            rows = slice(g * BLOCK, (g + 1) * BLOCK)
            den = both[rows, SWA_HEAD_DIM:SWA_HEAD_DIM + 1] + sink_terms[h * SWA_GROUP + g]
            col = (h * SWA_GROUP + g) * SWA_HEAD_DIM
            o_ref[:, col:col + SWA_HEAD_DIM] = both[rows, :SWA_HEAD_DIM] / den


def _swa_attention(proj, pos2, inv_freq, sinks, batch, seq):
    nb = seq // BLOCK
    assert Q_A_COL % SWA_WIDTH == 0 and K_A_COL % SWA_KV_WIDTH == 0
    assert V_A_COL % SWA_KV_WIDTH == 0
    qcol = Q_A_COL // SWA_WIDTH
    kcol = K_A_COL // SWA_KV_WIDTH
    vcol = V_A_COL // SWA_KV_WIDTH

    def cur(b, n):
        return b * nb + n

    def prev(b, n):
        return b * nb + jnp.maximum(n - 1, 0)

    return pl.pallas_call(
        _swa_kernel,
        out_shape=jax.ShapeDtypeStruct((batch * seq, SWA_WIDTH), F32),
        grid=(batch, nb),
        in_specs=[
            pl.BlockSpec(memory_space=pltpu.SMEM),
            pl.BlockSpec((BLOCK, SWA_WIDTH), lambda b, n: (cur(b, n), qcol)),
            pl.BlockSpec((BLOCK, SWA_KV_WIDTH), lambda b, n: (cur(b, n), kcol)),
            pl.BlockSpec((BLOCK, SWA_KV_WIDTH), lambda b, n: (cur(b, n), vcol)),
            pl.BlockSpec((BLOCK, SWA_KV_WIDTH), lambda b, n: (prev(b, n), vcol)),
            pl.BlockSpec((BLOCK, 1), lambda b, n: (cur(b, n), 0)),
            pl.BlockSpec((1, LANES), lambda b, n: (0, 0)),
        ],
        out_specs=pl.BlockSpec((BLOCK, SWA_WIDTH), lambda b, n: (cur(b, n), 0)),
        scratch_shapes=[pltpu.VMEM((BLOCK, SWA_KV_WIDTH), BF16)],
        compiler_params=_cparams(("arbitrary", "arbitrary")),
        name="swa_attention",
    )(sinks, proj, proj, proj, proj, pos2, inv_freq)


def _sb_kernel(q_ref, k_ref, v_ref, o_ref):
    n = pl.program_id(2)
    heads = [slice(h * SB_HEAD_DIM, (h + 1) * SB_HEAD_DIM)
             for h in range(SB_HEADS_PER_STEP)]
    rows = SB_HEADS_PER_STEP * BLOCK
    scale = 1.0 / math.sqrt(SB_HEAD_DIM)
    key_j = lax.broadcasted_iota(I32, (BLOCK, 2 * BLOCK), 0)
    out_c = lax.broadcasted_iota(I32, (BLOCK, 2 * BLOCK), 1)
    later_total = ((key_j > out_c) | (out_c >= BLOCK)).astype(BF16)
    q_row = lax.broadcasted_iota(I32, (rows, BLOCK), 0) % BLOCK
    k_col = lax.broadcasted_iota(I32, (rows, BLOCK), 1)
    causal = k_col < q_row

    def block(kb, carry, acc, mask):
        start = pl.multiple_of(kb * BLOCK, BLOCK)
        z = jnp.concatenate(
            [_dot_nt(q_ref[:, hd], k_ref[pl.ds(start, BLOCK), hd]) for hd in heads],
            axis=0) * scale
        t = jnp.log(1.0 + jnp.exp(-jnp.abs(z)))
        log_not = jnp.minimum(-z, 0.0) - t
        log_beta = jnp.minimum(z, 0.0) - t
        if mask is not None:
            log_not = jnp.where(mask, log_not, 0.0)
        terms, rest = [], log_not
        for _ in range(SUFFIX_TERMS):
            terms.append(rest.astype(BF16))
            rest = rest - terms[-1].astype(F32)
        parts = jnp.dot(jnp.concatenate(terms, axis=0), later_total,
                        preferred_element_type=F32)
        sums = parts[:rows]
        for t in range(1, SUFFIX_TERMS):
            sums = sums + parts[t * rows:(t + 1) * rows]
        a = jnp.exp(log_beta + sums[:, :BLOCK] + carry)
        if mask is not None:
            a = jnp.where(mask, a, 0.0)
        a = a.astype(BF16)
        pv = jnp.concatenate(
            [jnp.dot(a[h * BLOCK:(h + 1) * BLOCK], v_ref[pl.ds(start, BLOCK), hd],
                     preferred_element_type=F32) for h, hd in enumerate(heads)], axis=0)
        return carry + sums[:, BLOCK:], acc + pv

    zeros = jnp.zeros((rows, BLOCK), F32)
    carry, acc = block(n, zeros, zeros, causal)

    def cond(state):
        kb, carry, _ = state
        return jnp.logical_and(kb >= 0, jnp.max(carry) > EXP_ZERO_LOG)

    def body(state):
        kb, carry, acc = state
        carry, acc = block(kb, carry, acc, None)
        return kb - 1, carry, acc

    _, _, acc = lax.while_loop(cond, body, (n - 1, carry, acc))
    for h, hd in enumerate(heads):
        o_ref[:, hd] = acc[h * BLOCK:(h + 1) * BLOCK]


def _sb_attention(proj, batch, seq):
    nb = seq // BLOCK
    width = SB_HEADS_PER_STEP * SB_HEAD_DIM
    assert Q_B_COL % width == 0 and K_B_COL % width == 0 and V_B_COL % width == 0
    qcol = Q_B_COL // width
    kcol = K_B_COL // width
    vcol = V_B_COL // width
    return pl.pallas_call(
        _sb_kernel,
        out_shape=jax.ShapeDtypeStruct((batch * seq, SB_WIDTH), F32),
        grid=(batch, SB_HEADS // SB_HEADS_PER_STEP, nb),
        in_specs=[
            pl.BlockSpec((BLOCK, width), lambda b, h, n: (b * nb + n, qcol + h)),
            pl.BlockSpec((seq, width), lambda b, h, n: (b, kcol + h),
                         pipeline_mode=pl.Buffered(1)),
            pl.BlockSpec((seq, width), lambda b, h, n: (b, vcol + h),
                         pipeline_mode=pl.Buffered(1)),
        ],
        out_specs=pl.BlockSpec((BLOCK, width), lambda b, h, n: (b * nb + n, h)),
        compiler_params=_cparams(("arbitrary", "arbitrary", "arbitrary")),
        name="sb_attention",
    )(proj, proj, proj)


def _row_stats(chunks_ref, n_chunks, width):
    total = chunks_ref[0].sum(axis=-1, keepdims=True)
    for c in range(1, n_chunks):
        total = total + chunks_ref[c].sum(axis=-1, keepdims=True)
    mu = total / width
    sq = jnp.square(chunks_ref[0] - mu).sum(axis=-1, keepdims=True)
    for c in range(1, n_chunks):
        sq = sq + jnp.square(chunks_ref[c] - mu).sum(axis=-1, keepdims=True)
    return mu, lax.rsqrt(sq / width + LN_EPS)


def _outproj_kernel(oa_ref, ob_ref, ga_ref, gb_ref, wa_ref, wb_ref, x_ref, bo_ref,
                    lg_ref, lb_ref, y_ref, mu_ref, rs_ref, x1s_ref,
                    ma_ref, mb_ref, acc_ref):
    j = pl.program_id(1)
    n_chunks = acc_ref.shape[0]
    tn = acc_ref.shape[2]

    @pl.when(j == 0)
    def _():
        for o_ref, g_ref, m_ref in ((oa_ref, ga_ref, ma_ref), (ob_ref, gb_ref, mb_ref)):
            width = o_ref.shape[1]
            cols = [slice(c * tn, (c + 1) * tn) for c in range(width // tn)]
            sq = sum(jnp.square(o_ref[:, sl]).sum(axis=-1, keepdims=True) for sl in cols)
            r = lax.rsqrt(sq / width + RMS_EPS)
            for sl in cols:
                m_ref[:, sl] = (o_ref[:, sl] * r * g_ref[:, sl]).astype(BF16)

    mix = (jnp.dot(ma_ref[...], wa_ref[...], preferred_element_type=F32)
           + jnp.dot(mb_ref[...], wb_ref[...], preferred_element_type=F32))
    y = DEEPNORM_ALPHA * x_ref[...] + (mix + bo_ref[...])
    y_ref[...] = y
    acc_ref[j] = y

    @pl.when(j == n_chunks - 1)
    def _():
        mu, rs = _row_stats(acc_ref, n_chunks, n_chunks * tn)
        mu_ref[...] = mu
        rs_ref[...] = rs

        def normed(c):
            sl = slice(c * tn, (c + 1) * tn)
            return (acc_ref[c] - mu) * rs * lg_ref[:, sl] + lb_ref[:, sl]

        tm = acc_ref.shape[1]
        for c in range(n_chunks // 2):
            words = _pack_halves(normed(c), normed(c + n_chunks // 2))
            for q in range(tn // LANES):
                x1s_ref[pl.ds(c * (tn // LANES) + q, tm, stride=WORD_ROWS), :] = (
                    words[:, q * LANES:(q + 1) * LANES])


def _out_projection(o_a, o_b, g_a, g_b, w_b, x2, b_out, ln_g, ln_b):
    m, d = x2.shape
    tm, tn = TM_PROJ, TN_PROJ
    half = o_a.shape[1]
    once = dict(pipeline_mode=pl.Buffered(1))
    return pl.pallas_call(
        _outproj_kernel,
        out_shape=(jax.ShapeDtypeStruct((m, d), F32),
                   jax.ShapeDtypeStruct((m, 1), F32),
                   jax.ShapeDtypeStruct((m, 1), F32),
                   jax.ShapeDtypeStruct((m * WORD_ROWS, LANES), U32)),
        grid=(m // tm, d // tn),
        in_specs=[
            pl.BlockSpec((tm, half), lambda i, j: (i, 0), **once),
            pl.BlockSpec((tm, half), lambda i, j: (i, 0), **once),
            pl.BlockSpec((1, half), lambda i, j: (0, 0)),
            pl.BlockSpec((1, half), lambda i, j: (0, 0)),
            pl.BlockSpec((half, tn), lambda i, j: (0, j)),
            pl.BlockSpec((half, tn), lambda i, j: (1, j)),
            pl.BlockSpec((tm, tn), lambda i, j: (i, j)),
            pl.BlockSpec((1, tn), lambda i, j: (0, j)),
            pl.BlockSpec((1, d), lambda i, j: (0, 0)),
            pl.BlockSpec((1, d), lambda i, j: (0, 0)),
        ],
        out_specs=(pl.BlockSpec((tm, tn), lambda i, j: (i, j)),
                   pl.BlockSpec((tm, 1), lambda i, j: (i, 0)),
                   pl.BlockSpec((tm, 1), lambda i, j: (i, 0)),
                   pl.BlockSpec((tm * WORD_ROWS, LANES), lambda i, j: (i, 0))),
        scratch_shapes=[pltpu.VMEM((tm, half), BF16), pltpu.VMEM((tm, half), BF16),
                        pltpu.VMEM((d // tn, tm, tn), F32)],
        compiler_params=_cparams(("arbitrary", "arbitrary")),
        name="out_projection_ln1",
    )(o_a, o_b, g_a, g_b, w_b, w_b, x2, b_out, ln_g, ln_b)


ROUTER_COLS = LANES
EXPERT_LANE0 = N_GROUPS


def _router_kernel(x_ref, w_ref, b_ref, ids_ref, wts_ref, cnt_ref, carry_ref):
    i = pl.program_id(0)
    tm = wts_ref.shape[0]

    @pl.when(i == 0)
    def _():
        carry_ref[...] = jnp.zeros_like(carry_ref)

    logits = _dot_slab(x_ref, tm, [w_ref])[0] + b_ref[...]
    lane = lax.broadcasted_iota(I32, (tm, ROUTER_COLS), 1)
    big = jnp.int32(ROUTER_COLS)

    def first_argmax(vals):
        top = jnp.max(vals, axis=-1, keepdims=True)
        idx = jnp.min(jnp.where(vals == top, lane, big), axis=-1, keepdims=True)
        return top, idx

    is_group = lane < N_GROUPS
    g_logits = jnp.where(is_group, logits, -jnp.inf)
    g_top, g_idx = first_argmax(g_logits)
    g_w = 1.0 / jnp.sum(jnp.exp(g_logits - g_top), axis=-1, keepdims=True)

    first = EXPERT_LANE0 + g_idx * EXPERTS_PER_GROUP
    in_group = (lane >= first) & (lane < first + EXPERTS_PER_GROUP)
    e_logits = jnp.where(in_group, logits, -jnp.inf)
    top1, idx1 = first_argmax(e_logits)
    top2, idx2 = first_argmax(jnp.where(lane == idx1, -jnp.inf, e_logits))
    e2 = jnp.exp(top2 - top1)
    w1 = g_w / (1.0 + e2)
    w2 = g_w * e2 / (1.0 + e2)

    hit1 = lane == idx1
    hit2 = lane == idx2
    onehot = (hit1 | hit2).astype(BF16)
    r = lax.broadcasted_iota(I32, (tm, tm), 0)
    c = lax.broadcasted_iota(I32, (tm, tm), 1)
    before = (c < r).astype(BF16)
    prior = jnp.dot(before, onehot, preferred_element_type=F32) + carry_ref[0:1, :]
    rank1 = jnp.sum(jnp.where(hit1, prior, 0.0), axis=-1, keepdims=True)
    rank2 = jnp.sum(jnp.where(hit2, prior, 0.0), axis=-1, keepdims=True)
    counts = carry_ref[0:1, :] + jnp.sum(onehot.astype(F32), axis=0, keepdims=True)
    carry_ref[...] = jnp.broadcast_to(counts, carry_ref.shape)
    cnt_ref[...] = jnp.broadcast_to(counts, cnt_ref.shape).astype(I32)

    ids = jnp.where(lane == 0, idx1 - EXPERT_LANE0,
          jnp.where(lane == 1, idx2 - EXPERT_LANE0,
          jnp.where(lane == 2, rank1.astype(I32),
          jnp.where(lane == 3, rank2.astype(I32), 0))))
    ids_ref[...] = ids.T[:ids_ref.shape[0]]
    wts_ref[...] = jnp.where(lane == 0, w1, jnp.where(lane == 1, w2, 0.0))


def _router(x1s, w_r, b_r):
    m = x1s.shape[0] // WORD_ROWS
    tm = TM_PROJ
    return pl.pallas_call(
        _router_kernel,
        out_shape=(jax.ShapeDtypeStruct((SUBLANES, m), I32),
                   jax.ShapeDtypeStruct((m, ROUTER_COLS), F32),
                   jax.ShapeDtypeStruct((SUBLANES, ROUTER_COLS), I32)),
        grid=(m // tm,),
        in_specs=[
            pl.BlockSpec((tm * WORD_ROWS, LANES), lambda i: (i, 0)),
            pl.BlockSpec((D_MODEL, ROUTER_COLS), lambda i: (0, 0)),
            pl.BlockSpec((1, ROUTER_COLS), lambda i: (0, 0)),
        ],
        out_specs=(pl.BlockSpec((SUBLANES, tm), lambda i: (0, i)),
                   pl.BlockSpec((tm, ROUTER_COLS), lambda i: (i, 0)),
                   pl.BlockSpec((SUBLANES, ROUTER_COLS), lambda i: (0, 0))),
        scratch_shapes=[pltpu.VMEM((SUBLANES, ROUTER_COLS), F32)],
        compiler_params=_cparams(("arbitrary",)),
        name="router",
    )(x1s, w_r, b_r)


def _start_row_gather(idx_ref, first, n_items, span, src_ref, dst_ref, sem):
    def body(c, _):
        for u in range(ROW_CHUNK):
            r = c * ROW_CHUNK + u
            src_row = pl.multiple_of(idx_ref[first + r], span)
            dst_row = pl.multiple_of(r * span, span)
            pltpu.make_async_copy(src_ref.at[pl.ds(src_row, span)],
                                  dst_ref.at[pl.ds(dst_row, span)], sem).start()
        return 0

    lax.fori_loop(0, n_items // ROW_CHUNK, body, 0)


def _wait_row_gather(n_items, span, src_ref, dst_ref, sem):
    for _ in range(n_items):
        pltpu.make_async_copy(src_ref.at[pl.ds(0, span)], dst_ref.at[pl.ds(0, span)],
                              sem).wait()


def _expert_changed(te_ref, i):
    return jnp.logical_or(i == 0, te_ref[i] != te_ref[jnp.maximum(i - 1, 0)])


def _tile_row(i, nu):
    return jnp.minimum(i, nu[0] - 1)


def _stage_expert_weights(i, te_ref, nxt_ref, ws_ref, w_hbms, stage_ref, bf_refs, sem):
    expert = te_ref[i]
    slot = ws_ref[expert]

    def copies(e, s):
        return [pltpu.make_async_copy(w.at[e], stage_ref.at[s, n], sem.at[s])
                for n, w in enumerate(w_hbms)]

    @pl.when(i == 0)
    def _():
        for c in copies(expert, slot):
            c.start(priority=WEIGHT_DMA_PRIORITY)

    @pl.when(_expert_changed(te_ref, i))
    def _():
        for c in copies(expert, slot):
            c.wait()
        nxt = nxt_ref[expert]

        @pl.when(nxt < N_EXPERTS)
        def _():
            for c in copies(nxt, 1 - slot):
                c.start(priority=WEIGHT_DMA_PRIORITY)

        for n, bf_ref in enumerate(bf_refs):
            bf_ref[...] = stage_ref[slot, n].astype(BF16)


def _gateup_kernel(te_ref, nu_ref, tos_ref, nxt_ref, ws_ref, x_hbm, wg_hbm, wu_hbm, a_ref,
                   xbuf_ref, stage_ref, wgb_ref, wub_ref, sems, wsem):
    i = pl.program_id(0)
    n_used = nu_ref[0]
    slot = lax.rem(i, 2)

    def start(tile, buf):
        _start_row_gather(tos_ref, tile * TM_MOE, TM_MOE, WORD_ROWS, x_hbm,
                          xbuf_ref.at[buf], sems.at[buf])

    @pl.when(i == 0)
    def _():
        start(0, 0)

    @pl.when(i + 1 < n_used)
    def _():
        start(i + 1, 1 - slot)

    _stage_expert_weights(i, te_ref, nxt_ref, ws_ref, [wg_hbm, wu_hbm], stage_ref,
                          [wgb_ref, wub_ref], wsem)

    @pl.when(i < n_used)
    def _():
        _wait_row_gather(TM_MOE, WORD_ROWS, x_hbm, xbuf_ref.at[slot], sems.at[slot])
        gate, up = _dot_slab(xbuf_ref.at[slot], TM_MOE, [wgb_ref, wub_ref])
        a_ref[...] = (gate * jax.nn.sigmoid(gate) * up).astype(a_ref.dtype)

    @pl.when(i >= n_used)
    def _():
        a_ref[...] = jnp.zeros_like(a_ref)


def _grouped_gate_up(tile_expert, n_used, slab_of_slot, next_expert, stage_slot,
                     x1s, w_gate, w_up):
    p_rows = slab_of_slot.shape[0]
    _, d, f = w_gate.shape
    hbm = pl.BlockSpec(memory_space=pl.ANY)
    return pl.pallas_call(
        _gateup_kernel,
        out_shape=jax.ShapeDtypeStruct((p_rows, f), BF16),
        grid_spec=pltpu.PrefetchScalarGridSpec(
            num_scalar_prefetch=5,
            grid=(p_rows // TM_MOE,),
            in_specs=[hbm, hbm, hbm],
            out_specs=pl.BlockSpec((TM_MOE, f), lambda i, *_: (i, 0)),
            scratch_shapes=[pltpu.VMEM((2, TM_MOE * WORD_ROWS, LANES), U32),
                            pltpu.VMEM((2, 2, d, f), F32),
                            pltpu.VMEM((d, f), BF16), pltpu.VMEM((d, f), BF16),
                            pltpu.SemaphoreType.DMA((2,)), pltpu.SemaphoreType.DMA((2,))],
        ),
        compiler_params=_cparams(("arbitrary",)),
        name="moe_gate_up",
    )(tile_expert, n_used, slab_of_slot, next_expert, stage_slot, x1s, w_gate, w_up)


def _down_kernel(te_ref, nu_ref, nxt_ref, ws_ref, a_ref, wd_hbm, y_ref,
                 stage_ref, wdb_ref, wsem):
    i = pl.program_id(0)
    _stage_expert_weights(i, te_ref, nxt_ref, ws_ref, [wd_hbm], stage_ref, [wdb_ref], wsem)

    @pl.when(i < nu_ref[0])
    def _():
        y_ref[:, 0, :] = jnp.dot(a_ref[...], wdb_ref[...], preferred_element_type=F32)

    @pl.when(i >= nu_ref[0])
    def _():
        y_ref[...] = jnp.zeros_like(y_ref)


def _grouped_down(tile_expert, n_used, next_expert, stage_slot, act, w_down):
    p_rows, f = act.shape
    d = w_down.shape[2]
    return pl.pallas_call(
        _down_kernel,
        out_shape=jax.ShapeDtypeStruct((p_rows, 1, d), F32),
        grid_spec=pltpu.PrefetchScalarGridSpec(
            num_scalar_prefetch=4,
            grid=(p_rows // TM_MOE,),
            in_specs=[
                pl.BlockSpec((TM_MOE, f), lambda i, te, nu, *_: (_tile_row(i, nu), 0)),
                pl.BlockSpec(memory_space=pl.ANY),
            ],
            out_specs=pl.BlockSpec((TM_MOE, 1, d), lambda i, *_: (i, 0, 0)),
            scratch_shapes=[pltpu.VMEM((2, 1, f, d), F32), pltpu.VMEM((f, d), BF16),
                            pltpu.SemaphoreType.DMA((2,))],
        ),
        compiler_params=_cparams(("arbitrary",)),
        name="moe_down",
    )(tile_expert, n_used, next_expert, stage_slot, act, w_down)


def _combine_kernel(slot_ref, y_hbm, wts_ref, o_ref, buf_ref, sems):
    i = pl.program_id(0)
    tm = o_ref.shape[0]
    n_tokens = slot_ref.shape[0] // EXPERT_TOP_K
    slot = lax.rem(i, 2)

    def start(tile, buf):
        for k in range(EXPERT_TOP_K):
            _start_row_gather(slot_ref, k * n_tokens + tile * tm, tm, 1, y_hbm,
                              buf_ref.at[buf, k], sems.at[buf])

    @pl.when(i == 0)
    def _():
        start(0, 0)

    @pl.when(i + 1 < pl.num_programs(0))
    def _():
        start(i + 1, 1 - slot)

    _wait_row_gather(EXPERT_TOP_K * tm, 1, y_hbm, buf_ref.at[slot, 0], sems.at[slot])
    for c in range(o_ref.shape[1] // TN_PROJ):
        sl = slice(c * TN_PROJ, (c + 1) * TN_PROJ)
        o_ref[:, sl] = sum(wts_ref[:, k:k + 1] * buf_ref[slot, k, :, 0, sl]
                           for k in range(EXPERT_TOP_K))


def _combine_experts(slot_kt, y_sorted, wts):
    d = y_sorted.shape[2]
    m = slot_kt.shape[0] // EXPERT_TOP_K
    tm = TM_COMBINE
    return pl.pallas_call(
        _combine_kernel,
        out_shape=jax.ShapeDtypeStruct((m, d), F32),
        grid_spec=pltpu.PrefetchScalarGridSpec(
            num_scalar_prefetch=1,
            grid=(m // tm,),
            in_specs=[
                pl.BlockSpec(memory_space=pl.ANY),
                pl.BlockSpec((tm, ROUTER_COLS), lambda i, sl: (i, 0)),
            ],
            out_specs=pl.BlockSpec((tm, d), lambda i, sl: (i, 0)),
            scratch_shapes=[pltpu.VMEM((2, EXPERT_TOP_K, tm, 1, d), F32),
                            pltpu.SemaphoreType.DMA((2,))],
        ),
        compiler_params=_cparams(("arbitrary",)),
        name="moe_combine",
    )(slot_kt, y_sorted, wts)


def _final_kernel(x1s_ref, wg_ref, bg_ref, p_ref, wp_ref, y1_ref, mu_ref, rs_ref,
                  l1g_ref, l1b_ref, moe_ref, l2g_ref, l2b_ref,
                  o_ref, lo_ref, hi_ref, acc_ref, mu2_ref, rs2_ref):
    i = pl.program_id(0)
    j = pl.program_id(1)
    n_tiles = pl.num_programs(0) - 1
    n_chunks = acc_ref.shape[1]
    tn = acc_ref.shape[3]
    cur = lax.rem(i, 2)
    prv = 1 - cur

    @pl.when(jnp.logical_and(j == 0, i < n_tiles))
    def _():
        for s in range(WORD_ROWS):
            sl = slice(s * LANES, (s + 1) * LANES)
            lo_ref[:, sl], hi_ref[:, sl] = _unpack_halves(
                x1s_ref[pl.ds(s, lo_ref.shape[0], stride=WORD_ROWS), :])

    def build():
        gate = _dot_halves(lo_ref[...], hi_ref[...], wg_ref) + bg_ref[...]
        emb = jnp.dot(p_ref[...].astype(BF16), wp_ref[...].astype(BF16),
                      preferred_element_type=F32)
        x1 = (y1_ref[...] - mu_ref[...]) * rs_ref[...] * l1g_ref[...] + l1b_ref[...]
        acc_ref[cur, j] = DEEPNORM_ALPHA * x1 + moe_ref[...] + jax.nn.sigmoid(gate) * emb

    def emit():
        o_ref[...] = ((acc_ref[prv, j] - mu2_ref[prv]) * rs2_ref[prv]
                      * l2g_ref[...] + l2b_ref[...])

    @pl.when(i == 0)
    def _():
        build()

    @pl.when(jnp.logical_and(i > 0, i < n_tiles))
    def _():
        emit()
        build()

    @pl.when(i == n_tiles)
    def _():
        emit()

    @pl.when(jnp.logical_and(j == n_chunks - 1, i < n_tiles))
    def _():
        mu2_ref[cur], rs2_ref[cur] = _row_stats(acc_ref.at[cur], n_chunks, n_chunks * tn)


def _final_stage(x1s, w_pg_b, b_pg, p2, w_pp, y1, mu1, rs1, ln1_g, ln1_b,
                 moe, ln2_g, ln2_b):
    m, d = y1.shape
    tm, tn = TM_PROJ, TN_PROJ
    ple = p2.shape[1]
    n_chunks = d // tn

    n_tiles = m // tm

    def built(i):
        return jnp.minimum(i, n_tiles - 1)

    def emitted(i):
        return jnp.maximum(i - 1, 0)

    return pl.pallas_call(
        _final_kernel,
        out_shape=jax.ShapeDtypeStruct((m, d), F32),
        grid=(n_tiles + 1, n_chunks),
        in_specs=[
            pl.BlockSpec((tm * WORD_ROWS, LANES), lambda i, j: (built(i), 0),
                         pipeline_mode=pl.Buffered(1)),
            pl.BlockSpec((d, tn), lambda i, j: (0, j)),
            pl.BlockSpec((1, tn), lambda i, j: (0, j)),
            pl.BlockSpec((tm, ple), lambda i, j: (built(i), 0)),
            pl.BlockSpec((ple, tn), lambda i, j: (0, j)),
            pl.BlockSpec((tm, tn), lambda i, j: (built(i), j)),
            pl.BlockSpec((tm, 1), lambda i, j: (built(i), 0)),
            pl.BlockSpec((tm, 1), lambda i, j: (built(i), 0)),
            pl.BlockSpec((1, tn), lambda i, j: (0, j)),
            pl.BlockSpec((1, tn), lambda i, j: (0, j)),
            pl.BlockSpec((tm, tn), lambda i, j: (built(i), j)),
            pl.BlockSpec((1, tn), lambda i, j: (0, j)),
            pl.BlockSpec((1, tn), lambda i, j: (0, j)),
        ],
        out_specs=pl.BlockSpec((tm, tn), lambda i, j: (emitted(i), jnp.where(i == 0, 0, j))),
        scratch_shapes=[pltpu.VMEM((tm, d // 2), BF16), pltpu.VMEM((tm, d // 2), BF16),
                        pltpu.VMEM((2, n_chunks, tm, tn), F32),
                        pltpu.VMEM((2, tm, 1), F32), pltpu.VMEM((2, tm, 1), F32)],
        compiler_params=_cparams(("arbitrary", "arbitrary")),
        name="ple_moe_ln2",
    )(x1s, w_pg_b, b_pg, p2, w_pp, y1, mu1, rs1, ln1_g, ln1_b, moe, ln2_g, ln2_b)


def _routing_tables(ids_t, counts_row):
    counts = counts_row[EXPERT_LANE0:EXPERT_LANE0 + N_EXPERTS]
    tiles = (counts + TM_MOE - 1) // TM_MOE
    tile_end = jnp.cumsum(tiles)
    offsets = (tile_end - tiles) * TM_MOE
    n_used = tile_end[-1:]
    n_tokens = ids_t.shape[1]
    experts = ids_t[0:EXPERT_TOP_K]
    hit = experts[None] == jnp.arange(N_EXPERTS, dtype=I32)[:, None, None]
    first_slot = jnp.sum(jnp.where(hit, offsets[:, None, None], 0), axis=0)
    slots = (first_slot + ids_t[EXPERT_TOP_K:2 * EXPERT_TOP_K]).reshape(-1)
    n_tiles = (EXPERT_TOP_K * n_tokens) // TM_MOE + N_EXPERTS
    tile_ids = jnp.minimum(jnp.arange(n_tiles, dtype=I32), n_used - 1)
    tile_expert = jnp.sum(tile_end[None, :] <= tile_ids[:, None], axis=1).astype(I32)
    slab_of_slot = jnp.zeros((n_tiles * TM_MOE,), I32).at[slots].set(
        jnp.arange(slots.size, dtype=I32) % n_tokens * WORD_ROWS, unique_indices=True)
    eid = jnp.arange(N_EXPERTS, dtype=I32)
    later_owner = (eid[None, :] > eid[:, None]) & (tiles[None, :] > 0)
    next_expert = jnp.min(jnp.where(later_owner, eid[None, :], N_EXPERTS), axis=1).astype(I32)
    stage_slot = ((jnp.cumsum(tiles > 0) - 1) % 2).astype(I32)
    return (slots.astype(I32), slab_of_slot, tile_expert, n_used.astype(I32),
            next_expert, stage_slot)


def kernel(x, p, positions, w_in, b_in, sinks, g_norm_a, g_norm_b, w_out, b_out,
           ln1_g, ln1_b, w_group, b_group, w_er, b_er, w_gate, w_up, w_down,
           w_ple_gate, b_ple_gate, w_ple_proj, ln2_g, ln2_b):
    batch, seq, d = x.shape
    m = batch * seq
    row = lambda v: v.reshape(1, -1)
    x2 = x.reshape(m, d)
    for i in range(DEPTH):
        proj = _in_projection(x2, w_in[i].astype(BF16), row(b_in[i]))
        inv_freq = ROPE_THETA ** (-jnp.arange(0, SWA_HEAD_DIM, 2, dtype=F32) / SWA_HEAD_DIM)
        inv_freq = jnp.tile(inv_freq, LANES // inv_freq.shape[0]).reshape(1, LANES)
        o_a = _swa_attention(proj, positions.reshape(m, 1), inv_freq, sinks[i], batch, seq)
        o_b = _sb_attention(proj, batch, seq)
        y1, mu1, rs1, x1s = _out_projection(
            o_a, o_b, row(g_norm_a[i]), row(g_norm_b[i]), w_out[i].astype(BF16), x2,
            row(b_out[i]), row(ln1_g[i]), row(ln1_b[i]))
        pad = ROUTER_COLS - N_GROUPS - N_EXPERTS
        w_r = jnp.concatenate(
            [w_group[i], w_er[i].transpose(1, 0, 2).reshape(d, N_EXPERTS),
             jnp.zeros((d, pad), F32)], axis=1).astype(BF16)
        b_r = jnp.concatenate([b_group[i], b_er[i].reshape(-1), jnp.zeros((pad,), F32)])
        ids, wts, counts = _router(x1s, w_r, row(b_r))
        (slots, slab_of_slot, tile_expert, n_used, next_expert,
         stage_slot) = _routing_tables(ids, counts[0])
        act = _grouped_gate_up(tile_expert, n_used, slab_of_slot, next_expert, stage_slot,
                               x1s, w_gate[i], w_up[i])
        y_sorted = _grouped_down(tile_expert, n_used, next_expert, stage_slot, act, w_down[i])
        moe = _combine_experts(slots, y_sorted, wts)
        x2 = _final_stage(x1s, w_ple_gate[i].astype(BF16), row(b_ple_gate[i]),
                          p[i].reshape(m, PLE_DIM), w_ple_proj[i], y1, mu1, rs1,
                          row(ln1_g[i]), row(ln1_b[i]), moe, row(ln2_g[i]), row(ln2_b[i]))
    return x2.reshape(batch, seq, d)
```

```python
import math

import jax
import jax.numpy as jnp
from jax import lax
from jax.experimental import pallas as pl
from jax.experimental.pallas import tpu as pltpu

F32 = jnp.float32
BF16 = jnp.bfloat16
I32 = jnp.int32
U32 = jnp.uint32

D_MODEL = 4096
PLE_DIM = 256
BLOCK = 128
ROPE_THETA = 10000.0
LN_EPS = 1e-5
RMS_EPS = 1e-6
NEG_INF = -1e30

SWA_HEAD_DIM = 64
SWA_WIDTH = D_MODEL // 2
SWA_HEADS = SWA_WIDTH // SWA_HEAD_DIM
SWA_KV_HEADS = SWA_HEADS // 8
SWA_GROUP = SWA_HEADS // SWA_KV_HEADS
SWA_KV_WIDTH = SWA_KV_HEADS * SWA_HEAD_DIM

SB_HEAD_DIM = 128
SB_WIDTH = D_MODEL - SWA_WIDTH
SB_HEADS = SB_WIDTH // SB_HEAD_DIM

IN_WIDTH = SWA_WIDTH + 2 * SWA_KV_WIDTH + 3 * SB_WIDTH
A_WIDTH = SWA_WIDTH + 2 * SWA_KV_WIDTH
Q_B_COL = 0
K_B_COL = Q_B_COL + SB_WIDTH
V_B_COL = K_B_COL + SB_WIDTH
Q_A_COL = V_B_COL + SB_WIDTH
K_A_COL = Q_A_COL + SWA_WIDTH
V_A_COL = K_A_COL + SWA_KV_WIDTH

N_GROUPS = 4
EXPERTS_PER_GROUP = 8
N_EXPERTS = N_GROUPS * EXPERTS_PER_GROUP
EXPERT_TOP_K = 2
DEPTH = 1
DEEPNORM_ALPHA = (2.0 * DEPTH) ** 0.25

LANES = 128
SUBLANES = 8
VMEM_LIMIT_BYTES = 56 * 1024 * 1024

EXP_ZERO_LOG = -126 * math.log(2.0) - 2.0

TM_IN = 1024
TM_PROJ = 512
TN_PROJ = 512
TM_MOE = 256
TM_COMBINE = 256
SB_HEADS_PER_STEP = 16
SUFFIX_TERMS = 2
ROW_CHUNK = 16
WEIGHT_DMA_PRIORITY = 1


def _cparams(sem):
    return pltpu.CompilerParams(dimension_semantics=sem,
                                vmem_limit_bytes=VMEM_LIMIT_BYTES)


def _pack_halves(lo, hi):
    return lax.bitcast_convert_type(
        pltpu.pack_elementwise([lo, hi], packed_dtype=BF16), U32)


def _unpack_halves(words):
    return tuple(
        pltpu.unpack_elementwise(words, index=k, packed_dtype=BF16,
                                 unpacked_dtype=F32).astype(BF16) for k in (0, 1))


def _dot_halves(lo, hi, w_ref):
    half = lo.shape[1]
    return (jnp.dot(lo, w_ref[:half], preferred_element_type=F32)
            + jnp.dot(hi, w_ref[half:], preferred_element_type=F32))


WORD_ROWS = (D_MODEL // 2) // LANES
SLAB_PAIR = 2 * LANES


def _slab_words(slab_ref, pair, tokens):
    return jnp.concatenate(
        [slab_ref[pl.ds(2 * pair + k, tokens, stride=WORD_ROWS), :] for k in (0, 1)],
        axis=1)


def _dot_slab(slab_ref, tokens, w_refs):
    half = D_MODEL // 2
    outs = [None] * len(w_refs)
    for pair in range(half // SLAB_PAIR):
        lo, hi = _unpack_halves(_slab_words(slab_ref, pair, tokens))
        rows_lo = slice(pair * SLAB_PAIR, (pair + 1) * SLAB_PAIR)
        rows_hi = slice(half + pair * SLAB_PAIR, half + (pair + 1) * SLAB_PAIR)
        for n, w_ref in enumerate(w_refs):
            part = (jnp.dot(lo, w_ref[rows_lo], preferred_element_type=F32)
                    + jnp.dot(hi, w_ref[rows_hi], preferred_element_type=F32))
            outs[n] = part if outs[n] is None else outs[n] + part
    return outs


def _dot_nt(a, b):
    return lax.dot_general(a, b, (((1,), (1,)), ((), ())),
                           preferred_element_type=F32)


def _inproj_kernel(x_ref, w_ref, b_ref, o_ref, xb_ref):
    @pl.when(pl.program_id(1) == 0)
    def _():
        xb_ref[...] = x_ref[...].astype(BF16)

    acc = jnp.dot(xb_ref[...], w_ref[...], preferred_element_type=F32)
    o_ref[...] = (acc + b_ref[...]).astype(o_ref.dtype)


def _in_projection(x2, w_b, b):
    m, k = x2.shape
    n = w_b.shape[1]
    n_blocks = n // TN_PROJ
    assert A_WIDTH % TN_PROJ == 0
    a_blocks = A_WIDTH // TN_PROJ

    def src(j):
        return lax.rem(j + a_blocks, n_blocks)

    return pl.pallas_call(
        _inproj_kernel,
        out_shape=jax.ShapeDtypeStruct((m, n), BF16),
        grid=(m // TM_IN, n_blocks),
        in_specs=[
            pl.BlockSpec((TM_IN, k), lambda i, j: (i, 0)),
            pl.BlockSpec((k, TN_PROJ), lambda i, j: (0, src(j))),
            pl.BlockSpec((1, TN_PROJ), lambda i, j: (0, src(j))),
        ],
        out_specs=pl.BlockSpec((TM_IN, TN_PROJ), lambda i, j: (i, j)),
        scratch_shapes=[pltpu.VMEM((TM_IN, k), BF16)],
        compiler_params=_cparams(("arbitrary", "arbitrary")),
        name="in_projection",
    )(x2, w_b, b)


def _swa_kernel(sinks_ref, q_ref, kc_ref, vc_ref, vp_ref,
                posc_ref, invf_ref, o_ref, kprev_ref):
    n = pl.program_id(1)

    @pl.when(n == 0)
    def _():
        kprev_ref[...] = jnp.zeros_like(kprev_ref)
    lane = lax.broadcasted_iota(I32, (1, LANES), 1)
    first_half = (lane % SWA_HEAD_DIM) < (SWA_HEAD_DIM // 2)

    def tables(pos_ref):
        ang = pos_ref[...].astype(F32) * invf_ref[...]
        sin = jnp.sin(ang)
        return jnp.cos(ang), jnp.where(first_half, -sin, sin)

    def rope(x, cos, sin_signed):
        partner = jnp.where(first_half,
                            pltpu.roll(x, LANES - SWA_HEAD_DIM // 2, 1),
                            pltpu.roll(x, SWA_HEAD_DIM // 2, 1))
        return x * cos + partner * sin_signed

    cos_c, sin_c = tables(posc_ref)

    def rope_block(ref, cos, sin_signed):
        width = ref.shape[1]
        return [rope(ref[:, c * LANES:(c + 1) * LANES].astype(F32), cos,
                     sin_signed).astype(BF16) for c in range(width // LANES)]

    q_chunks = rope_block(q_ref, cos_c, sin_c)
    k_cur = rope_block(kc_ref, cos_c, sin_c)
    k_chunks = [jnp.concatenate([kprev_ref[:, c * LANES:(c + 1) * LANES], k_c], axis=0)
                for c, k_c in enumerate(k_cur)]
    for c, k_c in enumerate(k_cur):
        kprev_ref[:, c * LANES:(c + 1) * LANES] = k_c
    v_all = jnp.concatenate([vp_ref[...], vc_ref[...]], axis=0)

    qi = lax.broadcasted_iota(I32, (BLOCK, 2 * BLOCK), 0)
    kj = lax.broadcasted_iota(I32, (BLOCK, 2 * BLOCK), 1)
    rel = qi - (kj - BLOCK)
    valid = (rel >= 0) & (rel < BLOCK) & ((kj >= BLOCK) | (n > 0))

    def head_slice(chunks, head):
        half = head % 2
        return chunks[head // 2][:, half * SWA_HEAD_DIM:(half + 1) * SWA_HEAD_DIM]

    scale = 1.0 / math.sqrt(SWA_HEAD_DIM)
    scores = []
    for h in range(SWA_KV_HEADS):
        k_h = head_slice(k_chunks, h)
        q_h = jnp.concatenate(
            [head_slice(q_chunks, h * SWA_GROUP + g) for g in range(SWA_GROUP)],
            axis=0)
        scores.append(_dot_nt(q_h, k_h) * scale)
    weights, sink_terms = [], []
    for h in range(SWA_KV_HEADS):
        weights_h = []
        for g in range(SWA_GROUP):
            sink = sinks_ref[h * SWA_GROUP + g]
            s_g = jnp.where(valid, scores[h][g * BLOCK:(g + 1) * BLOCK], NEG_INF)
            m = jnp.maximum(jnp.max(s_g, axis=-1, keepdims=True), sink)
            weights_h.append(jnp.exp(s_g - m).astype(BF16))
            sink_terms.append(jnp.exp(sink - m))
        weights.append(jnp.concatenate(weights_h, axis=0))
    ones = jnp.ones((2 * BLOCK, SWA_HEAD_DIM), BF16)
    for h in range(SWA_KV_HEADS):
        v_h = v_all[:, h * SWA_HEAD_DIM:(h + 1) * SWA_HEAD_DIM]
        both = jnp.dot(weights[h], jnp.concatenate([v_h, ones], axis=1),
                       preferred_element_type=F32)
        for g in range(SWA_GROUP):
            rows = slice(g * BLOCK, (g + 1) * BLOCK)
            den = both[rows, SWA_HEAD_DIM:SWA_HEAD_DIM + 1] + sink_terms[h * SWA_GROUP + g]
            col = (h * SWA_GROUP + g) * SWA_HEAD_DIM
            o_ref[:, col:col + SWA_HEAD_DIM] = both[rows, :SWA_HEAD_DIM] / den


def _swa_attention(proj, pos2, inv_freq, sinks, batch, seq):
    nb = seq // BLOCK
    assert Q_A_COL % SWA_WIDTH == 0 and K_A_COL % SWA_KV_WIDTH == 0
    assert V_A_COL % SWA_KV_WIDTH == 0
    qcol = Q_A_COL // SWA_WIDTH
    kcol = K_A_COL // SWA_KV_WIDTH
    vcol = V_A_COL // SWA_KV_WIDTH

    def cur(b, n):
        return b * nb + n

    def prev(b, n):
        return b * nb + jnp.maximum(n - 1, 0)

    return pl.pallas_call(
        _swa_kernel,
        out_shape=jax.ShapeDtypeStruct((batch * seq, SWA_WIDTH), F32),
        grid=(batch, nb),
        in_specs=[
            pl.BlockSpec(memory_space=pltpu.SMEM),
            pl.BlockSpec((BLOCK, SWA_WIDTH), lambda b, n: (cur(b, n), qcol)),
            pl.BlockSpec((BLOCK, SWA_KV_WIDTH), lambda b, n: (cur(b, n), kcol)),
            pl.BlockSpec((BLOCK, SWA_KV_WIDTH), lambda b, n: (cur(b, n), vcol)),
            pl.BlockSpec((BLOCK, SWA_KV_WIDTH), lambda b, n: (prev(b, n), vcol)),
            pl.BlockSpec((BLOCK, 1), lambda b, n: (cur(b, n), 0)),
            pl.BlockSpec((1, LANES), lambda b, n: (0, 0)),
        ],
        out_specs=pl.BlockSpec((BLOCK, SWA_WIDTH), lambda b, n: (cur(b, n), 0)),
        scratch_shapes=[pltpu.VMEM((BLOCK, SWA_KV_WIDTH), BF16)],
        compiler_params=_cparams(("arbitrary", "arbitrary")),
        name="swa_attention",
    )(sinks, proj, proj, proj, proj, pos2, inv_freq)


def _sb_kernel(q_ref, k_ref, v_ref, o_ref):
    n = pl.program_id(2)
    heads = [slice(h * SB_HEAD_DIM, (h + 1) * SB_HEAD_DIM)
             for h in range(SB_HEADS_PER_STEP)]
    rows = SB_HEADS_PER_STEP * BLOCK
    scale = 1.0 / math.sqrt(SB_HEAD_DIM)
    key_j = lax.broadcasted_iota(I32, (BLOCK, 2 * BLOCK), 0)
    out_c = lax.broadcasted_iota(I32, (BLOCK, 2 * BLOCK), 1)
    later_total = ((key_j > out_c) | (out_c >= BLOCK)).astype(BF16)
    q_row = lax.broadcasted_iota(I32, (rows, BLOCK), 0) % BLOCK
    k_col = lax.broadcasted_iota(I32, (rows, BLOCK), 1)
    causal = k_col < q_row

    def block(kb, carry, acc, mask):
        start = pl.multiple_of(kb * BLOCK, BLOCK)
        z = jnp.concatenate(
            [_dot_nt(q_ref[:, hd], k_ref[pl.ds(start, BLOCK), hd]) for hd in heads],
            axis=0) * scale
        t = jnp.log(1.0 + jnp.exp(-jnp.abs(z)))
        log_not = jnp.minimum(-z, 0.0) - t
        log_beta = jnp.minimum(z, 0.0) - t
        if mask is not None:
            log_not = jnp.where(mask, log_not, 0.0)
        terms, rest = [], log_not
        for _ in range(SUFFIX_TERMS):
            terms.append(rest.astype(BF16))
            rest = rest - terms[-1].astype(F32)
        parts = jnp.dot(jnp.concatenate(terms, axis=0), later_total,
                        preferred_element_type=F32)
        sums = parts[:rows]
        for t in range(1, SUFFIX_TERMS):
            sums = sums + parts[t * rows:(t + 1) * rows]
        a = jnp.exp(log_beta + sums[:, :BLOCK] + carry)
        if mask is not None:
            a = jnp.where(mask, a, 0.0)
        a = a.astype(BF16)
        pv = jnp.concatenate(
            [jnp.dot(a[h * BLOCK:(h + 1) * BLOCK], v_ref[pl.ds(start, BLOCK), hd],
                     preferred_element_type=F32) for h, hd in enumerate(heads)], axis=0)
        return carry + sums[:, BLOCK:], acc + pv

    zeros = jnp.zeros((rows, BLOCK), F32)
    carry, acc = block(n, zeros, zeros, causal)

    def cond(state):
        kb, carry, _ = state
        return jnp.logical_and(kb >= 0, jnp.max(carry) > EXP_ZERO_LOG)

    def body(state):
        kb, carry, acc = state
        carry, acc = block(kb, carry, acc, None)
        return kb - 1, carry, acc

    _, _, acc = lax.while_loop(cond, body, (n - 1, carry, acc))
    for h, hd in enumerate(heads):
        o_ref[:, hd] = acc[h * BLOCK:(h + 1) * BLOCK]


def _sb_attention(proj, batch, seq):
    nb = seq // BLOCK
    width = SB_HEADS_PER_STEP * SB_HEAD_DIM
    assert Q_B_COL % width == 0 and K_B_COL % width == 0 and V_B_COL % width == 0
    qcol = Q_B_COL // width
    kcol = K_B_COL // width
    vcol = V_B_COL // width
    return pl.pallas_call(
        _sb_kernel,
        out_shape=jax.ShapeDtypeStruct((batch * seq, SB_WIDTH), F32),
        grid=(batch, SB_HEADS // SB_HEADS_PER_STEP, nb),
        in_specs=[
            pl.BlockSpec((BLOCK, width), lambda b, h, n: (b * nb + n, qcol + h)),
            pl.BlockSpec((seq, width), lambda b, h, n: (b, kcol + h),
                         pipeline_mode=pl.Buffered(1)),
            pl.BlockSpec((seq, width), lambda b, h, n: (b, vcol + h),
                         pipeline_mode=pl.Buffered(1)),
        ],
        out_specs=pl.BlockSpec((BLOCK, width), lambda b, h, n: (b * nb + n, h)),
        compiler_params=_cparams(("arbitrary", "arbitrary", "arbitrary")),
        name="sb_attention",
    )(proj, proj, proj)


def _row_stats(chunks_ref, n_chunks, width):
    total = chunks_ref[0].sum(axis=-1, keepdims=True)
    for c in range(1, n_chunks):
        total = total + chunks_ref[c].sum(axis=-1, keepdims=True)
    mu = total / width
    sq = jnp.square(chunks_ref[0] - mu).sum(axis=-1, keepdims=True)
    for c in range(1, n_chunks):
        sq = sq + jnp.square(chunks_ref[c] - mu).sum(axis=-1, keepdims=True)
    return mu, lax.rsqrt(sq / width + LN_EPS)


def _outproj_kernel(oa_ref, ob_ref, ga_ref, gb_ref, wa_ref, wb_ref, x_ref, bo_ref,
                    lg_ref, lb_ref, y_ref, mu_ref, rs_ref, x1s_ref,
                    ma_ref, mb_ref, acc_ref):
    j = pl.program_id(1)
    n_chunks = acc_ref.shape[0]
    tn = acc_ref.shape[2]

    @pl.when(j == 0)
    def _():
        for o_ref, g_ref, m_ref in ((oa_ref, ga_ref, ma_ref), (ob_ref, gb_ref, mb_ref)):
            width = o_ref.shape[1]
            cols = [slice(c * tn, (c + 1) * tn) for c in range(width // tn)]
            sq = sum(jnp.square(o_ref[:, sl]).sum(axis=-1, keepdims=True) for sl in cols)
            r = lax.rsqrt(sq / width + RMS_EPS)
            for sl in cols:
                m_ref[:, sl] = (o_ref[:, sl] * r * g_ref[:, sl]).astype(BF16)

    mix = (jnp.dot(ma_ref[...], wa_ref[...], preferred_element_type=F32)
           + jnp.dot(mb_ref[...], wb_ref[...], preferred_element_type=F32))
    y = DEEPNORM_ALPHA * x_ref[...] + (mix + bo_ref[...])
    y_ref[...] = y
    acc_ref[j] = y

    @pl.when(j == n_chunks - 1)
    def _():
        mu, rs = _row_stats(acc_ref, n_chunks, n_chunks * tn)
        mu_ref[...] = mu
        rs_ref[...] = rs

        def normed(c):
            sl = slice(c * tn, (c + 1) * tn)
            return (acc_ref[c] - mu) * rs * lg_ref[:, sl] + lb_ref[:, sl]

        tm = acc_ref.shape[1]
        for c in range(n_chunks // 2):
            words = _pack_halves(normed(c), normed(c + n_chunks // 2))
            for q in range(tn // LANES):
                x1s_ref[pl.ds(c * (tn // LANES) + q, tm, stride=WORD_ROWS), :] = (
                    words[:, q * LANES:(q + 1) * LANES])


def _out_projection(o_a, o_b, g_a, g_b, w_b, x2, b_out, ln_g, ln_b):
    m, d = x2.shape
    tm, tn = TM_PROJ, TN_PROJ
    half = o_a.shape[1]
    once = dict(pipeline_mode=pl.Buffered(1))
    return pl.pallas_call(
        _outproj_kernel,
        out_shape=(jax.ShapeDtypeStruct((m, d), F32),
                   jax.ShapeDtypeStruct((m, 1), F32),
                   jax.ShapeDtypeStruct((m, 1), F32),
                   jax.ShapeDtypeStruct((m * WORD_ROWS, LANES), U32)),
        grid=(m // tm, d // tn),
        in_specs=[
            pl.BlockSpec((tm, half), lambda i, j: (i, 0), **once),
            pl.BlockSpec((tm, half), lambda i, j: (i, 0), **once),
            pl.BlockSpec((1, half), lambda i, j: (0, 0)),
            pl.BlockSpec((1, half), lambda i, j: (0, 0)),
            pl.BlockSpec((half, tn), lambda i, j: (0, j)),
            pl.BlockSpec((half, tn), lambda i, j: (1, j)),
            pl.BlockSpec((tm, tn), lambda i, j: (i, j)),
            pl.BlockSpec((1, tn), lambda i, j: (0, j)),
            pl.BlockSpec((1, d), lambda i, j: (0, 0)),
            pl.BlockSpec((1, d), lambda i, j: (0, 0)),
        ],
        out_specs=(pl.BlockSpec((tm, tn), lambda i, j: (i, j)),
                   pl.BlockSpec((tm, 1), lambda i, j: (i, 0)),
                   pl.BlockSpec((tm, 1), lambda i, j: (i, 0)),
                   pl.BlockSpec((tm * WORD_ROWS, LANES), lambda i, j: (i, 0))),
        scratch_shapes=[pltpu.VMEM((tm, half), BF16), pltpu.VMEM((tm, half), BF16),
                        pltpu.VMEM((d // tn, tm, tn), F32)],
        compiler_params=_cparams(("arbitrary", "arbitrary")),
        name="out_projection_ln1",
    )(o_a, o_b, g_a, g_b, w_b, w_b, x2, b_out, ln_g, ln_b)


ROUTER_COLS = LANES
EXPERT_LANE0 = N_GROUPS


def _router_kernel(x_ref, w_ref, b_ref, ids_ref, wts_ref, cnt_ref, carry_ref):
    i = pl.program_id(0)
    tm = wts_ref.shape[0]

    @pl.when(i == 0)
    def _():
        carry_ref[...] = jnp.zeros_like(carry_ref)

    logits = _dot_slab(x_ref, tm, [w_ref])[0] + b_ref[...]
    lane = lax.broadcasted_iota(I32, (tm, ROUTER_COLS), 1)
    big = jnp.int32(ROUTER_COLS)

    def first_argmax(vals):
        top = jnp.max(vals, axis=-1, keepdims=True)
        idx = jnp.min(jnp.where(vals == top, lane, big), axis=-1, keepdims=True)
        return top, idx

    is_group = lane < N_GROUPS
    g_logits = jnp.where(is_group, logits, -jnp.inf)
    g_top, g_idx = first_argmax(g_logits)
    g_w = 1.0 / jnp.sum(jnp.exp(g_logits - g_top), axis=-1, keepdims=True)

    first = EXPERT_LANE0 + g_idx * EXPERTS_PER_GROUP
    in_group = (lane >= first) & (lane < first + EXPERTS_PER_GROUP)
    e_logits = jnp.where(in_group, logits, -jnp.inf)
    top1, idx1 = first_argmax(e_logits)
    top2, idx2 = first_argmax(jnp.where(lane == idx1, -jnp.inf, e_logits))
    e2 = jnp.exp(top2 - top1)
    w1 = g_w / (1.0 + e2)
    w2 = g_w * e2 / (1.0 + e2)

    hit1 = lane == idx1
    hit2 = lane == idx2
    onehot = (hit1 | hit2).astype(BF16)
    r = lax.broadcasted_iota(I32, (tm, tm), 0)
    c = lax.broadcasted_iota(I32, (tm, tm), 1)
    before = (c < r).astype(BF16)
    prior = jnp.dot(before, onehot, preferred_element_type=F32) + carry_ref[0:1, :]
    rank1 = jnp.sum(jnp.where(hit1, prior, 0.0), axis=-1, keepdims=True)
    rank2 = jnp.sum(jnp.where(hit2, prior, 0.0), axis=-1, keepdims=True)
    counts = carry_ref[0:1, :] + jnp.sum(onehot.astype(F32), axis=0, keepdims=True)
    carry_ref[...] = jnp.broadcast_to(counts, carry_ref.shape)
    cnt_ref[...] = jnp.broadcast_to(counts, cnt_ref.shape).astype(I32)

    ids = jnp.where(lane == 0, idx1 - EXPERT_LANE0,
          jnp.where(lane == 1, idx2 - EXPERT_LANE0,
          jnp.where(lane == 2, rank1.astype(I32),
          jnp.where(lane == 3, rank2.astype(I32), 0))))
    ids_ref[...] = ids.T[:ids_ref.shape[0]]
    wts_ref[...] = jnp.where(lane == 0, w1, jnp.where(lane == 1, w2, 0.0))


def _router(x1s, w_r, b_r):
    m = x1s.shape[0] // WORD_ROWS
    tm = TM_PROJ
    return pl.pallas_call(
        _router_kernel,
        out_shape=(jax.ShapeDtypeStruct((SUBLANES, m), I32),
                   jax.ShapeDtypeStruct((m, ROUTER_COLS), F32),
                   jax.ShapeDtypeStruct((SUBLANES, ROUTER_COLS), I32)),
        grid=(m // tm,),
        in_specs=[
            pl.BlockSpec((tm * WORD_ROWS, LANES), lambda i: (i, 0)),
            pl.BlockSpec((D_MODEL, ROUTER_COLS), lambda i: (0, 0)),
            pl.BlockSpec((1, ROUTER_COLS), lambda i: (0, 0)),
        ],
        out_specs=(pl.BlockSpec((SUBLANES, tm), lambda i: (0, i)),
                   pl.BlockSpec((tm, ROUTER_COLS), lambda i: (i, 0)),
                   pl.BlockSpec((SUBLANES, ROUTER_COLS), lambda i: (0, 0))),
        scratch_shapes=[pltpu.VMEM((SUBLANES, ROUTER_COLS), F32)],
        compiler_params=_cparams(("arbitrary",)),
        name="router",
    )(x1s, w_r, b_r)


def _start_row_gather(idx_ref, first, n_items, span, src_ref, dst_ref, sem):
    def body(c, _):
        for u in range(ROW_CHUNK):
            r = c * ROW_CHUNK + u
            src_row = pl.multiple_of(idx_ref[first + r], span)
            dst_row = pl.multiple_of(r * span, span)
            pltpu.make_async_copy(src_ref.at[pl.ds(src_row, span)],
                                  dst_ref.at[pl.ds(dst_row, span)], sem).start()
        return 0

    lax.fori_loop(0, n_items // ROW_CHUNK, body, 0)


def _wait_row_gather(n_items, span, src_ref, dst_ref, sem):
    def body(c, _):
        for _u in range(ROW_CHUNK):
            pltpu.make_async_copy(src_ref.at[pl.ds(0, span)], dst_ref.at[pl.ds(0, span)],
                                  sem).wait()
        return 0

    lax.fori_loop(0, n_items // ROW_CHUNK, body, 0)


def _expert_changed(te_ref, i):
    return jnp.logical_or(i == 0, te_ref[i] != te_ref[jnp.maximum(i - 1, 0)])


def _tile_row(i, nu):
    return jnp.minimum(i, nu[0] - 1)


def _stage_expert_weights(i, te_ref, nxt_ref, ws_ref, w_hbms, stage_ref, bf_refs, sem):
    expert = te_ref[i]
    slot = ws_ref[expert]

    def copies(e, s):
        return [pltpu.make_async_copy(w.at[e], stage_ref.at[s, n], sem.at[s])
                for n, w in enumerate(w_hbms)]

    @pl.when(i == 0)
    def _():
        for c in copies(expert, slot):
            c.start(priority=WEIGHT_DMA_PRIORITY)

    @pl.when(_expert_changed(te_ref, i))
    def _():
        for c in copies(expert, slot):
            c.wait()
        nxt = nxt_ref[expert]

        @pl.when(nxt < N_EXPERTS)
        def _():
            for c in copies(nxt, 1 - slot):
                c.start(priority=WEIGHT_DMA_PRIORITY)

        for n, bf_ref in enumerate(bf_refs):
            bf_ref[...] = stage_ref[slot, n].astype(BF16)


def _gateup_kernel(te_ref, nu_ref, tos_ref, nxt_ref, ws_ref, x_hbm, wg_hbm, wu_hbm, a_ref,
                   xbuf_ref, stage_ref, wgb_ref, wub_ref, sems, wsem):
    i = pl.program_id(0)
    n_used = nu_ref[0]
    slot = lax.rem(i, 2)

    def start(tile, buf):
        _start_row_gather(tos_ref, tile * TM_MOE, TM_MOE, WORD_ROWS, x_hbm,
                          xbuf_ref.at[buf], sems.at[buf])

    @pl.when(i == 0)
    def _():
        start(0, 0)

    @pl.when(i + 1 < n_used)
    def _():
        start(i + 1, 1 - slot)

    _stage_expert_weights(i, te_ref, nxt_ref, ws_ref, [wg_hbm, wu_hbm], stage_ref,
                          [wgb_ref, wub_ref], wsem)

    @pl.when(i < n_used)
    def _():
        _wait_row_gather(TM_MOE, WORD_ROWS, x_hbm, xbuf_ref.at[slot], sems.at[slot])
        gate, up = _dot_slab(xbuf_ref.at[slot], TM_MOE, [wgb_ref, wub_ref])
        a_ref[...] = (gate * jax.nn.sigmoid(gate) * up).astype(a_ref.dtype)

    @pl.when(i >= n_used)
    def _():
        a_ref[...] = jnp.zeros_like(a_ref)


def _grouped_gate_up(tile_expert, n_used, slab_of_slot, next_expert, stage_slot,
                     x1s, w_gate, w_up):
    p_rows = slab_of_slot.shape[0]
    _, d, f = w_gate.shape
    hbm = pl.BlockSpec(memory_space=pl.ANY)
    return pl.pallas_call(
        _gateup_kernel,
        out_shape=jax.ShapeDtypeStruct((p_rows, f), BF16),
        grid_spec=pltpu.PrefetchScalarGridSpec(
            num_scalar_prefetch=5,
            grid=(p_rows // TM_MOE,),
            in_specs=[hbm, hbm, hbm],
            out_specs=pl.BlockSpec((TM_MOE, f), lambda i, *_: (i, 0)),
            scratch_shapes=[pltpu.VMEM((2, TM_MOE * WORD_ROWS, LANES), U32),
                            pltpu.VMEM((2, 2, d, f), F32),
                            pltpu.VMEM((d, f), BF16), pltpu.VMEM((d, f), BF16),
                            pltpu.SemaphoreType.DMA((2,)), pltpu.SemaphoreType.DMA((2,))],
        ),
        compiler_params=_cparams(("arbitrary",)),
        name="moe_gate_up",
    )(tile_expert, n_used, slab_of_slot, next_expert, stage_slot, x1s, w_gate, w_up)


def _down_kernel(te_ref, nu_ref, nxt_ref, ws_ref, a_ref, wd_hbm, y_ref,
                 stage_ref, wdb_ref, wsem):
    i = pl.program_id(0)
    _stage_expert_weights(i, te_ref, nxt_ref, ws_ref, [wd_hbm], stage_ref, [wdb_ref], wsem)

    @pl.when(i < nu_ref[0])
    def _():
        y_ref[:, 0, :] = jnp.dot(a_ref[...], wdb_ref[...], preferred_element_type=F32)

    @pl.when(i >= nu_ref[0])
    def _():
        y_ref[...] = jnp.zeros_like(y_ref)


def _grouped_down(tile_expert, n_used, next_expert, stage_slot, act, w_down):
    p_rows, f = act.shape
    d = w_down.shape[2]
    return pl.pallas_call(
        _down_kernel,
        out_shape=jax.ShapeDtypeStruct((p_rows, 1, d), F32),
        grid_spec=pltpu.PrefetchScalarGridSpec(
            num_scalar_prefetch=4,
            grid=(p_rows // TM_MOE,),
            in_specs=[
                pl.BlockSpec((TM_MOE, f), lambda i, te, nu, *_: (_tile_row(i, nu), 0)),
                pl.BlockSpec(memory_space=pl.ANY),
            ],
            out_specs=pl.BlockSpec((TM_MOE, 1, d), lambda i, *_: (i, 0, 0)),
            scratch_shapes=[pltpu.VMEM((2, 1, f, d), F32), pltpu.VMEM((f, d), BF16),
                            pltpu.SemaphoreType.DMA((2,))],
        ),
        compiler_params=_cparams(("arbitrary",)),
        name="moe_down",
    )(tile_expert, n_used, next_expert, stage_slot, act, w_down)


def _combine_kernel(slot_ref, y_hbm, wts_ref, o_ref, buf_ref, sems):
    i = pl.program_id(0)
    tm = o_ref.shape[0]
    n_tokens = slot_ref.shape[0] // EXPERT_TOP_K
    slot = lax.rem(i, 2)

    def start(tile, buf):
        for k in range(EXPERT_TOP_K):
            _start_row_gather(slot_ref, k * n_tokens + tile * tm, tm, 1, y_hbm,
                              buf_ref.at[buf, k], sems.at[buf])

    @pl.when(i == 0)
    def _():
        start(0, 0)

    @pl.when(i + 1 < pl.num_programs(0))
    def _():
        start(i + 1, 1 - slot)

    _wait_row_gather(EXPERT_TOP_K * tm, 1, y_hbm, buf_ref.at[slot, 0], sems.at[slot])
    for c in range(o_ref.shape[1] // TN_PROJ):
        sl = slice(c * TN_PROJ, (c + 1) * TN_PROJ)
        o_ref[:, sl] = sum(wts_ref[:, k:k + 1] * buf_ref[slot, k, :, 0, sl]
                           for k in range(EXPERT_TOP_K))


def _combine_experts(slot_kt, y_sorted, wts):
    d = y_sorted.shape[2]
    m = slot_kt.shape[0] // EXPERT_TOP_K
    tm = TM_COMBINE
    return pl.pallas_call(
        _combine_kernel,
        out_shape=jax.ShapeDtypeStruct((m, d), F32),
        grid_spec=pltpu.PrefetchScalarGridSpec(
            num_scalar_prefetch=1,
            grid=(m // tm,),
            in_specs=[
                pl.BlockSpec(memory_space=pl.ANY),
                pl.BlockSpec((tm, ROUTER_COLS), lambda i, sl: (i, 0)),
            ],
            out_specs=pl.BlockSpec((tm, d), lambda i, sl: (i, 0)),
            scratch_shapes=[pltpu.VMEM((2, EXPERT_TOP_K, tm, 1, d), F32),
                            pltpu.SemaphoreType.DMA((2,))],
        ),
        compiler_params=_cparams(("arbitrary",)),
        name="moe_combine",
    )(slot_kt, y_sorted, wts)


def _final_kernel(x1s_ref, wg_ref, bg_ref, p_ref, wp_ref, y1_ref, mu_ref, rs_ref,
                  l1g_ref, l1b_ref, moe_ref, l2g_ref, l2b_ref,
                  o_ref, lo_ref, hi_ref, acc_ref, mu2_ref, rs2_ref):
    i = pl.program_id(0)
    j = pl.program_id(1)
    n_tiles = pl.num_programs(0) - 1
    n_chunks = acc_ref.shape[1]
    tn = acc_ref.shape[3]
    cur = lax.rem(i, 2)
    prv = 1 - cur

    @pl.when(jnp.logical_and(j == 0, i < n_tiles))
    def _():
        for s in range(WORD_ROWS):
            sl = slice(s * LANES, (s + 1) * LANES)
            lo_ref[:, sl], hi_ref[:, sl] = _unpack_halves(
                x1s_ref[pl.ds(s, lo_ref.shape[0], stride=WORD_ROWS), :])

    def build():
        gate = _dot_halves(lo_ref[...], hi_ref[...], wg_ref) + bg_ref[...]
        emb = jnp.dot(p_ref[...].astype(BF16), wp_ref[...].astype(BF16),
                      preferred_element_type=F32)
        x1 = (y1_ref[...] - mu_ref[...]) * rs_ref[...] * l1g_ref[...] + l1b_ref[...]
        acc_ref[cur, j] = DEEPNORM_ALPHA * x1 + moe_ref[...] + jax.nn.sigmoid(gate) * emb

    def emit():
        o_ref[...] = ((acc_ref[prv, j] - mu2_ref[prv]) * rs2_ref[prv]
                      * l2g_ref[...] + l2b_ref[...])

    @pl.when(i == 0)
    def _():
        build()

    @pl.when(jnp.logical_and(i > 0, i < n_tiles))
    def _():
        emit()
        build()

    @pl.when(i == n_tiles)
    def _():
        emit()

    @pl.when(jnp.logical_and(j == n_chunks - 1, i < n_tiles))
    def _():
        mu2_ref[cur], rs2_ref[cur] = _row_stats(acc_ref.at[cur], n_chunks, n_chunks * tn)


def _final_stage(x1s, w_pg_b, b_pg, p2, w_pp, y1, mu1, rs1, ln1_g, ln1_b,
                 moe, ln2_g, ln2_b):
    m, d = y1.shape
    tm, tn = TM_PROJ, TN_PROJ
    ple = p2.shape[1]
    n_chunks = d // tn

    n_tiles = m // tm

    def built(i):
        return jnp.minimum(i, n_tiles - 1)

    def emitted(i):
        return jnp.maximum(i - 1, 0)

    return pl.pallas_call(
        _final_kernel,
        out_shape=jax.ShapeDtypeStruct((m, d), F32),
        grid=(n_tiles + 1, n_chunks),
        in_specs=[
            pl.BlockSpec((tm * WORD_ROWS, LANES), lambda i, j: (built(i), 0),
                         pipeline_mode=pl.Buffered(1)),
            pl.BlockSpec((d, tn), lambda i, j: (0, j)),
            pl.BlockSpec((1, tn), lambda i, j: (0, j)),
            pl.BlockSpec((tm, ple), lambda i, j: (built(i), 0)),
            pl.BlockSpec((ple, tn), lambda i, j: (0, j)),
            pl.BlockSpec((tm, tn), lambda i, j: (built(i), j)),
            pl.BlockSpec((tm, 1), lambda i, j: (built(i), 0)),
            pl.BlockSpec((tm, 1), lambda i, j: (built(i), 0)),
            pl.BlockSpec((1, tn), lambda i, j: (0, j)),
            pl.BlockSpec((1, tn), lambda i, j: (0, j)),
            pl.BlockSpec((tm, tn), lambda i, j: (built(i), j)),
            pl.BlockSpec((1, tn), lambda i, j: (0, j)),
            pl.BlockSpec((1, tn), lambda i, j: (0, j)),
        ],
        out_specs=pl.BlockSpec((tm, tn), lambda i, j: (emitted(i), jnp.where(i == 0, 0, j))),
        scratch_shapes=[pltpu.VMEM((tm, d // 2), BF16), pltpu.VMEM((tm, d // 2), BF16),
                        pltpu.VMEM((2, n_chunks, tm, tn), F32),
                        pltpu.VMEM((2, tm, 1), F32), pltpu.VMEM((2, tm, 1), F32)],
        compiler_params=_cparams(("arbitrary", "arbitrary")),
        name="ple_moe_ln2",
    )(x1s, w_pg_b, b_pg, p2, w_pp, y1, mu1, rs1, ln1_g, ln1_b, moe, ln2_g, ln2_b)


def _routing_tables(ids_t, counts_row):
    counts = counts_row[EXPERT_LANE0:EXPERT_LANE0 + N_EXPERTS]
    tiles = (counts + TM_MOE - 1) // TM_MOE
    tile_end = jnp.cumsum(tiles)
    offsets = (tile_end - tiles) * TM_MOE
    n_used = tile_end[-1:]
    n_tokens = ids_t.shape[1]
    experts = ids_t[0:EXPERT_TOP_K]
    hit = experts[None] == jnp.arange(N_EXPERTS, dtype=I32)[:, None, None]
    first_slot = jnp.sum(jnp.where(hit, offsets[:, None, None], 0), axis=0)
    slots = (first_slot + ids_t[EXPERT_TOP_K:2 * EXPERT_TOP_K]).reshape(-1)
    n_tiles = (EXPERT_TOP_K * n_tokens) // TM_MOE + N_EXPERTS
    tile_ids = jnp.minimum(jnp.arange(n_tiles, dtype=I32), n_used - 1)
    tile_expert = jnp.sum(tile_end[None, :] <= tile_ids[:, None], axis=1).astype(I32)
    slab_of_slot = jnp.zeros((n_tiles * TM_MOE,), I32).at[slots].set(
        jnp.arange(slots.size, dtype=I32) % n_tokens * WORD_ROWS, unique_indices=True,
        mode="promise_in_bounds")
    eid = jnp.arange(N_EXPERTS, dtype=I32)
    later_owner = (eid[None, :] > eid[:, None]) & (tiles[None, :] > 0)
    next_expert = jnp.min(jnp.where(later_owner, eid[None, :], N_EXPERTS), axis=1).astype(I32)
    stage_slot = ((jnp.cumsum(tiles > 0) - 1) % 2).astype(I32)
    return (slots.astype(I32), slab_of_slot, tile_expert, n_used.astype(I32),
            next_expert, stage_slot)


def kernel(x, p, positions, w_in, b_in, sinks, g_norm_a, g_norm_b, w_out, b_out,
           ln1_g, ln1_b, w_group, b_group, w_er, b_er, w_gate, w_up, w_down,
           w_ple_gate, b_ple_gate, w_ple_proj, ln2_g, ln2_b):
    batch, seq, d = x.shape
    m = batch * seq
    row = lambda v: v.reshape(1, -1)
    x2 = x.reshape(m, d)
    for i in range(DEPTH):
        proj = _in_projection(x2, w_in[i].astype(BF16), row(b_in[i]))
        inv_freq = ROPE_THETA ** (-jnp.arange(0, SWA_HEAD_DIM, 2, dtype=F32) / SWA_HEAD_DIM)
        inv_freq = jnp.tile(inv_freq, LANES // inv_freq.shape[0]).reshape(1, LANES)
        o_a = _swa_attention(proj, positions.reshape(m, 1), inv_freq, sinks[i], batch, seq)
        o_b = _sb_attention(proj, batch, seq)
        y1, mu1, rs1, x1s = _out_projection(
            o_a, o_b, row(g_norm_a[i]), row(g_norm_b[i]), w_out[i].astype(BF16), x2,
            row(b_out[i]), row(ln1_g[i]), row(ln1_b[i]))
        pad = ROUTER_COLS - N_GROUPS - N_EXPERTS
        w_r = jnp.concatenate(
            [w_group[i], w_er[i].transpose(1, 0, 2).reshape(d, N_EXPERTS),
             jnp.zeros((d, pad), F32)], axis=1).astype(BF16)
        b_r = jnp.concatenate([b_group[i], b_er[i].reshape(-1), jnp.zeros((pad,), F32)])
        ids, wts, counts = _router(x1s, w_r, row(b_r))
        (slots, slab_of_slot, tile_expert, n_used, next_expert,
         stage_slot) = _routing_tables(ids, counts[0])
        act = _grouped_gate_up(tile_expert, n_used, slab_of_slot, next_expert, stage_slot,
                               x1s, w_gate[i], w_up[i])
        y_sorted = _grouped_down(tile_expert, n_used, next_expert, stage_slot, act, w_down[i])
        moe = _combine_experts(slots, y_sorted, wts)
        x2 = _final_stage(x1s, w_ple_gate[i].astype(BF16), row(b_ple_gate[i]),
                          p[i].reshape(m, PLE_DIM), w_ple_proj[i], y1, mu1, rs1,
                          row(ln1_g[i]), row(ln1_b[i]), moe, row(ln2_g[i]), row(ln2_b[i]))
    return x2.reshape(batch, seq, d)
```

```python
import math

import jax
import jax.numpy as jnp
from jax import lax
from jax.experimental import pallas as pl
from jax.experimental.pallas import tpu as pltpu

F32 = jnp.float32
BF16 = jnp.bfloat16
I32 = jnp.int32
U32 = jnp.uint32

D_MODEL = 4096
PLE_DIM = 256
BLOCK = 128
ROPE_THETA = 10000.0
LN_EPS = 1e-5
RMS_EPS = 1e-6
NEG_INF = -1e30

SWA_HEAD_DIM = 64
SWA_WIDTH = D_MODEL // 2
SWA_HEADS = SWA_WIDTH // SWA_HEAD_DIM
SWA_KV_HEADS = SWA_HEADS // 8
SWA_GROUP = SWA_HEADS // SWA_KV_HEADS
SWA_KV_WIDTH = SWA_KV_HEADS * SWA_HEAD_DIM

SB_HEAD_DIM = 128
SB_WIDTH = D_MODEL - SWA_WIDTH
SB_HEADS = SB_WIDTH // SB_HEAD_DIM

IN_WIDTH = SWA_WIDTH + 2 * SWA_KV_WIDTH + 3 * SB_WIDTH
A_WIDTH = SWA_WIDTH + 2 * SWA_KV_WIDTH
Q_B_COL = 0
K_B_COL = Q_B_COL + SB_WIDTH
V_B_COL = K_B_COL + SB_WIDTH
Q_A_COL = V_B_COL + SB_WIDTH
K_A_COL = Q_A_COL + SWA_WIDTH
V_A_COL = K_A_COL + SWA_KV_WIDTH

N_GROUPS = 4
EXPERTS_PER_GROUP = 8
N_EXPERTS = N_GROUPS * EXPERTS_PER_GROUP
EXPERT_TOP_K = 2
DEPTH = 1
DEEPNORM_ALPHA = (2.0 * DEPTH) ** 0.25

LANES = 128
SUBLANES = 8
VMEM_LIMIT_BYTES = 56 * 1024 * 1024

EXP_ZERO_LOG = -126 * math.log(2.0) - 2.0

TM_IN = 1024
TM_PROJ = 512
TN_PROJ = 512
TM_MOE = 256
TM_COMBINE = 256
SB_HEADS_PER_STEP = 16
SUFFIX_TERMS = 2
ROW_CHUNK = 16
WEIGHT_DMA_PRIORITY = 1


def _cparams(sem):
    return pltpu.CompilerParams(dimension_semantics=sem,
                                vmem_limit_bytes=VMEM_LIMIT_BYTES)


def _pack_halves(lo, hi):
    return lax.bitcast_convert_type(
        pltpu.pack_elementwise([lo, hi], packed_dtype=BF16), U32)


def _unpack_halves(words):
    return tuple(
        pltpu.unpack_elementwise(words, index=k, packed_dtype=BF16,
                                 unpacked_dtype=F32).astype(BF16) for k in (0, 1))


def _dot_halves(lo, hi, w_ref):
    half = lo.shape[1]
    return (jnp.dot(lo, w_ref[:half], preferred_element_type=F32)
            + jnp.dot(hi, w_ref[half:], preferred_element_type=F32))


WORD_ROWS = (D_MODEL // 2) // LANES
SLAB_PAIR = 2 * LANES


def _slab_words(slab_ref, pair, tokens):
    return jnp.concatenate(
        [slab_ref[pl.ds(2 * pair + k, tokens, stride=WORD_ROWS), :] for k in (0, 1)],
        axis=1)


def _dot_slab(slab_ref, tokens, w_refs):
    half = D_MODEL // 2
    outs = [None] * len(w_refs)
    for pair in range(half // SLAB_PAIR):
        lo, hi = _unpack_halves(_slab_words(slab_ref, pair, tokens))
        rows_lo = slice(pair * SLAB_PAIR, (pair + 1) * SLAB_PAIR)
        rows_hi = slice(half + pair * SLAB_PAIR, half + (pair + 1) * SLAB_PAIR)
        for n, w_ref in enumerate(w_refs):
            part = (jnp.dot(lo, w_ref[rows_lo], preferred_element_type=F32)
                    + jnp.dot(hi, w_ref[rows_hi], preferred_element_type=F32))
            outs[n] = part if outs[n] is None else outs[n] + part
    return outs


def _dot_nt(a, b):
    return lax.dot_general(a, b, (((1,), (1,)), ((), ())),
                           preferred_element_type=F32)


def _inproj_kernel(x_ref, w_ref, b_ref, o_ref, xb_ref):
    @pl.when(pl.program_id(1) == 0)
    def _():
        xb_ref[...] = x_ref[...].astype(BF16)

    acc = jnp.dot(xb_ref[...], w_ref[...], preferred_element_type=F32)
    o_ref[...] = (acc + b_ref[...]).astype(o_ref.dtype)


def _in_projection(x2, w_b, b):
    m, k = x2.shape
    n = w_b.shape[1]
    n_blocks = n // TN_PROJ
    assert A_WIDTH % TN_PROJ == 0
    a_blocks = A_WIDTH // TN_PROJ

    def src(j):
        return lax.rem(j + a_blocks, n_blocks)

    return pl.pallas_call(
        _inproj_kernel,
        out_shape=jax.ShapeDtypeStruct((m, n), BF16),
        grid=(m // TM_IN, n_blocks),
        in_specs=[
            pl.BlockSpec((TM_IN, k), lambda i, j: (i, 0)),
            pl.BlockSpec((k, TN_PROJ), lambda i, j: (0, src(j))),
            pl.BlockSpec((1, TN_PROJ), lambda i, j: (0, src(j))),
        ],
        out_specs=pl.BlockSpec((TM_IN, TN_PROJ), lambda i, j: (i, j)),
        scratch_shapes=[pltpu.VMEM((TM_IN, k), BF16)],
        compiler_params=_cparams(("arbitrary", "arbitrary")),
        name="in_projection",
    )(x2, w_b, b)


def _swa_kernel(sinks_ref, q_ref, kc_ref, vc_ref, vp_ref,
                posc_ref, invf_ref, o_ref, kprev_ref):
    n = pl.program_id(1)

    @pl.when(n == 0)
    def _():
        kprev_ref[...] = jnp.zeros_like(kprev_ref)
    lane = lax.broadcasted_iota(I32, (1, LANES), 1)
    first_half = (lane % SWA_HEAD_DIM) < (SWA_HEAD_DIM // 2)

    def tables(pos_ref):
        ang = pos_ref[...].astype(F32) * invf_ref[...]
        sin = jnp.sin(ang)
        return jnp.cos(ang), jnp.where(first_half, -sin, sin)

    def rope(x, cos, sin_signed):
        partner = jnp.where(first_half,
                            pltpu.roll(x, LANES - SWA_HEAD_DIM // 2, 1),
                            pltpu.roll(x, SWA_HEAD_DIM // 2, 1))
        return x * cos + partner * sin_signed

    cos_c, sin_c = tables(posc_ref)

    def rope_block(ref, cos, sin_signed):
        width = ref.shape[1]
        return [rope(ref[:, c * LANES:(c + 1) * LANES].astype(F32), cos,
                     sin_signed).astype(BF16) for c in range(width // LANES)]

    q_chunks = rope_block(q_ref, cos_c, sin_c)
    k_cur = rope_block(kc_ref, cos_c, sin_c)
    k_chunks = [jnp.concatenate([kprev_ref[:, c * LANES:(c + 1) * LANES], k_c], axis=0)
                for c, k_c in enumerate(k_cur)]
    for c, k_c in enumerate(k_cur):
        kprev_ref[:, c * LANES:(c + 1) * LANES] = k_c
    v_all = jnp.concatenate([vp_ref[...], vc_ref[...]], axis=0)

    qi = lax.broadcasted_iota(I32, (BLOCK, 2 * BLOCK), 0)
    kj = lax.broadcasted_iota(I32, (BLOCK, 2 * BLOCK), 1)
    rel = qi - (kj - BLOCK)
    valid = (rel >= 0) & (rel < BLOCK) & ((kj >= BLOCK) | (n > 0))

    def head_slice(chunks, head):
        half = head % 2
        return chunks[head // 2][:, half * SWA_HEAD_DIM:(half + 1) * SWA_HEAD_DIM]

    scale = 1.0 / math.sqrt(SWA_HEAD_DIM)
    scores = []
    for h in range(SWA_KV_HEADS):
        k_h = head_slice(k_chunks, h)
        q_h = jnp.concatenate(
            [head_slice(q_chunks, h * SWA_GROUP + g) for g in range(SWA_GROUP)],
            axis=0)
        scores.append(_dot_nt(q_h, k_h) * scale)
    weights, sink_terms = [], []
    for h in range(SWA_KV_HEADS):
        weights_h = []
        for g in range(SWA_GROUP):
            sink = sinks_ref[h * SWA_GROUP + g]
            s_g = jnp.where(valid, scores[h][g * BLOCK:(g + 1) * BLOCK], NEG_INF)
            m = jnp.maximum(jnp.max(s_g, axis=-1, keepdims=True), sink)
            weights_h.append(jnp.exp(s_g - m).astype(BF16))
            sink_terms.append(jnp.exp(sink - m))
        weights.append(jnp.concatenate(weights_h, axis=0))
    ones = jnp.ones((2 * BLOCK, SWA_HEAD_DIM), BF16)
    for h in range(SWA_KV_HEADS):
        v_h = v_all[:, h * SWA_HEAD_DIM:(h + 1) * SWA_HEAD_DIM]
        both = jnp.dot(weights[h], jnp.concatenate([v_h, ones], axis=1),
                       preferred_element_type=F32)
        for g in range(SWA_GROUP):
            rows = slice(g * BLOCK, (g + 1) * BLOCK)
            den = both[rows, SWA_HEAD_DIM:SWA_HEAD_DIM + 1] + sink_terms[h * SWA_GROUP + g]
            col = (h * SWA_GROUP + g) * SWA_HEAD_DIM
            o_ref[:, col:col + SWA_HEAD_DIM] = both[rows, :SWA_HEAD_DIM] / den


def _swa_attention(proj, pos2, inv_freq, sinks, batch, seq):
    nb = seq // BLOCK
    assert Q_A_COL % SWA_WIDTH == 0 and K_A_COL % SWA_KV_WIDTH == 0
    assert V_A_COL % SWA_KV_WIDTH == 0
    qcol = Q_A_COL // SWA_WIDTH
    kcol = K_A_COL // SWA_KV_WIDTH
    vcol = V_A_COL // SWA_KV_WIDTH

    def cur(b, n):
        return b * nb + n

    def prev(b, n):
        return b * nb + jnp.maximum(n - 1, 0)

    return pl.pallas_call(
        _swa_kernel,
        out_shape=jax.ShapeDtypeStruct((batch * seq, SWA_WIDTH), F32),
        grid=(batch, nb),
        in_specs=[
            pl.BlockSpec(memory_space=pltpu.SMEM),
            pl.BlockSpec((BLOCK, SWA_WIDTH), lambda b, n: (cur(b, n), qcol)),
            pl.BlockSpec((BLOCK, SWA_KV_WIDTH), lambda b, n: (cur(b, n), kcol)),
            pl.BlockSpec((BLOCK, SWA_KV_WIDTH), lambda b, n: (cur(b, n), vcol)),
            pl.BlockSpec((BLOCK, SWA_KV_WIDTH), lambda b, n: (prev(b, n), vcol)),
            pl.BlockSpec((BLOCK, 1), lambda b, n: (cur(b, n), 0)),
            pl.BlockSpec((1, LANES), lambda b, n: (0, 0)),
        ],
        out_specs=pl.BlockSpec((BLOCK, SWA_WIDTH), lambda b, n: (cur(b, n), 0)),
        scratch_shapes=[pltpu.VMEM((BLOCK, SWA_KV_WIDTH), BF16)],
        compiler_params=_cparams(("arbitrary", "arbitrary")),
        name="swa_attention",
    )(sinks, proj, proj, proj, proj, pos2, inv_freq)


def _sb_kernel(q_ref, k_ref, v_ref, o_ref):
    n = pl.program_id(2)
    heads = [slice(h * SB_HEAD_DIM, (h + 1) * SB_HEAD_DIM)
             for h in range(SB_HEADS_PER_STEP)]
    rows = SB_HEADS_PER_STEP * BLOCK
    scale = 1.0 / math.sqrt(SB_HEAD_DIM)
    key_j = lax.broadcasted_iota(I32, (BLOCK, 2 * BLOCK), 0)
    out_c = lax.broadcasted_iota(I32, (BLOCK, 2 * BLOCK), 1)
    later_total = ((key_j > out_c) | (out_c >= BLOCK)).astype(BF16)
    q_row = lax.broadcasted_iota(I32, (rows, BLOCK), 0) % BLOCK
    k_col = lax.broadcasted_iota(I32, (rows, BLOCK), 1)
    causal = k_col < q_row

    def block(kb, carry, acc, mask):
        start = pl.multiple_of(kb * BLOCK, BLOCK)
        z = jnp.concatenate(
            [_dot_nt(q_ref[:, hd], k_ref[pl.ds(start, BLOCK), hd]) for hd in heads],
            axis=0) * scale
        t = jnp.log(1.0 + jnp.exp(-jnp.abs(z)))
        log_not = jnp.minimum(-z, 0.0) - t
        log_beta = log_not + z
        if mask is not None:
            log_not = jnp.where(mask, log_not, 0.0)
        terms, rest = [], log_not
        for _ in range(SUFFIX_TERMS):
            terms.append(rest.astype(BF16))
            rest = rest - terms[-1].astype(F32)
        parts = jnp.dot(jnp.concatenate(terms, axis=0), later_total,
                        preferred_element_type=F32)
        sums = parts[:rows]
        for t in range(1, SUFFIX_TERMS):
            sums = sums + parts[t * rows:(t + 1) * rows]
        a = jnp.exp(log_beta + sums[:, :BLOCK] + carry)
        if mask is not None:
            a = jnp.where(mask, a, 0.0)
        a = a.astype(BF16)
        pv = jnp.concatenate(
            [jnp.dot(a[h * BLOCK:(h + 1) * BLOCK], v_ref[pl.ds(start, BLOCK), hd],
                     preferred_element_type=F32) for h, hd in enumerate(heads)], axis=0)
        return carry + sums[:, BLOCK:], acc + pv

    zeros = jnp.zeros((rows, BLOCK), F32)
    carry, acc = block(n, zeros, zeros, causal)

    def cond(state):
        kb, carry, _ = state
        return jnp.logical_and(kb >= 0, jnp.max(carry) > EXP_ZERO_LOG)

    def body(state):
        kb, carry, acc = state
        carry, acc = block(kb, carry, acc, None)
        return kb - 1, carry, acc

    _, _, acc = lax.while_loop(cond, body, (n - 1, carry, acc))
    for h, hd in enumerate(heads):
        o_ref[:, hd] = acc[h * BLOCK:(h + 1) * BLOCK]


def _sb_attention(proj, batch, seq):
    nb = seq // BLOCK
    width = SB_HEADS_PER_STEP * SB_HEAD_DIM
    assert Q_B_COL % width == 0 and K_B_COL % width == 0 and V_B_COL % width == 0
    qcol = Q_B_COL // width
    kcol = K_B_COL // width
    vcol = V_B_COL // width
    return pl.pallas_call(
        _sb_kernel,
        out_shape=jax.ShapeDtypeStruct((batch * seq, SB_WIDTH), F32),
        grid=(batch, SB_HEADS // SB_HEADS_PER_STEP, nb),
        in_specs=[
            pl.BlockSpec((BLOCK, width), lambda b, h, n: (b * nb + n, qcol + h)),
            pl.BlockSpec((seq, width), lambda b, h, n: (b, kcol + h),
                         pipeline_mode=pl.Buffered(1)),
            pl.BlockSpec((seq, width), lambda b, h, n: (b, vcol + h),
                         pipeline_mode=pl.Buffered(1)),
        ],
        out_specs=pl.BlockSpec((BLOCK, width), lambda b, h, n: (b * nb + n, h)),
        compiler_params=_cparams(("arbitrary", "arbitrary", "arbitrary")),
        name="sb_attention",
    )(proj, proj, proj)


def _row_stats(chunks_ref, n_chunks, width):
    total = chunks_ref[0].sum(axis=-1, keepdims=True)
    for c in range(1, n_chunks):
        total = total + chunks_ref[c].sum(axis=-1, keepdims=True)
    mu = total / width
    sq = jnp.square(chunks_ref[0] - mu).sum(axis=-1, keepdims=True)
    for c in range(1, n_chunks):
        sq = sq + jnp.square(chunks_ref[c] - mu).sum(axis=-1, keepdims=True)
    return mu, lax.rsqrt(sq / width + LN_EPS)


def _outproj_kernel(oa_ref, ob_ref, ga_ref, gb_ref, wa_ref, wb_ref, x_ref, bo_ref,
                    lg_ref, lb_ref, y_ref, mu_ref, rs_ref, x1s_ref,
                    ma_ref, mb_ref, acc_ref):
    j = pl.program_id(1)
    n_chunks = acc_ref.shape[0]
    tn = acc_ref.shape[2]

    @pl.when(j == 0)
    def _():
        for o_ref, g_ref, m_ref in ((oa_ref, ga_ref, ma_ref), (ob_ref, gb_ref, mb_ref)):
            width = o_ref.shape[1]
            cols = [slice(c * tn, (c + 1) * tn) for c in range(width // tn)]
            sq = sum(jnp.square(o_ref[:, sl]).sum(axis=-1, keepdims=True) for sl in cols)
            r = lax.rsqrt(sq / width + RMS_EPS)
            for sl in cols:
                m_ref[:, sl] = (o_ref[:, sl] * r * g_ref[:, sl]).astype(BF16)

    mix = (jnp.dot(ma_ref[...], wa_ref[...], preferred_element_type=F32)
           + jnp.dot(mb_ref[...], wb_ref[...], preferred_element_type=F32))
    y = DEEPNORM_ALPHA * x_ref[...] + (mix + bo_ref[...])
    y_ref[...] = y
    acc_ref[j] = y

    @pl.when(j == n_chunks - 1)
    def _():
        mu, rs = _row_stats(acc_ref, n_chunks, n_chunks * tn)
        mu_ref[...] = mu
        rs_ref[...] = rs

        def normed(c):
            sl = slice(c * tn, (c + 1) * tn)
            return (acc_ref[c] - mu) * rs * lg_ref[:, sl] + lb_ref[:, sl]

        tm = acc_ref.shape[1]
        for c in range(n_chunks // 2):
            words = _pack_halves(normed(c), normed(c + n_chunks // 2))
            for q in range(tn // LANES):
                x1s_ref[pl.ds(c * (tn // LANES) + q, tm, stride=WORD_ROWS), :] = (
                    words[:, q * LANES:(q + 1) * LANES])


def _out_projection(o_a, o_b, g_a, g_b, w_b, x2, b_out, ln_g, ln_b):
    m, d = x2.shape
    tm, tn = TM_PROJ, TN_PROJ
    half = o_a.shape[1]
    once = dict(pipeline_mode=pl.Buffered(1))
    return pl.pallas_call(
        _outproj_kernel,
        out_shape=(jax.ShapeDtypeStruct((m, d), F32),
                   jax.ShapeDtypeStruct((m, 1), F32),
                   jax.ShapeDtypeStruct((m, 1), F32),
                   jax.ShapeDtypeStruct((m * WORD_ROWS, LANES), U32)),
        grid=(m // tm, d // tn),
        in_specs=[
            pl.BlockSpec((tm, half), lambda i, j: (i, 0), **once),
            pl.BlockSpec((tm, half), lambda i, j: (i, 0), **once),
            pl.BlockSpec((1, half), lambda i, j: (0, 0)),
            pl.BlockSpec((1, half), lambda i, j: (0, 0)),
            pl.BlockSpec((half, tn), lambda i, j: (0, j)),
            pl.BlockSpec((half, tn), lambda i, j: (1, j)),
            pl.BlockSpec((tm, tn), lambda i, j: (i, j)),
            pl.BlockSpec((1, tn), lambda i, j: (0, j)),
            pl.BlockSpec((1, d), lambda i, j: (0, 0)),
            pl.BlockSpec((1, d), lambda i, j: (0, 0)),
        ],
        out_specs=(pl.BlockSpec((tm, tn), lambda i, j: (i, j)),
                   pl.BlockSpec((tm, 1), lambda i, j: (i, 0)),
                   pl.BlockSpec((tm, 1), lambda i, j: (i, 0)),
                   pl.BlockSpec((tm * WORD_ROWS, LANES), lambda i, j: (i, 0))),
        scratch_shapes=[pltpu.VMEM((tm, half), BF16), pltpu.VMEM((tm, half), BF16),
                        pltpu.VMEM((d // tn, tm, tn), F32)],
        compiler_params=_cparams(("arbitrary", "arbitrary")),
        name="out_projection_ln1",
    )(o_a, o_b, g_a, g_b, w_b, w_b, x2, b_out, ln_g, ln_b)


ROUTER_COLS = LANES
EXPERT_LANE0 = N_GROUPS


def _router_kernel(x_ref, w_ref, b_ref, ids_ref, wts_ref, cnt_ref, carry_ref):
    i = pl.program_id(0)
    tm = wts_ref.shape[0]

    @pl.when(i == 0)
    def _():
        carry_ref[...] = jnp.zeros_like(carry_ref)

    logits = _dot_slab(x_ref, tm, [w_ref])[0] + b_ref[...]
    lane = lax.broadcasted_iota(I32, (tm, ROUTER_COLS), 1)
    big = jnp.int32(ROUTER_COLS)

    def first_argmax(vals):
        top = jnp.max(vals, axis=-1, keepdims=True)
        idx = jnp.min(jnp.where(vals == top, lane, big), axis=-1, keepdims=True)
        return top, idx

    is_group = lane < N_GROUPS
    g_logits = jnp.where(is_group, logits, -jnp.inf)
    g_top, g_idx = first_argmax(g_logits)
    g_w = 1.0 / jnp.sum(jnp.exp(g_logits - g_top), axis=-1, keepdims=True)

    first = EXPERT_LANE0 + g_idx * EXPERTS_PER_GROUP
    in_group = (lane >= first) & (lane < first + EXPERTS_PER_GROUP)
    e_logits = jnp.where(in_group, logits, -jnp.inf)
    top1, idx1 = first_argmax(e_logits)
    top2, idx2 = first_argmax(jnp.where(lane == idx1, -jnp.inf, e_logits))
    e2 = jnp.exp(top2 - top1)
    w1 = g_w / (1.0 + e2)
    w2 = g_w * e2 / (1.0 + e2)

    hit1 = lane == idx1
    hit2 = lane == idx2
    onehot = (hit1 | hit2).astype(BF16)
    r = lax.broadcasted_iota(I32, (tm, tm), 0)
    c = lax.broadcasted_iota(I32, (tm, tm), 1)
    before = (c < r).astype(BF16)
    prior = jnp.dot(before, onehot, preferred_element_type=F32) + carry_ref[0:1, :]
    rank1 = jnp.sum(jnp.where(hit1, prior, 0.0), axis=-1, keepdims=True)
    rank2 = jnp.sum(jnp.where(hit2, prior, 0.0), axis=-1, keepdims=True)
    counts = carry_ref[0:1, :] + jnp.sum(onehot.astype(F32), axis=0, keepdims=True)
    carry_ref[...] = jnp.broadcast_to(counts, carry_ref.shape)
    cnt_ref[...] = jnp.broadcast_to(counts, cnt_ref.shape).astype(I32)

    ids = jnp.where(lane == 0, idx1 - EXPERT_LANE0,
          jnp.where(lane == 1, idx2 - EXPERT_LANE0,
          jnp.where(lane == 2, rank1.astype(I32),
          jnp.where(lane == 3, rank2.astype(I32), 0))))
    ids_ref[...] = ids.T[:ids_ref.shape[0]]
    wts_ref[...] = jnp.where(lane == 0, w1, jnp.where(lane == 1, w2, 0.0))


def _router(x1s, w_r, b_r):
    m = x1s.shape[0] // WORD_ROWS
    tm = TM_PROJ
    return pl.pallas_call(
        _router_kernel,
        out_shape=(jax.ShapeDtypeStruct((SUBLANES, m), I32),
                   jax.ShapeDtypeStruct((m, ROUTER_COLS), F32),
                   jax.ShapeDtypeStruct((SUBLANES, ROUTER_COLS), I32)),
        grid=(m // tm,),
        in_specs=[
            pl.BlockSpec((tm * WORD_ROWS, LANES), lambda i: (i, 0)),
            pl.BlockSpec((D_MODEL, ROUTER_COLS), lambda i: (0, 0)),
            pl.BlockSpec((1, ROUTER_COLS), lambda i: (0, 0)),
        ],
        out_specs=(pl.BlockSpec((SUBLANES, tm), lambda i: (0, i)),
                   pl.BlockSpec((tm, ROUTER_COLS), lambda i: (i, 0)),
                   pl.BlockSpec((SUBLANES, ROUTER_COLS), lambda i: (0, 0))),
        scratch_shapes=[pltpu.VMEM((SUBLANES, ROUTER_COLS), F32)],
        compiler_params=_cparams(("arbitrary",)),
        name="router",
    )(x1s, w_r, b_r)


def _start_row_gather(idx_ref, first, n_items, span, src_ref, dst_ref, sem):
    def body(c, _):
        for u in range(ROW_CHUNK):
            r = c * ROW_CHUNK + u
            src_row = pl.multiple_of(idx_ref[first + r], span)
            dst_row = pl.multiple_of(r * span, span)
            pltpu.make_async_copy(src_ref.at[pl.ds(src_row, span)],
                                  dst_ref.at[pl.ds(dst_row, span)], sem).start()
        return 0

    lax.fori_loop(0, n_items // ROW_CHUNK, body, 0)


def _wait_row_gather(n_items, span, src_ref, dst_ref, sem):
    def body(c, _):
        for _u in range(ROW_CHUNK):
            pltpu.make_async_copy(src_ref.at[pl.ds(0, span)], dst_ref.at[pl.ds(0, span)],
                                  sem).wait()
        return 0

    lax.fori_loop(0, n_items // ROW_CHUNK, body, 0)


def _expert_changed(te_ref, i):
    return jnp.logical_or(i == 0, te_ref[i] != te_ref[jnp.maximum(i - 1, 0)])


def _tile_row(i, nu):
    return jnp.minimum(i, nu[0] - 1)


def _stage_expert_weights(i, te_ref, nxt_ref, ws_ref, w_hbms, stage_ref, bf_refs, sem):
    expert = te_ref[i]
    slot = ws_ref[expert]

    def copies(e, s):
        return [pltpu.make_async_copy(w.at[e], stage_ref.at[s, n], sem.at[s])
                for n, w in enumerate(w_hbms)]

    @pl.when(i == 0)
    def _():
        for c in copies(expert, slot):
            c.start(priority=WEIGHT_DMA_PRIORITY)

    @pl.when(_expert_changed(te_ref, i))
    def _():
        for c in copies(expert, slot):
            c.wait()
        nxt = nxt_ref[expert]

        @pl.when(nxt < N_EXPERTS)
        def _():
            for c in copies(nxt, 1 - slot):
                c.start(priority=WEIGHT_DMA_PRIORITY)

        for n, bf_ref in enumerate(bf_refs):
            bf_ref[...] = stage_ref[slot, n].astype(BF16)


def _gateup_kernel(te_ref, nu_ref, tos_ref, nxt_ref, ws_ref, x_hbm, wg_hbm, wu_hbm, a_ref,
                   xbuf_ref, stage_ref, wgb_ref, wub_ref, sems, wsem):
    i = pl.program_id(0)
    n_used = nu_ref[0]
    slot = lax.rem(i, 2)

    def start(tile, buf):
        _start_row_gather(tos_ref, tile * TM_MOE, TM_MOE, WORD_ROWS, x_hbm,
                          xbuf_ref.at[buf], sems.at[buf])

    @pl.when(i == 0)
    def _():
        start(0, 0)

    @pl.when(i + 1 < n_used)
    def _():
        start(i + 1, 1 - slot)

    _stage_expert_weights(i, te_ref, nxt_ref, ws_ref, [wg_hbm, wu_hbm], stage_ref,
                          [wgb_ref, wub_ref], wsem)

    @pl.when(i < n_used)
    def _():
        _wait_row_gather(TM_MOE, WORD_ROWS, x_hbm, xbuf_ref.at[slot], sems.at[slot])
        gate, up = _dot_slab(xbuf_ref.at[slot], TM_MOE, [wgb_ref, wub_ref])
        a_ref[...] = (gate * jax.nn.sigmoid(gate) * up).astype(a_ref.dtype)

    @pl.when(i >= n_used)
    def _():
        a_ref[...] = jnp.zeros_like(a_ref)


def _grouped_gate_up(tile_expert, n_used, slab_of_slot, next_expert, stage_slot,
                     x1s, w_gate, w_up):
    p_rows = slab_of_slot.shape[0]
    _, d, f = w_gate.shape
    hbm = pl.BlockSpec(memory_space=pl.ANY)
    return pl.pallas_call(
        _gateup_kernel,
        out_shape=jax.ShapeDtypeStruct((p_rows, f), BF16),
        grid_spec=pltpu.PrefetchScalarGridSpec(
            num_scalar_prefetch=5,
            grid=(p_rows // TM_MOE,),
            in_specs=[hbm, hbm, hbm],
            out_specs=pl.BlockSpec((TM_MOE, f), lambda i, *_: (i, 0)),
            scratch_shapes=[pltpu.VMEM((2, TM_MOE * WORD_ROWS, LANES), U32),
                            pltpu.VMEM((2, 2, d, f), F32),
                            pltpu.VMEM((d, f), BF16), pltpu.VMEM((d, f), BF16),
                            pltpu.SemaphoreType.DMA((2,)), pltpu.SemaphoreType.DMA((2,))],
        ),
        compiler_params=_cparams(("arbitrary",)),
        name="moe_gate_up",
    )(tile_expert, n_used, slab_of_slot, next_expert, stage_slot, x1s, w_gate, w_up)


def _down_kernel(te_ref, nu_ref, nxt_ref, ws_ref, a_ref, wd_hbm, y_ref,
                 stage_ref, wdb_ref, wsem):
    i = pl.program_id(0)
    _stage_expert_weights(i, te_ref, nxt_ref, ws_ref, [wd_hbm], stage_ref, [wdb_ref], wsem)

    @pl.when(i < nu_ref[0])
    def _():
        y_ref[:, 0, :] = jnp.dot(a_ref[...], wdb_ref[...], preferred_element_type=F32)

    @pl.when(i >= nu_ref[0])
    def _():
        y_ref[...] = jnp.zeros_like(y_ref)


def _grouped_down(tile_expert, n_used, next_expert, stage_slot, act, w_down):
    p_rows, f = act.shape
    d = w_down.shape[2]
    return pl.pallas_call(
        _down_kernel,
        out_shape=jax.ShapeDtypeStruct((p_rows, 1, d), F32),
        grid_spec=pltpu.PrefetchScalarGridSpec(
            num_scalar_prefetch=4,
            grid=(p_rows // TM_MOE,),
            in_specs=[
                pl.BlockSpec((TM_MOE, f), lambda i, te, nu, *_: (_tile_row(i, nu), 0)),
                pl.BlockSpec(memory_space=pl.ANY),
            ],
            out_specs=pl.BlockSpec((TM_MOE, 1, d), lambda i, *_: (i, 0, 0)),
            scratch_shapes=[pltpu.VMEM((2, 1, f, d), F32), pltpu.VMEM((f, d), BF16),
                            pltpu.SemaphoreType.DMA((2,))],
        ),
        compiler_params=_cparams(("arbitrary",)),
        name="moe_down",
    )(tile_expert, n_used, next_expert, stage_slot, act, w_down)


def _combine_kernel(slot_ref, y_hbm, wts_ref, o_ref, buf_ref, sems):
    i = pl.program_id(0)
    tm = o_ref.shape[0]
    n_tokens = slot_ref.shape[0] // EXPERT_TOP_K
    slot = lax.rem(i, 2)

    def start(tile, buf):
        for k in range(EXPERT_TOP_K):
            _start_row_gather(slot_ref, k * n_tokens + tile * tm, tm, 1, y_hbm,
                              buf_ref.at[buf, k], sems.at[buf])

    @pl.when(i == 0)
    def _():
        start(0, 0)

    @pl.when(i + 1 < pl.num_programs(0))
    def _():
        start(i + 1, 1 - slot)

    _wait_row_gather(EXPERT_TOP_K * tm, 1, y_hbm, buf_ref.at[slot, 0], sems.at[slot])
    for c in range(o_ref.shape[1] // TN_PROJ):
        sl = slice(c * TN_PROJ, (c + 1) * TN_PROJ)
        o_ref[:, sl] = sum(wts_ref[:, k:k + 1] * buf_ref[slot, k, :, 0, sl]
                           for k in range(EXPERT_TOP_K))


def _combine_experts(slot_kt, y_sorted, wts):
    d = y_sorted.shape[2]
    m = slot_kt.shape[0] // EXPERT_TOP_K
    tm = TM_COMBINE
    return pl.pallas_call(
        _combine_kernel,
        out_shape=jax.ShapeDtypeStruct((m, d), F32),
        grid_spec=pltpu.PrefetchScalarGridSpec(
            num_scalar_prefetch=1,
            grid=(m // tm,),
            in_specs=[
                pl.BlockSpec(memory_space=pl.ANY),
                pl.BlockSpec((tm, ROUTER_COLS), lambda i, sl: (i, 0)),
            ],
            out_specs=pl.BlockSpec((tm, d), lambda i, sl: (i, 0)),
            scratch_shapes=[pltpu.VMEM((2, EXPERT_TOP_K, tm, 1, d), F32),
                            pltpu.SemaphoreType.DMA((2,))],
        ),
        compiler_params=_cparams(("arbitrary",)),
        name="moe_combine",
    )(slot_kt, y_sorted, wts)


def _final_kernel(x1s_ref, wg_ref, bg_ref, p_ref, wp_ref, y1_ref, mu_ref, rs_ref,
                  l1g_ref, l1b_ref, moe_ref, l2g_ref, l2b_ref,
                  o_ref, lo_ref, hi_ref, acc_ref, mu2_ref, rs2_ref):
    i = pl.program_id(0)
    j = pl.program_id(1)
    n_tiles = pl.num_programs(0) - 1
    n_chunks = acc_ref.shape[1]
    tn = acc_ref.shape[3]
    cur = lax.rem(i, 2)
    prv = 1 - cur

    @pl.when(jnp.logical_and(j == 0, i < n_tiles))
    def _():
        for s in range(WORD_ROWS):
            sl = slice(s * LANES, (s + 1) * LANES)
            lo_ref[:, sl], hi_ref[:, sl] = _unpack_halves(
                x1s_ref[pl.ds(s, lo_ref.shape[0], stride=WORD_ROWS), :])

    def build():
        gate = _dot_halves(lo_ref[...], hi_ref[...], wg_ref) + bg_ref[...]
        emb = jnp.dot(p_ref[...].astype(BF16), wp_ref[...].astype(BF16),
                      preferred_element_type=F32)
        x1 = (y1_ref[...] - mu_ref[...]) * rs_ref[...] * l1g_ref[...] + l1b_ref[...]
        acc_ref[cur, j] = DEEPNORM_ALPHA * x1 + moe_ref[...] + jax.nn.sigmoid(gate) * emb

    def emit():
        o_ref[...] = ((acc_ref[prv, j] - mu2_ref[prv]) * rs2_ref[prv]
                      * l2g_ref[...] + l2b_ref[...])

    @pl.when(i == 0)
    def _():
        build()

    @pl.when(jnp.logical_and(i > 0, i < n_tiles))
    def _():
        emit()
        build()

    @pl.when(i == n_tiles)
    def _():
        emit()

    @pl.when(jnp.logical_and(j == n_chunks - 1, i < n_tiles))
    def _():
        mu2_ref[cur], rs2_ref[cur] = _row_stats(acc_ref.at[cur], n_chunks, n_chunks * tn)


def _final_stage(x1s, w_pg_b, b_pg, p2, w_pp, y1, mu1, rs1, ln1_g, ln1_b,
                 moe, ln2_g, ln2_b):
    m, d = y1.shape
    tm, tn = TM_PROJ, TN_PROJ
    ple = p2.shape[1]
    n_chunks = d // tn

    n_tiles = m // tm

    def built(i):
        return jnp.minimum(i, n_tiles - 1)

    def emitted(i):
        return jnp.maximum(i - 1, 0)

    return pl.pallas_call(
        _final_kernel,
        out_shape=jax.ShapeDtypeStruct((m, d), F32),
        grid=(n_tiles + 1, n_chunks),
        in_specs=[
            pl.BlockSpec((tm * WORD_ROWS, LANES), lambda i, j: (built(i), 0),
                         pipeline_mode=pl.Buffered(1)),
            pl.BlockSpec((d, tn), lambda i, j: (0, j)),
            pl.BlockSpec((1, tn), lambda i, j: (0, j)),
            pl.BlockSpec((tm, ple), lambda i, j: (built(i), 0)),
            pl.BlockSpec((ple, tn), lambda i, j: (0, j)),
            pl.BlockSpec((tm, tn), lambda i, j: (built(i), j)),
            pl.BlockSpec((tm, 1), lambda i, j: (built(i), 0)),
            pl.BlockSpec((tm, 1), lambda i, j: (built(i), 0)),
            pl.BlockSpec((1, tn), lambda i, j: (0, j)),
            pl.BlockSpec((1, tn), lambda i, j: (0, j)),
            pl.BlockSpec((tm, tn), lambda i, j: (built(i), j)),
            pl.BlockSpec((1, tn), lambda i, j: (0, j)),
            pl.BlockSpec((1, tn), lambda i, j: (0, j)),
        ],
        out_specs=pl.BlockSpec((tm, tn), lambda i, j: (emitted(i), jnp.where(i == 0, 0, j))),
        scratch_shapes=[pltpu.VMEM((tm, d // 2), BF16), pltpu.VMEM((tm, d // 2), BF16),
                        pltpu.VMEM((2, n_chunks, tm, tn), F32),
                        pltpu.VMEM((2, tm, 1), F32), pltpu.VMEM((2, tm, 1), F32)],
        compiler_params=_cparams(("arbitrary", "arbitrary")),
        name="ple_moe_ln2",
    )(x1s, w_pg_b, b_pg, p2, w_pp, y1, mu1, rs1, ln1_g, ln1_b, moe, ln2_g, ln2_b)


def _routing_tables(ids_t, counts_row):
    counts = counts_row[EXPERT_LANE0:EXPERT_LANE0 + N_EXPERTS]
    tiles = (counts + TM_MOE - 1) // TM_MOE
    tile_end = jnp.cumsum(tiles)
    offsets = (tile_end - tiles) * TM_MOE
    n_used = tile_end[-1:]
    n_tokens = ids_t.shape[1]
    experts = ids_t[0:EXPERT_TOP_K]
    hit = experts[None] == jnp.arange(N_EXPERTS, dtype=I32)[:, None, None]
    first_slot = jnp.sum(jnp.where(hit, offsets[:, None, None], 0), axis=0)
    slots = (first_slot + ids_t[EXPERT_TOP_K:2 * EXPERT_TOP_K]).reshape(-1)
    n_tiles = (EXPERT_TOP_K * n_tokens) // TM_MOE + N_EXPERTS
    tile_ids = jnp.minimum(jnp.arange(n_tiles, dtype=I32), n_used - 1)
    tile_expert = jnp.sum(tile_end[None, :] <= tile_ids[:, None], axis=1).astype(I32)
    slab_of_slot = jnp.zeros((n_tiles * TM_MOE,), I32).at[slots].set(
        jnp.arange(slots.size, dtype=I32) % n_tokens * WORD_ROWS, unique_indices=True)
    eid = jnp.arange(N_EXPERTS, dtype=I32)
    later_owner = (eid[None, :] > eid[:, None]) & (tiles[None, :] > 0)
    next_expert = jnp.min(jnp.where(later_owner, eid[None, :], N_EXPERTS), axis=1).astype(I32)
    stage_slot = ((jnp.cumsum(tiles > 0) - 1) % 2).astype(I32)
    return (slots.astype(I32), slab_of_slot, tile_expert, n_used.astype(I32),
            next_expert, stage_slot)


def kernel(x, p, positions, w_in, b_in, sinks, g_norm_a, g_norm_b, w_out, b_out,
           ln1_g, ln1_b, w_group, b_group, w_er, b_er, w_gate, w_up, w_down,
           w_ple_gate, b_ple_gate, w_ple_proj, ln2_g, ln2_b):
    batch, seq, d = x.shape
    m = batch * seq
    row = lambda v: v.reshape(1, -1)
    x2 = x.reshape(m, d)
    for i in range(DEPTH):
        proj = _in_projection(x2, w_in[i].astype(BF16), row(b_in[i]))
        inv_freq = ROPE_THETA ** (-jnp.arange(0, SWA_HEAD_DIM, 2, dtype=F32) / SWA_HEAD_DIM)
        inv_freq = jnp.tile(inv_freq, LANES // inv_freq.shape[0]).reshape(1, LANES)
        o_a = _swa_attention(proj, positions.reshape(m, 1), inv_freq, sinks[i], batch, seq)
        o_b = _sb_attention(proj, batch, seq)
        y1, mu1, rs1, x1s = _out_projection(
            o_a, o_b, row(g_norm_a[i]), row(g_norm_b[i]), w_out[i].astype(BF16), x2,
            row(b_out[i]), row(ln1_g[i]), row(ln1_b[i]))
        pad = ROUTER_COLS - N_GROUPS - N_EXPERTS
        w_r = jnp.concatenate(
            [w_group[i], w_er[i].transpose(1, 0, 2).reshape(d, N_EXPERTS),
             jnp.zeros((d, pad), F32)], axis=1).astype(BF16)
        b_r = jnp.concatenate([b_group[i], b_er[i].reshape(-1), jnp.zeros((pad,), F32)])
        ids, wts, counts = _router(x1s, w_r, row(b_r))
        (slots, slab_of_slot, tile_expert, n_used, next_expert,
         stage_slot) = _routing_tables(ids, counts[0])
        act = _grouped_gate_up(tile_expert, n_used, slab_of_slot, next_expert, stage_slot,
                               x1s, w_gate[i], w_up[i])
        y_sorted = _grouped_down(tile_expert, n_used, next_expert, stage_slot, act, w_down[i])
        moe = _combine_experts(slots, y_sorted, wts)
        x2 = _final_stage(x1s, w_ple_gate[i].astype(BF16), row(b_ple_gate[i]),
                          p[i].reshape(m, PLE_DIM), w_ple_proj[i], y1, mu1, rs1,
                          row(ln1_g[i]), row(ln1_b[i]), moe, row(ln2_g[i]), row(ln2_b[i]))
    return x2.reshape(batch, seq, d)
```

```python
import math

import jax
import jax.numpy as jnp
from jax import lax
from jax.experimental import pallas as pl
from jax.experimental.pallas import tpu as pltpu

F32 = jnp.float32
BF16 = jnp.bfloat16
I32 = jnp.int32
U32 = jnp.uint32

D_MODEL = 4096
PLE_DIM = 256
BLOCK = 128
ROPE_THETA = 10000.0
LN_EPS = 1e-5
RMS_EPS = 1e-6
NEG_INF = -1e30

SWA_HEAD_DIM = 64
SWA_WIDTH = D_MODEL // 2
SWA_HEADS = SWA_WIDTH // SWA_HEAD_DIM
SWA_KV_HEADS = SWA_HEADS // 8
SWA_GROUP = SWA_HEADS // SWA_KV_HEADS
SWA_KV_WIDTH = SWA_KV_HEADS * SWA_HEAD_DIM

SB_HEAD_DIM = 128
SB_WIDTH = D_MODEL - SWA_WIDTH
SB_HEADS = SB_WIDTH // SB_HEAD_DIM

IN_WIDTH = SWA_WIDTH + 2 * SWA_KV_WIDTH + 3 * SB_WIDTH
A_WIDTH = SWA_WIDTH + 2 * SWA_KV_WIDTH
Q_B_COL = 0
K_B_COL = Q_B_COL + SB_WIDTH
V_B_COL = K_B_COL + SB_WIDTH
Q_A_COL = V_B_COL + SB_WIDTH
K_A_COL = Q_A_COL + SWA_WIDTH
V_A_COL = K_A_COL + SWA_KV_WIDTH

N_GROUPS = 4
EXPERTS_PER_GROUP = 8
N_EXPERTS = N_GROUPS * EXPERTS_PER_GROUP
EXPERT_TOP_K = 2
DEPTH = 1
DEEPNORM_ALPHA = (2.0 * DEPTH) ** 0.25

LANES = 128
SUBLANES = 8
VMEM_LIMIT_BYTES = 56 * 1024 * 1024

EXP_ZERO_LOG = -126 * math.log(2.0) - 2.0

TM_IN = 1024
TM_PROJ = 512
TN_PROJ = 512
TM_MOE = 256
TM_COMBINE = 256
SB_HEADS_PER_STEP = 16
SUFFIX_TERMS = 2
ROW_CHUNK = 16
WEIGHT_DMA_PRIORITY = 1


def _cparams(sem):
    return pltpu.CompilerParams(dimension_semantics=sem,
                                vmem_limit_bytes=VMEM_LIMIT_BYTES)


def _pack_halves(lo, hi):
    return lax.bitcast_convert_type(
        pltpu.pack_elementwise([lo, hi], packed_dtype=BF16), U32)


def _unpack_halves(words):
    return tuple(
        pltpu.unpack_elementwise(words, index=k, packed_dtype=BF16,
                                 unpacked_dtype=F32).astype(BF16) for k in (0, 1))


def _dot_halves(lo, hi, w_ref):
    half = lo.shape[1]
    return (jnp.dot(lo, w_ref[:half], preferred_element_type=F32)
            + jnp.dot(hi, w_ref[half:], preferred_element_type=F32))


WORD_ROWS = (D_MODEL // 2) // LANES
SLAB_PAIR = 2 * LANES


def _slab_words(slab_ref, pair, tokens):
    return jnp.concatenate(
        [slab_ref[pl.ds(2 * pair + k, tokens, stride=WORD_ROWS), :] for k in (0, 1)],
        axis=1)


def _dot_slab(slab_ref, tokens, w_refs):
    half = D_MODEL // 2
    outs = [None] * len(w_refs)
    for pair in range(half // SLAB_PAIR):
        lo, hi = _unpack_halves(_slab_words(slab_ref, pair, tokens))
        rows_lo = slice(pair * SLAB_PAIR, (pair + 1) * SLAB_PAIR)
        rows_hi = slice(half + pair * SLAB_PAIR, half + (pair + 1) * SLAB_PAIR)
        for n, w_ref in enumerate(w_refs):
            part = (jnp.dot(lo, w_ref[rows_lo], preferred_element_type=F32)
                    + jnp.dot(hi, w_ref[rows_hi], preferred_element_type=F32))
            outs[n] = part if outs[n] is None else outs[n] + part
    return outs


def _dot_nt(a, b):
    return lax.dot_general(a, b, (((1,), (1,)), ((), ())),
                           preferred_element_type=F32)


def _inproj_kernel(x_ref, w_ref, b_ref, o_ref, xb_ref):
    @pl.when(pl.program_id(1) == 0)
    def _():
        xb_ref[...] = x_ref[...].astype(BF16)

    acc = jnp.dot(xb_ref[...], w_ref[...], preferred_element_type=F32)
    o_ref[...] = (acc + b_ref[...]).astype(o_ref.dtype)


def _in_projection(x2, w_b, b):
    m, k = x2.shape
    n = w_b.shape[1]
    n_blocks = n // TN_PROJ
    assert A_WIDTH % TN_PROJ == 0
    a_blocks = A_WIDTH // TN_PROJ

    def src(j):
        return lax.rem(j + a_blocks, n_blocks)

    return pl.pallas_call(
        _inproj_kernel,
        out_shape=jax.ShapeDtypeStruct((m, n), BF16),
        grid=(m // TM_IN, n_blocks),
        in_specs=[
            pl.BlockSpec((TM_IN, k), lambda i, j: (i, 0)),
            pl.BlockSpec((k, TN_PROJ), lambda i, j: (0, src(j))),
            pl.BlockSpec((1, TN_PROJ), lambda i, j: (0, src(j))),
        ],
        out_specs=pl.BlockSpec((TM_IN, TN_PROJ), lambda i, j: (i, j)),
        scratch_shapes=[pltpu.VMEM((TM_IN, k), BF16)],
        compiler_params=_cparams(("arbitrary", "arbitrary")),
        name="in_projection",
    )(x2, w_b, b)


def _swa_kernel(sinks_ref, q_ref, kc_ref, vc_ref, vp_ref,
                posc_ref, invf_ref, o_ref, kprev_ref):
    n = pl.program_id(1)

    @pl.when(n == 0)
    def _():
        kprev_ref[...] = jnp.zeros_like(kprev_ref)
    lane = lax.broadcasted_iota(I32, (1, LANES), 1)
    first_half = (lane % SWA_HEAD_DIM) < (SWA_HEAD_DIM // 2)

    def tables(pos_ref):
        ang = pos_ref[...].astype(F32) * invf_ref[...]
        sin = jnp.sin(ang)
        return jnp.cos(ang), jnp.where(first_half, -sin, sin)

    def rope(x, cos, sin_signed):
        partner = jnp.where(first_half,
                            pltpu.roll(x, LANES - SWA_HEAD_DIM // 2, 1),
                            pltpu.roll(x, SWA_HEAD_DIM // 2, 1))
        return x * cos + partner * sin_signed

    cos_c, sin_c = tables(posc_ref)

    def rope_block(ref, cos, sin_signed):
        width = ref.shape[1]
        return [rope(ref[:, c * LANES:(c + 1) * LANES].astype(F32), cos,
                     sin_signed).astype(BF16) for c in range(width // LANES)]

    q_chunks = rope_block(q_ref, cos_c, sin_c)
    k_cur = rope_block(kc_ref, cos_c, sin_c)
    k_chunks = [jnp.concatenate([kprev_ref[:, c * LANES:(c + 1) * LANES], k_c], axis=0)
                for c, k_c in enumerate(k_cur)]
    for c, k_c in enumerate(k_cur):
        kprev_ref[:, c * LANES:(c + 1) * LANES] = k_c
    v_all = jnp.concatenate([vp_ref[...], vc_ref[...]], axis=0)

    qi = lax.broadcasted_iota(I32, (BLOCK, 2 * BLOCK), 0)
    kj = lax.broadcasted_iota(I32, (BLOCK, 2 * BLOCK), 1)
    rel = qi - (kj - BLOCK)
    valid = (rel >= 0) & (rel < BLOCK) & ((kj >= BLOCK) | (n > 0))

    def head_slice(chunks, head):
        half = head % 2
        return chunks[head // 2][:, half * SWA_HEAD_DIM:(half + 1) * SWA_HEAD_DIM]

    scale = 1.0 / math.sqrt(SWA_HEAD_DIM)
    scores = []
    for h in range(SWA_KV_HEADS):
        k_h = head_slice(k_chunks, h)
        q_h = jnp.concatenate(
            [head_slice(q_chunks, h * SWA_GROUP + g) for g in range(SWA_GROUP)],
            axis=0)
        scores.append(_dot_nt(q_h, k_h) * scale)
    weights, sink_terms = [], []
    for h in range(SWA_KV_HEADS):
        weights_h = []
        for g in range(SWA_GROUP):
            sink = sinks_ref[h * SWA_GROUP + g]
            s_g = jnp.where(valid, scores[h][g * BLOCK:(g + 1) * BLOCK], NEG_INF)
            m = jnp.maximum(jnp.max(s_g, axis=-1, keepdims=True), sink)
            weights_h.append(jnp.exp(s_g - m).astype(BF16))
            sink_terms.append(jnp.exp(sink - m))
        weights.append(jnp.concatenate(weights_h, axis=0))
    ones = jnp.ones((2 * BLOCK, SWA_HEAD_DIM), BF16)
    for h in range(SWA_KV_HEADS):
        v_h = v_all[:, h * SWA_HEAD_DIM:(h + 1) * SWA_HEAD_DIM]
        both = jnp.dot(weights[h], jnp.concatenate([v_h, ones], axis=1),
                       preferred_element_type=F32)
        for g in range(SWA_GROUP):
            rows = slice(g * BLOCK, (g + 1) * BLOCK)
            den = both[rows, SWA_HEAD_DIM:SWA_HEAD_DIM + 1] + sink_terms[h * SWA_GROUP + g]
            col = (h * SWA_GROUP + g) * SWA_HEAD_DIM
            o_ref[:, col:col + SWA_HEAD_DIM] = both[rows, :SWA_HEAD_DIM] / den


def _swa_attention(proj, pos2, inv_freq, sinks, batch, seq):
    nb = seq // BLOCK
    assert Q_A_COL % SWA_WIDTH == 0 and K_A_COL % SWA_KV_WIDTH == 0
    assert V_A_COL % SWA_KV_WIDTH == 0
    qcol = Q_A_COL // SWA_WIDTH
    kcol = K_A_COL // SWA_KV_WIDTH
    vcol = V_A_COL // SWA_KV_WIDTH

    def cur(b, n):
        return b * nb + n

    def prev(b, n):
        return b * nb + jnp.maximum(n - 1, 0)

    return pl.pallas_call(
        _swa_kernel,
        out_shape=jax.ShapeDtypeStruct((batch * seq, SWA_WIDTH), F32),
        grid=(batch, nb),
        in_specs=[
            pl.BlockSpec(memory_space=pltpu.SMEM),
            pl.BlockSpec((BLOCK, SWA_WIDTH), lambda b, n: (cur(b, n), qcol)),
            pl.BlockSpec((BLOCK, SWA_KV_WIDTH), lambda b, n: (cur(b, n), kcol)),
            pl.BlockSpec((BLOCK, SWA_KV_WIDTH), lambda b, n: (cur(b, n), vcol)),
            pl.BlockSpec((BLOCK, SWA_KV_WIDTH), lambda b, n: (prev(b, n), vcol)),
            pl.BlockSpec((BLOCK, 1), lambda b, n: (cur(b, n), 0)),
            pl.BlockSpec((1, LANES), lambda b, n: (0, 0)),
        ],
        out_specs=pl.BlockSpec((BLOCK, SWA_WIDTH), lambda b, n: (cur(b, n), 0)),
        scratch_shapes=[pltpu.VMEM((BLOCK, SWA_KV_WIDTH), BF16)],
        compiler_params=_cparams(("arbitrary", "arbitrary")),
        name="swa_attention",
    )(sinks, proj, proj, proj, proj, pos2, inv_freq)


def _sb_kernel(q_ref, k_ref, v_ref, o_ref):
    n = pl.program_id(2)
    heads = [slice(h * SB_HEAD_DIM, (h + 1) * SB_HEAD_DIM)
             for h in range(SB_HEADS_PER_STEP)]
    rows = SB_HEADS_PER_STEP * BLOCK
    scale = 1.0 / math.sqrt(SB_HEAD_DIM)
    key_j = lax.broadcasted_iota(I32, (BLOCK, 2 * BLOCK), 0)
    out_c = lax.broadcasted_iota(I32, (BLOCK, 2 * BLOCK), 1)
    later_total = ((key_j > out_c) | (out_c >= BLOCK)).astype(BF16)
    q_row = lax.broadcasted_iota(I32, (rows, BLOCK), 0) % BLOCK
    k_col = lax.broadcasted_iota(I32, (rows, BLOCK), 1)
    causal = k_col < q_row

    def block(kb, carry, acc, mask):
        start = pl.multiple_of(kb * BLOCK, BLOCK)
        z = jnp.concatenate(
            [_dot_nt(q_ref[:, hd], k_ref[pl.ds(start, BLOCK), hd]) for hd in heads],
            axis=0) * scale
        t = jnp.log(1.0 + jnp.exp(-jnp.abs(z)))
        log_not = jnp.minimum(-z, 0.0) - t
        log_beta = log_not + z
        if mask is not None:
            log_not = jnp.where(mask, log_not, 0.0)
        terms, rest = [], log_not
        for _ in range(SUFFIX_TERMS):
            terms.append(rest.astype(BF16))
            rest = rest - terms[-1].astype(F32)
        parts = jnp.dot(jnp.concatenate(terms, axis=0), later_total,
                        preferred_element_type=F32)
        sums = parts[:rows]
        for t in range(1, SUFFIX_TERMS):
            sums = sums + parts[t * rows:(t + 1) * rows]
        a = jnp.exp(log_beta + sums[:, :BLOCK] + carry)
        if mask is not None:
            a = jnp.where(mask, a, 0.0)
        a = a.astype(BF16)
        pv = jnp.concatenate(
            [jnp.dot(a[h * BLOCK:(h + 1) * BLOCK], v_ref[pl.ds(start, BLOCK), hd],
                     preferred_element_type=F32) for h, hd in enumerate(heads)], axis=0)
        return carry + sums[:, BLOCK:], acc + pv

    zeros = jnp.zeros((rows, BLOCK), F32)
    carry, acc = block(n, zeros, zeros, causal)

    def cond(state):
        kb, carry, _ = state
        return jnp.logical_and(kb >= 0, jnp.max(carry) > EXP_ZERO_LOG)

    def body(state):
        kb, carry, acc = state
        carry, acc = block(kb, carry, acc, None)
        return kb - 1, carry, acc

    _, _, acc = lax.while_loop(cond, body, (n - 1, carry, acc))
    for h, hd in enumerate(heads):
        o_ref[:, hd] = acc[h * BLOCK:(h + 1) * BLOCK]


def _sb_attention(proj, batch, seq):
    nb = seq // BLOCK
    width = SB_HEADS_PER_STEP * SB_HEAD_DIM
    assert Q_B_COL % width == 0 and K_B_COL % width == 0 and V_B_COL % width == 0
    qcol = Q_B_COL // width
    kcol = K_B_COL // width
    vcol = V_B_COL // width
    return pl.pallas_call(
        _sb_kernel,
        out_shape=jax.ShapeDtypeStruct((batch * seq, SB_WIDTH), F32),
        grid=(batch, SB_HEADS // SB_HEADS_PER_STEP, nb),
        in_specs=[
            pl.BlockSpec((BLOCK, width), lambda b, h, n: (b * nb + n, qcol + h)),
            pl.BlockSpec((seq, width), lambda b, h, n: (b, kcol + h),
                         pipeline_mode=pl.Buffered(1)),
            pl.BlockSpec((seq, width), lambda b, h, n: (b, vcol + h),
                         pipeline_mode=pl.Buffered(1)),
        ],
        out_specs=pl.BlockSpec((BLOCK, width), lambda b, h, n: (b * nb + n, h)),
        compiler_params=_cparams(("arbitrary", "arbitrary", "arbitrary")),
        name="sb_attention",
    )(proj, proj, proj)


def _row_stats(chunks_ref, n_chunks, width):
    total = chunks_ref[0].sum(axis=-1, keepdims=True)
    for c in range(1, n_chunks):
        total = total + chunks_ref[c].sum(axis=-1, keepdims=True)
    mu = total / width
    sq = jnp.square(chunks_ref[0] - mu).sum(axis=-1, keepdims=True)
    for c in range(1, n_chunks):
        sq = sq + jnp.square(chunks_ref[c] - mu).sum(axis=-1, keepdims=True)
    return mu, lax.rsqrt(sq / width + LN_EPS)


def _outproj_kernel(oa_ref, ob_ref, ga_ref, gb_ref, wa_ref, wb_ref, x_ref, bo_ref,
                    lg_ref, lb_ref, y_ref, mu_ref, rs_ref, x1s_ref,
                    ma_ref, mb_ref, acc_ref):
    j = pl.program_id(1)
    n_chunks = acc_ref.shape[0]
    tn = acc_ref.shape[2]

    @pl.when(j == 0)
    def _():
        for o_ref, g_ref, m_ref in ((oa_ref, ga_ref, ma_ref), (ob_ref, gb_ref, mb_ref)):
            width = o_ref.shape[1]
            cols = [slice(c * tn, (c + 1) * tn) for c in range(width // tn)]
            sq = sum(jnp.square(o_ref[:, sl]).sum(axis=-1, keepdims=True) for sl in cols)
            r = lax.rsqrt(sq / width + RMS_EPS)
            for sl in cols:
                m_ref[:, sl] = (o_ref[:, sl] * r * g_ref[:, sl]).astype(BF16)

    mix = (jnp.dot(ma_ref[...], wa_ref[...], preferred_element_type=F32)
           + jnp.dot(mb_ref[...], wb_ref[...], preferred_element_type=F32))
    y = DEEPNORM_ALPHA * x_ref[...] + (mix + bo_ref[...])
    y_ref[...] = y
    acc_ref[j] = y

    @pl.when(j == n_chunks - 1)
    def _():
        mu, rs = _row_stats(acc_ref, n_chunks, n_chunks * tn)
        mu_ref[...] = mu
        rs_ref[...] = rs

        def normed(c):
            sl = slice(c * tn, (c + 1) * tn)
            return (acc_ref[c] - mu) * rs * lg_ref[:, sl] + lb_ref[:, sl]

        tm = acc_ref.shape[1]
        for c in range(n_chunks // 2):
            words = _pack_halves(normed(c), normed(c + n_chunks // 2))
            for q in range(tn // LANES):
                x1s_ref[pl.ds(c * (tn // LANES) + q, tm, stride=WORD_ROWS), :] = (
                    words[:, q * LANES:(q + 1) * LANES])


def _out_projection(o_a, o_b, g_a, g_b, w_b, x2, b_out, ln_g, ln_b):
    m, d = x2.shape
    tm, tn = TM_PROJ, TN_PROJ
    half = o_a.shape[1]
    once = dict(pipeline_mode=pl.Buffered(1))
    return pl.pallas_call(
        _outproj_kernel,
        out_shape=(jax.ShapeDtypeStruct((m, d), F32),
                   jax.ShapeDtypeStruct((m, 1), F32),
                   jax.ShapeDtypeStruct((m, 1), F32),
                   jax.ShapeDtypeStruct((m * WORD_ROWS, LANES), U32)),
        grid=(m // tm, d // tn),
        in_specs=[
            pl.BlockSpec((tm, half), lambda i, j: (i, 0), **once),
            pl.BlockSpec((tm, half), lambda i, j: (i, 0), **once),
            pl.BlockSpec((1, half), lambda i, j: (0, 0)),
            pl.BlockSpec((1, half), lambda i, j: (0, 0)),
            pl.BlockSpec((half, tn), lambda i, j: (0, j)),
            pl.BlockSpec((half, tn), lambda i, j: (1, j)),
            pl.BlockSpec((tm, tn), lambda i, j: (i, j)),
            pl.BlockSpec((1, tn), lambda i, j: (0, j)),
            pl.BlockSpec((1, d), lambda i, j: (0, 0)),
            pl.BlockSpec((1, d), lambda i, j: (0, 0)),
        ],
        out_specs=(pl.BlockSpec((tm, tn), lambda i, j: (i, j)),
                   pl.BlockSpec((tm, 1), lambda i, j: (i, 0)),
                   pl.BlockSpec((tm, 1), lambda i, j: (i, 0)),
                   pl.BlockSpec((tm * WORD_ROWS, LANES), lambda i, j: (i, 0))),
        scratch_shapes=[pltpu.VMEM((tm, half), BF16), pltpu.VMEM((tm, half), BF16),
                        pltpu.VMEM((d // tn, tm, tn), F32)],
        compiler_params=_cparams(("arbitrary", "arbitrary")),
        name="out_projection_ln1",
    )(o_a, o_b, g_a, g_b, w_b, w_b, x2, b_out, ln_g, ln_b)


ROUTER_COLS = LANES
EXPERT_LANE0 = N_GROUPS


def _router_kernel(x_ref, w_ref, b_ref, ids_ref, wts_ref, cnt_ref, carry_ref):
    i = pl.program_id(0)
    tm = wts_ref.shape[0]

    @pl.when(i == 0)
    def _():
        carry_ref[...] = jnp.zeros_like(carry_ref)

    logits = _dot_slab(x_ref, tm, [w_ref])[0] + b_ref[...]
    lane = lax.broadcasted_iota(I32, (tm, ROUTER_COLS), 1)
    big = jnp.int32(ROUTER_COLS)

    def first_argmax(vals):
        top = jnp.max(vals, axis=-1, keepdims=True)
        idx = jnp.min(jnp.where(vals == top, lane, big), axis=-1, keepdims=True)
        return top, idx

    is_group = lane < N_GROUPS
    g_logits = jnp.where(is_group, logits, -jnp.inf)
    g_top, g_idx = first_argmax(g_logits)
    g_w = 1.0 / jnp.sum(jnp.exp(g_logits - g_top), axis=-1, keepdims=True)

    first = EXPERT_LANE0 + g_idx * EXPERTS_PER_GROUP
    in_group = (lane >= first) & (lane < first + EXPERTS_PER_GROUP)
    e_logits = jnp.where(in_group, logits, -jnp.inf)
    top1, idx1 = first_argmax(e_logits)
    top2, idx2 = first_argmax(jnp.where(lane == idx1, -jnp.inf, e_logits))
    e2 = jnp.exp(top2 - top1)
    w1 = g_w / (1.0 + e2)
    w2 = g_w * e2 / (1.0 + e2)

    hit1 = lane == idx1
    hit2 = lane == idx2
    onehot = (hit1 | hit2).astype(BF16)
    r = lax.broadcasted_iota(I32, (tm, tm), 0)
    c = lax.broadcasted_iota(I32, (tm, tm), 1)
    before = (c < r).astype(BF16)
    prior = jnp.dot(before, onehot, preferred_element_type=F32) + carry_ref[0:1, :]
    rank1 = jnp.sum(jnp.where(hit1, prior, 0.0), axis=-1, keepdims=True)
    rank2 = jnp.sum(jnp.where(hit2, prior, 0.0), axis=-1, keepdims=True)
    counts = carry_ref[0:1, :] + jnp.sum(onehot.astype(F32), axis=0, keepdims=True)
    carry_ref[...] = jnp.broadcast_to(counts, carry_ref.shape)
    cnt_ref[...] = jnp.broadcast_to(counts, cnt_ref.shape).astype(I32)

    ids = jnp.where(lane == 0, idx1 - EXPERT_LANE0,
          jnp.where(lane == 1, idx2 - EXPERT_LANE0,
          jnp.where(lane == 2, rank1.astype(I32),
          jnp.where(lane == 3, rank2.astype(I32), 0))))
    ids_ref[...] = ids.T[:ids_ref.shape[0]]
    wts_ref[...] = jnp.where(lane == 0, w1, jnp.where(lane == 1, w2, 0.0))


def _router(x1s, w_r, b_r):
    m = x1s.shape[0] // WORD_ROWS
    tm = TM_PROJ
    return pl.pallas_call(
        _router_kernel,
        out_shape=(jax.ShapeDtypeStruct((SUBLANES, m), I32),
                   jax.ShapeDtypeStruct((m, ROUTER_COLS), F32),
                   jax.ShapeDtypeStruct((SUBLANES, ROUTER_COLS), I32)),
        grid=(m // tm,),
        in_specs=[
            pl.BlockSpec((tm * WORD_ROWS, LANES), lambda i: (i, 0)),
            pl.BlockSpec((D_MODEL, ROUTER_COLS), lambda i: (0, 0)),
            pl.BlockSpec((1, ROUTER_COLS), lambda i: (0, 0)),
        ],
        out_specs=(pl.BlockSpec((SUBLANES, tm), lambda i: (0, i)),
                   pl.BlockSpec((tm, ROUTER_COLS), lambda i: (i, 0)),
                   pl.BlockSpec((SUBLANES, ROUTER_COLS), lambda i: (0, 0))),
        scratch_shapes=[pltpu.VMEM((SUBLANES, ROUTER_COLS), F32)],
        compiler_params=_cparams(("arbitrary",)),
        name="router",
    )(x1s, w_r, b_r)


def _start_row_gather(idx_ref, first, n_items, span, src_ref, dst_ref, sem,
                      both_threads=False):
    def body(c, _):
        for u in range(ROW_CHUNK):
            r = c * ROW_CHUNK + u
            src_row = pl.multiple_of(idx_ref[first + r], span)
            dst_row = pl.multiple_of(r * span, span)
            pltpu.make_async_copy(src_ref.at[pl.ds(src_row, span)],
                                  dst_ref.at[pl.ds(dst_row, span)], sem).start(
                                      priority=u % 2 if both_threads else 0)
        return 0

    lax.fori_loop(0, n_items // ROW_CHUNK, body, 0)


def _wait_row_gather(n_items, span, src_ref, dst_ref, sem):
    def body(c, _):
        for _u in range(ROW_CHUNK):
            pltpu.make_async_copy(src_ref.at[pl.ds(0, span)], dst_ref.at[pl.ds(0, span)],
                                  sem).wait()
        return 0

    lax.fori_loop(0, n_items // ROW_CHUNK, body, 0)


def _expert_changed(te_ref, i):
    return jnp.logical_or(i == 0, te_ref[i] != te_ref[jnp.maximum(i - 1, 0)])


def _tile_row(i, nu):
    return jnp.minimum(i, nu[0] - 1)


def _stage_expert_weights(i, te_ref, nxt_ref, ws_ref, w_hbms, stage_ref, bf_refs, sem):
    expert = te_ref[i]
    slot = ws_ref[expert]

    def copies(e, s):
        return [pltpu.make_async_copy(w.at[e], stage_ref.at[s, n], sem.at[s])
                for n, w in enumerate(w_hbms)]

    @pl.when(i == 0)
    def _():
        for c in copies(expert, slot):
            c.start(priority=WEIGHT_DMA_PRIORITY)

    @pl.when(_expert_changed(te_ref, i))
    def _():
        for c in copies(expert, slot):
            c.wait()
        nxt = nxt_ref[expert]

        @pl.when(nxt < N_EXPERTS)
        def _():
            for c in copies(nxt, 1 - slot):
                c.start(priority=WEIGHT_DMA_PRIORITY)

        for n, bf_ref in enumerate(bf_refs):
            bf_ref[...] = stage_ref[slot, n].astype(BF16)


def _gateup_kernel(te_ref, nu_ref, tos_ref, nxt_ref, ws_ref, x_hbm, wg_hbm, wu_hbm, a_ref,
                   xbuf_ref, stage_ref, wgb_ref, wub_ref, sems, wsem):
    i = pl.program_id(0)
    n_used = nu_ref[0]
    slot = lax.rem(i, 2)

    def start(tile, buf):
        _start_row_gather(tos_ref, tile * TM_MOE, TM_MOE, WORD_ROWS, x_hbm,
                          xbuf_ref.at[buf], sems.at[buf])

    @pl.when(i == 0)
    def _():
        start(0, 0)

    @pl.when(i + 1 < n_used)
    def _():
        start(i + 1, 1 - slot)

    _stage_expert_weights(i, te_ref, nxt_ref, ws_ref, [wg_hbm, wu_hbm], stage_ref,
                          [wgb_ref, wub_ref], wsem)

    @pl.when(i < n_used)
    def _():
        _wait_row_gather(TM_MOE, WORD_ROWS, x_hbm, xbuf_ref.at[slot], sems.at[slot])
        gate, up = _dot_slab(xbuf_ref.at[slot], TM_MOE, [wgb_ref, wub_ref])
        a_ref[...] = (gate * jax.nn.sigmoid(gate) * up).astype(a_ref.dtype)

    @pl.when(i >= n_used)
    def _():
        a_ref[...] = jnp.zeros_like(a_ref)


def _grouped_gate_up(tile_expert, n_used, slab_of_slot, next_expert, stage_slot,
                     x1s, w_gate, w_up):
    p_rows = slab_of_slot.shape[0]
    _, d, f = w_gate.shape
    hbm = pl.BlockSpec(memory_space=pl.ANY)
    return pl.pallas_call(
        _gateup_kernel,
        out_shape=jax.ShapeDtypeStruct((p_rows, f), BF16),
        grid_spec=pltpu.PrefetchScalarGridSpec(
            num_scalar_prefetch=5,
            grid=(p_rows // TM_MOE,),
            in_specs=[hbm, hbm, hbm],
            out_specs=pl.BlockSpec((TM_MOE, f), lambda i, *_: (i, 0)),
            scratch_shapes=[pltpu.VMEM((2, TM_MOE * WORD_ROWS, LANES), U32),
                            pltpu.VMEM((2, 2, d, f), F32),
                            pltpu.VMEM((d, f), BF16), pltpu.VMEM((d, f), BF16),
                            pltpu.SemaphoreType.DMA((2,)), pltpu.SemaphoreType.DMA((2,))],
        ),
        compiler_params=_cparams(("arbitrary",)),
        name="moe_gate_up",
    )(tile_expert, n_used, slab_of_slot, next_expert, stage_slot, x1s, w_gate, w_up)


def _down_kernel(te_ref, nu_ref, nxt_ref, ws_ref, a_ref, wd_hbm, y_ref,
                 stage_ref, wdb_ref, wsem):
    i = pl.program_id(0)
    _stage_expert_weights(i, te_ref, nxt_ref, ws_ref, [wd_hbm], stage_ref, [wdb_ref], wsem)

    @pl.when(i < nu_ref[0])
    def _():
        y_ref[:, 0, :] = jnp.dot(a_ref[...], wdb_ref[...], preferred_element_type=F32)

    @pl.when(i >= nu_ref[0])
    def _():
        y_ref[...] = jnp.zeros_like(y_ref)


def _grouped_down(tile_expert, n_used, next_expert, stage_slot, act, w_down):
    p_rows, f = act.shape
    d = w_down.shape[2]
    return pl.pallas_call(
        _down_kernel,
        out_shape=jax.ShapeDtypeStruct((p_rows, 1, d), F32),
        grid_spec=pltpu.PrefetchScalarGridSpec(
            num_scalar_prefetch=4,
            grid=(p_rows // TM_MOE,),
            in_specs=[
                pl.BlockSpec((TM_MOE, f), lambda i, te, nu, *_: (_tile_row(i, nu), 0)),
                pl.BlockSpec(memory_space=pl.ANY),
            ],
            out_specs=pl.BlockSpec((TM_MOE, 1, d), lambda i, *_: (i, 0, 0)),
            scratch_shapes=[pltpu.VMEM((2, 1, f, d), F32), pltpu.VMEM((f, d), BF16),
                            pltpu.SemaphoreType.DMA((2,))],
        ),
        compiler_params=_cparams(("arbitrary",)),
        name="moe_down",
    )(tile_expert, n_used, next_expert, stage_slot, act, w_down)


def _combine_kernel(slot_ref, y_hbm, wts_ref, o_ref, buf_ref, sems):
    i = pl.program_id(0)
    tm = o_ref.shape[0]
    n_tokens = slot_ref.shape[0] // EXPERT_TOP_K
    slot = lax.rem(i, 2)

    def start(tile, buf):
        for k in range(EXPERT_TOP_K):
            _start_row_gather(slot_ref, k * n_tokens + tile * tm, tm, 1, y_hbm,
                              buf_ref.at[buf, k], sems.at[buf], both_threads=True)

    @pl.when(i == 0)
    def _():
        start(0, 0)

    @pl.when(i + 1 < pl.num_programs(0))
    def _():
        start(i + 1, 1 - slot)

    _wait_row_gather(EXPERT_TOP_K * tm, 1, y_hbm, buf_ref.at[slot, 0], sems.at[slot])
    for c in range(o_ref.shape[1] // TN_PROJ):
        sl = slice(c * TN_PROJ, (c + 1) * TN_PROJ)
        o_ref[:, sl] = sum(wts_ref[:, k:k + 1] * buf_ref[slot, k, :, 0, sl]
                           for k in range(EXPERT_TOP_K))


def _combine_experts(slot_kt, y_sorted, wts):
    d = y_sorted.shape[2]
    m = slot_kt.shape[0] // EXPERT_TOP_K
    tm = TM_COMBINE
    return pl.pallas_call(
        _combine_kernel,
        out_shape=jax.ShapeDtypeStruct((m, d), F32),
        grid_spec=pltpu.PrefetchScalarGridSpec(
            num_scalar_prefetch=1,
            grid=(m // tm,),
            in_specs=[
                pl.BlockSpec(memory_space=pl.ANY),
                pl.BlockSpec((tm, ROUTER_COLS), lambda i, sl: (i, 0)),
            ],
            out_specs=pl.BlockSpec((tm, d), lambda i, sl: (i, 0)),
            scratch_shapes=[pltpu.VMEM((2, EXPERT_TOP_K, tm, 1, d), F32),
                            pltpu.SemaphoreType.DMA((2,))],
        ),
        compiler_params=_cparams(("arbitrary",)),
        name="moe_combine",
    )(slot_kt, y_sorted, wts)


def _final_kernel(x1s_ref, wg_ref, bg_ref, p_ref, wp_ref, y1_ref, mu_ref, rs_ref,
                  l1g_ref, l1b_ref, moe_ref, l2g_ref, l2b_ref,
                  o_ref, lo_ref, hi_ref, acc_ref, mu2_ref, rs2_ref):
    i = pl.program_id(0)
    j = pl.program_id(1)
    n_tiles = pl.num_programs(0) - 1
    n_chunks = acc_ref.shape[1]
    tn = acc_ref.shape[3]
    cur = lax.rem(i, 2)
    prv = 1 - cur

    @pl.when(jnp.logical_and(j == 0, i < n_tiles))
    def _():
        for s in range(WORD_ROWS):
            sl = slice(s * LANES, (s + 1) * LANES)
            lo_ref[:, sl], hi_ref[:, sl] = _unpack_halves(
                x1s_ref[pl.ds(s, lo_ref.shape[0], stride=WORD_ROWS), :])

    def build():
        gate = _dot_halves(lo_ref[...], hi_ref[...], wg_ref) + bg_ref[...]
        emb = jnp.dot(p_ref[...].astype(BF16), wp_ref[...].astype(BF16),
                      preferred_element_type=F32)
        x1 = (y1_ref[...] - mu_ref[...]) * rs_ref[...] * l1g_ref[...] + l1b_ref[...]
        acc_ref[cur, j] = DEEPNORM_ALPHA * x1 + moe_ref[...] + jax.nn.sigmoid(gate) * emb

    def emit():
        o_ref[...] = ((acc_ref[prv, j] - mu2_ref[prv]) * rs2_ref[prv]
                      * l2g_ref[...] + l2b_ref[...])

    @pl.when(i == 0)
    def _():
        build()

    @pl.when(jnp.logical_and(i > 0, i < n_tiles))
    def _():
        emit()
        build()

    @pl.when(i == n_tiles)
    def _():
        emit()

    @pl.when(jnp.logical_and(j == n_chunks - 1, i < n_tiles))
    def _():
        mu2_ref[cur], rs2_ref[cur] = _row_stats(acc_ref.at[cur], n_chunks, n_chunks * tn)


def _final_stage(x1s, w_pg_b, b_pg, p2, w_pp, y1, mu1, rs1, ln1_g, ln1_b,
                 moe, ln2_g, ln2_b):
    m, d = y1.shape
    tm, tn = TM_PROJ, TN_PROJ
    ple = p2.shape[1]
    n_chunks = d // tn

    n_tiles = m // tm

    def built(i):
        return jnp.minimum(i, n_tiles - 1)

    def emitted(i):
        return jnp.maximum(i - 1, 0)

    return pl.pallas_call(
        _final_kernel,
        out_shape=jax.ShapeDtypeStruct((m, d), F32),
        grid=(n_tiles + 1, n_chunks),
        in_specs=[
            pl.BlockSpec((tm * WORD_ROWS, LANES), lambda i, j: (built(i), 0),
                         pipeline_mode=pl.Buffered(1)),
            pl.BlockSpec((d, tn), lambda i, j: (0, j)),
            pl.BlockSpec((1, tn), lambda i, j: (0, j)),
            pl.BlockSpec((tm, ple), lambda i, j: (built(i), 0)),
            pl.BlockSpec((ple, tn), lambda i, j: (0, j)),
            pl.BlockSpec((tm, tn), lambda i, j: (built(i), j)),
            pl.BlockSpec((tm, 1), lambda i, j: (built(i), 0)),
            pl.BlockSpec((tm, 1), lambda i, j: (built(i), 0)),
            pl.BlockSpec((1, tn), lambda i, j: (0, j)),
            pl.BlockSpec((1, tn), lambda i, j: (0, j)),
            pl.BlockSpec((tm, tn), lambda i, j: (built(i), j)),
            pl.BlockSpec((1, tn), lambda i, j: (0, j)),
            pl.BlockSpec((1, tn), lambda i, j: (0, j)),
        ],
        out_specs=pl.BlockSpec((tm, tn), lambda i, j: (emitted(i), jnp.where(i == 0, 0, j))),
        scratch_shapes=[pltpu.VMEM((tm, d // 2), BF16), pltpu.VMEM((tm, d // 2), BF16),
                        pltpu.VMEM((2, n_chunks, tm, tn), F32),
                        pltpu.VMEM((2, tm, 1), F32), pltpu.VMEM((2, tm, 1), F32)],
        compiler_params=_cparams(("arbitrary", "arbitrary")),
        name="ple_moe_ln2",
    )(x1s, w_pg_b, b_pg, p2, w_pp, y1, mu1, rs1, ln1_g, ln1_b, moe, ln2_g, ln2_b)


def _routing_tables(ids_t, counts_row):
    counts = counts_row[EXPERT_LANE0:EXPERT_LANE0 + N_EXPERTS]
    tiles = (counts + TM_MOE - 1) // TM_MOE
    tile_end = jnp.cumsum(tiles)
    offsets = (tile_end - tiles) * TM_MOE
    n_used = tile_end[-1:]
    n_tokens = ids_t.shape[1]
    experts = ids_t[0:EXPERT_TOP_K]
    hit = experts[None] == jnp.arange(N_EXPERTS, dtype=I32)[:, None, None]
    first_slot = jnp.sum(jnp.where(hit, offsets[:, None, None], 0), axis=0)
    slots = (first_slot + ids_t[EXPERT_TOP_K:2 * EXPERT_TOP_K]).reshape(-1)
    n_tiles = (EXPERT_TOP_K * n_tokens) // TM_MOE + N_EXPERTS
    tile_ids = jnp.minimum(jnp.arange(n_tiles, dtype=I32), n_used - 1)
    tile_expert = jnp.sum(tile_end[None, :] <= tile_ids[:, None], axis=1).astype(I32)
    slab_of_slot = jnp.zeros((n_tiles * TM_MOE,), I32).at[slots].set(
        jnp.arange(slots.size, dtype=I32) % n_tokens * WORD_ROWS, unique_indices=True)
    eid = jnp.arange(N_EXPERTS, dtype=I32)
    later_owner = (eid[None, :] > eid[:, None]) & (tiles[None, :] > 0)
    next_expert = jnp.min(jnp.where(later_owner, eid[None, :], N_EXPERTS), axis=1).astype(I32)
    stage_slot = ((jnp.cumsum(tiles > 0) - 1) % 2).astype(I32)
    return (slots.astype(I32), slab_of_slot, tile_expert, n_used.astype(I32),
            next_expert, stage_slot)


def kernel(x, p, positions, w_in, b_in, sinks, g_norm_a, g_norm_b, w_out, b_out,
           ln1_g, ln1_b, w_group, b_group, w_er, b_er, w_gate, w_up, w_down,
           w_ple_gate, b_ple_gate, w_ple_proj, ln2_g, ln2_b):
    batch, seq, d = x.shape
    m = batch * seq
    row = lambda v: v.reshape(1, -1)
    x2 = x.reshape(m, d)
    for i in range(DEPTH):
        proj = _in_projection(x2, w_in[i].astype(BF16), row(b_in[i]))
        inv_freq = ROPE_THETA ** (-jnp.arange(0, SWA_HEAD_DIM, 2, dtype=F32) / SWA_HEAD_DIM)
        inv_freq = jnp.tile(inv_freq, LANES // inv_freq.shape[0]).reshape(1, LANES)
        o_a = _swa_attention(proj, positions.reshape(m, 1), inv_freq, sinks[i], batch, seq)
        o_b = _sb_attention(proj, batch, seq)
        y1, mu1, rs1, x1s = _out_projection(
            o_a, o_b, row(g_norm_a[i]), row(g_norm_b[i]), w_out[i].astype(BF16), x2,
            row(b_out[i]), row(ln1_g[i]), row(ln1_b[i]))
        pad = ROUTER_COLS - N_GROUPS - N_EXPERTS
        w_r = jnp.concatenate(
            [w_group[i], w_er[i].transpose(1, 0, 2).reshape(d, N_EXPERTS),
             jnp.zeros((d, pad), F32)], axis=1).astype(BF16)
        b_r = jnp.concatenate([b_group[i], b_er[i].reshape(-1), jnp.zeros((pad,), F32)])
        ids, wts, counts = _router(x1s, w_r, row(b_r))
        (slots, slab_of_slot, tile_expert, n_used, next_expert,
         stage_slot) = _routing_tables(ids, counts[0])
        act = _grouped_gate_up(tile_expert, n_used, slab_of_slot, next_expert, stage_slot,
                               x1s, w_gate[i], w_up[i])
        y_sorted = _grouped_down(tile_expert, n_used, next_expert, stage_slot, act, w_down[i])
        moe = _combine_experts(slots, y_sorted, wts)
        x2 = _final_stage(x1s, w_ple_gate[i].astype(BF16), row(b_ple_gate[i]),
                          p[i].reshape(m, PLE_DIM), w_ple_proj[i], y1, mu1, rs1,
                          row(ln1_g[i]), row(ln1_b[i]), moe, row(ln2_g[i]), row(ln2_b[i]))
    return x2.reshape(batch, seq, d)
```

```python
import math

import jax
import jax.numpy as jnp
from jax import lax
from jax.experimental import pallas as pl
from jax.experimental.pallas import tpu as pltpu

F32 = jnp.float32
BF16 = jnp.bfloat16
I32 = jnp.int32
U32 = jnp.uint32

D_MODEL = 4096
PLE_DIM = 256
BLOCK = 128
ROPE_THETA = 10000.0
LN_EPS = 1e-5
RMS_EPS = 1e-6
NEG_INF = -1e30

SWA_HEAD_DIM = 64
SWA_WIDTH = D_MODEL // 2
SWA_HEADS = SWA_WIDTH // SWA_HEAD_DIM
SWA_KV_HEADS = SWA_HEADS // 8
SWA_GROUP = SWA_HEADS // SWA_KV_HEADS
SWA_KV_WIDTH = SWA_KV_HEADS * SWA_HEAD_DIM

SB_HEAD_DIM = 128
SB_WIDTH = D_MODEL - SWA_WIDTH
SB_HEADS = SB_WIDTH // SB_HEAD_DIM

IN_WIDTH = SWA_WIDTH + 2 * SWA_KV_WIDTH + 3 * SB_WIDTH
A_WIDTH = SWA_WIDTH + 2 * SWA_KV_WIDTH
Q_B_COL = 0
K_B_COL = Q_B_COL + SB_WIDTH
V_B_COL = K_B_COL + SB_WIDTH
Q_A_COL = V_B_COL + SB_WIDTH
K_A_COL = Q_A_COL + SWA_WIDTH
V_A_COL = K_A_COL + SWA_KV_WIDTH

N_GROUPS = 4
EXPERTS_PER_GROUP = 8
N_EXPERTS = N_GROUPS * EXPERTS_PER_GROUP
EXPERT_TOP_K = 2
DEPTH = 1
DEEPNORM_ALPHA = (2.0 * DEPTH) ** 0.25

LANES = 128
SUBLANES = 8
VMEM_LIMIT_BYTES = 56 * 1024 * 1024

EXP_ZERO_LOG = -126 * math.log(2.0) - 2.0

TM_IN = 1024
TM_PROJ = 512
TN_PROJ = 512
TM_MOE = 256
TM_COMBINE = 512
SB_HEADS_PER_STEP = 16
SUFFIX_TERMS = 2
ROW_CHUNK = 16
WEIGHT_DMA_PRIORITY = 1


def _cparams(sem):
    return pltpu.CompilerParams(dimension_semantics=sem,
                                vmem_limit_bytes=VMEM_LIMIT_BYTES)


def _pack_halves(lo, hi):
    return lax.bitcast_convert_type(
        pltpu.pack_elementwise([lo, hi], packed_dtype=BF16), U32)


def _unpack_halves(words):
    return tuple(
        pltpu.unpack_elementwise(words, index=k, packed_dtype=BF16,
                                 unpacked_dtype=F32).astype(BF16) for k in (0, 1))


def _dot_halves(lo, hi, w_ref):
    half = lo.shape[1]
    return (jnp.dot(lo, w_ref[:half], preferred_element_type=F32)
            + jnp.dot(hi, w_ref[half:], preferred_element_type=F32))


WORD_ROWS = (D_MODEL // 2) // LANES
SLAB_PAIR = 2 * LANES


def _slab_words(slab_ref, pair, tokens):
    return jnp.concatenate(
        [slab_ref[pl.ds(2 * pair + k, tokens, stride=WORD_ROWS), :] for k in (0, 1)],
        axis=1)


def _dot_slab(slab_ref, tokens, w_refs):
    half = D_MODEL // 2
    outs = [None] * len(w_refs)
    for pair in range(half // SLAB_PAIR):
        lo, hi = _unpack_halves(_slab_words(slab_ref, pair, tokens))
        rows_lo = slice(pair * SLAB_PAIR, (pair + 1) * SLAB_PAIR)
        rows_hi = slice(half + pair * SLAB_PAIR, half + (pair + 1) * SLAB_PAIR)
        for n, w_ref in enumerate(w_refs):
            part = (jnp.dot(lo, w_ref[rows_lo], preferred_element_type=F32)
                    + jnp.dot(hi, w_ref[rows_hi], preferred_element_type=F32))
            outs[n] = part if outs[n] is None else outs[n] + part
    return outs


def _dot_nt(a, b):
    return lax.dot_general(a, b, (((1,), (1,)), ((), ())),
                           preferred_element_type=F32)


def _inproj_kernel(x_ref, w_ref, b_ref, o_ref, xb_ref):
    @pl.when(pl.program_id(1) == 0)
    def _():
        xb_ref[...] = x_ref[...].astype(BF16)

    acc = jnp.dot(xb_ref[...], w_ref[...], preferred_element_type=F32)
    o_ref[...] = (acc + b_ref[...]).astype(o_ref.dtype)


def _in_projection(x2, w_b, b):
    m, k = x2.shape
    n = w_b.shape[1]
    n_blocks = n // TN_PROJ
    assert A_WIDTH % TN_PROJ == 0
    a_blocks = A_WIDTH // TN_PROJ

    def src(j):
        return lax.rem(j + a_blocks, n_blocks)

    return pl.pallas_call(
        _inproj_kernel,
        out_shape=jax.ShapeDtypeStruct((m, n), BF16),
        grid=(m // TM_IN, n_blocks),
        in_specs=[
            pl.BlockSpec((TM_IN, k), lambda i, j: (i, 0)),
            pl.BlockSpec((k, TN_PROJ), lambda i, j: (0, src(j))),
            pl.BlockSpec((1, TN_PROJ), lambda i, j: (0, src(j))),
        ],
        out_specs=pl.BlockSpec((TM_IN, TN_PROJ), lambda i, j: (i, j)),
        scratch_shapes=[pltpu.VMEM((TM_IN, k), BF16)],
        compiler_params=_cparams(("arbitrary", "arbitrary")),
        name="in_projection",
    )(x2, w_b, b)


def _swa_kernel(sinks_ref, q_ref, kc_ref, vc_ref, vp_ref,
                posc_ref, invf_ref, o_ref, kprev_ref):
    n = pl.program_id(1)

    @pl.when(n == 0)
    def _():
        kprev_ref[...] = jnp.zeros_like(kprev_ref)
    lane = lax.broadcasted_iota(I32, (1, LANES), 1)
    first_half = (lane % SWA_HEAD_DIM) < (SWA_HEAD_DIM // 2)

    def tables(pos_ref):
        ang = pos_ref[...].astype(F32) * invf_ref[...]
        sin = jnp.sin(ang)
        return jnp.cos(ang), jnp.where(first_half, -sin, sin)

    def rope(x, cos, sin_signed):
        partner = jnp.where(first_half,
                            pltpu.roll(x, LANES - SWA_HEAD_DIM // 2, 1),
                            pltpu.roll(x, SWA_HEAD_DIM // 2, 1))
        return x * cos + partner * sin_signed

    cos_c, sin_c = tables(posc_ref)

    def rope_block(ref, cos, sin_signed):
        width = ref.shape[1]
        return [rope(ref[:, c * LANES:(c + 1) * LANES].astype(F32), cos,
                     sin_signed).astype(BF16) for c in range(width // LANES)]

    q_chunks = rope_block(q_ref, cos_c, sin_c)
    k_cur = rope_block(kc_ref, cos_c, sin_c)
    k_chunks = [jnp.concatenate([kprev_ref[:, c * LANES:(c + 1) * LANES], k_c], axis=0)
                for c, k_c in enumerate(k_cur)]
    for c, k_c in enumerate(k_cur):
        kprev_ref[:, c * LANES:(c + 1) * LANES] = k_c
    v_all = jnp.concatenate([vp_ref[...], vc_ref[...]], axis=0)

    qi = lax.broadcasted_iota(I32, (BLOCK, 2 * BLOCK), 0)
    kj = lax.broadcasted_iota(I32, (BLOCK, 2 * BLOCK), 1)
    rel = qi - (kj - BLOCK)
    valid = (rel >= 0) & (rel < BLOCK) & ((kj >= BLOCK) | (n > 0))

    def head_slice(chunks, head):
        half = head % 2
        return chunks[head // 2][:, half * SWA_HEAD_DIM:(half + 1) * SWA_HEAD_DIM]

    scale = 1.0 / math.sqrt(SWA_HEAD_DIM)
    scores = []
    for h in range(SWA_KV_HEADS):
        k_h = head_slice(k_chunks, h)
        q_h = jnp.concatenate(
            [head_slice(q_chunks, h * SWA_GROUP + g) for g in range(SWA_GROUP)],
            axis=0)
        scores.append(_dot_nt(q_h, k_h) * scale)
    weights, sink_terms = [], []
    for h in range(SWA_KV_HEADS):
        weights_h = []
        for g in range(SWA_GROUP):
            sink = sinks_ref[h * SWA_GROUP + g]
            s_g = jnp.where(valid, scores[h][g * BLOCK:(g + 1) * BLOCK], NEG_INF)
            m = jnp.maximum(jnp.max(s_g, axis=-1, keepdims=True), sink)
            weights_h.append(jnp.exp(s_g - m).astype(BF16))
            sink_terms.append(jnp.exp(sink - m))
        weights.append(jnp.concatenate(weights_h, axis=0))
    ones = jnp.ones((2 * BLOCK, SWA_HEAD_DIM), BF16)
    for h in range(SWA_KV_HEADS):
        v_h = v_all[:, h * SWA_HEAD_DIM:(h + 1) * SWA_HEAD_DIM]
        both = jnp.dot(weights[h], jnp.concatenate([v_h, ones], axis=1),
                       preferred_element_type=F32)
        for g in range(SWA_GROUP):
            rows = slice(g * BLOCK, (g + 1) * BLOCK)
            den = both[rows, SWA_HEAD_DIM:SWA_HEAD_DIM + 1] + sink_terms[h * SWA_GROUP + g]
            col = (h * SWA_GROUP + g) * SWA_HEAD_DIM
            o_ref[:, col:col + SWA_HEAD_DIM] = both[rows, :SWA_HEAD_DIM] / den


def _swa_attention(proj, pos2, inv_freq, sinks, batch, seq):
    nb = seq // BLOCK
    assert Q_A_COL % SWA_WIDTH == 0 and K_A_COL % SWA_KV_WIDTH == 0
    assert V_A_COL % SWA_KV_WIDTH == 0
    qcol = Q_A_COL // SWA_WIDTH
    kcol = K_A_COL // SWA_KV_WIDTH
    vcol = V_A_COL // SWA_KV_WIDTH

    def cur(b, n):
        return b * nb + n

    def prev(b, n):
        return b * nb + jnp.maximum(n - 1, 0)

    return pl.pallas_call(
        _swa_kernel,
        out_shape=jax.ShapeDtypeStruct((batch * seq, SWA_WIDTH), F32),
        grid=(batch, nb),
        in_specs=[
            pl.BlockSpec(memory_space=pltpu.SMEM),
            pl.BlockSpec((BLOCK, SWA_WIDTH), lambda b, n: (cur(b, n), qcol)),
            pl.BlockSpec((BLOCK, SWA_KV_WIDTH), lambda b, n: (cur(b, n), kcol)),
            pl.BlockSpec((BLOCK, SWA_KV_WIDTH), lambda b, n: (cur(b, n), vcol)),
            pl.BlockSpec((BLOCK, SWA_KV_WIDTH), lambda b, n: (prev(b, n), vcol)),
            pl.BlockSpec((BLOCK, 1), lambda b, n: (cur(b, n), 0)),
            pl.BlockSpec((1, LANES), lambda b, n: (0, 0)),
        ],
        out_specs=pl.BlockSpec((BLOCK, SWA_WIDTH), lambda b, n: (cur(b, n), 0)),
        scratch_shapes=[pltpu.VMEM((BLOCK, SWA_KV_WIDTH), BF16)],
        compiler_params=_cparams(("arbitrary", "arbitrary")),
        name="swa_attention",
    )(sinks, proj, proj, proj, proj, pos2, inv_freq)


def _sb_kernel(q_ref, k_ref, v_ref, o_ref):
    n = pl.program_id(2)
    heads = [slice(h * SB_HEAD_DIM, (h + 1) * SB_HEAD_DIM)
             for h in range(SB_HEADS_PER_STEP)]
    rows = SB_HEADS_PER_STEP * BLOCK
    scale = 1.0 / math.sqrt(SB_HEAD_DIM)
    key_j = lax.broadcasted_iota(I32, (BLOCK, 2 * BLOCK), 0)
    out_c = lax.broadcasted_iota(I32, (BLOCK, 2 * BLOCK), 1)
    later_total = ((key_j > out_c) | (out_c >= BLOCK)).astype(BF16)
    q_row = lax.broadcasted_iota(I32, (rows, BLOCK), 0) % BLOCK
    k_col = lax.broadcasted_iota(I32, (rows, BLOCK), 1)
    causal = k_col < q_row

    def block(kb, carry, acc, mask):
        start = pl.multiple_of(kb * BLOCK, BLOCK)
        z = jnp.concatenate(
            [_dot_nt(q_ref[:, hd], k_ref[pl.ds(start, BLOCK), hd]) for hd in heads],
            axis=0) * scale
        t = jnp.log(1.0 + jnp.exp(-jnp.abs(z)))
        log_not = jnp.minimum(-z, 0.0) - t
        log_beta = log_not + z
        if mask is not None:
            log_not = jnp.where(mask, log_not, 0.0)
        terms, rest = [], log_not
        for _ in range(SUFFIX_TERMS):
            terms.append(rest.astype(BF16))
            rest = rest - terms[-1].astype(F32)
        parts = jnp.dot(jnp.concatenate(terms, axis=0), later_total,
                        preferred_element_type=F32)
        sums = parts[:rows]
        for t in range(1, SUFFIX_TERMS):
            sums = sums + parts[t * rows:(t + 1) * rows]
        a = jnp.exp(log_beta + sums[:, :BLOCK] + carry)
        if mask is not None:
            a = jnp.where(mask, a, 0.0)
        a = a.astype(BF16)
        pv = jnp.concatenate(
            [jnp.dot(a[h * BLOCK:(h + 1) * BLOCK], v_ref[pl.ds(start, BLOCK), hd],
                     preferred_element_type=F32) for h, hd in enumerate(heads)], axis=0)
        return carry + sums[:, BLOCK:], acc + pv

    zeros = jnp.zeros((rows, BLOCK), F32)
    carry, acc = block(n, zeros, zeros, causal)

    def cond(state):
        kb, carry, _ = state
        return jnp.logical_and(kb >= 0, jnp.max(carry) > EXP_ZERO_LOG)

    def body(state):
        kb, carry, acc = state
        carry, acc = block(kb, carry, acc, None)
        return kb - 1, carry, acc

    _, _, acc = lax.while_loop(cond, body, (n - 1, carry, acc))
    for h, hd in enumerate(heads):
        o_ref[:, hd] = acc[h * BLOCK:(h + 1) * BLOCK]


def _sb_attention(proj, batch, seq):
    nb = seq // BLOCK
    width = SB_HEADS_PER_STEP * SB_HEAD_DIM
    assert Q_B_COL % width == 0 and K_B_COL % width == 0 and V_B_COL % width == 0
    qcol = Q_B_COL // width
    kcol = K_B_COL // width
    vcol = V_B_COL // width
    return pl.pallas_call(
        _sb_kernel,
        out_shape=jax.ShapeDtypeStruct((batch * seq, SB_WIDTH), F32),
        grid=(batch, SB_HEADS // SB_HEADS_PER_STEP, nb),
        in_specs=[
            pl.BlockSpec((BLOCK, width), lambda b, h, n: (b * nb + n, qcol + h)),
            pl.BlockSpec((seq, width), lambda b, h, n: (b, kcol + h),
                         pipeline_mode=pl.Buffered(1)),
            pl.BlockSpec((seq, width), lambda b, h, n: (b, vcol + h),
                         pipeline_mode=pl.Buffered(1)),
        ],
        out_specs=pl.BlockSpec((BLOCK, width), lambda b, h, n: (b * nb + n, h)),
        compiler_params=_cparams(("arbitrary", "arbitrary", "arbitrary")),
        name="sb_attention",
    )(proj, proj, proj)


def _row_stats(chunks_ref, n_chunks, width):
    total = chunks_ref[0].sum(axis=-1, keepdims=True)
    for c in range(1, n_chunks):
        total = total + chunks_ref[c].sum(axis=-1, keepdims=True)
    mu = total / width
    sq = jnp.square(chunks_ref[0] - mu).sum(axis=-1, keepdims=True)
    for c in range(1, n_chunks):
        sq = sq + jnp.square(chunks_ref[c] - mu).sum(axis=-1, keepdims=True)
    return mu, lax.rsqrt(sq / width + LN_EPS)


def _outproj_kernel(oa_ref, ob_ref, ga_ref, gb_ref, wa_ref, wb_ref, x_ref, bo_ref,
                    lg_ref, lb_ref, y_ref, mu_ref, rs_ref, x1s_ref,
                    ma_ref, mb_ref, acc_ref):
    j = pl.program_id(1)
    n_chunks = acc_ref.shape[0]
    tn = acc_ref.shape[2]

    @pl.when(j == 0)
    def _():
        for o_ref, g_ref, m_ref in ((oa_ref, ga_ref, ma_ref), (ob_ref, gb_ref, mb_ref)):
            width = o_ref.shape[1]
            cols = [slice(c * tn, (c + 1) * tn) for c in range(width // tn)]
            sq = sum(jnp.square(o_ref[:, sl]).sum(axis=-1, keepdims=True) for sl in cols)
            r = lax.rsqrt(sq / width + RMS_EPS)
            for sl in cols:
                m_ref[:, sl] = (o_ref[:, sl] * r * g_ref[:, sl]).astype(BF16)

    mix = (jnp.dot(ma_ref[...], wa_ref[...], preferred_element_type=F32)
           + jnp.dot(mb_ref[...], wb_ref[...], preferred_element_type=F32))
    y = DEEPNORM_ALPHA * x_ref[...] + (mix + bo_ref[...])
    y_ref[...] = y
    acc_ref[j] = y

    @pl.when(j == n_chunks - 1)
    def _():
        mu, rs = _row_stats(acc_ref, n_chunks, n_chunks * tn)
        mu_ref[...] = mu
        rs_ref[...] = rs

        def normed(c):
            sl = slice(c * tn, (c + 1) * tn)
            return (acc_ref[c] - mu) * rs * lg_ref[:, sl] + lb_ref[:, sl]

        tm = acc_ref.shape[1]
        for c in range(n_chunks // 2):
            words = _pack_halves(normed(c), normed(c + n_chunks // 2))
            for q in range(tn // LANES):
                x1s_ref[pl.ds(c * (tn // LANES) + q, tm, stride=WORD_ROWS), :] = (
                    words[:, q * LANES:(q + 1) * LANES])


def _out_projection(o_a, o_b, g_a, g_b, w_b, x2, b_out, ln_g, ln_b):
    m, d = x2.shape
    tm, tn = TM_PROJ, TN_PROJ
    half = o_a.shape[1]
    once = dict(pipeline_mode=pl.Buffered(1))
    return pl.pallas_call(
        _outproj_kernel,
        out_shape=(jax.ShapeDtypeStruct((m, d), F32),
                   jax.ShapeDtypeStruct((m, 1), F32),
                   jax.ShapeDtypeStruct((m, 1), F32),
                   jax.ShapeDtypeStruct((m * WORD_ROWS, LANES), U32)),
        grid=(m // tm, d // tn),
        in_specs=[
            pl.BlockSpec((tm, half), lambda i, j: (i, 0), **once),
            pl.BlockSpec((tm, half), lambda i, j: (i, 0), **once),
            pl.BlockSpec((1, half), lambda i, j: (0, 0)),
            pl.BlockSpec((1, half), lambda i, j: (0, 0)),
            pl.BlockSpec((half, tn), lambda i, j: (0, j)),
            pl.BlockSpec((half, tn), lambda i, j: (1, j)),
            pl.BlockSpec((tm, tn), lambda i, j: (i, j)),
            pl.BlockSpec((1, tn), lambda i, j: (0, j)),
            pl.BlockSpec((1, d), lambda i, j: (0, 0)),
            pl.BlockSpec((1, d), lambda i, j: (0, 0)),
        ],
        out_specs=(pl.BlockSpec((tm, tn), lambda i, j: (i, j)),
                   pl.BlockSpec((tm, 1), lambda i, j: (i, 0)),
                   pl.BlockSpec((tm, 1), lambda i, j: (i, 0)),
                   pl.BlockSpec((tm * WORD_ROWS, LANES), lambda i, j: (i, 0))),
        scratch_shapes=[pltpu.VMEM((tm, half), BF16), pltpu.VMEM((tm, half), BF16),
                        pltpu.VMEM((d // tn, tm, tn), F32)],
        compiler_params=_cparams(("arbitrary", "arbitrary")),
        name="out_projection_ln1",
    )(o_a, o_b, g_a, g_b, w_b, w_b, x2, b_out, ln_g, ln_b)


ROUTER_COLS = LANES
EXPERT_LANE0 = N_GROUPS


def _router_kernel(x_ref, w_ref, b_ref, ids_ref, wts_ref, cnt_ref, carry_ref):
    i = pl.program_id(0)
    tm = wts_ref.shape[0]

    @pl.when(i == 0)
    def _():
        carry_ref[...] = jnp.zeros_like(carry_ref)

    logits = _dot_slab(x_ref, tm, [w_ref])[0] + b_ref[...]
    lane = lax.broadcasted_iota(I32, (tm, ROUTER_COLS), 1)
    big = jnp.int32(ROUTER_COLS)

    def first_argmax(vals):
        top = jnp.max(vals, axis=-1, keepdims=True)
        idx = jnp.min(jnp.where(vals == top, lane, big), axis=-1, keepdims=True)
        return top, idx

    is_group = lane < N_GROUPS
    g_logits = jnp.where(is_group, logits, -jnp.inf)
    g_top, g_idx = first_argmax(g_logits)
    g_w = 1.0 / jnp.sum(jnp.exp(g_logits - g_top), axis=-1, keepdims=True)

    first = EXPERT_LANE0 + g_idx * EXPERTS_PER_GROUP
    in_group = (lane >= first) & (lane < first + EXPERTS_PER_GROUP)
    e_logits = jnp.where(in_group, logits, -jnp.inf)
    top1, idx1 = first_argmax(e_logits)
    top2, idx2 = first_argmax(jnp.where(lane == idx1, -jnp.inf, e_logits))
    e2 = jnp.exp(top2 - top1)
    w1 = g_w / (1.0 + e2)
    w2 = g_w * e2 / (1.0 + e2)

    hit1 = lane == idx1
    hit2 = lane == idx2
    onehot = (hit1 | hit2).astype(BF16)
    r = lax.broadcasted_iota(I32, (tm, tm), 0)
    c = lax.broadcasted_iota(I32, (tm, tm), 1)
    before = (c < r).astype(BF16)
    prior = jnp.dot(before, onehot, preferred_element_type=F32) + carry_ref[0:1, :]
    rank1 = jnp.sum(jnp.where(hit1, prior, 0.0), axis=-1, keepdims=True)
    rank2 = jnp.sum(jnp.where(hit2, prior, 0.0), axis=-1, keepdims=True)
    counts = carry_ref[0:1, :] + jnp.sum(onehot.astype(F32), axis=0, keepdims=True)
    carry_ref[...] = jnp.broadcast_to(counts, carry_ref.shape)
    cnt_ref[...] = jnp.broadcast_to(counts, cnt_ref.shape).astype(I32)

    ids = jnp.where(lane == 0, idx1 - EXPERT_LANE0,
          jnp.where(lane == 1, idx2 - EXPERT_LANE0,
          jnp.where(lane == 2, rank1.astype(I32),
          jnp.where(lane == 3, rank2.astype(I32), 0))))
    ids_ref[...] = ids.T[:ids_ref.shape[0]]
    wts_ref[...] = jnp.where(lane == 0, w1, jnp.where(lane == 1, w2, 0.0))


def _router(x1s, w_r, b_r):
    m = x1s.shape[0] // WORD_ROWS
    tm = TM_PROJ
    return pl.pallas_call(
        _router_kernel,
        out_shape=(jax.ShapeDtypeStruct((SUBLANES, m), I32),
                   jax.ShapeDtypeStruct((m, ROUTER_COLS), F32),
                   jax.ShapeDtypeStruct((SUBLANES, ROUTER_COLS), I32)),
        grid=(m // tm,),
        in_specs=[
            pl.BlockSpec((tm * WORD_ROWS, LANES), lambda i: (i, 0)),
            pl.BlockSpec((D_MODEL, ROUTER_COLS), lambda i: (0, 0)),
            pl.BlockSpec((1, ROUTER_COLS), lambda i: (0, 0)),
        ],
        out_specs=(pl.BlockSpec((SUBLANES, tm), lambda i: (0, i)),
                   pl.BlockSpec((tm, ROUTER_COLS), lambda i: (i, 0)),
                   pl.BlockSpec((SUBLANES, ROUTER_COLS), lambda i: (0, 0))),
        scratch_shapes=[pltpu.VMEM((SUBLANES, ROUTER_COLS), F32)],
        compiler_params=_cparams(("arbitrary",)),
        name="router",
    )(x1s, w_r, b_r)


def _start_row_gather(idx_ref, first, n_items, span, src_ref, dst_ref, sem):
    def body(c, _):
        for u in range(ROW_CHUNK):
            r = c * ROW_CHUNK + u
            src_row = pl.multiple_of(idx_ref[first + r], span)
            dst_row = pl.multiple_of(r * span, span)
            pltpu.make_async_copy(src_ref.at[pl.ds(src_row, span)],
                                  dst_ref.at[pl.ds(dst_row, span)], sem).start()
        return 0

    lax.fori_loop(0, n_items // ROW_CHUNK, body, 0)


def _wait_row_gather(n_items, span, src_ref, dst_ref, sem):
    def body(c, _):
        for _u in range(ROW_CHUNK):
            pltpu.make_async_copy(src_ref.at[pl.ds(0, span)], dst_ref.at[pl.ds(0, span)],
                                  sem).wait()
        return 0

    lax.fori_loop(0, n_items // ROW_CHUNK, body, 0)


def _expert_changed(te_ref, i):
    return jnp.logical_or(i == 0, te_ref[i] != te_ref[jnp.maximum(i - 1, 0)])


def _tile_row(i, nu):
    return jnp.minimum(i, nu[0] - 1)


def _stage_expert_weights(i, te_ref, nxt_ref, ws_ref, w_hbms, stage_ref, bf_refs, sem):
    expert = te_ref[i]
    slot = ws_ref[expert]

    def copies(e, s):
        return [pltpu.make_async_copy(w.at[e], stage_ref.at[s, n], sem.at[s])
                for n, w in enumerate(w_hbms)]

    @pl.when(i == 0)
    def _():
        for c in copies(expert, slot):
            c.start(priority=WEIGHT_DMA_PRIORITY)

    @pl.when(_expert_changed(te_ref, i))
    def _():
        for c in copies(expert, slot):
            c.wait()
        nxt = nxt_ref[expert]

        @pl.when(nxt < N_EXPERTS)
        def _():
            for c in copies(nxt, 1 - slot):
                c.start(priority=WEIGHT_DMA_PRIORITY)

        for n, bf_ref in enumerate(bf_refs):
            bf_ref[...] = stage_ref[slot, n].astype(BF16)


def _gateup_kernel(te_ref, nu_ref, tos_ref, nxt_ref, ws_ref, x_hbm, wg_hbm, wu_hbm, a_ref,
                   xbuf_ref, stage_ref, wgb_ref, wub_ref, sems, wsem):
    i = pl.program_id(0)
    n_used = nu_ref[0]
    slot = lax.rem(i, 2)

    def start(tile, buf):
        _start_row_gather(tos_ref, tile * TM_MOE, TM_MOE, WORD_ROWS, x_hbm,
                          xbuf_ref.at[buf], sems.at[buf])

    @pl.when(i == 0)
    def _():
        start(0, 0)

    @pl.when(i + 1 < n_used)
    def _():
        start(i + 1, 1 - slot)

    _stage_expert_weights(i, te_ref, nxt_ref, ws_ref, [wg_hbm, wu_hbm], stage_ref,
                          [wgb_ref, wub_ref], wsem)

    @pl.when(i < n_used)
    def _():
        _wait_row_gather(TM_MOE, WORD_ROWS, x_hbm, xbuf_ref.at[slot], sems.at[slot])
        gate, up = _dot_slab(xbuf_ref.at[slot], TM_MOE, [wgb_ref, wub_ref])
        a_ref[...] = (gate * jax.nn.sigmoid(gate) * up).astype(a_ref.dtype)

    @pl.when(i >= n_used)
    def _():
        a_ref[...] = jnp.zeros_like(a_ref)


def _grouped_gate_up(tile_expert, n_used, slab_of_slot, next_expert, stage_slot,
                     x1s, w_gate, w_up):
    p_rows = slab_of_slot.shape[0]
    _, d, f = w_gate.shape
    hbm = pl.BlockSpec(memory_space=pl.ANY)
    return pl.pallas_call(
        _gateup_kernel,
        out_shape=jax.ShapeDtypeStruct((p_rows, f), BF16),
        grid_spec=pltpu.PrefetchScalarGridSpec(
            num_scalar_prefetch=5,
            grid=(p_rows // TM_MOE,),
            in_specs=[hbm, hbm, hbm],
            out_specs=pl.BlockSpec((TM_MOE, f), lambda i, *_: (i, 0)),
            scratch_shapes=[pltpu.VMEM((2, TM_MOE * WORD_ROWS, LANES), U32),
                            pltpu.VMEM((2, 2, d, f), F32),
                            pltpu.VMEM((d, f), BF16), pltpu.VMEM((d, f), BF16),
                            pltpu.SemaphoreType.DMA((2,)), pltpu.SemaphoreType.DMA((2,))],
        ),
        compiler_params=_cparams(("arbitrary",)),
        name="moe_gate_up",
    )(tile_expert, n_used, slab_of_slot, next_expert, stage_slot, x1s, w_gate, w_up)


def _down_kernel(te_ref, nu_ref, nxt_ref, ws_ref, a_ref, wd_hbm, y_ref,
                 stage_ref, wdb_ref, wsem):
    i = pl.program_id(0)
    _stage_expert_weights(i, te_ref, nxt_ref, ws_ref, [wd_hbm], stage_ref, [wdb_ref], wsem)

    @pl.when(i < nu_ref[0])
    def _():
        y_ref[:, 0, :] = jnp.dot(a_ref[...], wdb_ref[...], preferred_element_type=F32)

    @pl.when(i >= nu_ref[0])
    def _():
        y_ref[...] = jnp.zeros_like(y_ref)


def _grouped_down(tile_expert, n_used, next_expert, stage_slot, act, w_down):
    p_rows, f = act.shape
    d = w_down.shape[2]
    return pl.pallas_call(
        _down_kernel,
        out_shape=jax.ShapeDtypeStruct((p_rows, 1, d), F32),
        grid_spec=pltpu.PrefetchScalarGridSpec(
            num_scalar_prefetch=4,
            grid=(p_rows // TM_MOE,),
            in_specs=[
                pl.BlockSpec((TM_MOE, f), lambda i, te, nu, *_: (_tile_row(i, nu), 0)),
                pl.BlockSpec(memory_space=pl.ANY),
            ],
            out_specs=pl.BlockSpec((TM_MOE, 1, d), lambda i, *_: (i, 0, 0)),
            scratch_shapes=[pltpu.VMEM((2, 1, f, d), F32), pltpu.VMEM((f, d), BF16),
                            pltpu.SemaphoreType.DMA((2,))],
        ),
        compiler_params=_cparams(("arbitrary",)),
        name="moe_down",
    )(tile_expert, n_used, next_expert, stage_slot, act, w_down)


def _combine_kernel(slot_ref, y_hbm, wts_ref, o_ref, buf_ref, sems):
    i = pl.program_id(0)
    tm = o_ref.shape[0]
    n_tokens = slot_ref.shape[0] // EXPERT_TOP_K
    slot = lax.rem(i, 2)

    def start(tile, buf):
        for k in range(EXPERT_TOP_K):
            _start_row_gather(slot_ref, k * n_tokens + tile * tm, tm, 1, y_hbm,
                              buf_ref.at[buf, k], sems.at[buf])

    @pl.when(i == 0)
    def _():
        start(0, 0)

    @pl.when(i + 1 < pl.num_programs(0))
    def _():
        start(i + 1, 1 - slot)

    _wait_row_gather(EXPERT_TOP_K * tm, 1, y_hbm, buf_ref.at[slot, 0], sems.at[slot])
    for c in range(o_ref.shape[1] // TN_PROJ):
        sl = slice(c * TN_PROJ, (c + 1) * TN_PROJ)
        o_ref[:, sl] = sum(wts_ref[:, k:k + 1] * buf_ref[slot, k, :, 0, sl]
                           for k in range(EXPERT_TOP_K))


def _combine_experts(slot_kt, y_sorted, wts):
    d = y_sorted.shape[2]
    m = slot_kt.shape[0] // EXPERT_TOP_K
    tm = TM_COMBINE
    return pl.pallas_call(
        _combine_kernel,
        out_shape=jax.ShapeDtypeStruct((m, d), F32),
        grid_spec=pltpu.PrefetchScalarGridSpec(
            num_scalar_prefetch=1,
            grid=(m // tm,),
            in_specs=[
                pl.BlockSpec(memory_space=pl.ANY),
                pl.BlockSpec((tm, ROUTER_COLS), lambda i, sl: (i, 0)),
            ],
            out_specs=pl.BlockSpec((tm, d), lambda i, sl: (i, 0)),
            scratch_shapes=[pltpu.VMEM((2, EXPERT_TOP_K, tm, 1, d), F32),
                            pltpu.SemaphoreType.DMA((2,))],
        ),
        compiler_params=_cparams(("arbitrary",)),
        name="moe_combine",
    )(slot_kt, y_sorted, wts)


def _final_kernel(x1s_ref, wg_ref, bg_ref, p_ref, wp_ref, y1_ref, mu_ref, rs_ref,
                  l1g_ref, l1b_ref, moe_ref, l2g_ref, l2b_ref,
                  o_ref, lo_ref, hi_ref, acc_ref, mu2_ref, rs2_ref):
    i = pl.program_id(0)
    j = pl.program_id(1)
    n_tiles = pl.num_programs(0) - 1
    n_chunks = acc_ref.shape[1]
    tn = acc_ref.shape[3]
    cur = lax.rem(i, 2)
    prv = 1 - cur

    @pl.when(jnp.logical_and(j == 0, i < n_tiles))
    def _():
        for s in range(WORD_ROWS):
            sl = slice(s * LANES, (s + 1) * LANES)
            lo_ref[:, sl], hi_ref[:, sl] = _unpack_halves(
                x1s_ref[pl.ds(s, lo_ref.shape[0], stride=WORD_ROWS), :])

    def build():
        gate = _dot_halves(lo_ref[...], hi_ref[...], wg_ref) + bg_ref[...]
        emb = jnp.dot(p_ref[...].astype(BF16), wp_ref[...].astype(BF16),
                      preferred_element_type=F32)
        x1 = (y1_ref[...] - mu_ref[...]) * rs_ref[...] * l1g_ref[...] + l1b_ref[...]
        acc_ref[cur, j] = DEEPNORM_ALPHA * x1 + moe_ref[...] + jax.nn.sigmoid(gate) * emb

    def emit():
        o_ref[...] = ((acc_ref[prv, j] - mu2_ref[prv]) * rs2_ref[prv]
                      * l2g_ref[...] + l2b_ref[...])

    @pl.when(i == 0)
    def _():
        build()

    @pl.when(jnp.logical_and(i > 0, i < n_tiles))
    def _():
        emit()
        build()

    @pl.when(i == n_tiles)
    def _():
        emit()

    @pl.when(jnp.logical_and(j == n_chunks - 1, i < n_tiles))
    def _():
        mu2_ref[cur], rs2_ref[cur] = _row_stats(acc_ref.at[cur], n_chunks, n_chunks * tn)


def _final_stage(x1s, w_pg_b, b_pg, p2, w_pp, y1, mu1, rs1, ln1_g, ln1_b,
                 moe, ln2_g, ln2_b):
    m, d = y1.shape
    tm, tn = TM_PROJ, TN_PROJ
    ple = p2.shape[1]
    n_chunks = d // tn

    n_tiles = m // tm

    def built(i):
        return jnp.minimum(i, n_tiles - 1)

    def emitted(i):
        return jnp.maximum(i - 1, 0)

    return pl.pallas_call(
        _final_kernel,
        out_shape=jax.ShapeDtypeStruct((m, d), F32),
        grid=(n_tiles + 1, n_chunks),
        in_specs=[
            pl.BlockSpec((tm * WORD_ROWS, LANES), lambda i, j: (built(i), 0),
                         pipeline_mode=pl.Buffered(1)),
            pl.BlockSpec((d, tn), lambda i, j: (0, j)),
            pl.BlockSpec((1, tn), lambda i, j: (0, j)),
            pl.BlockSpec((tm, ple), lambda i, j: (built(i), 0)),
            pl.BlockSpec((ple, tn), lambda i, j: (0, j)),
            pl.BlockSpec((tm, tn), lambda i, j: (built(i), j)),
            pl.BlockSpec((tm, 1), lambda i, j: (built(i), 0)),
            pl.BlockSpec((tm, 1), lambda i, j: (built(i), 0)),
            pl.BlockSpec((1, tn), lambda i, j: (0, j)),
            pl.BlockSpec((1, tn), lambda i, j: (0, j)),
            pl.BlockSpec((tm, tn), lambda i, j: (built(i), j)),
            pl.BlockSpec((1, tn), lambda i, j: (0, j)),
            pl.BlockSpec((1, tn), lambda i, j: (0, j)),
        ],
        out_specs=pl.BlockSpec((tm, tn), lambda i, j: (emitted(i), jnp.where(i == 0, 0, j))),
        scratch_shapes=[pltpu.VMEM((tm, d // 2), BF16), pltpu.VMEM((tm, d // 2), BF16),
                        pltpu.VMEM((2, n_chunks, tm, tn), F32),
                        pltpu.VMEM((2, tm, 1), F32), pltpu.VMEM((2, tm, 1), F32)],
        compiler_params=_cparams(("arbitrary", "arbitrary")),
        name="ple_moe_ln2",
    )(x1s, w_pg_b, b_pg, p2, w_pp, y1, mu1, rs1, ln1_g, ln1_b, moe, ln2_g, ln2_b)


def _routing_tables(ids_t, counts_row):
    counts = counts_row[EXPERT_LANE0:EXPERT_LANE0 + N_EXPERTS]
    tiles = (counts + TM_MOE - 1) // TM_MOE
    tile_end = jnp.cumsum(tiles)
    offsets = (tile_end - tiles) * TM_MOE
    n_used = tile_end[-1:]
    n_tokens = ids_t.shape[1]
    experts = ids_t[0:EXPERT_TOP_K]
    hit = experts[None] == jnp.arange(N_EXPERTS, dtype=I32)[:, None, None]
    first_slot = jnp.sum(jnp.where(hit, offsets[:, None, None], 0), axis=0)
    slots = (first_slot + ids_t[EXPERT_TOP_K:2 * EXPERT_TOP_K]).reshape(-1)
    n_tiles = (EXPERT_TOP_K * n_tokens) // TM_MOE + N_EXPERTS
    tile_ids = jnp.minimum(jnp.arange(n_tiles, dtype=I32), n_used - 1)
    tile_expert = jnp.sum(tile_end[None, :] <= tile_ids[:, None], axis=1).astype(I32)
    slab_of_slot = jnp.zeros((n_tiles * TM_MOE,), I32).at[slots].set(
        jnp.arange(slots.size, dtype=I32) % n_tokens * WORD_ROWS, unique_indices=True)
    eid = jnp.arange(N_EXPERTS, dtype=I32)
    later_owner = (eid[None, :] > eid[:, None]) & (tiles[None, :] > 0)
    next_expert = jnp.min(jnp.where(later_owner, eid[None, :], N_EXPERTS), axis=1).astype(I32)
    stage_slot = ((jnp.cumsum(tiles > 0) - 1) % 2).astype(I32)
    return (slots.astype(I32), slab_of_slot, tile_expert, n_used.astype(I32),
            next_expert, stage_slot)


def kernel(x, p, positions, w_in, b_in, sinks, g_norm_a, g_norm_b, w_out, b_out,
           ln1_g, ln1_b, w_group, b_group, w_er, b_er, w_gate, w_up, w_down,
           w_ple_gate, b_ple_gate, w_ple_proj, ln2_g, ln2_b):
    batch, seq, d = x.shape
    m = batch * seq
    row = lambda v: v.reshape(1, -1)
    x2 = x.reshape(m, d)
    for i in range(DEPTH):
        proj = _in_projection(x2, w_in[i].astype(BF16), row(b_in[i]))
        inv_freq = ROPE_THETA ** (-jnp.arange(0, SWA_HEAD_DIM, 2, dtype=F32) / SWA_HEAD_DIM)
        inv_freq = jnp.tile(inv_freq, LANES // inv_freq.shape[0]).reshape(1, LANES)
        o_a = _swa_attention(proj, positions.reshape(m, 1), inv_freq, sinks[i], batch, seq)
        o_b = _sb_attention(proj, batch, seq)
        y1, mu1, rs1, x1s = _out_projection(
            o_a, o_b, row(g_norm_a[i]), row(g_norm_b[i]), w_out[i].astype(BF16), x2,
            row(b_out[i]), row(ln1_g[i]), row(ln1_b[i]))
        pad = ROUTER_COLS - N_GROUPS - N_EXPERTS
        w_r = jnp.concatenate(
            [w_group[i], w_er[i].transpose(1, 0, 2).reshape(d, N_EXPERTS),
             jnp.zeros((d, pad), F32)], axis=1).astype(BF16)
        b_r = jnp.concatenate([b_group[i], b_er[i].reshape(-1), jnp.zeros((pad,), F32)])
        ids, wts, counts = _router(x1s, w_r, row(b_r))
        (slots, slab_of_slot, tile_expert, n_used, next_expert,
         stage_slot) = _routing_tables(ids, counts[0])
        act = _grouped_gate_up(tile_expert, n_used, slab_of_slot, next_expert, stage_slot,
                               x1s, w_gate[i], w_up[i])
        y_sorted = _grouped_down(tile_expert, n_used, next_expert, stage_slot, act, w_down[i])
        moe = _combine_experts(slots, y_sorted, wts)
        x2 = _final_stage(x1s, w_ple_gate[i].astype(BF16), row(b_ple_gate[i]),
                          p[i].reshape(m, PLE_DIM), w_ple_proj[i], y1, mu1, rs1,
                          row(ln1_g[i]), row(ln1_b[i]), moe, row(ln2_g[i]), row(ln2_b[i]))
    return x2.reshape(batch, seq, d)
```
